```python
import jax, jax.numpy as jnp
from jax import lax
import numpy as np

D_MODEL = 1024
BATCH = 8
SEQ = 2048
DEPTH = 1
DEC_BATCH = 128
DEC_SEQ = 4
PAST_LEN = 16384
PAGE_SIZE = 128

POOL_WINDOWS = (2, 4, 8, 16)
POOL_GROUPS = 4
POOL_GROUP_DIM = D_MODEL // 8
D_POOL = POOL_GROUPS * POOL_GROUP_DIM
POOL_BUF = 16 - 1
RET_HEADS = 4
RET_DK = D_MODEL // 8
RET_DV = D_MODEL // 4
D_RET_K = RET_HEADS * RET_DK
D_RET_V = RET_HEADS * RET_DV
RET_CHUNK = 128
ROPE_BASE = 10000.0
IN_WIDTHS = (D_POOL, D_RET_K, D_RET_K, D_RET_V, D_RET_V, D_MODEL, D_MODEL)
D_IN = D_POOL + 2 * D_RET_K + 2 * D_RET_V + 2 * D_MODEL
N_EXPERTS = 64
TOP_K = 8
N_GROUPS = 8
TOPK_GROUPS = 4
D_EXPERT = D_MODEL // 4
ROUTED_SCALE = 2.5
N_MOD = 6
EPS = 1e-6

kernel_name = "hybrid_pool_retention_moe_decode_step"


def rmsnorm(x, g):
    xf = x.astype(jnp.float32)
    y = xf * lax.rsqrt(jnp.mean(xf * xf, axis=-1, keepdims=True) + EPS)
    return (y * g.astype(jnp.float32)).astype(x.dtype)


def modulate(h, shift, scale):
    return h * (1 + scale[:, None, :]) + shift[:, None, :]


def split_columns(z):
    outs, off = [], 0
    for w in IN_WIDTHS:
        outs.append(z[..., off:off + w])
        off += w
    return outs


def rotary(x, pos):
    half = x.shape[-1] // 2
    inv = jnp.power(ROPE_BASE, -jnp.arange(half, dtype=jnp.float32) / half)
    ang = pos.astype(jnp.float32)[:, None] * inv[None, :]
    cos = jnp.cos(ang)[None, :, None, :]
    sin = jnp.sin(ang)[None, :, None, :]
    xf = x.astype(jnp.float32)
    x1, x2 = xf[..., :half], xf[..., half:]
    return jnp.concatenate([x1 * cos - x2 * sin, x1 * sin + x2 * cos], axis=-1).astype(x.dtype)


def pool_mix(u_ext, start_pos, L):
    B = u_ext.shape[0]
    uf = u_ext.astype(jnp.float32)
    cs = jnp.concatenate([jnp.zeros((B, 1, D_POOL), jnp.float32), jnp.cumsum(uf, axis=1)], axis=1)
    pos = start_pos + jnp.arange(L, dtype=jnp.int32)
    outs = []
    for gi, w in enumerate(POOL_WINDOWS):
        lo, hi = gi * POOL_GROUP_DIM, (gi + 1) * POOL_GROUP_DIM
        win = cs[:, POOL_BUF + 1:POOL_BUF + 1 + L, lo:hi] - cs[:, POOL_BUF + 1 - w:POOL_BUF + 1 - w + L, lo:hi]
        cnt = jnp.minimum(w, pos + 1).astype(jnp.float32)
        outs.append(win / cnt[None, :, None])
    mean = jnp.concatenate(outs, axis=-1)
    return (mean - uf[:, POOL_BUF:]).astype(u_ext.dtype)


def retention_scan(S0, q, k, v, log_g, chunk):
    B, L, H, _ = q.shape
    nc = L // chunk
    idx = jnp.arange(chunk, dtype=jnp.float32)
    diff = idx[:, None] - idx[None, :]
    causal = diff >= 0
    intra = jnp.where(causal[None], jnp.exp(log_g[:, None, None] * jnp.maximum(diff, 0.0)[None]), 0.0)
    q_dec = jnp.exp(log_g[None, :] * (idx[:, None] + 1.0))
    k_dec = jnp.exp(log_g[None, :] * (chunk - 1.0 - idx[:, None]))
    s_dec = jnp.exp(log_g * chunk)

    def to_chunks(t):
        return t.reshape(B, nc, chunk, *t.shape[2:]).swapaxes(0, 1)

    def step(S, qkv):
        qc, kc, vc = qkv
        qf, kf, vf = qc.astype(jnp.float32), kc.astype(jnp.float32), vc.astype(jnp.float32)
        scores = jnp.einsum('bchd,bmhd->bhcm', qf, kf) * intra[None]
        inner = jnp.einsum('bhcm,bmhe->bche', scores, vf)
        cross = jnp.einsum('bchd,bhde->bche', qf * q_dec[None, :, :, None], S)
        S_new = s_dec[None, :, None, None] * S + jnp.einsum('bchd,bche->bhde', kf * k_dec[None, :, :, None], vf)
        return S_new, inner + cross

    S_fin, o = lax.scan(step, S0, (to_chunks(q), to_chunks(k), to_chunks(v)))
    o = o.swapaxes(0, 1).reshape(B, L, H, -1)
    return o, S_fin


def token_mixer(h, pool_prefix, S0, start_pos, chunk, log_g,
                w_in, w_pool_group, pool_scale, w_branch_pool, w_branch_ret, w_out):
    B, L, _ = h.shape
    u, q, k, v, g, gate_a, gate_b = split_columns(h @ w_in)
    u_ext = jnp.concatenate([pool_prefix.astype(u.dtype), u], axis=1)
    pm = pool_mix(u_ext, start_pos, L)
    pm = jnp.einsum('blgc,gcd->blgd', pm.reshape(B, L, POOL_GROUPS, POOL_GROUP_DIM), w_pool_group)
    pm = pm.reshape(B, L, D_POOL) * pool_scale
    a = pm @ w_branch_pool
    pos = start_pos + jnp.arange(L, dtype=jnp.int32)
    qh = rotary(q.reshape(B, L, RET_HEADS, RET_DK), pos)
    kh = rotary(k.reshape(B, L, RET_HEADS, RET_DK), pos) * (RET_DK ** -0.5)
    vh = v.reshape(B, L, RET_HEADS, RET_DV)
    o, S_new = retention_scan(S0, qh, kh, vh, log_g, chunk)
    o = o * lax.rsqrt(jnp.mean(o * o, axis=-1, keepdims=True) + EPS)
    o = o.reshape(B, L, D_RET_V).astype(h.dtype)
    r = (jax.nn.silu(g) * o) @ w_branch_ret
    merged = jax.nn.sigmoid(gate_a) * a + jax.nn.sigmoid(gate_b) * r
    return merged @ w_out, u_ext[:, -POOL_BUF:], S_new


def moe(h, w_router, b_router, w_exp_gate, w_exp_up, w_exp_down, w_sh_gate, w_sh_up, w_sh_down):
    T = h.shape[0]
    s = jax.nn.sigmoid(h.astype(jnp.float32) @ w_router.astype(jnp.float32))
    sb = s + b_router.astype(jnp.float32)
    grp_score = lax.top_k(sb.reshape(T, N_GROUPS, N_EXPERTS // N_GROUPS), 2)[0].sum(-1)
    _, gidx = lax.top_k(grp_score, TOPK_GROUPS)
    gmask = jax.nn.one_hot(gidx, N_GROUPS, dtype=jnp.float32).sum(1) > 0
    emask = jnp.repeat(gmask, N_EXPERTS // N_GROUPS, axis=1)
    _, eidx = lax.top_k(jnp.where(emask, sb, -jnp.inf), TOP_K)
    sel = jnp.take_along_axis(s, eidx, axis=1)
    wts = sel / jnp.sum(sel, axis=-1, keepdims=True) * ROUTED_SCALE
    gates = jnp.einsum('tke,tk->te', jax.nn.one_hot(eidx, N_EXPERTS, dtype=jnp.float32), wts)

    def expert_step(acc, ew):
        g_e, wg, wu, wd = ew
        hid = jax.nn.silu(h @ wg) * (h @ wu)
        return acc + g_e[:, None] * (hid @ wd).astype(jnp.float32), None

    acc, _ = lax.scan(expert_step, jnp.zeros((T, D_MODEL), jnp.float32),
                      (gates.T, w_exp_gate, w_exp_up, w_exp_down))
    shared = (jax.nn.silu(h @ w_sh_gate) * (h @ w_sh_up)) @ w_sh_down
    return (acc + shared.astype(jnp.float32)).astype(h.dtype)


def setup_inputs(seed: int = 0) -> dict:
    key = jax.random.key(seed)
    ks = jax.random.split(key, 32)
    nrm = jax.random.normal
    f32 = jnp.float32
    inp = {
        "x_prompt": nrm(ks[0], (BATCH, SEQ, D_MODEL), f32),
        "x_sample": nrm(ks[1], (DEC_BATCH, DEC_SEQ, D_MODEL), f32),
        "c_prompt": nrm(ks[2], (BATCH, D_MODEL), f32),
        "c_sample": nrm(ks[3], (DEC_BATCH, D_MODEL), f32),
        "state_pool": nrm(ks[4], (DEPTH, DEC_BATCH, POOL_BUF, D_POOL), f32),
        "state_ret": 0.5 * nrm(ks[5], (DEPTH, DEC_BATCH, RET_HEADS, RET_DK, RET_DV), f32),
        "w_ada": 0.5 * D_MODEL ** -0.5 * nrm(ks[6], (DEPTH, D_MODEL, N_MOD * D_MODEL), f32),
        "b_ada": 0.01 * nrm(ks[7], (DEPTH, N_MOD * D_MODEL), f32),
        "g_norm1": 1.0 + 0.01 * nrm(ks[8], (DEPTH, D_MODEL), f32),
        "w_in": D_MODEL ** -0.5 * nrm(ks[9], (DEPTH, D_MODEL, D_IN), f32),
        "w_pool_group": POOL_GROUP_DIM ** -0.5 * nrm(ks[10], (DEPTH, POOL_GROUPS, POOL_GROUP_DIM, POOL_GROUP_DIM), f32),
        "pool_scale": 1.0 + 0.1 * nrm(ks[11], (DEPTH, D_POOL), f32),
        "w_branch_pool": D_POOL ** -0.5 * nrm(ks[12], (DEPTH, D_POOL, D_MODEL), f32),
        "w_branch_ret": D_RET_V ** -0.5 * nrm(ks[13], (DEPTH, D_RET_V, D_MODEL), f32),
        "w_out": D_MODEL ** -0.5 * nrm(ks[14], (DEPTH, D_MODEL, D_MODEL), f32),
        "g_norm2": 1.0 + 0.01 * nrm(ks[15], (DEPTH, D_MODEL), f32),
        "w_router": D_MODEL ** -0.5 * nrm(ks[16], (DEPTH, D_MODEL, N_EXPERTS), f32),
        "b_router": 0.01 * nrm(ks[17], (DEPTH, N_EXPERTS), f32),
        "w_exp_gate": D_MODEL ** -0.5 * nrm(ks[18], (DEPTH, N_EXPERTS, D_MODEL, D_EXPERT), f32),
        "w_exp_up": D_MODEL ** -0.5 * nrm(ks[19], (DEPTH, N_EXPERTS, D_MODEL, D_EXPERT), f32),
        "w_exp_down": D_EXPERT ** -0.5 * nrm(ks[20], (DEPTH, N_EXPERTS, D_EXPERT, D_MODEL), f32),
        "w_sh_gate": D_MODEL ** -0.5 * nrm(ks[21], (DEPTH, D_MODEL, D_EXPERT), f32),
        "w_sh_up": D_MODEL ** -0.5 * nrm(ks[22], (DEPTH, D_MODEL, D_EXPERT), f32),
        "w_sh_down": D_EXPERT ** -0.5 * nrm(ks[23], (DEPTH, D_EXPERT, D_MODEL), f32),
        "g_final": 1.0 + 0.01 * nrm(ks[24], (D_MODEL,), f32),
    }
    return inp


def reference(x_prompt, x_sample, c_prompt, c_sample, state_pool, state_ret,
              w_ada, b_ada, g_norm1, w_in, w_pool_group, pool_scale, w_branch_pool,
              w_branch_ret, w_out, g_norm2, w_router, b_router, w_exp_gate, w_exp_up,
              w_exp_down, w_sh_gate, w_sh_up, w_sh_down, g_final):
    B, S, D = x_prompt.shape
    DB, DS, _ = x_sample.shape
    log_g = jnp.log(1.0 - jnp.power(2.0, -5.0 - jnp.arange(RET_HEADS, dtype=jnp.float32)))
    xp, xs = x_prompt, x_sample
    pool_p, ret_p, pool_s, ret_s = [], [], [], []
    for l in range(DEPTH):
        mod_p = (c_prompt @ w_ada[l] + b_ada[l]).reshape(B, N_MOD, D)
        mod_s = (c_sample @ w_ada[l] + b_ada[l]).reshape(DB, N_MOD, D)
        mix_w = (log_g, w_in[l], w_pool_group[l], pool_scale[l], w_branch_pool[l], w_branch_ret[l], w_out[l])
        hp = modulate(rmsnorm(xp, g_norm1[l]), mod_p[:, 0], mod_p[:, 1])
        hs = modulate(rmsnorm(xs, g_norm1[l]), mod_s[:, 0], mod_s[:, 1])
        mp, bp, sp = token_mixer(hp, jnp.zeros((B, POOL_BUF, D_POOL), hp.dtype),
                                 jnp.zeros((B, RET_HEADS, RET_DK, RET_DV), jnp.float32),
                                 0, RET_CHUNK, *mix_w)
        ms, bs, ss = token_mixer(hs, state_pool[l], state_ret[l].astype(jnp.float32),
                                 PAST_LEN, DS, *mix_w)
        xp = xp + mod_p[:, 2][:, None, :] * mp
        xs = xs + mod_s[:, 2][:, None, :] * ms
        pool_p.append(bp)
        ret_p.append(sp.astype(x_prompt.dtype))
        pool_s.append(bs)
        ret_s.append(ss.astype(state_ret.dtype))
        hp2 = modulate(rmsnorm(xp, g_norm2[l]), mod_p[:, 3], mod_p[:, 4])
        hs2 = modulate(rmsnorm(xs, g_norm2[l]), mod_s[:, 3], mod_s[:, 4])
        flat = jnp.concatenate([hp2.reshape(B * S, D), hs2.reshape(DB * DS, D)], axis=0)
        f = moe(flat, w_router[l], b_router[l], w_exp_gate[l], w_exp_up[l], w_exp_down[l],
                w_sh_gate[l], w_sh_up[l], w_sh_down[l])
        xp = xp + mod_p[:, 5][:, None, :] * f[:B * S].reshape(B, S, D)
        xs = xs + mod_s[:, 5][:, None, :] * f[B * S:].reshape(DB, DS, D)
    y_prompt = rmsnorm(xp, g_final)
    y_sample = rmsnorm(xs, g_final)
    new_pool_prompt = jnp.stack(pool_p, axis=0)
    new_ret_prompt = jnp.stack(ret_p, axis=0)
    new_pool_sample = jnp.stack(pool_s, axis=0)
    new_ret_sample = jnp.stack(ret_s, axis=0)
    return (y_prompt, y_sample, new_pool_prompt, new_ret_prompt, new_pool_sample, new_ret_sample)
```

```python
import functools

import jax
import jax.numpy as jnp
from jax import lax
from jax.experimental import pallas as pl
from jax.experimental.pallas import tpu as pltpu

F32 = jnp.float32
BF16 = jnp.bfloat16

PAST_LEN = 16384
POOL_WINDOWS = (2, 4, 8, 16)
RET_CHUNK = 128
ROPE_BASE = 10000.0
TOP_K = 8
N_GROUPS = 8
TOPK_GROUPS = 4
ROUTED_SCALE = 2.5
N_MOD = 6
EPS = 1e-6

V7X_VMEM_BYTES = 64 * 1024 * 1024
SUBLANES = 8
LANES = 128

SAMPLE_SEQS_PER_STEP = 8
TOKEN_BLOCK = 256
EXPERT_TOKEN_BLOCK = 512
POOL_PREV = 16


def _vmem_limit(nbytes):
    return int(min(nbytes, V7X_VMEM_BYTES - 8 * 1024 * 1024))


def _dot(a, b):
    return jnp.dot(a, b, preferred_element_type=F32)


def _dot_nt(a, b):
    return lax.dot_general(a, b, (((1,), (1,)), ((), ())), preferred_element_type=F32)


def _dot_tn(a, b):
    return lax.dot_general(a, b, (((0,), (0,)), ((), ())), preferred_element_type=F32)


def _split_bf16(x, parts):
    out = []
    for _ in range(parts):
        p = x.astype(BF16)
        out.append(p)
        x = x - p.astype(F32)
    return out


def _rms_mod(x, g, shift, scale):
    y = x * lax.rsqrt(jnp.mean(x * x, axis=-1, keepdims=True) + EPS)
    return (y * g) * (1.0 + scale) + shift


def _rotary(x, cosb, sinb):
    return x * cosb + pltpu.roll(x, x.shape[-1] // 2, axis=1) * sinb


def _group_norm(o):
    return o * lax.rsqrt(jnp.mean(o * o, axis=-1, keepdims=True) + EPS)


def _ada_kernel(c_ref, w_ref, b_ref, o_ref):
    o_ref[...] = _dot(c_ref[...].astype(BF16), w_ref[...].astype(BF16)) + b_ref[...]


def _ada(c_all, w_ada, b_ada):
    m, d = c_all.shape
    n = w_ada.shape[1]
    tn = n // 4
    return pl.pallas_call(
        _ada_kernel,
        out_shape=jax.ShapeDtypeStruct((m, n), F32),
        grid=(n // tn,),
        in_specs=[
            pl.BlockSpec((m, d), lambda j: (0, 0)),
            pl.BlockSpec((d, tn), lambda j: (0, j)),
            pl.BlockSpec((1, tn), lambda j: (0, j)),
        ],
        out_specs=pl.BlockSpec((m, tn), lambda j: (0, j)),
        compiler_params=pltpu.CompilerParams(
            dimension_semantics=("arbitrary",),
            vmem_limit_bytes=_vmem_limit(4 * d * tn * 4 + 8 * m * (d + tn) * 4),
        ),
        name="ada",
    )(c_all, w_ada, b_ada.reshape(1, n))


class _Dims:
    def __init__(self, d_model, d_pool, heads, dk, dv):
        self.d = d_model
        self.d_pool = d_pool
        self.gd = d_pool // len(POOL_WINDOWS)
        self.heads = heads
        self.dk = dk
        self.dv = dv
        self.d_k = heads * dk
        self.d_v = heads * dv
        widths = (d_pool, self.d_k, self.d_k, self.d_v, self.d_v, d_model, d_model)
        offs = [0]
        for w in widths:
            offs.append(offs[-1] + w)
        self.cols = tuple((offs[i], offs[i + 1]) for i in range(len(widths)))
        self.d_in = offs[-1]


def _proj(hb, win_ref, cols):
    return _dot(hb, win_ref[:, cols[0]:cols[1]])


def _pool_branch(pm_parts, wpg_ref, ps_ref, wbp_ref):
    pg = [_dot(pm.astype(BF16), wpg_ref[gi]) for gi, pm in enumerate(pm_parts)]
    pm = jnp.concatenate(pg, axis=1) * ps_ref[...]
    return _dot(pm.astype(BF16), wbp_ref[...])


def _merge_out(x, gate, a, o, g, ga, gb, wbr_ref, wout_ref):
    r = _dot((jax.nn.silu(g) * o).astype(BF16), wbr_ref[...])
    merged = jax.nn.sigmoid(ga) * a + jax.nn.sigmoid(gb) * r
    m = _dot(merged.astype(BF16), wout_ref[...])
    return x + gate * m


def _mixer_prompt_kernel(dims, x_ref, mod_ref, g1_ref, win_ref, wpg_ref, ps_ref, wbp_ref,
                         wbr_ref, wout_ref, cos_ref, sin_ref, intra_ref, qdec_ref,
                         kdec_ref, sdec_ref, x1_ref, pool_ref, ret_ref, uext_ref):
    c = pl.program_id(1)
    L = x_ref.shape[1]

    @pl.when(c == 0)
    def _init():
        ret_ref[...] = jnp.zeros_like(ret_ref)
        uext_ref[0:POOL_PREV, :] = jnp.zeros((POOL_PREV, dims.d_pool), F32)

    x = x_ref[0]
    shift, scale, gate = mod_ref[0, 0:1, :], mod_ref[0, 1:2, :], mod_ref[0, 2:3, :]
    hb = _rms_mod(x, g1_ref[...], shift, scale).astype(BF16)

    u = _proj(hb, win_ref, dims.cols[0])
    uext_ref[POOL_PREV:POOL_PREV + L, :] = u
    pos = c * L + lax.broadcasted_iota(jnp.int32, (L, 1), 0)
    pm_parts = []
    for gi, w in enumerate(POOL_WINDOWS):
        s = uext_ref[:, gi * dims.gd:(gi + 1) * dims.gd]
        sh = 1
        while sh < w:
            s = s + pltpu.roll(s, sh, axis=0)
            sh *= 2
        cnt = jnp.minimum(w, pos + 1).astype(F32)
        pm_parts.append(s[POOL_PREV:, :] / cnt - u[:, gi * dims.gd:(gi + 1) * dims.gd])
    a = _pool_branch(pm_parts, wpg_ref, ps_ref, wbp_ref)
    uext_ref[0:POOL_PREV, :] = uext_ref[L:L + POOL_PREV, :]
    pool_ref[0] = uext_ref[0:POOL_PREV, :]

    q = _proj(hb, win_ref, dims.cols[1])
    k = _proj(hb, win_ref, dims.cols[2])
    v = _proj(hb, win_ref, dims.cols[3])
    cosb, sinb = cos_ref[...], sin_ref[...]
    outs = []
    for h in range(dims.heads):
        qr = _rotary(q[:, h * dims.dk:(h + 1) * dims.dk], cosb, sinb)
        kr = _rotary(k[:, h * dims.dk:(h + 1) * dims.dk], cosb, sinb) * (dims.dk ** -0.5)
        vb = v[:, h * dims.dv:(h + 1) * dims.dv].astype(BF16)
        scores = _dot_nt(qr.astype(BF16), kr.astype(BF16)) * intra_ref[h]
        s_prev = ret_ref[0, 0, h]
        o = _dot(scores.astype(BF16), vb) + _dot((qr * qdec_ref[h]).astype(BF16),
                                                  s_prev.astype(BF16))
        ret_ref[0, 0, h] = sdec_ref[h] * s_prev + _dot_tn((kr * kdec_ref[h]).astype(BF16), vb)
        outs.append(_group_norm(o))
    o = jnp.concatenate(outs, axis=1)

    g = _proj(hb, win_ref, dims.cols[4])
    ga = _proj(hb, win_ref, dims.cols[5])
    gb = _proj(hb, win_ref, dims.cols[6])
    x1_ref[0] = _merge_out(x, gate, a, o, g, ga, gb, wbr_ref, wout_ref)


def _const_spec(shape):
    zeros = (0,) * len(shape)
    return pl.BlockSpec(shape, lambda *_: zeros)


def _mixer_prompt(dims, x, mod, g1, weights, tables):
    b, s, d = x.shape
    L = RET_CHUNK
    nc = s // L
    win, wpg, ps, wbp, wbr, wout = weights
    cosb, sinb, intra, qdec, kdec, sdec = tables
    in_specs = [
        pl.BlockSpec((1, L, d), lambda i, c: (i, c, 0)),
        pl.BlockSpec((1, N_MOD, d), lambda i, c: (i, 0, 0)),
        _const_spec(g1.shape), _const_spec(win.shape), _const_spec(wpg.shape),
        _const_spec(ps.shape), _const_spec(wbp.shape), _const_spec(wbr.shape),
        _const_spec(wout.shape),
        pl.BlockSpec((L, dims.dk), lambda i, c: (c, 0)),
        pl.BlockSpec((L, dims.dk), lambda i, c: (c, 0)),
        _const_spec(intra.shape), _const_spec(qdec.shape), _const_spec(kdec.shape),
        _const_spec(sdec.shape),
    ]
    out_shape = (
        jax.ShapeDtypeStruct((b, s, d), F32),
        jax.ShapeDtypeStruct((b, POOL_PREV, dims.d_pool), F32),
        jax.ShapeDtypeStruct((1, b, dims.heads, dims.dk, dims.dv), F32),
    )
    out_specs = (
        pl.BlockSpec((1, L, d), lambda i, c: (i, c, 0)),
        pl.BlockSpec((1, POOL_PREV, dims.d_pool), lambda i, c: (i, 0, 0)),
        pl.BlockSpec((1, 1, dims.heads, dims.dk, dims.dv), lambda i, c: (0, i, 0, 0, 0)),
    )
    weight_bytes = 2 * (win.size + wpg.size + wbp.size + wbr.size + wout.size)
    return pl.pallas_call(
        functools.partial(_mixer_prompt_kernel, dims),
        out_shape=out_shape,
        grid=(b, nc),
        in_specs=in_specs,
        out_specs=out_specs,
        scratch_shapes=[pltpu.VMEM((POOL_PREV + L, dims.d_pool), F32)],
        compiler_params=pltpu.CompilerParams(
            dimension_semantics=("arbitrary", "arbitrary"),
            vmem_limit_bytes=_vmem_limit(2 * weight_bytes + 16 * 1024 * 1024),
        ),
        name="mixer_prompt",
    )(x, mod, g1, win, wpg, ps, wbp, wbr, wout, cosb, sinb, intra, qdec, kdec, sdec)


def _mixer_sample_kernel(dims, ds, x_ref, modt_ref, g1_ref, win_ref, wpg_ref, ps_ref, wbp_ref,
                         wbr_ref, wout_ref, cos_ref, sin_ref, intra_ref, qdec_ref, kdec_ref,
                         sdec_ref, bandp_ref, bandu_ref, cnt_ref, poolin_ref, sret_ref,
                         x1_ref, u_ref, ret_ref):
    R = x_ref.shape[0]
    ns = R // ds
    x = x_ref[...]
    shift, scale, gate = modt_ref[0], modt_ref[1], modt_ref[2]
    hb = _rms_mod(x, g1_ref[...], shift, scale).astype(BF16)

    u = _proj(hb, win_ref, dims.cols[0])
    u_ref[...] = u
    hist = poolin_ref[...].reshape(ns * POOL_PREV, dims.d_pool)
    pm_parts = []
    for gi in range(len(POOL_WINDOWS)):
        lo, hi = gi * dims.gd, (gi + 1) * dims.gd
        win_sum = jnp.zeros((R, dims.gd), F32)
        for part in _split_bf16(hist[:, lo:hi], 3):
            win_sum = win_sum + _dot(bandp_ref[gi], part)
        for part in _split_bf16(u[:, lo:hi], 3):
            win_sum = win_sum + _dot(bandu_ref[gi], part)
        pm_parts.append(win_sum / cnt_ref[gi] - u[:, lo:hi])
    a = _pool_branch(pm_parts, wpg_ref, ps_ref, wbp_ref)

    q = _proj(hb, win_ref, dims.cols[1])
    k = _proj(hb, win_ref, dims.cols[2])
    v = _proj(hb, win_ref, dims.cols[3])
    cosb, sinb = cos_ref[...], sin_ref[...]
    row_seq = lax.broadcasted_iota(jnp.int32, (R, ns * dims.dk), 0) // ds
    col_seq = lax.broadcasted_iota(jnp.int32, (R, ns * dims.dk), 1) // dims.dk
    own = row_seq == col_seq
    outs = []
    for h in range(dims.heads):
        qr = _rotary(q[:, h * dims.dk:(h + 1) * dims.dk], cosb, sinb)
        kr = _rotary(k[:, h * dims.dk:(h + 1) * dims.dk], cosb, sinb) * (dims.dk ** -0.5)
        vb = v[:, h * dims.dv:(h + 1) * dims.dv].astype(BF16)
        scores = _dot_nt(qr.astype(BF16), kr.astype(BF16)) * intra_ref[h]
        s_prev = sret_ref[:, h].reshape(ns * dims.dk, dims.dv)
        qd = jnp.concatenate([qr * qdec_ref[h]] * ns, axis=1)
        kd = jnp.concatenate([kr * kdec_ref[h]] * ns, axis=1)
        qexp = jnp.where(own, qd, 0.0).astype(BF16)
        kexp = jnp.where(own, kd, 0.0).astype(BF16)
        o = _dot(scores.astype(BF16), vb) + _dot(qexp, s_prev.astype(BF16))
        s_new = sdec_ref[h] * s_prev + _dot_tn(kexp, vb)
        ret_ref[:, h] = s_new.reshape(ns, dims.dk, dims.dv)
        outs.append(_group_norm(o))
    o = jnp.concatenate(outs, axis=1)

    g = _proj(hb, win_ref, dims.cols[4])
    ga = _proj(hb, win_ref, dims.cols[5])
    gb = _proj(hb, win_ref, dims.cols[6])
    x1_ref[...] = _merge_out(x, gate, a, o, g, ga, gb, wbr_ref, wout_ref)


def _mixer_sample(dims, ds, x2d, modt, g1, weights, tables, pool_in, state_ret):
    t, d = x2d.shape
    ns = SAMPLE_SEQS_PER_STEP
    R = ns * ds
    win, wpg, ps, wbp, wbr, wout = weights
    cosb, sinb, intra, qdec, kdec, sdec, bandp, bandu, cnt = tables
    db = state_ret.shape[0]
    in_specs = [
        pl.BlockSpec((R, d), lambda i: (i, 0)),
        pl.BlockSpec((N_MOD, R, d), lambda i: (0, i, 0)),
        _const_spec(g1.shape), _const_spec(win.shape), _const_spec(wpg.shape),
        _const_spec(ps.shape), _const_spec(wbp.shape), _const_spec(wbr.shape),
        _const_spec(wout.shape), _const_spec(cosb.shape), _const_spec(sinb.shape),
        _const_spec(intra.shape), _const_spec(qdec.shape), _const_spec(kdec.shape),
        _const_spec(sdec.shape), _const_spec(bandp.shape), _const_spec(bandu.shape),
        _const_spec(cnt.shape),
        pl.BlockSpec((ns, POOL_PREV, dims.d_pool), lambda i: (i, 0, 0)),
        pl.BlockSpec((ns, dims.heads, dims.dk, dims.dv), lambda i: (i, 0, 0, 0)),
    ]
    out_shape = (
        jax.ShapeDtypeStruct((t, d), F32),
        jax.ShapeDtypeStruct((t, dims.d_pool), F32),
        jax.ShapeDtypeStruct(state_ret.shape, state_ret.dtype),
    )
    out_specs = (
        pl.BlockSpec((R, d), lambda i: (i, 0)),
        pl.BlockSpec((R, dims.d_pool), lambda i: (i, 0)),
        pl.BlockSpec((ns, dims.heads, dims.dk, dims.dv), lambda i: (i, 0, 0, 0)),
    )
    weight_bytes = 2 * (win.size + wpg.size + wbp.size + wbr.size + wout.size)
    state_bytes = 4 * ns * dims.heads * dims.dk * dims.dv
    return pl.pallas_call(
        functools.partial(_mixer_sample_kernel, dims, ds),
        out_shape=out_shape,
        grid=(db // ns,),
        in_specs=in_specs,
        out_specs=out_specs,
        compiler_params=pltpu.CompilerParams(
            dimension_semantics=("arbitrary",),
            vmem_limit_bytes=_vmem_limit(2 * weight_bytes + 4 * state_bytes + 8 * 1024 * 1024),
        ),
        name="mixer_sample",
    )(x2d, modt, g1, win, wpg, ps, wbp, wbr, wout, cosb, sinb, intra, qdec, kdec, sdec,
      bandp, bandu, cnt, pool_in, state_ret)


def _rotary_tables(pos, dk):
    half = dk // 2
    inv = jnp.power(ROPE_BASE, -jnp.arange(half, dtype=F32) / half)
    ang = pos.astype(F32)[:, None] * inv[None, :]
    cos, sin = jnp.cos(ang), jnp.sin(ang)
    return jnp.concatenate([cos, cos], axis=1), jnp.concatenate([-sin, sin], axis=1)


def _decay_tables(log_g, chunk, reps, dk, dv):
    idx = jnp.arange(chunk, dtype=F32)
    diff = idx[:, None] - idx[None, :]
    intra = jnp.where((diff >= 0)[None],
                      jnp.exp(log_g[:, None, None] * jnp.maximum(diff, 0.0)[None]), 0.0)
    heads = log_g.shape[0]
    eye = jnp.eye(reps, dtype=F32)
    intra = jnp.einsum("ab,hij->haibj", eye, intra).reshape(heads, reps * chunk, reps * chunk)
    q_dec = jnp.exp(log_g[:, None] * (idx[None, :] + 1.0))
    k_dec = jnp.exp(log_g[:, None] * (chunk - 1.0 - idx[None, :]))
    s_dec = jnp.exp(log_g * chunk)
    q_dec = jnp.broadcast_to(jnp.tile(q_dec, (1, reps))[:, :, None], (heads, reps * chunk, dk))
    k_dec = jnp.broadcast_to(jnp.tile(k_dec, (1, reps))[:, :, None], (heads, reps * chunk, dk))
    s_dec = jnp.broadcast_to(s_dec[:, None, None], (heads, 1, dv))
    return intra, q_dec, k_dec, s_dec


def _sample_pool_tables(ns, ds, gd, start_pos):
    n = jnp.arange(ds)
    i = jnp.arange(POOL_PREV)
    eye = jnp.eye(ns, dtype=F32)
    bandp, bandu, cnt = [], [], []
    for w in POOL_WINDOWS:
        bp = ((i[None, :] - POOL_PREV) >= (n[:, None] - w + 1)).astype(F32)
        bu = ((n[None, :] <= n[:, None]) & (n[None, :] >= n[:, None] - w + 1)).astype(F32)
        bandp.append(jnp.einsum("ab,ni->anbi", eye, bp).reshape(ns * ds, ns * POOL_PREV))
        bandu.append(jnp.einsum("ab,nm->anbm", eye, bu).reshape(ns * ds, ns * ds))
        c = jnp.minimum(w, start_pos + n + 1).astype(F32)
        cnt.append(jnp.broadcast_to(jnp.tile(c, ns)[:, None], (ns * ds, gd)))
    return (jnp.stack(bandp).astype(BF16), jnp.stack(bandu).astype(BF16), jnp.stack(cnt))


def _first_max_onehot(v, idx, big, axes):
    m = jnp.max(v, axis=axes, keepdims=True)
    first = jnp.min(jnp.where(v == m, idx, big), axis=axes, keepdims=True)
    return idx == first


def _route_kernel(n_prompt_blocks, xp_ref, xs_ref, modp_ref, modt_ref, g2_ref, wr_ref, br_ref,
                  h2_ref, gates_ref):
    i = pl.program_id(0)
    is_p = i < n_prompt_blocks
    x = jnp.where(is_p, xp_ref[...], xs_ref[...])
    shift = jnp.where(is_p, modp_ref[0, 3:4, :], modt_ref[3])
    scale = jnp.where(is_p, modp_ref[0, 4:5, :], modt_ref[4])
    h2 = _rms_mod(x, g2_ref[...], shift, scale)
    h2_ref[...] = h2.astype(BF16)

    hh, hl = _split_bf16(h2, 2)
    wh, wl = _split_bf16(wr_ref[...], 2)
    logits = _dot_nt(wh, hh) + (_dot_nt(wh, hl) + _dot_nt(wl, hh))
    s = jax.nn.sigmoid(logits)
    e, t = s.shape
    ge = e // N_GROUPS
    neg = -jnp.inf
    s3 = s.reshape(N_GROUPS, ge, t)
    sb3 = s3 + br_ref[...]

    j3 = lax.broadcasted_iota(jnp.int32, (N_GROUPS, ge, t), 1)
    top1 = _first_max_onehot(sb3, j3, ge, 1)
    m1 = jnp.max(sb3, axis=1, keepdims=True)
    m2 = jnp.max(jnp.where(top1, neg, sb3), axis=1, keepdims=True)
    gv = m1 + m2

    g3 = lax.broadcasted_iota(jnp.int32, (N_GROUPS, 1, t), 0)
    gsel = jnp.zeros((N_GROUPS, 1, t), F32)
    for _ in range(TOPK_GROUPS):
        hit = _first_max_onehot(gv, g3, N_GROUPS, 0)
        gsel = jnp.where(hit, 1.0, gsel)
        gv = jnp.where(hit, neg, gv)

    e3 = lax.broadcasted_iota(jnp.int32, (N_GROUPS, ge, t), 0) * ge + j3
    ev = jnp.where(gsel > 0.5, sb3, neg)
    sel = jnp.zeros((N_GROUPS, ge, t), F32)
    for _ in range(TOP_K):
        hit = _first_max_onehot(ev, e3, e, (0, 1))
        sel = jnp.where(hit, s3, sel)
        ev = jnp.where(hit, neg, ev)
    tot = jnp.sum(sel, axis=(0, 1), keepdims=True)
    gates_ref[...] = (sel / tot * ROUTED_SCALE).reshape(e, t)


def _route(xp2d, xs2d, modp, modt, g2, wr_t, br):
    tp, d = xp2d.shape
    ts = xs2d.shape[0]
    tb = TOKEN_BLOCK
    npb, nsb = tp // tb, ts // tb
    e = wr_t.shape[0]
    seq = tp // modp.shape[0]
    bpb = seq // tb

    def p_idx(i):
        return jnp.minimum(i, npb - 1)

    def s_idx(i):
        return jnp.maximum(i - npb, 0)

    return pl.pallas_call(
        functools.partial(_route_kernel, npb),
        out_shape=(jax.ShapeDtypeStruct((tp + ts, d), BF16),
                   jax.ShapeDtypeStruct((e, tp + ts), F32)),
        grid=(npb + nsb,),
        in_specs=[
            pl.BlockSpec((tb, d), lambda i: (p_idx(i), 0)),
            pl.BlockSpec((tb, d), lambda i: (s_idx(i), 0)),
            pl.BlockSpec((1, N_MOD, d), lambda i: (p_idx(i) // bpb, 0, 0)),
            pl.BlockSpec((N_MOD, tb, d), lambda i: (0, s_idx(i), 0)),
            _const_spec(g2.shape), _const_spec(wr_t.shape), _const_spec(br.shape),
        ],
        out_specs=(pl.BlockSpec((tb, d), lambda i: (i, 0)),
                   pl.BlockSpec((e, tb), lambda i: (0, i))),
        compiler_params=pltpu.CompilerParams(
            dimension_semantics=("arbitrary",),
            vmem_limit_bytes=_vmem_limit(32 * 1024 * 1024),
        ),
        name="route",
    )(xp2d, xs2d, modp, modt, g2, wr_t, br)


def _experts_kernel(h_ref, gates_ref, wg_ref, wu_ref, wd_ref, f_ref, acc_ref):
    e = pl.program_id(1)

    @pl.when(e == 0)
    def _init():
        acc_ref[...] = jnp.zeros_like(acc_ref)

    hb = h_ref[...]
    gates = gates_ref[...]
    lane = lax.broadcasted_iota(jnp.int32, gates.shape, 1)
    g_e = jnp.sum(jnp.where(lane == e, gates, 0.0), axis=1, keepdims=True)
    hid = jax.nn.silu(_dot(hb, wg_ref[0])) * _dot(hb, wu_ref[0])
    acc_ref[...] += _dot((g_e * hid).astype(BF16), wd_ref[0])

    @pl.when(e == pl.num_programs(1) - 1)
    def _fin():
        f_ref[...] = acc_ref[...]


def _experts(h2, gates, wg, wu, wd):
    t, d = h2.shape
    ne, _, de = wg.shape
    tm = EXPERT_TOKEN_BLOCK
    return pl.pallas_call(
        _experts_kernel,
        out_shape=jax.ShapeDtypeStruct((t, d), F32),
        grid=(t // tm, ne),
        in_specs=[
            pl.BlockSpec((tm, d), lambda i, e: (i, 0)),
            pl.BlockSpec((tm, ne), lambda i, e: (i, 0)),
            pl.BlockSpec((1, d, de), lambda i, e: (e, 0, 0)),
            pl.BlockSpec((1, d, de), lambda i, e: (e, 0, 0)),
            pl.BlockSpec((1, de, d), lambda i, e: (e, 0, 0)),
        ],
        out_specs=pl.BlockSpec((tm, d), lambda i, e: (i, 0)),
        scratch_shapes=[pltpu.VMEM((tm, d), F32)],
        compiler_params=pltpu.CompilerParams(
            dimension_semantics=("arbitrary", "arbitrary"),
            vmem_limit_bytes=_vmem_limit(32 * 1024 * 1024),
        ),
        name="experts",
    )(h2, gates, wg, wu, wd)


def _final_kernel(n_prompt_blocks, xp_ref, xs_ref, modp_ref, modt_ref, h2_ref, f_ref,
                  wsg_ref, wsu_ref, wsd_ref, gf_ref, yp_ref, ys_ref):
    i = pl.program_id(0)
    is_p = i < n_prompt_blocks
    x = jnp.where(is_p, xp_ref[...], xs_ref[...])
    gate = jnp.where(is_p, modp_ref[0, 5:6, :], modt_ref[5])
    hb = h2_ref[...]
    hid = jax.nn.silu(_dot(hb, wsg_ref[...])) * _dot(hb, wsu_ref[...])
    f = f_ref[...] + _dot(hid.astype(BF16), wsd_ref[...])
    x2 = x + gate * f
    y = x2 * lax.rsqrt(jnp.mean(x2 * x2, axis=-1, keepdims=True) + EPS) * gf_ref[...]

    @pl.when(is_p)
    def _p():
        yp_ref[...] = y

    @pl.when(jnp.logical_not(is_p))
    def _s():
        ys_ref[...] = y


def _final(xp2d, xs2d, modp, modt, h2, f, wsg, wsu, wsd, gf):
    tp, d = xp2d.shape
    ts = xs2d.shape[0]
    tb = TOKEN_BLOCK
    npb, nsb = tp // tb, ts // tb
    seq = tp // modp.shape[0]
    bpb = seq // tb

    def p_idx(i):
        return jnp.minimum(i, npb - 1)

    def s_idx(i):
        return jnp.maximum(i - npb, 0)

    return pl.pallas_call(
        functools.partial(_final_kernel, npb),
        out_shape=(jax.ShapeDtypeStruct((tp, d), F32), jax.ShapeDtypeStruct((ts, d), F32)),
        grid=(npb + nsb,),
        in_specs=[
            pl.BlockSpec((tb, d), lambda i: (p_idx(i), 0)),
            pl.BlockSpec((tb, d), lambda i: (s_idx(i), 0)),
            pl.BlockSpec((1, N_MOD, d), lambda i: (p_idx(i) // bpb, 0, 0)),
            pl.BlockSpec((N_MOD, tb, d), lambda i: (0, s_idx(i), 0)),
            pl.BlockSpec((tb, d), lambda i: (i, 0)),
            pl.BlockSpec((tb, d), lambda i: (i, 0)),
            _const_spec(wsg.shape), _const_spec(wsu.shape), _const_spec(wsd.shape),
            _const_spec(gf.shape),
        ],
        out_specs=(pl.BlockSpec((tb, d), lambda i: (p_idx(i), 0)),
                   pl.BlockSpec((tb, d), lambda i: (s_idx(i), 0))),
        compiler_params=pltpu.CompilerParams(
            dimension_semantics=("arbitrary",),
            vmem_limit_bytes=_vmem_limit(32 * 1024 * 1024),
        ),
        name="final",
    )(xp2d, xs2d, modp, modt, h2, f, wsg, wsu, wsd, gf)


def kernel(x_prompt, x_sample, c_prompt, c_sample, state_pool, state_ret, w_ada, b_ada, g_norm1,
           w_in, w_pool_group, pool_scale, w_branch_pool, w_branch_ret, w_out, g_norm2, w_router,
           b_router, w_exp_gate, w_exp_up, w_exp_down, w_sh_gate, w_sh_up, w_sh_down, g_final):
    b, s, d = x_prompt.shape
    db, ds, _ = x_sample.shape
    depth = w_ada.shape[0]
    heads, dk, dv = state_ret.shape[2:]
    pool_buf, d_pool = state_pool.shape[2:]
    dims = _Dims(d, d_pool, heads, dk, dv)
    assert dims.d_in == w_in.shape[2]
    assert pool_buf == max(POOL_WINDOWS) - 1 and pool_buf < POOL_PREV
    assert s % RET_CHUNK == 0 and s % TOKEN_BLOCK == 0 and (db * ds) % TOKEN_BLOCK == 0
    assert db % SAMPLE_SEQS_PER_STEP == 0 and (b * s + db * ds) % EXPERT_TOKEN_BLOCK == 0

    log_g = jnp.log(1.0 - jnp.power(2.0, -5.0 - jnp.arange(heads, dtype=F32)))
    ns = SAMPLE_SEQS_PER_STEP
    cos_p, sin_p = _rotary_tables(jnp.arange(s, dtype=jnp.int32), dk)
    cos_s, sin_s = _rotary_tables(PAST_LEN + jnp.arange(ds, dtype=jnp.int32), dk)
    cos_s, sin_s = jnp.tile(cos_s, (ns, 1)), jnp.tile(sin_s, (ns, 1))
    dec_p = _decay_tables(log_g, RET_CHUNK, 1, dk, dv)
    dec_s = _decay_tables(log_g, ds, ns, dk, dv)
    pool_tabs = _sample_pool_tables(ns, ds, dims.gd, PAST_LEN)

    assert depth == 1, "the final kernel fuses the output norm into the (single) layer"
    l = 0
    xs = x_sample.reshape(db * ds, d)
    mod = _ada(jnp.concatenate([c_prompt, c_sample], axis=0), w_ada[l], b_ada[l])
    modp = mod[:b].reshape(b, N_MOD, d)
    mods = mod[b:].reshape(db, N_MOD, d)
    modt = jnp.repeat(mods.transpose(1, 0, 2), ds, axis=1)
    g1 = g_norm1[l].reshape(1, d)
    weights = (w_in[l].astype(BF16), w_pool_group[l].astype(BF16),
               pool_scale[l].reshape(1, d_pool), w_branch_pool[l].astype(BF16),
               w_branch_ret[l].astype(BF16), w_out[l].astype(BF16))

    x1p, pool16, ret_p = _mixer_prompt(dims, x_prompt, modp, g1, weights,
                                       (cos_p, sin_p) + dec_p)
    pool_in = jnp.pad(state_pool[l], ((0, 0), (POOL_PREV - pool_buf, 0), (0, 0)))
    x1s, u_s, ret_s = _mixer_sample(dims, ds, xs, modt, g1, weights,
                                    (cos_s, sin_s) + dec_s + pool_tabs, pool_in,
                                    state_ret[l].astype(F32))
    pool_p = pool16[:, POOL_PREV - pool_buf:]
    pool_s = jnp.concatenate([state_pool[l], u_s.reshape(db, ds, d_pool)], axis=1)[:, -pool_buf:]

    x1p2d = x1p.reshape(b * s, d)
    ne = w_router.shape[2]
    h2, gates_t = _route(x1p2d, x1s, modp, modt, g_norm2[l].reshape(1, d), w_router[l].T,
                         b_router[l].reshape(N_GROUPS, ne // N_GROUPS, 1))
    f = _experts(h2, gates_t.T, w_exp_gate[l].astype(BF16), w_exp_up[l].astype(BF16),
                 w_exp_down[l].astype(BF16))
    yp, ys = _final(x1p2d, x1s, modp, modt, h2, f, w_sh_gate[l].astype(BF16),
                    w_sh_up[l].astype(BF16), w_sh_down[l].astype(BF16), g_final.reshape(1, d))

    return (yp.reshape(b, s, d), ys.reshape(db, ds, d), pool_p[None],
            ret_p.astype(x_prompt.dtype), pool_s[None], ret_s.astype(state_ret.dtype)[None])
```

```python
import functools

import jax
import jax.numpy as jnp
from jax import lax
from jax.experimental import pallas as pl
from jax.experimental.pallas import tpu as pltpu

F32 = jnp.float32
BF16 = jnp.bfloat16

PAST_LEN = 16384
POOL_WINDOWS = (2, 4, 8, 16)
RET_CHUNK = 128
ROPE_BASE = 10000.0
TOP_K = 8
N_GROUPS = 8
TOPK_GROUPS = 4
ROUTED_SCALE = 2.5
N_MOD = 6
EPS = 1e-6

V7X_VMEM_BYTES = 64 * 1024 * 1024
SUBLANES = 8
LANES = 128

SAMPLE_SEQS_PER_STEP = 8
TOKEN_BLOCK = 256
EXPERT_TILE = 256
DMA_ROWS = 16
SLOT_ROWS = 3 * DMA_ROWS
ONEHOT_ROWS = 256
POOL_PREV = 16


def _vmem_limit(nbytes):
    return int(min(nbytes, V7X_VMEM_BYTES - 8 * 1024 * 1024))


def _dot(a, b):
    return jnp.dot(a, b, preferred_element_type=F32)


def _dot_nt(a, b):
    return lax.dot_general(a, b, (((1,), (1,)), ((), ())), preferred_element_type=F32)


def _dot_tn(a, b):
    return lax.dot_general(a, b, (((0,), (0,)), ((), ())), preferred_element_type=F32)


def _split_bf16(x, parts):
    out = []
    for _ in range(parts):
        p = x.astype(BF16)
        out.append(p)
        x = x - p.astype(F32)
    return out


def _rms_mod(x, g, shift, scale):
    y = x * lax.rsqrt(jnp.mean(x * x, axis=-1, keepdims=True) + EPS)
    return (y * g) * (1.0 + scale) + shift


def _rotary(x, cosb, sinb):
    return x * cosb + pltpu.roll(x, x.shape[-1] // 2, axis=1) * sinb


def _group_norm(o):
    return o * lax.rsqrt(jnp.mean(o * o, axis=-1, keepdims=True) + EPS)


def _ada_kernel(c_ref, w_ref, b_ref, o_ref):
    o_ref[...] = _dot(c_ref[...].astype(BF16), w_ref[...].astype(BF16)) + b_ref[...]


def _ada(c_all, w_ada, b_ada):
    m, d = c_all.shape
    n = w_ada.shape[1]
    tn = n // 4
    return pl.pallas_call(
        _ada_kernel,
        out_shape=jax.ShapeDtypeStruct((m, n), F32),
        grid=(n // tn,),
        in_specs=[
            pl.BlockSpec((m, d), lambda j: (0, 0)),
            pl.BlockSpec((d, tn), lambda j: (0, j)),
            pl.BlockSpec((1, tn), lambda j: (0, j)),
        ],
        out_specs=pl.BlockSpec((m, tn), lambda j: (0, j)),
        compiler_params=pltpu.CompilerParams(
            dimension_semantics=("arbitrary",),
            vmem_limit_bytes=_vmem_limit(4 * d * tn * 4 + 8 * m * (d + tn) * 4),
        ),
        name="ada",
    )(c_all, w_ada, b_ada.reshape(1, n))


class _Dims:
    def __init__(self, d_model, d_pool, heads, dk, dv):
        self.d = d_model
        self.d_pool = d_pool
        self.gd = d_pool // len(POOL_WINDOWS)
        self.heads = heads
        self.dk = dk
        self.dv = dv
        self.d_k = heads * dk
        self.d_v = heads * dv
        widths = (d_pool, self.d_k, self.d_k, self.d_v, self.d_v, d_model, d_model)
        offs = [0]
        for w in widths:
            offs.append(offs[-1] + w)
        self.cols = tuple((offs[i], offs[i + 1]) for i in range(len(widths)))
        self.d_in = offs[-1]


def _proj(hb, win_ref, cols):
    return _dot(hb, win_ref[:, cols[0]:cols[1]])


def _pool_branch(pm_parts, wpg_ref, ps_ref, wbp_ref):
    pg = [_dot(pm.astype(BF16), wpg_ref[gi]) for gi, pm in enumerate(pm_parts)]
    pm = jnp.concatenate(pg, axis=1) * ps_ref[...]
    return _dot(pm.astype(BF16), wbp_ref[...])


def _merge_out(x, gate, a, o, g, ga, gb, wbr_ref, wout_ref):
    r = _dot((jax.nn.silu(g) * o).astype(BF16), wbr_ref[...])
    merged = jax.nn.sigmoid(ga) * a + jax.nn.sigmoid(gb) * r
    m = _dot(merged.astype(BF16), wout_ref[...])
    return x + gate * m


def _mixer_prompt_kernel(dims, x_ref, mod_ref, g1_ref, win_ref, wpg_ref, ps_ref, wbp_ref,
                         wbr_ref, wout_ref, cos_ref, sin_ref, intra_ref, qdec_ref,
                         kdec_ref, sdec_ref, x1_ref, pool_ref, ret_ref, uext_ref):
    c = pl.program_id(1)
    L = x_ref.shape[1]

    @pl.when(c == 0)
    def _init():
        ret_ref[...] = jnp.zeros_like(ret_ref)
        uext_ref[0:POOL_PREV, :] = jnp.zeros((POOL_PREV, dims.d_pool), F32)

    x = x_ref[0]
    shift, scale, gate = mod_ref[0, 0:1, :], mod_ref[0, 1:2, :], mod_ref[0, 2:3, :]
    hb = _rms_mod(x, g1_ref[...], shift, scale).astype(BF16)

    u = _proj(hb, win_ref, dims.cols[0])
    uext_ref[POOL_PREV:POOL_PREV + L, :] = u
    pos = c * L + lax.broadcasted_iota(jnp.int32, (L, 1), 0)
    pm_parts = []
    for gi, w in enumerate(POOL_WINDOWS):
        s = uext_ref[:, gi * dims.gd:(gi + 1) * dims.gd]
        sh = 1
        while sh < w:
            s = s + pltpu.roll(s, sh, axis=0)
            sh *= 2
        cnt = jnp.minimum(w, pos + 1).astype(F32)
        pm_parts.append(s[POOL_PREV:, :] / cnt - u[:, gi * dims.gd:(gi + 1) * dims.gd])
    a = _pool_branch(pm_parts, wpg_ref, ps_ref, wbp_ref)
    uext_ref[0:POOL_PREV, :] = uext_ref[L:L + POOL_PREV, :]
    pool_ref[0] = uext_ref[0:POOL_PREV, :]

    q = _proj(hb, win_ref, dims.cols[1])
    k = _proj(hb, win_ref, dims.cols[2])
    v = _proj(hb, win_ref, dims.cols[3])
    cosb, sinb = cos_ref[...], sin_ref[...]
    outs = []
    for h in range(dims.heads):
        qr = _rotary(q[:, h * dims.dk:(h + 1) * dims.dk], cosb, sinb)
        kr = _rotary(k[:, h * dims.dk:(h + 1) * dims.dk], cosb, sinb) * (dims.dk ** -0.5)
        vb = v[:, h * dims.dv:(h + 1) * dims.dv].astype(BF16)
        scores = _dot_nt(qr.astype(BF16), kr.astype(BF16)) * intra_ref[h]
        s_prev = ret_ref[0, 0, h]
        o = _dot(scores.astype(BF16), vb) + _dot((qr * qdec_ref[h]).astype(BF16),
                                                  s_prev.astype(BF16))
        ret_ref[0, 0, h] = sdec_ref[h] * s_prev + _dot_tn((kr * kdec_ref[h]).astype(BF16), vb)
        outs.append(_group_norm(o))
    o = jnp.concatenate(outs, axis=1)

    g = _proj(hb, win_ref, dims.cols[4])
    ga = _proj(hb, win_ref, dims.cols[5])
    gb = _proj(hb, win_ref, dims.cols[6])
    x1_ref[0] = _merge_out(x, gate, a, o, g, ga, gb, wbr_ref, wout_ref)


def _const_spec(shape):
    zeros = (0,) * len(shape)
    return pl.BlockSpec(shape, lambda *_: zeros)


def _mixer_prompt(dims, x, mod, g1, weights, tables):
    b, s, d = x.shape
    L = RET_CHUNK
    nc = s // L
    win, wpg, ps, wbp, wbr, wout = weights
    cosb, sinb, intra, qdec, kdec, sdec = tables
    in_specs = [
        pl.BlockSpec((1, L, d), lambda i, c: (i, c, 0)),
        pl.BlockSpec((1, N_MOD, d), lambda i, c: (i, 0, 0)),
        _const_spec(g1.shape), _const_spec(win.shape), _const_spec(wpg.shape),
        _const_spec(ps.shape), _const_spec(wbp.shape), _const_spec(wbr.shape),
        _const_spec(wout.shape),
        pl.BlockSpec((L, dims.dk), lambda i, c: (c, 0)),
        pl.BlockSpec((L, dims.dk), lambda i, c: (c, 0)),
        _const_spec(intra.shape), _const_spec(qdec.shape), _const_spec(kdec.shape),
        _const_spec(sdec.shape),
    ]
    out_shape = (
        jax.ShapeDtypeStruct((b, s, d), F32),
        jax.ShapeDtypeStruct((b, POOL_PREV, dims.d_pool), F32),
        jax.ShapeDtypeStruct((1, b, dims.heads, dims.dk, dims.dv), F32),
    )
    out_specs = (
        pl.BlockSpec((1, L, d), lambda i, c: (i, c, 0)),
        pl.BlockSpec((1, POOL_PREV, dims.d_pool), lambda i, c: (i, 0, 0)),
        pl.BlockSpec((1, 1, dims.heads, dims.dk, dims.dv), lambda i, c: (0, i, 0, 0, 0)),
    )
    weight_bytes = 2 * (win.size + wpg.size + wbp.size + wbr.size + wout.size)
    return pl.pallas_call(
        functools.partial(_mixer_prompt_kernel, dims),
        out_shape=out_shape,
        grid=(b, nc),
        in_specs=in_specs,
        out_specs=out_specs,
        scratch_shapes=[pltpu.VMEM((POOL_PREV + L, dims.d_pool), F32)],
        compiler_params=pltpu.CompilerParams(
            dimension_semantics=("arbitrary", "arbitrary"),
            vmem_limit_bytes=_vmem_limit(2 * weight_bytes + 16 * 1024 * 1024),
        ),
        name="mixer_prompt",
    )(x, mod, g1, win, wpg, ps, wbp, wbr, wout, cosb, sinb, intra, qdec, kdec, sdec)


def _mixer_sample_kernel(dims, ds, x_ref, modt_ref, g1_ref, win_ref, wpg_ref, ps_ref, wbp_ref,
                         wbr_ref, wout_ref, cos_ref, sin_ref, intra_ref, qdec_ref, kdec_ref,
                         sdec_ref, bandp_ref, bandu_ref, cnt_ref, poolin_ref, sret_ref,
                         x1_ref, u_ref, ret_ref):
    R = x_ref.shape[0]
    ns = R // ds
    x = x_ref[...]
    shift, scale, gate = modt_ref[0], modt_ref[1], modt_ref[2]
    hb = _rms_mod(x, g1_ref[...], shift, scale).astype(BF16)

    u = _proj(hb, win_ref, dims.cols[0])
    u_ref[...] = u
    hist = poolin_ref[...].reshape(ns * POOL_PREV, dims.d_pool)
    pm_parts = []
    for gi in range(len(POOL_WINDOWS)):
        lo, hi = gi * dims.gd, (gi + 1) * dims.gd
        win_sum = jnp.zeros((R, dims.gd), F32)
        for part in _split_bf16(hist[:, lo:hi], 3):
            win_sum = win_sum + _dot(bandp_ref[gi], part)
        for part in _split_bf16(u[:, lo:hi], 3):
            win_sum = win_sum + _dot(bandu_ref[gi], part)
        pm_parts.append(win_sum / cnt_ref[gi] - u[:, lo:hi])
    a = _pool_branch(pm_parts, wpg_ref, ps_ref, wbp_ref)

    q = _proj(hb, win_ref, dims.cols[1])
    k = _proj(hb, win_ref, dims.cols[2])
    v = _proj(hb, win_ref, dims.cols[3])
    cosb, sinb = cos_ref[...], sin_ref[...]
    row_seq = lax.broadcasted_iota(jnp.int32, (R, ns * dims.dk), 0) // ds
    col_seq = lax.broadcasted_iota(jnp.int32, (R, ns * dims.dk), 1) // dims.dk
    own = row_seq == col_seq
    outs = []
    for h in range(dims.heads):
        qr = _rotary(q[:, h * dims.dk:(h + 1) * dims.dk], cosb, sinb)
        kr = _rotary(k[:, h * dims.dk:(h + 1) * dims.dk], cosb, sinb) * (dims.dk ** -0.5)
        vb = v[:, h * dims.dv:(h + 1) * dims.dv].astype(BF16)
        scores = _dot_nt(qr.astype(BF16), kr.astype(BF16)) * intra_ref[h]
        s_prev = sret_ref[:, h].reshape(ns * dims.dk, dims.dv)
        qd = jnp.concatenate([qr * qdec_ref[h]] * ns, axis=1)
        kd = jnp.concatenate([kr * kdec_ref[h]] * ns, axis=1)
        qexp = jnp.where(own, qd, 0.0).astype(BF16)
        kexp = jnp.where(own, kd, 0.0).astype(BF16)
        o = _dot(scores.astype(BF16), vb) + _dot(qexp, s_prev.astype(BF16))
        s_new = sdec_ref[h] * s_prev + _dot_tn(kexp, vb)
        ret_ref[:, h] = s_new.reshape(ns, dims.dk, dims.dv)
        outs.append(_group_norm(o))
    o = jnp.concatenate(outs, axis=1)

    g = _proj(hb, win_ref, dims.cols[4])
    ga = _proj(hb, win_ref, dims.cols[5])
    gb = _proj(hb, win_ref, dims.cols[6])
    x1_ref[...] = _merge_out(x, gate, a, o, g, ga, gb, wbr_ref, wout_ref)


def _mixer_sample(dims, ds, x2d, modt, g1, weights, tables, pool_in, state_ret):
    t, d = x2d.shape
    ns = SAMPLE_SEQS_PER_STEP
    R = ns * ds
    win, wpg, ps, wbp, wbr, wout = weights
    cosb, sinb, intra, qdec, kdec, sdec, bandp, bandu, cnt = tables
    db = state_ret.shape[0]
    in_specs = [
        pl.BlockSpec((R, d), lambda i: (i, 0)),
        pl.BlockSpec((N_MOD, R, d), lambda i: (0, i, 0)),
        _const_spec(g1.shape), _const_spec(win.shape), _const_spec(wpg.shape),
        _const_spec(ps.shape), _const_spec(wbp.shape), _const_spec(wbr.shape),
        _const_spec(wout.shape), _const_spec(cosb.shape), _const_spec(sinb.shape),
        _const_spec(intra.shape), _const_spec(qdec.shape), _const_spec(kdec.shape),
        _const_spec(sdec.shape), _const_spec(bandp.shape), _const_spec(bandu.shape),
        _const_spec(cnt.shape),
        pl.BlockSpec((ns, POOL_PREV, dims.d_pool), lambda i: (i, 0, 0)),
        pl.BlockSpec((ns, dims.heads, dims.dk, dims.dv), lambda i: (i, 0, 0, 0)),
    ]
    out_shape = (
        jax.ShapeDtypeStruct((t, d), F32),
        jax.ShapeDtypeStruct((t, dims.d_pool), F32),
        jax.ShapeDtypeStruct(state_ret.shape, state_ret.dtype),
    )
    out_specs = (
        pl.BlockSpec((R, d), lambda i: (i, 0)),
        pl.BlockSpec((R, dims.d_pool), lambda i: (i, 0)),
        pl.BlockSpec((ns, dims.heads, dims.dk, dims.dv), lambda i: (i, 0, 0, 0)),
    )
    weight_bytes = 2 * (win.size + wpg.size + wbp.size + wbr.size + wout.size)
    state_bytes = 4 * ns * dims.heads * dims.dk * dims.dv
    return pl.pallas_call(
        functools.partial(_mixer_sample_kernel, dims, ds),
        out_shape=out_shape,
        grid=(db // ns,),
        in_specs=in_specs,
        out_specs=out_specs,
        compiler_params=pltpu.CompilerParams(
            dimension_semantics=("arbitrary",),
            vmem_limit_bytes=_vmem_limit(2 * weight_bytes + 4 * state_bytes + 8 * 1024 * 1024),
        ),
        name="mixer_sample",
    )(x2d, modt, g1, win, wpg, ps, wbp, wbr, wout, cosb, sinb, intra, qdec, kdec, sdec,
      bandp, bandu, cnt, pool_in, state_ret)


def _rotary_tables(pos, dk):
    half = dk // 2
    inv = jnp.power(ROPE_BASE, -jnp.arange(half, dtype=F32) / half)
    ang = pos.astype(F32)[:, None] * inv[None, :]
    cos, sin = jnp.cos(ang), jnp.sin(ang)
    return jnp.concatenate([cos, cos], axis=1), jnp.concatenate([-sin, sin], axis=1)


def _decay_tables(log_g, chunk, reps, dk, dv):
    idx = jnp.arange(chunk, dtype=F32)
    diff = idx[:, None] - idx[None, :]
    intra = jnp.where((diff >= 0)[None],
                      jnp.exp(log_g[:, None, None] * jnp.maximum(diff, 0.0)[None]), 0.0)
    heads = log_g.shape[0]
    eye = jnp.eye(reps, dtype=F32)
    intra = jnp.einsum("ab,hij->haibj", eye, intra).reshape(heads, reps * chunk, reps * chunk)
    q_dec = jnp.exp(log_g[:, None] * (idx[None, :] + 1.0))
    k_dec = jnp.exp(log_g[:, None] * (chunk - 1.0 - idx[None, :]))
    s_dec = jnp.exp(log_g * chunk)
    q_dec = jnp.broadcast_to(jnp.tile(q_dec, (1, reps))[:, :, None], (heads, reps * chunk, dk))
    k_dec = jnp.broadcast_to(jnp.tile(k_dec, (1, reps))[:, :, None], (heads, reps * chunk, dk))
    s_dec = jnp.broadcast_to(s_dec[:, None, None], (heads, 1, dv))
    return intra, q_dec, k_dec, s_dec


def _sample_pool_tables(ns, ds, gd, start_pos):
    n = jnp.arange(ds)
    i = jnp.arange(POOL_PREV)
    eye = jnp.eye(ns, dtype=F32)
    bandp, bandu, cnt = [], [], []
    for w in POOL_WINDOWS:
        bp = ((i[None, :] - POOL_PREV) >= (n[:, None] - w + 1)).astype(F32)
        bu = ((n[None, :] <= n[:, None]) & (n[None, :] >= n[:, None] - w + 1)).astype(F32)
        bandp.append(jnp.einsum("ab,ni->anbi", eye, bp).reshape(ns * ds, ns * POOL_PREV))
        bandu.append(jnp.einsum("ab,nm->anbm", eye, bu).reshape(ns * ds, ns * ds))
        c = jnp.minimum(w, start_pos + n + 1).astype(F32)
        cnt.append(jnp.broadcast_to(jnp.tile(c, ns)[:, None], (ns * ds, gd)))
    return (jnp.stack(bandp).astype(BF16), jnp.stack(bandu).astype(BF16), jnp.stack(cnt))


def _first_max_onehot(v, idx, big, axes):
    m = jnp.max(v, axis=axes, keepdims=True)
    first = jnp.min(jnp.where(v == m, idx, big), axis=axes, keepdims=True)
    return idx == first


def _route_kernel(n_prompt_blocks, xp_ref, xs_ref, modp_ref, modt_ref, g2_ref, wr_ref, br_ref,
                  before_ref, below_ref, h2_ref, pos_ref, wts_ref, cnt_ref):
    i = pl.program_id(0)
    is_p = i < n_prompt_blocks
    x = jnp.where(is_p, xp_ref[...], xs_ref[...])
    shift = jnp.where(is_p, modp_ref[0, 3:4, :], modt_ref[3])
    scale = jnp.where(is_p, modp_ref[0, 4:5, :], modt_ref[4])
    h2 = _rms_mod(x, g2_ref[...], shift, scale)
    h2_ref[...] = h2.astype(BF16)

    hh, hl = _split_bf16(h2, 2)
    wh, wl = _split_bf16(wr_ref[...], 2)
    logits = _dot_nt(wh, hh) + (_dot_nt(wh, hl) + _dot_nt(wl, hh))
    s = jax.nn.sigmoid(logits)
    e, t = s.shape
    ge = e // N_GROUPS
    neg = -jnp.inf
    s3 = s.reshape(N_GROUPS, ge, t)
    sb3 = s3 + br_ref[...]

    j3 = lax.broadcasted_iota(jnp.int32, (N_GROUPS, ge, t), 1)
    top1 = _first_max_onehot(sb3, j3, ge, 1)
    m1 = jnp.max(sb3, axis=1, keepdims=True)
    m2 = jnp.max(jnp.where(top1, neg, sb3), axis=1, keepdims=True)
    gv = m1 + m2

    g3 = lax.broadcasted_iota(jnp.int32, (N_GROUPS, 1, t), 0)
    gsel = jnp.zeros((N_GROUPS, 1, t), F32)
    for _ in range(TOPK_GROUPS):
        hit = _first_max_onehot(gv, g3, N_GROUPS, 0)
        gsel = jnp.where(hit, 1.0, gsel)
        gv = jnp.where(hit, neg, gv)

    e3 = lax.broadcasted_iota(jnp.int32, (N_GROUPS, ge, t), 0) * ge + j3
    ev = jnp.where(gsel > 0.5, sb3, neg)
    sel = jnp.zeros((N_GROUPS, ge, t), F32)
    chosen = jnp.zeros((N_GROUPS, ge, t), F32)
    hits = []
    for _ in range(TOP_K):
        hit = _first_max_onehot(ev, e3, e, (0, 1))
        hits.append(hit)
        sel = jnp.where(hit, s3, sel)
        chosen = jnp.where(hit, 1.0, chosen)
        ev = jnp.where(hit, neg, ev)
    tot = jnp.sum(sel, axis=(0, 1), keepdims=True)
    gates = sel / tot * ROUTED_SCALE

    chosen2 = chosen.reshape(e, t)
    rank = _dot(chosen2.astype(BF16), before_ref[...])
    cnt = jnp.sum(chosen2, axis=1, keepdims=True)
    ovf_units = jnp.floor((jnp.maximum(cnt - SLOT_ROWS, 0.0) + (DMA_ROWS - 1)) / DMA_ROWS)
    ovf_off = _dot(below_ref[...], jnp.broadcast_to(ovf_units, (e, LANES)).astype(BF16))[:, 0:1]
    e_col = lax.broadcasted_iota(jnp.int32, (e, 1), 0).astype(F32)
    slot = jnp.where(rank < SLOT_ROWS, e_col * SLOT_ROWS + rank,
                     e * SLOT_ROWS + ovf_off * DMA_ROWS + (rank - SLOT_ROWS))
    slot3 = slot.reshape(N_GROUPS, ge, t)
    for k, hit in enumerate(hits):
        pos_k = jnp.sum(jnp.where(hit, slot3, 0.0), axis=(0, 1), keepdims=True)
        w_k = jnp.sum(jnp.where(hit, gates, 0.0), axis=(0, 1), keepdims=True)
        pos_ref[k:k + 1, :] = pos_k.reshape(1, t).astype(jnp.int32)
        wts_ref[k:k + 1, :] = w_k.reshape(1, t)
    cnt_ref[0] = jnp.broadcast_to(cnt, (e, LANES)).astype(jnp.int32)


def _route(xp2d, xs2d, modp, modt, g2, wr_t, br):
    tp, d = xp2d.shape
    ts = xs2d.shape[0]
    tb = TOKEN_BLOCK
    npb, nsb = tp // tb, ts // tb
    e = wr_t.shape[0]
    seq = tp // modp.shape[0]
    bpb = seq // tb

    def p_idx(i):
        return jnp.minimum(i, npb - 1)

    def s_idx(i):
        return jnp.maximum(i - npb, 0)

    t = tp + ts
    nb = npb + nsb
    tok = jnp.arange(tb)
    before = (tok[:, None] < tok[None, :]).astype(BF16)
    ex = jnp.arange(e)
    below = (ex[None, :] < ex[:, None]).astype(BF16)
    return pl.pallas_call(
        functools.partial(_route_kernel, npb),
        out_shape=(jax.ShapeDtypeStruct((t, d), BF16),
                   jax.ShapeDtypeStruct((TOP_K, t), jnp.int32),
                   jax.ShapeDtypeStruct((TOP_K, t), F32),
                   jax.ShapeDtypeStruct((nb, e, LANES), jnp.int32)),
        grid=(nb,),
        in_specs=[
            pl.BlockSpec((tb, d), lambda i: (p_idx(i), 0)),
            pl.BlockSpec((tb, d), lambda i: (s_idx(i), 0)),
            pl.BlockSpec((1, N_MOD, d), lambda i: (p_idx(i) // bpb, 0, 0)),
            pl.BlockSpec((N_MOD, tb, d), lambda i: (0, s_idx(i), 0)),
            _const_spec(g2.shape), _const_spec(wr_t.shape), _const_spec(br.shape),
            _const_spec(before.shape), _const_spec(below.shape),
        ],
        out_specs=(pl.BlockSpec((tb, d), lambda i: (i, 0)),
                   pl.BlockSpec((TOP_K, tb), lambda i: (0, i)),
                   pl.BlockSpec((TOP_K, tb), lambda i: (0, i)),
                   pl.BlockSpec((1, e, LANES), lambda i: (i, 0, 0))),
        compiler_params=pltpu.CompilerParams(
            dimension_semantics=("arbitrary",),
            vmem_limit_bytes=_vmem_limit(32 * 1024 * 1024),
        ),
        name="route",
    )(xp2d, xs2d, modp, modt, g2, wr_t, br, before, below)


class _Layout:
    def __init__(self, counts, n_tiles_cap):
        nb, ne = counts.shape
        u, g, tm = SLOT_ROWS, DMA_ROWS, EXPERT_TILE
        pc = (counts + g - 1) // g * g
        ovf_units = (jnp.maximum(counts - u, 0) + g - 1) // g
        rows_e = jnp.sum(pc, axis=0)
        region = (rows_e + u + tm - 1) // tm * tm
        goff = jnp.cumsum(region) - region
        self.dst = (goff[None, :] + jnp.cumsum(pc, axis=0) - pc).reshape(-1)
        self.ovf_units = ovf_units.reshape(-1)
        self.ovf_off = (jnp.cumsum(ovf_units, axis=1) - ovf_units).reshape(-1)
        ovf_tot = jnp.sum(ovf_units, axis=1)
        self.ovf_tot = ovf_tot
        self.n_chunks = (ne * u + ovf_tot * g + ONEHOT_ROWS - 1) // ONEHOT_ROWS
        self.tail_start = goff + rows_e
        self.tail_units = (region - rows_e) // g
        tiles_e = region // tm
        tile_end = jnp.cumsum(tiles_e)
        self.n_used = tile_end[-1:]
        tile = jnp.minimum(jnp.arange(n_tiles_cap), tile_end[-1] - 1)
        self.tile_expert = jnp.sum((tile_end[None, :] <= tile[:, None]).astype(jnp.int32), axis=1)
        self.tile_first = (tile == (tile_end - tiles_e)[self.tile_expert]).astype(jnp.int32)


def _block_buffer_rows(tb, ne):
    rows = ne * SLOT_ROWS + tb * TOP_K + ne * (DMA_ROWS - 1)
    return (rows + ONEHOT_ROWS - 1) // ONEHOT_ROWS * ONEHOT_ROWS


def _expert_rows_cap(t, tb, ne):
    nb = t // tb
    rows = t * TOP_K + nb * ne * (DMA_ROWS - 1) + ne * (SLOT_ROWS + EXPERT_TILE - 1)
    return (rows + EXPERT_TILE - 1) // EXPERT_TILE * EXPERT_TILE


def _slot_copy(buf_ref, slot, e, hbm_ref, row, sem, to_hbm):
    src = buf_ref.at[slot, pl.ds(pl.multiple_of(e * SLOT_ROWS, DMA_ROWS), SLOT_ROWS)]
    dst = hbm_ref.at[pl.ds(pl.multiple_of(row, DMA_ROWS), SLOT_ROWS)]
    return pltpu.make_async_copy(src, dst, sem) if to_hbm else pltpu.make_async_copy(dst, src, sem)


def _unit_copy(buf_ref, slot, buf_row, hbm_ref, row, sem, to_hbm):
    src = buf_ref.at[slot, pl.ds(pl.multiple_of(buf_row, DMA_ROWS), DMA_ROWS)]
    dst = hbm_ref.at[pl.ds(pl.multiple_of(row, DMA_ROWS), DMA_ROWS)]
    return pltpu.make_async_copy(src, dst, sem) if to_hbm else pltpu.make_async_copy(dst, src, sem)


def _block_copies(ne, blk, slot, dst_ref, ovfu_ref, ovfo_ref, buf_ref, hbm_ref, sem, to_hbm, wait):
    def per_expert(e, carry):
        row = dst_ref[blk * ne + e]
        c = _slot_copy(buf_ref, slot, e, hbm_ref, row, sem, to_hbm)
        c.wait() if wait else c.start()

        def per_unit(i, carry2):
            buf_row = ne * SLOT_ROWS + (ovfo_ref[blk * ne + e] + i) * DMA_ROWS
            cu = _unit_copy(buf_ref, slot, buf_row, hbm_ref, row + SLOT_ROWS + i * DMA_ROWS, sem,
                            to_hbm)
            cu.wait() if wait else cu.start()
            return carry2

        lax.fori_loop(0, ovfu_ref[blk * ne + e], per_unit, 0)
        return carry

    lax.fori_loop(0, ne, per_expert, 0)


def _onehot_rows(pos, r0):
    return lax.broadcasted_iota(jnp.int32, (ONEHOT_ROWS, pos.shape[1]), 0) + r0


def _dispatch_kernel(ne, dst_ref, ovfu_ref, ovfo_ref, nchunk_ref, tails_ref, tailu_ref,
                     h2_ref, pos_ref, xs_hbm, buf_ref, zero_ref, sem, zsem):
    b = pl.program_id(0)
    nb = pl.num_programs(0)
    slot = b % 2
    pos = pos_ref[...]
    h2 = h2_ref[...]

    def chunk(ci, carry):
        r0 = pl.multiple_of(ci * ONEHOT_ROWS, ONEHOT_ROWS)
        rows = _onehot_rows(pos, r0)
        hit = pos[0:1, :] == rows
        for k in range(1, TOP_K):
            hit = jnp.logical_or(hit, pos[k:k + 1, :] == rows)
        onehot = jnp.where(hit, 1.0, 0.0).astype(BF16)
        buf_ref[slot, pl.ds(r0, ONEHOT_ROWS), :] = _dot(onehot, h2).astype(BF16)
        return carry

    lax.fori_loop(0, nchunk_ref[b], chunk, 0)

    @pl.when(b > 0)
    def _prev():
        _block_copies(ne, b - 1, 1 - slot, dst_ref, ovfu_ref, ovfo_ref, buf_ref, xs_hbm,
                      sem.at[1 - slot], True, True)

    _block_copies(ne, b, slot, dst_ref, ovfu_ref, ovfo_ref, buf_ref, xs_hbm, sem.at[slot], True,
                  False)

    @pl.when(b == nb - 1)
    def _last():
        _block_copies(ne, b, slot, dst_ref, ovfu_ref, ovfo_ref, buf_ref, xs_hbm, sem.at[slot],
                      True, True)
        zero_ref[...] = jnp.zeros_like(zero_ref)

        def tail(wait):
            def per_expert(e, carry):
                def per_unit(i, carry2):
                    row = pl.multiple_of(tails_ref[e] + i * DMA_ROWS, DMA_ROWS)
                    c = pltpu.make_async_copy(zero_ref, xs_hbm.at[pl.ds(row, DMA_ROWS)], zsem)
                    c.wait() if wait else c.start()
                    return carry2
                lax.fori_loop(0, tailu_ref[e], per_unit, 0)
                return carry
            lax.fori_loop(0, ne, per_expert, 0)

        tail(False)
        tail(True)


def _dispatch(h2, pos, lay, ne, rows_cap):
    t, d = h2.shape
    tb = TOKEN_BLOCK
    buf_rows = _block_buffer_rows(tb, ne)
    return pl.pallas_call(
        functools.partial(_dispatch_kernel, ne),
        out_shape=jax.ShapeDtypeStruct((rows_cap, d), BF16),
        grid_spec=pltpu.PrefetchScalarGridSpec(
            num_scalar_prefetch=6,
            grid=(t // tb,),
            in_specs=[
                pl.BlockSpec((tb, d), lambda i, *_: (i, 0)),
                pl.BlockSpec((TOP_K, tb), lambda i, *_: (0, i)),
            ],
            out_specs=pl.BlockSpec(memory_space=pl.ANY),
            scratch_shapes=[
                pltpu.VMEM((2, buf_rows, d), BF16),
                pltpu.VMEM((DMA_ROWS, d), BF16),
                pltpu.SemaphoreType.DMA((2,)),
                pltpu.SemaphoreType.DMA(()),
            ],
        ),
        compiler_params=pltpu.CompilerParams(
            dimension_semantics=("arbitrary",),
            vmem_limit_bytes=_vmem_limit(4 * buf_rows * d + 16 * 1024 * 1024),
        ),
        name="dispatch",
    )(lay.dst, lay.ovf_units, lay.ovf_off, lay.n_chunks, lay.tail_start, lay.tail_units, h2, pos)


def _experts_kernel(te_ref, first_ref, nused_ref, x_ref, wg_ref, wu_ref, wd_ref, y_ref,
                    wgb_ref, wub_ref, wdb_ref):
    i = pl.program_id(0)

    @pl.when(i < nused_ref[0])
    def _tile():
        @pl.when(first_ref[i] == 1)
        def _cast():
            wgb_ref[...] = wg_ref[0].astype(BF16)
            wub_ref[...] = wu_ref[0].astype(BF16)
            wdb_ref[...] = wd_ref[0].astype(BF16)

        x = x_ref[...]
        hid = jax.nn.silu(_dot(x, wgb_ref[...])) * _dot(x, wub_ref[...])
        y_ref[...] = _dot(hid.astype(BF16), wdb_ref[...]).astype(BF16)


def _experts(xs, lay, wg, wu, wd):
    rows_cap, d = xs.shape
    ne, _, de = wg.shape
    tm = EXPERT_TILE

    def tile_idx(i, te, first, nused):
        return jnp.minimum(i, nused[0] - 1)

    return pl.pallas_call(
        _experts_kernel,
        out_shape=jax.ShapeDtypeStruct((rows_cap, d), BF16),
        grid_spec=pltpu.PrefetchScalarGridSpec(
            num_scalar_prefetch=3,
            grid=(rows_cap // tm,),
            in_specs=[
                pl.BlockSpec((tm, d), lambda i, *s: (tile_idx(i, *s), 0)),
                pl.BlockSpec((1, d, de), lambda i, te, *s: (te[i], 0, 0)),
                pl.BlockSpec((1, d, de), lambda i, te, *s: (te[i], 0, 0)),
                pl.BlockSpec((1, de, d), lambda i, te, *s: (te[i], 0, 0)),
            ],
            out_specs=pl.BlockSpec((tm, d), lambda i, *s: (tile_idx(i, *s), 0)),
            scratch_shapes=[pltpu.VMEM((d, de), BF16), pltpu.VMEM((d, de), BF16),
                            pltpu.VMEM((de, d), BF16)],
        ),
        compiler_params=pltpu.CompilerParams(
            dimension_semantics=("arbitrary",),
            vmem_limit_bytes=_vmem_limit(32 * 1024 * 1024),
        ),
        name="experts",
    )(lay.tile_expert, lay.tile_first, lay.n_used, xs, wg, wu, wd)


def _final_kernel(n_prompt_blocks, ne, dst_ref, ovfu_ref, ovfo_ref, nchunk_ref,
                  xp_ref, xs_ref, modp_ref, modt_ref, h2_ref, pos_ref, wts_ref,
                  wsg_ref, wsu_ref, wsd_ref, gf_ref, ye_hbm, yp_ref, ys_ref,
                  buf_ref, acc_ref, sem):
    i = pl.program_id(0)
    nb = pl.num_programs(0)
    slot = i % 2

    def fetch(blk, sl, wait):
        _block_copies(ne, blk, sl, dst_ref, ovfu_ref, ovfo_ref, buf_ref, ye_hbm, sem.at[sl],
                      False, wait)

    @pl.when(i == 0)
    def _first():
        buf_ref[...] = jnp.zeros_like(buf_ref)
        fetch(0, 0, False)

    @pl.when(i + 1 < nb)
    def _prefetch():
        fetch(i + 1, 1 - slot, False)

    fetch(i, slot, True)

    pos = pos_ref[...]
    wts = wts_ref[...]
    acc_ref[...] = jnp.zeros_like(acc_ref)

    def chunk(ci, carry):
        r0 = pl.multiple_of(ci * ONEHOT_ROWS, ONEHOT_ROWS)
        rows = _onehot_rows(pos, r0)
        sel = jnp.zeros(rows.shape, F32)
        for k in range(TOP_K):
            sel = jnp.where(pos[k:k + 1, :] == rows, wts[k:k + 1, :], sel)
        acc_ref[...] += _dot_tn(sel.astype(BF16), buf_ref[slot, pl.ds(r0, ONEHOT_ROWS), :])
        return carry

    lax.fori_loop(0, nchunk_ref[i], chunk, 0)

    is_p = i < n_prompt_blocks
    x = jnp.where(is_p, xp_ref[...], xs_ref[...])
    gate = jnp.where(is_p, modp_ref[0, 5:6, :], modt_ref[5])
    hb = h2_ref[...]
    hid = jax.nn.silu(_dot(hb, wsg_ref[...])) * _dot(hb, wsu_ref[...])
    f = acc_ref[...] + _dot(hid.astype(BF16), wsd_ref[...])
    x2 = x + gate * f
    y = x2 * lax.rsqrt(jnp.mean(x2 * x2, axis=-1, keepdims=True) + EPS) * gf_ref[...]

    @pl.when(is_p)
    def _p():
        yp_ref[...] = y

    @pl.when(jnp.logical_not(is_p))
    def _s():
        ys_ref[...] = y


def _final(xp2d, xs2d, modp, modt, h2, pos, wts, ye, lay, ne, wsg, wsu, wsd, gf):
    tp, d = xp2d.shape
    ts = xs2d.shape[0]
    tb = TOKEN_BLOCK
    npb, nsb = tp // tb, ts // tb
    seq = tp // modp.shape[0]
    bpb = seq // tb
    buf_rows = _block_buffer_rows(tb, ne)

    def p_idx(i):
        return jnp.minimum(i, npb - 1)

    def s_idx(i):
        return jnp.maximum(i - npb, 0)

    def const(shape):
        zeros = (0,) * len(shape)
        return pl.BlockSpec(shape, lambda i, *_: zeros)

    return pl.pallas_call(
        functools.partial(_final_kernel, npb, ne),
        out_shape=(jax.ShapeDtypeStruct((tp, d), F32), jax.ShapeDtypeStruct((ts, d), F32)),
        grid_spec=pltpu.PrefetchScalarGridSpec(
            num_scalar_prefetch=4,
            grid=(npb + nsb,),
            in_specs=[
                pl.BlockSpec((tb, d), lambda i, *_: (p_idx(i), 0)),
                pl.BlockSpec((tb, d), lambda i, *_: (s_idx(i), 0)),
                pl.BlockSpec((1, N_MOD, d), lambda i, *_: (p_idx(i) // bpb, 0, 0)),
                pl.BlockSpec((N_MOD, tb, d), lambda i, *_: (0, s_idx(i), 0)),
                pl.BlockSpec((tb, d), lambda i, *_: (i, 0)),
                pl.BlockSpec((TOP_K, tb), lambda i, *_: (0, i)),
                pl.BlockSpec((TOP_K, tb), lambda i, *_: (0, i)),
                const(wsg.shape), const(wsu.shape), const(wsd.shape), const(gf.shape),
                pl.BlockSpec(memory_space=pl.ANY),
            ],
            out_specs=(pl.BlockSpec((tb, d), lambda i, *_: (p_idx(i), 0)),
                       pl.BlockSpec((tb, d), lambda i, *_: (s_idx(i), 0))),
            scratch_shapes=[
                pltpu.VMEM((2, buf_rows, d), BF16),
                pltpu.VMEM((tb, d), F32),
                pltpu.SemaphoreType.DMA((2,)),
            ],
        ),
        compiler_params=pltpu.CompilerParams(
            dimension_semantics=("arbitrary",),
            vmem_limit_bytes=_vmem_limit(4 * buf_rows * d + 20 * 1024 * 1024),
        ),
        name="final",
    )(lay.dst, lay.ovf_units, lay.ovf_off, lay.n_chunks,
      xp2d, xs2d, modp, modt, h2, pos, wts, wsg, wsu, wsd, gf, ye)


def kernel(x_prompt, x_sample, c_prompt, c_sample, state_pool, state_ret, w_ada, b_ada, g_norm1,
           w_in, w_pool_group, pool_scale, w_branch_pool, w_branch_ret, w_out, g_norm2, w_router,
           b_router, w_exp_gate, w_exp_up, w_exp_down, w_sh_gate, w_sh_up, w_sh_down, g_final):
    b, s, d = x_prompt.shape
    db, ds, _ = x_sample.shape
    depth = w_ada.shape[0]
    heads, dk, dv = state_ret.shape[2:]
    pool_buf, d_pool = state_pool.shape[2:]
    dims = _Dims(d, d_pool, heads, dk, dv)
    assert dims.d_in == w_in.shape[2]
    assert pool_buf == max(POOL_WINDOWS) - 1 and pool_buf < POOL_PREV
    assert s % RET_CHUNK == 0 and s % TOKEN_BLOCK == 0 and (db * ds) % TOKEN_BLOCK == 0
    assert db % SAMPLE_SEQS_PER_STEP == 0 and SLOT_ROWS % DMA_ROWS == 0

    log_g = jnp.log(1.0 - jnp.power(2.0, -5.0 - jnp.arange(heads, dtype=F32)))
    ns = SAMPLE_SEQS_PER_STEP
    cos_p, sin_p = _rotary_tables(jnp.arange(s, dtype=jnp.int32), dk)
    cos_s, sin_s = _rotary_tables(PAST_LEN + jnp.arange(ds, dtype=jnp.int32), dk)
    cos_s, sin_s = jnp.tile(cos_s, (ns, 1)), jnp.tile(sin_s, (ns, 1))
    dec_p = _decay_tables(log_g, RET_CHUNK, 1, dk, dv)
    dec_s = _decay_tables(log_g, ds, ns, dk, dv)
    pool_tabs = _sample_pool_tables(ns, ds, dims.gd, PAST_LEN)

    assert depth == 1, "the final kernel fuses the output norm into the (single) layer"
    l = 0
    xs = x_sample.reshape(db * ds, d)
    mod = _ada(jnp.concatenate([c_prompt, c_sample], axis=0), w_ada[l], b_ada[l])
    modp = mod[:b].reshape(b, N_MOD, d)
    mods = mod[b:].reshape(db, N_MOD, d)
    modt = jnp.repeat(mods.transpose(1, 0, 2), ds, axis=1)
    g1 = g_norm1[l].reshape(1, d)
    weights = (w_in[l].astype(BF16), w_pool_group[l].astype(BF16),
               pool_scale[l].reshape(1, d_pool), w_branch_pool[l].astype(BF16),
               w_branch_ret[l].astype(BF16), w_out[l].astype(BF16))

    x1p, pool16, ret_p = _mixer_prompt(dims, x_prompt, modp, g1, weights,
                                       (cos_p, sin_p) + dec_p)
    pool_in = jnp.pad(state_pool[l], ((0, 0), (POOL_PREV - pool_buf, 0), (0, 0)))
    x1s, u_s, ret_s = _mixer_sample(dims, ds, xs, modt, g1, weights,
                                    (cos_s, sin_s) + dec_s + pool_tabs, pool_in,
                                    state_ret[l].astype(F32))
    pool_p = pool16[:, POOL_PREV - pool_buf:]
    pool_s = jnp.concatenate([state_pool[l], u_s.reshape(db, ds, d_pool)], axis=1)[:, -pool_buf:]

    x1p2d = x1p.reshape(b * s, d)
    ne = w_router.shape[2]
    h2, pos, wts, counts = _route(x1p2d, x1s, modp, modt, g_norm2[l].reshape(1, d),
                                  w_router[l].T,
                                  b_router[l].reshape(N_GROUPS, ne // N_GROUPS, 1))
    rows_cap = _expert_rows_cap(b * s + db * ds, TOKEN_BLOCK, ne)
    lay = _Layout(counts[:, :, 0], rows_cap // EXPERT_TILE)
    xe = _dispatch(h2, pos, lay, ne, rows_cap)
    ye = _experts(xe, lay, w_exp_gate[l], w_exp_up[l], w_exp_down[l])
    yp, ys = _final(x1p2d, x1s, modp, modt, h2, pos, wts, ye, lay, ne,
                    w_sh_gate[l].astype(BF16), w_sh_up[l].astype(BF16),
                    w_sh_down[l].astype(BF16), g_final.reshape(1, d))

    return (yp.reshape(b, s, d), ys.reshape(db, ds, d), pool_p[None],
            ret_p.astype(x_prompt.dtype), pool_s[None], ret_s.astype(state_ret.dtype)[None])
```

```python
import functools

import jax
import jax.numpy as jnp
from jax import lax
from jax.experimental import pallas as pl
from jax.experimental.pallas import tpu as pltpu

F32 = jnp.float32
BF16 = jnp.bfloat16

PAST_LEN = 16384
POOL_WINDOWS = (2, 4, 8, 16)
RET_CHUNK = 128
ROPE_BASE = 10000.0
TOP_K = 8
N_GROUPS = 8
TOPK_GROUPS = 4
ROUTED_SCALE = 2.5
N_MOD = 6
EPS = 1e-6

V7X_VMEM_BYTES = 64 * 1024 * 1024
SUBLANES = 8
LANES = 128

SAMPLE_SEQS_PER_STEP = 8
TOKEN_BLOCK = 256
DMA_ROWS = 16
SLOT_ROWS = 3 * DMA_ROWS
ONEHOT_ROWS = 256
SLOT_CHUNK_EXPERTS = 16
EXPERT_TILE = 512
POOL_PREV = 16


def _vmem_limit(nbytes):
    return int(min(nbytes, V7X_VMEM_BYTES - 8 * 1024 * 1024))


def _dot(a, b):
    return jnp.dot(a, b, preferred_element_type=F32)


def _dot_nt(a, b):
    return lax.dot_general(a, b, (((1,), (1,)), ((), ())), preferred_element_type=F32)


def _dot_tn(a, b):
    return lax.dot_general(a, b, (((0,), (0,)), ((), ())), preferred_element_type=F32)


def _split_bf16(x, parts):
    out = []
    for _ in range(parts):
        p = x.astype(BF16)
        out.append(p)
        x = x - p.astype(F32)
    return out


def _rms_mod(x, g, shift, scale):
    y = x * lax.rsqrt(jnp.mean(x * x, axis=-1, keepdims=True) + EPS)
    return (y * g) * (1.0 + scale) + shift


def _rotary(x, cosb, sinb):
    return x * cosb + pltpu.roll(x, x.shape[-1] // 2, axis=1) * sinb


def _group_norm(o):
    return o * lax.rsqrt(jnp.mean(o * o, axis=-1, keepdims=True) + EPS)


def _ada_kernel(c_ref, w_ref, b_ref, o_ref):
    o_ref[...] = _dot(c_ref[...].astype(BF16), w_ref[...].astype(BF16)) + b_ref[...]


def _ada(c_all, w_ada, b_ada):
    m, d = c_all.shape
    n = w_ada.shape[1]
    tn = n // 4
    return pl.pallas_call(
        _ada_kernel,
        out_shape=jax.ShapeDtypeStruct((m, n), F32),
        grid=(n // tn,),
        in_specs=[
            pl.BlockSpec((m, d), lambda j: (0, 0)),
            pl.BlockSpec((d, tn), lambda j: (0, j)),
            pl.BlockSpec((1, tn), lambda j: (0, j)),
        ],
        out_specs=pl.BlockSpec((m, tn), lambda j: (0, j)),
        compiler_params=pltpu.CompilerParams(
            dimension_semantics=("arbitrary",),
            vmem_limit_bytes=_vmem_limit(4 * d * tn * 4 + 8 * m * (d + tn) * 4),
        ),
        name="ada",
    )(c_all, w_ada, b_ada.reshape(1, n))


class _Dims:
    def __init__(self, d_model, d_pool, heads, dk, dv):
        self.d = d_model
        self.d_pool = d_pool
        self.gd = d_pool // len(POOL_WINDOWS)
        self.heads = heads
        self.dk = dk
        self.dv = dv
        self.d_k = heads * dk
        self.d_v = heads * dv
        widths = (d_pool, self.d_k, self.d_k, self.d_v, self.d_v, d_model, d_model)
        offs = [0]
        for w in widths:
            offs.append(offs[-1] + w)
        self.cols = tuple((offs[i], offs[i + 1]) for i in range(len(widths)))
        self.d_in = offs[-1]


def _proj(hb, win_ref, cols):
    return _dot(hb, win_ref[:, cols[0]:cols[1]])


def _pool_branch(pm_parts, wpg_ref, ps_ref, wbp_ref):
    pg = [_dot(pm.astype(BF16), wpg_ref[gi]) for gi, pm in enumerate(pm_parts)]
    pm = jnp.concatenate(pg, axis=1) * ps_ref[...]
    return _dot(pm.astype(BF16), wbp_ref[...])


def _merge_out(x, gate, a, o, g, ga, gb, wbr_ref, wout_ref):
    r = _dot((jax.nn.silu(g) * o).astype(BF16), wbr_ref[...])
    merged = jax.nn.sigmoid(ga) * a + jax.nn.sigmoid(gb) * r
    m = _dot(merged.astype(BF16), wout_ref[...])
    return x + gate * m


def _mixer_prompt_kernel(dims, x_ref, mod_ref, g1_ref, win_ref, wpg_ref, ps_ref, wbp_ref,
                         wbr_ref, wout_ref, cos_ref, sin_ref, intra_ref, qdec_ref,
                         kdec_ref, sdec_ref, x1_ref, pool_ref, ret_ref, uext_ref):
    c = pl.program_id(1)
    L = x_ref.shape[1]

    @pl.when(c == 0)
    def _init():
        ret_ref[...] = jnp.zeros_like(ret_ref)
        uext_ref[0:POOL_PREV, :] = jnp.zeros((POOL_PREV, dims.d_pool), F32)

    x = x_ref[0]
    shift, scale, gate = mod_ref[0, 0:1, :], mod_ref[0, 1:2, :], mod_ref[0, 2:3, :]
    hb = _rms_mod(x, g1_ref[...], shift, scale).astype(BF16)

    u = _proj(hb, win_ref, dims.cols[0])
    uext_ref[POOL_PREV:POOL_PREV + L, :] = u
    pos = c * L + lax.broadcasted_iota(jnp.int32, (L, 1), 0)
    pm_parts = []
    for gi, w in enumerate(POOL_WINDOWS):
        s = uext_ref[:, gi * dims.gd:(gi + 1) * dims.gd]
        sh = 1
        while sh < w:
            s = s + pltpu.roll(s, sh, axis=0)
            sh *= 2
        cnt = jnp.minimum(w, pos + 1).astype(F32)
        pm_parts.append(s[POOL_PREV:, :] / cnt - u[:, gi * dims.gd:(gi + 1) * dims.gd])
    a = _pool_branch(pm_parts, wpg_ref, ps_ref, wbp_ref)
    uext_ref[0:POOL_PREV, :] = uext_ref[L:L + POOL_PREV, :]
    pool_ref[0] = uext_ref[0:POOL_PREV, :]

    q = _proj(hb, win_ref, dims.cols[1])
    k = _proj(hb, win_ref, dims.cols[2])
    v = _proj(hb, win_ref, dims.cols[3])
    cosb, sinb = cos_ref[...], sin_ref[...]
    outs = []
    for h in range(dims.heads):
        qr = _rotary(q[:, h * dims.dk:(h + 1) * dims.dk], cosb, sinb)
        kr = _rotary(k[:, h * dims.dk:(h + 1) * dims.dk], cosb, sinb) * (dims.dk ** -0.5)
        vb = v[:, h * dims.dv:(h + 1) * dims.dv].astype(BF16)
        scores = _dot_nt(qr.astype(BF16), kr.astype(BF16)) * intra_ref[h]
        s_prev = ret_ref[0, 0, h]
        o = _dot(scores.astype(BF16), vb) + _dot((qr * qdec_ref[h]).astype(BF16),
                                                  s_prev.astype(BF16))
        ret_ref[0, 0, h] = sdec_ref[h] * s_prev + _dot_tn((kr * kdec_ref[h]).astype(BF16), vb)
        outs.append(_group_norm(o))
    o = jnp.concatenate(outs, axis=1)

    g = _proj(hb, win_ref, dims.cols[4])
    ga = _proj(hb, win_ref, dims.cols[5])
    gb = _proj(hb, win_ref, dims.cols[6])
    x1_ref[0] = _merge_out(x, gate, a, o, g, ga, gb, wbr_ref, wout_ref)


def _const_spec(shape):
    zeros = (0,) * len(shape)
    return pl.BlockSpec(shape, lambda *_: zeros)


def _mixer_prompt(dims, x, mod, g1, weights, tables):
    b, s, d = x.shape
    L = RET_CHUNK
    nc = s // L
    win, wpg, ps, wbp, wbr, wout = weights
    cosb, sinb, intra, qdec, kdec, sdec = tables
    in_specs = [
        pl.BlockSpec((1, L, d), lambda i, c: (i, c, 0)),
        pl.BlockSpec((1, N_MOD, d), lambda i, c: (i, 0, 0)),
        _const_spec(g1.shape), _const_spec(win.shape), _const_spec(wpg.shape),
        _const_spec(ps.shape), _const_spec(wbp.shape), _const_spec(wbr.shape),
        _const_spec(wout.shape),
        pl.BlockSpec((L, dims.dk), lambda i, c: (c, 0)),
        pl.BlockSpec((L, dims.dk), lambda i, c: (c, 0)),
        _const_spec(intra.shape), _const_spec(qdec.shape), _const_spec(kdec.shape),
        _const_spec(sdec.shape),
    ]
    out_shape = (
        jax.ShapeDtypeStruct((b, s, d), F32),
        jax.ShapeDtypeStruct((b, POOL_PREV, dims.d_pool), F32),
        jax.ShapeDtypeStruct((1, b, dims.heads, dims.dk, dims.dv), F32),
    )
    out_specs = (
        pl.BlockSpec((1, L, d), lambda i, c: (i, c, 0)),
        pl.BlockSpec((1, POOL_PREV, dims.d_pool), lambda i, c: (i, 0, 0)),
        pl.BlockSpec((1, 1, dims.heads, dims.dk, dims.dv), lambda i, c: (0, i, 0, 0, 0)),
    )
    weight_bytes = 2 * (win.size + wpg.size + wbp.size + wbr.size + wout.size)
    return pl.pallas_call(
        functools.partial(_mixer_prompt_kernel, dims),
        out_shape=out_shape,
        grid=(b, nc),
        in_specs=in_specs,
        out_specs=out_specs,
        scratch_shapes=[pltpu.VMEM((POOL_PREV + L, dims.d_pool), F32)],
        compiler_params=pltpu.CompilerParams(
            dimension_semantics=("arbitrary", "arbitrary"),
            vmem_limit_bytes=_vmem_limit(2 * weight_bytes + 16 * 1024 * 1024),
        ),
        name="mixer_prompt",
    )(x, mod, g1, win, wpg, ps, wbp, wbr, wout, cosb, sinb, intra, qdec, kdec, sdec)


def _mixer_sample_kernel(dims, ds, x_ref, modt_ref, g1_ref, win_ref, wpg_ref, ps_ref, wbp_ref,
                         wbr_ref, wout_ref, cos_ref, sin_ref, intra_ref, qdec_ref, kdec_ref,
                         sdec_ref, bandp_ref, bandu_ref, cnt_ref, poolin_ref, sret_ref,
                         x1_ref, u_ref, ret_ref):
    R = x_ref.shape[0]
    ns = R // ds
    x = x_ref[...]
    shift, scale, gate = modt_ref[0], modt_ref[1], modt_ref[2]
    hb = _rms_mod(x, g1_ref[...], shift, scale).astype(BF16)

    u = _proj(hb, win_ref, dims.cols[0])
    u_ref[...] = u
    hist = poolin_ref[...].reshape(ns * POOL_PREV, dims.d_pool)
    pm_parts = []
    for gi in range(len(POOL_WINDOWS)):
        lo, hi = gi * dims.gd, (gi + 1) * dims.gd
        win_sum = jnp.zeros((R, dims.gd), F32)
        for part in _split_bf16(hist[:, lo:hi], 3):
            win_sum = win_sum + _dot(bandp_ref[gi], part)
        for part in _split_bf16(u[:, lo:hi], 3):
            win_sum = win_sum + _dot(bandu_ref[gi], part)
        pm_parts.append(win_sum / cnt_ref[gi] - u[:, lo:hi])
    a = _pool_branch(pm_parts, wpg_ref, ps_ref, wbp_ref)

    q = _proj(hb, win_ref, dims.cols[1])
    k = _proj(hb, win_ref, dims.cols[2])
    v = _proj(hb, win_ref, dims.cols[3])
    cosb, sinb = cos_ref[...], sin_ref[...]
    row_seq = lax.broadcasted_iota(jnp.int32, (R, ns * dims.dk), 0) // ds
    col_seq = lax.broadcasted_iota(jnp.int32, (R, ns * dims.dk), 1) // dims.dk
    own = row_seq == col_seq
    outs = []
    for h in range(dims.heads):
        qr = _rotary(q[:, h * dims.dk:(h + 1) * dims.dk], cosb, sinb)
        kr = _rotary(k[:, h * dims.dk:(h + 1) * dims.dk], cosb, sinb) * (dims.dk ** -0.5)
        vb = v[:, h * dims.dv:(h + 1) * dims.dv].astype(BF16)
        scores = _dot_nt(qr.astype(BF16), kr.astype(BF16)) * intra_ref[h]
        s_prev = sret_ref[:, h].reshape(ns * dims.dk, dims.dv)
        qd = jnp.concatenate([qr * qdec_ref[h]] * ns, axis=1)
        kd = jnp.concatenate([kr * kdec_ref[h]] * ns, axis=1)
        qexp = jnp.where(own, qd, 0.0).astype(BF16)
        kexp = jnp.where(own, kd, 0.0).astype(BF16)
        o = _dot(scores.astype(BF16), vb) + _dot(qexp, s_prev.astype(BF16))
        s_new = sdec_ref[h] * s_prev + _dot_tn(kexp, vb)
        ret_ref[:, h] = s_new.reshape(ns, dims.dk, dims.dv)
        outs.append(_group_norm(o))
    o = jnp.concatenate(outs, axis=1)

    g = _proj(hb, win_ref, dims.cols[4])
    ga = _proj(hb, win_ref, dims.cols[5])
    gb = _proj(hb, win_ref, dims.cols[6])
    x1_ref[...] = _merge_out(x, gate, a, o, g, ga, gb, wbr_ref, wout_ref)


def _mixer_sample(dims, ds, x2d, modt, g1, weights, tables, pool_in, state_ret):
    t, d = x2d.shape
    ns = SAMPLE_SEQS_PER_STEP
    R = ns * ds
    win, wpg, ps, wbp, wbr, wout = weights
    cosb, sinb, intra, qdec, kdec, sdec, bandp, bandu, cnt = tables
    db = state_ret.shape[0]
    in_specs = [
        pl.BlockSpec((R, d), lambda i: (i, 0)),
        pl.BlockSpec((N_MOD, R, d), lambda i: (0, i, 0)),
        _const_spec(g1.shape), _const_spec(win.shape), _const_spec(wpg.shape),
        _const_spec(ps.shape), _const_spec(wbp.shape), _const_spec(wbr.shape),
        _const_spec(wout.shape), _const_spec(cosb.shape), _const_spec(sinb.shape),
        _const_spec(intra.shape), _const_spec(qdec.shape), _const_spec(kdec.shape),
        _const_spec(sdec.shape), _const_spec(bandp.shape), _const_spec(bandu.shape),
        _const_spec(cnt.shape),
        pl.BlockSpec((ns, POOL_PREV, dims.d_pool), lambda i: (i, 0, 0)),
        pl.BlockSpec((ns, dims.heads, dims.dk, dims.dv), lambda i: (i, 0, 0, 0)),
    ]
    out_shape = (
        jax.ShapeDtypeStruct((t, d), F32),
        jax.ShapeDtypeStruct((t, dims.d_pool), F32),
        jax.ShapeDtypeStruct(state_ret.shape, state_ret.dtype),
    )
    out_specs = (
        pl.BlockSpec((R, d), lambda i: (i, 0)),
        pl.BlockSpec((R, dims.d_pool), lambda i: (i, 0)),
        pl.BlockSpec((ns, dims.heads, dims.dk, dims.dv), lambda i: (i, 0, 0, 0)),
    )
    weight_bytes = 2 * (win.size + wpg.size + wbp.size + wbr.size + wout.size)
    state_bytes = 4 * ns * dims.heads * dims.dk * dims.dv
    return pl.pallas_call(
        functools.partial(_mixer_sample_kernel, dims, ds),
        out_shape=out_shape,
        grid=(db // ns,),
        in_specs=in_specs,
        out_specs=out_specs,
        compiler_params=pltpu.CompilerParams(
            dimension_semantics=("arbitrary",),
            vmem_limit_bytes=_vmem_limit(2 * weight_bytes + 4 * state_bytes + 8 * 1024 * 1024),
        ),
        name="mixer_sample",
    )(x2d, modt, g1, win, wpg, ps, wbp, wbr, wout, cosb, sinb, intra, qdec, kdec, sdec,
      bandp, bandu, cnt, pool_in, state_ret)


def _rotary_tables(pos, dk):
    half = dk // 2
    inv = jnp.power(ROPE_BASE, -jnp.arange(half, dtype=F32) / half)
    ang = pos.astype(F32)[:, None] * inv[None, :]
    cos, sin = jnp.cos(ang), jnp.sin(ang)
    return jnp.concatenate([cos, cos], axis=1), jnp.concatenate([-sin, sin], axis=1)


def _decay_tables(log_g, chunk, reps, dk, dv):
    idx = jnp.arange(chunk, dtype=F32)
    diff = idx[:, None] - idx[None, :]
    intra = jnp.where((diff >= 0)[None],
                      jnp.exp(log_g[:, None, None] * jnp.maximum(diff, 0.0)[None]), 0.0)
    heads = log_g.shape[0]
    eye = jnp.eye(reps, dtype=F32)
    intra = jnp.einsum("ab,hij->haibj", eye, intra).reshape(heads, reps * chunk, reps * chunk)
    q_dec = jnp.exp(log_g[:, None] * (idx[None, :] + 1.0))
    k_dec = jnp.exp(log_g[:, None] * (chunk - 1.0 - idx[None, :]))
    s_dec = jnp.exp(log_g * chunk)
    q_dec = jnp.broadcast_to(jnp.tile(q_dec, (1, reps))[:, :, None], (heads, reps * chunk, dk))
    k_dec = jnp.broadcast_to(jnp.tile(k_dec, (1, reps))[:, :, None], (heads, reps * chunk, dk))
    s_dec = jnp.broadcast_to(s_dec[:, None, None], (heads, 1, dv))
    return intra, q_dec, k_dec, s_dec


def _sample_pool_tables(ns, ds, gd, start_pos):
    n = jnp.arange(ds)
    i = jnp.arange(POOL_PREV)
    eye = jnp.eye(ns, dtype=F32)
    bandp, bandu, cnt = [], [], []
    for w in POOL_WINDOWS:
        bp = ((i[None, :] - POOL_PREV) >= (n[:, None] - w + 1)).astype(F32)
        bu = ((n[None, :] <= n[:, None]) & (n[None, :] >= n[:, None] - w + 1)).astype(F32)
        bandp.append(jnp.einsum("ab,ni->anbi", eye, bp).reshape(ns * ds, ns * POOL_PREV))
        bandu.append(jnp.einsum("ab,nm->anbm", eye, bu).reshape(ns * ds, ns * ds))
        c = jnp.minimum(w, start_pos + n + 1).astype(F32)
        cnt.append(jnp.broadcast_to(jnp.tile(c, ns)[:, None], (ns * ds, gd)))
    return (jnp.stack(bandp).astype(BF16), jnp.stack(bandu).astype(BF16), jnp.stack(cnt))


def _first_max_onehot(v, idx, big, axes):
    m = jnp.max(v, axis=axes, keepdims=True)
    first = jnp.min(jnp.where(v == m, idx, big), axis=axes, keepdims=True)
    return idx == first


def _route_kernel(n_prompt_blocks, xp_ref, xs_ref, modp_ref, modt_ref, g2_ref, wr_ref, br_ref,
                  before_ref, below_ref, h2_ref, pos_ref, wts_ref, cnt_ref, rank_ref, gates_ref):
    i = pl.program_id(0)
    is_p = i < n_prompt_blocks
    x = jnp.where(is_p, xp_ref[...], xs_ref[...])
    shift = jnp.where(is_p, modp_ref[0, 3:4, :], modt_ref[3])
    scale = jnp.where(is_p, modp_ref[0, 4:5, :], modt_ref[4])
    h2 = _rms_mod(x, g2_ref[...], shift, scale)
    h2_ref[...] = h2.astype(BF16)

    hh, hl = _split_bf16(h2, 2)
    wh, wl = _split_bf16(wr_ref[...], 2)
    logits = _dot_nt(wh, hh) + (_dot_nt(wh, hl) + _dot_nt(wl, hh))
    s = jax.nn.sigmoid(logits)
    e, t = s.shape
    ge = e // N_GROUPS
    neg = -jnp.inf
    s3 = s.reshape(N_GROUPS, ge, t)
    sb3 = s3 + br_ref[...]

    j3 = lax.broadcasted_iota(jnp.int32, (N_GROUPS, ge, t), 1)
    top1 = _first_max_onehot(sb3, j3, ge, 1)
    m1 = jnp.max(sb3, axis=1, keepdims=True)
    m2 = jnp.max(jnp.where(top1, neg, sb3), axis=1, keepdims=True)
    gv = m1 + m2

    g3 = lax.broadcasted_iota(jnp.int32, (N_GROUPS, 1, t), 0)
    gsel = jnp.zeros((N_GROUPS, 1, t), F32)
    for _ in range(TOPK_GROUPS):
        hit = _first_max_onehot(gv, g3, N_GROUPS, 0)
        gsel = jnp.where(hit, 1.0, gsel)
        gv = jnp.where(hit, neg, gv)

    e3 = lax.broadcasted_iota(jnp.int32, (N_GROUPS, ge, t), 0) * ge + j3
    ev = jnp.where(gsel > 0.5, sb3, neg)
    sel = jnp.zeros((N_GROUPS, ge, t), F32)
    chosen = jnp.zeros((N_GROUPS, ge, t), F32)
    hits = []
    for _ in range(TOP_K):
        hit = _first_max_onehot(ev, e3, e, (0, 1))
        hits.append(hit)
        sel = jnp.where(hit, s3, sel)
        chosen = jnp.where(hit, 1.0, chosen)
        ev = jnp.where(hit, neg, ev)
    tot = jnp.sum(sel, axis=(0, 1), keepdims=True)
    gates = sel / tot * ROUTED_SCALE

    chosen2 = chosen.reshape(e, t)
    rank = _dot(chosen2.astype(BF16), before_ref[...])
    cnt = jnp.sum(chosen2, axis=1, keepdims=True)
    ovf_units = jnp.floor((jnp.maximum(cnt - SLOT_ROWS, 0.0) + (DMA_ROWS - 1)) / DMA_ROWS)
    ovf_off = _dot(below_ref[...], jnp.broadcast_to(ovf_units, (e, LANES)).astype(BF16))[:, 0:1]
    e_col = lax.broadcasted_iota(jnp.int32, (e, 1), 0).astype(F32)
    slot = jnp.where(rank < SLOT_ROWS, e_col * SLOT_ROWS + rank,
                     e * SLOT_ROWS + ovf_off * DMA_ROWS + (rank - SLOT_ROWS))
    slot3 = slot.reshape(N_GROUPS, ge, t)
    for k, hit in enumerate(hits):
        pos_k = jnp.sum(jnp.where(hit, slot3, 0.0), axis=(0, 1), keepdims=True)
        w_k = jnp.sum(jnp.where(hit, gates, 0.0), axis=(0, 1), keepdims=True)
        pos_ref[k:k + 1, :] = pos_k.reshape(1, t).astype(jnp.int32)
        wts_ref[k:k + 1, :] = w_k.reshape(1, t)
    cnt_ref[0] = jnp.broadcast_to(cnt, (e, LANES)).astype(jnp.int32)
    rank_ref[...] = jnp.where(chosen2 > 0.5, rank, -1.0).astype(jnp.int32)
    gates_ref[...] = gates.reshape(e, t)


def _route(xp2d, xs2d, modp, modt, g2, wr_t, br):
    tp, d = xp2d.shape
    ts = xs2d.shape[0]
    tb = TOKEN_BLOCK
    npb, nsb = tp // tb, ts // tb
    e = wr_t.shape[0]
    seq = tp // modp.shape[0]
    bpb = seq // tb

    def p_idx(i):
        return jnp.minimum(i, npb - 1)

    def s_idx(i):
        return jnp.maximum(i - npb, 0)

    t = tp + ts
    nb = npb + nsb
    tok = jnp.arange(tb)
    before = (tok[:, None] < tok[None, :]).astype(BF16)
    ex = jnp.arange(e)
    below = (ex[None, :] < ex[:, None]).astype(BF16)
    return pl.pallas_call(
        functools.partial(_route_kernel, npb),
        out_shape=(jax.ShapeDtypeStruct((t, d), BF16),
                   jax.ShapeDtypeStruct((TOP_K, t), jnp.int32),
                   jax.ShapeDtypeStruct((TOP_K, t), F32),
                   jax.ShapeDtypeStruct((nb, e, LANES), jnp.int32),
                   jax.ShapeDtypeStruct((e, t), jnp.int32),
                   jax.ShapeDtypeStruct((e, t), F32)),
        grid=(nb,),
        in_specs=[
            pl.BlockSpec((tb, d), lambda i: (p_idx(i), 0)),
            pl.BlockSpec((tb, d), lambda i: (s_idx(i), 0)),
            pl.BlockSpec((1, N_MOD, d), lambda i: (p_idx(i) // bpb, 0, 0)),
            pl.BlockSpec((N_MOD, tb, d), lambda i: (0, s_idx(i), 0)),
            _const_spec(g2.shape), _const_spec(wr_t.shape), _const_spec(br.shape),
            _const_spec(before.shape), _const_spec(below.shape),
        ],
        out_specs=(pl.BlockSpec((tb, d), lambda i: (i, 0)),
                   pl.BlockSpec((TOP_K, tb), lambda i: (0, i)),
                   pl.BlockSpec((TOP_K, tb), lambda i: (0, i)),
                   pl.BlockSpec((1, e, LANES), lambda i: (i, 0, 0)),
                   pl.BlockSpec((e, tb), lambda i: (0, i)),
                   pl.BlockSpec((e, tb), lambda i: (0, i))),
        compiler_params=pltpu.CompilerParams(
            dimension_semantics=("arbitrary",),
            vmem_limit_bytes=_vmem_limit(32 * 1024 * 1024),
        ),
        name="route",
    )(xp2d, xs2d, modp, modt, g2, wr_t, br, before, below)


class _Layout:
    def __init__(self, counts, n_tiles_cap):
        nb, ne = counts.shape
        u, g, tm = SLOT_ROWS, DMA_ROWS, EXPERT_TILE
        pc = (counts + g - 1) // g * g
        ovf_units = (jnp.maximum(counts - u, 0) + g - 1) // g
        rows_e = jnp.sum(pc, axis=0)
        region = (rows_e + u + tm - 1) // tm * tm
        goff = jnp.cumsum(region) - region
        dst = goff[None, :] + jnp.cumsum(pc, axis=0) - pc
        self.dst = dst.reshape(-1)
        ovf_end = jnp.cumsum(ovf_units, axis=1)
        unit = jnp.arange(_overflow_units_cap(ne))
        owner = jnp.sum((ovf_end[:, None, :] <= unit[None, :, None]).astype(jnp.int32), axis=2)
        owner = jnp.minimum(owner, ne - 1)
        first = jnp.take_along_axis(ovf_end - ovf_units, owner, axis=1)
        base = jnp.take_along_axis(dst, owner, axis=1)
        self.ovf_dst = (base + u + (unit[None, :] - first) * g).reshape(-1)
        self.ovf_tot = ovf_end[:, -1]
        self.tail_start = goff + rows_e
        self.tail_units = (region - rows_e) // g
        tiles_e = region // tm
        tile_end = jnp.cumsum(tiles_e)
        self.n_used = tile_end[-1:]
        tile = jnp.minimum(jnp.arange(n_tiles_cap), tile_end[-1] - 1)
        self.tile_expert = jnp.sum((tile_end[None, :] <= tile[:, None]).astype(jnp.int32), axis=1)
        self.tile_first = (tile == (tile_end - tiles_e)[self.tile_expert]).astype(jnp.int32)


def _overflow_units_cap(ne):
    return TOKEN_BLOCK * TOP_K // DMA_ROWS + ne


def _block_buffer_rows(tb, ne):
    rows = ne * SLOT_ROWS + tb * TOP_K + ne * (DMA_ROWS - 1)
    return (rows + ONEHOT_ROWS - 1) // ONEHOT_ROWS * ONEHOT_ROWS


def _expert_rows_cap(t, tb, ne):
    nb = t // tb
    rows = t * TOP_K + nb * ne * (DMA_ROWS - 1) + ne * (SLOT_ROWS + EXPERT_TILE - 1)
    return (rows + EXPERT_TILE - 1) // EXPERT_TILE * EXPERT_TILE


def _slot_copy(buf_ref, slot, e, hbm_ref, row, sem, to_hbm):
    src = buf_ref.at[slot, pl.ds(pl.multiple_of(e * SLOT_ROWS, DMA_ROWS), SLOT_ROWS)]
    dst = hbm_ref.at[pl.ds(pl.multiple_of(row, DMA_ROWS), SLOT_ROWS)]
    return pltpu.make_async_copy(src, dst, sem) if to_hbm else pltpu.make_async_copy(dst, src, sem)


def _unit_copy(buf_ref, slot, buf_row, hbm_ref, row, sem, to_hbm):
    src = buf_ref.at[slot, pl.ds(pl.multiple_of(buf_row, DMA_ROWS), DMA_ROWS)]
    dst = hbm_ref.at[pl.ds(pl.multiple_of(row, DMA_ROWS), DMA_ROWS)]
    return pltpu.make_async_copy(src, dst, sem) if to_hbm else pltpu.make_async_copy(dst, src, sem)


def _slots_wait(ne, buf_ref, slot, hbm_ref, sem, to_hbm):
    src = buf_ref.at[slot, pl.ds(0, ne * SLOT_ROWS)]
    dst = hbm_ref.at[pl.ds(0, ne * SLOT_ROWS)]
    c = pltpu.make_async_copy(src, dst, sem) if to_hbm else pltpu.make_async_copy(dst, src, sem)
    c.wait()


def _overflow_copies(ne, blk, slot, odst_ref, otot_ref, buf_ref, hbm_ref, sem, to_hbm, wait):
    def per_unit(i, carry):
        row = odst_ref[blk * _overflow_units_cap(ne) + i]
        c = _unit_copy(buf_ref, slot, ne * SLOT_ROWS + i * DMA_ROWS, hbm_ref, row, sem, to_hbm)
        c.wait() if wait else c.start()
        return carry

    lax.fori_loop(0, otot_ref[blk], per_unit, 0)


def _slot_chunks(ne):
    return [range(c, c + SLOT_CHUNK_EXPERTS) for c in range(0, ne, SLOT_CHUNK_EXPERTS)]


def _overflow_chunks(n_units):
    return lax.shift_right_logical(n_units * DMA_ROWS + (ONEHOT_ROWS - 1),
                                   ONEHOT_ROWS.bit_length() - 1)


def _overflow_select(pos, vals, r0):
    rows = lax.broadcasted_iota(jnp.int32, (ONEHOT_ROWS, pos.shape[1]), 0) + r0
    sel = jnp.zeros(rows.shape, F32)
    for k in range(TOP_K):
        v = 1.0 if vals is None else vals[k:k + 1, :]
        sel = jnp.where(pos[k:k + 1, :] == rows, v, sel)
    return sel


def _dispatch_kernel(ne, dst_ref, odst_ref, otot_ref, tails_ref, tailu_ref,
                     h2_ref, rank_ref, pos_ref, xs_hbm, buf_ref, zero_ref, sem, zsem):
    b = pl.program_id(0)
    nb = pl.num_programs(0)
    slot = b % 2
    h2 = h2_ref[...]
    j = lax.broadcasted_iota(jnp.int32, (SLOT_ROWS, h2.shape[0]), 0)

    for ci, experts in enumerate(_slot_chunks(ne)):
        onehot = jnp.concatenate(
            [jnp.where(rank_ref[e:e + 1, :] == j, 1.0, 0.0) for e in experts], axis=0)
        r0 = experts[0] * SLOT_ROWS
        buf_ref[slot, r0:r0 + len(experts) * SLOT_ROWS, :] = (
            _dot(onehot.astype(BF16), h2).astype(BF16))
        if ci == 0:
            @pl.when(b > 0)
            def _prev():
                _slots_wait(ne, buf_ref, 1 - slot, xs_hbm, sem.at[1 - slot], True)
                _overflow_copies(ne, b - 1, 1 - slot, odst_ref, otot_ref, buf_ref, xs_hbm,
                                 sem.at[1 - slot], True, True)
        for e in experts:
            _slot_copy(buf_ref, slot, e, xs_hbm, dst_ref[b * ne + e], sem.at[slot], True).start()

    pos = pos_ref[...]

    def chunk(ci, carry):
        r0 = pl.multiple_of(ne * SLOT_ROWS + ci * ONEHOT_ROWS, ONEHOT_ROWS)
        onehot = _overflow_select(pos, None, r0).astype(BF16)
        buf_ref[slot, pl.ds(r0, ONEHOT_ROWS), :] = _dot(onehot, h2).astype(BF16)
        return carry

    lax.fori_loop(0, _overflow_chunks(otot_ref[b]), chunk, 0)
    _overflow_copies(ne, b, slot, odst_ref, otot_ref, buf_ref, xs_hbm, sem.at[slot], True, False)

    @pl.when(b == nb - 1)
    def _last():
        _slots_wait(ne, buf_ref, slot, xs_hbm, sem.at[slot], True)
        _overflow_copies(ne, b, slot, odst_ref, otot_ref, buf_ref, xs_hbm, sem.at[slot], True,
                         True)
        zero_ref[...] = jnp.zeros_like(zero_ref)

        def tail(wait):
            def per_expert(e, carry):
                def per_unit(i, carry2):
                    row = pl.multiple_of(tails_ref[e] + i * DMA_ROWS, DMA_ROWS)
                    c = pltpu.make_async_copy(zero_ref, xs_hbm.at[pl.ds(row, DMA_ROWS)], zsem)
                    c.wait() if wait else c.start()
                    return carry2
                lax.fori_loop(0, tailu_ref[e], per_unit, 0)
                return carry
            lax.fori_loop(0, ne, per_expert, 0)

        tail(False)
        tail(True)


def _dispatch(h2, rank, pos, lay, ne, rows_cap):
    t, d = h2.shape
    tb = TOKEN_BLOCK
    buf_rows = _block_buffer_rows(tb, ne)
    return pl.pallas_call(
        functools.partial(_dispatch_kernel, ne),
        out_shape=jax.ShapeDtypeStruct((rows_cap, d), BF16),
        grid_spec=pltpu.PrefetchScalarGridSpec(
            num_scalar_prefetch=5,
            grid=(t // tb,),
            in_specs=[
                pl.BlockSpec((tb, d), lambda i, *_: (i, 0)),
                pl.BlockSpec((ne, tb), lambda i, *_: (0, i)),
                pl.BlockSpec((TOP_K, tb), lambda i, *_: (0, i)),
            ],
            out_specs=pl.BlockSpec(memory_space=pl.ANY),
            scratch_shapes=[
                pltpu.VMEM((2, buf_rows, d), BF16),
                pltpu.VMEM((DMA_ROWS, d), BF16),
                pltpu.SemaphoreType.DMA((2,)),
                pltpu.SemaphoreType.DMA(()),
            ],
        ),
        compiler_params=pltpu.CompilerParams(
            dimension_semantics=("arbitrary",),
            vmem_limit_bytes=_vmem_limit(4 * buf_rows * d + 16 * 1024 * 1024),
        ),
        name="dispatch",
    )(lay.dst, lay.ovf_dst, lay.ovf_tot, lay.tail_start, lay.tail_units, h2, rank, pos)


def _experts_kernel(te_ref, first_ref, nused_ref, x_ref, wg_ref, wu_ref, wd_ref, y_ref,
                    wgb_ref, wub_ref, wdb_ref):
    i = pl.program_id(0)

    @pl.when(i < nused_ref[0])
    def _tile():
        @pl.when(first_ref[i] == 1)
        def _cast():
            wgb_ref[...] = wg_ref[0].astype(BF16)
            wub_ref[...] = wu_ref[0].astype(BF16)
            wdb_ref[...] = wd_ref[0].astype(BF16)

        x = x_ref[...]
        hid = jax.nn.silu(_dot(x, wgb_ref[...])) * _dot(x, wub_ref[...])
        y_ref[...] = _dot(hid.astype(BF16), wdb_ref[...]).astype(BF16)


def _experts(xs, lay, wg, wu, wd):
    rows_cap, d = xs.shape
    ne, _, de = wg.shape
    tm = EXPERT_TILE

    def tile_idx(i, te, first, nused):
        return jnp.minimum(i, nused[0] - 1)

    return pl.pallas_call(
        _experts_kernel,
        out_shape=jax.ShapeDtypeStruct((rows_cap, d), BF16),
        grid_spec=pltpu.PrefetchScalarGridSpec(
            num_scalar_prefetch=3,
            grid=(rows_cap // tm,),
            in_specs=[
                pl.BlockSpec((tm, d), lambda i, *s: (tile_idx(i, *s), 0)),
                pl.BlockSpec((1, d, de), lambda i, te, *s: (te[i], 0, 0)),
                pl.BlockSpec((1, d, de), lambda i, te, *s: (te[i], 0, 0)),
                pl.BlockSpec((1, de, d), lambda i, te, *s: (te[i], 0, 0)),
            ],
            out_specs=pl.BlockSpec((tm, d), lambda i, *s: (tile_idx(i, *s), 0)),
            scratch_shapes=[pltpu.VMEM((d, de), BF16), pltpu.VMEM((d, de), BF16),
                            pltpu.VMEM((de, d), BF16)],
        ),
        compiler_params=pltpu.CompilerParams(
            dimension_semantics=("arbitrary",),
            vmem_limit_bytes=_vmem_limit(32 * 1024 * 1024),
        ),
        name="experts",
    )(lay.tile_expert, lay.tile_first, lay.n_used, xs, wg, wu, wd)


def _final_kernel(n_prompt_blocks, ne, dst_ref, odst_ref, otot_ref,
                  xp_ref, xs_ref, modp_ref, modt_ref, h2_ref, rank_ref, gates_ref, pos_ref, wts_ref,
                  wsg_ref, wsu_ref, wsd_ref, gf_ref, ye_hbm, yp_ref, ys_ref,
                  buf_ref, acc_ref, sem):
    i = pl.program_id(0)
    nb = pl.num_programs(0)
    slot = i % 2

    def fetch(blk, sl):
        for e in range(ne):
            _slot_copy(buf_ref, sl, e, ye_hbm, dst_ref[blk * ne + e], sem.at[sl], False).start()
        _overflow_copies(ne, blk, sl, odst_ref, otot_ref, buf_ref, ye_hbm, sem.at[sl], False, False)

    @pl.when(i == 0)
    def _first():
        buf_ref[...] = jnp.zeros_like(buf_ref)
        fetch(0, 0)

    @pl.when(i + 1 < nb)
    def _prefetch():
        fetch(i + 1, 1 - slot)

    _slots_wait(ne, buf_ref, slot, ye_hbm, sem.at[slot], False)
    _overflow_copies(ne, i, slot, odst_ref, otot_ref, buf_ref, ye_hbm, sem.at[slot], False, True)

    tb = h2_ref.shape[0]
    j = lax.broadcasted_iota(jnp.int32, (SLOT_ROWS, tb), 0)
    acc = jnp.zeros((tb, h2_ref.shape[1]), F32)
    for experts in _slot_chunks(ne):
        sel = jnp.concatenate(
            [jnp.where(rank_ref[e:e + 1, :] == j, gates_ref[e:e + 1, :], 0.0) for e in experts],
            axis=0)
        r0 = experts[0] * SLOT_ROWS
        acc = acc + _dot_tn(sel.astype(BF16), buf_ref[slot, r0:r0 + len(experts) * SLOT_ROWS, :])
    acc_ref[...] = acc

    pos = pos_ref[...]
    wts = wts_ref[...]

    def chunk(ci, carry):
        r0 = pl.multiple_of(ne * SLOT_ROWS + ci * ONEHOT_ROWS, ONEHOT_ROWS)
        sel = _overflow_select(pos, wts, r0).astype(BF16)
        acc_ref[...] += _dot_tn(sel, buf_ref[slot, pl.ds(r0, ONEHOT_ROWS), :])
        return carry

    lax.fori_loop(0, _overflow_chunks(otot_ref[i]), chunk, 0)

    is_p = i < n_prompt_blocks
    x = jnp.where(is_p, xp_ref[...], xs_ref[...])
    gate = jnp.where(is_p, modp_ref[0, 5:6, :], modt_ref[5])
    hb = h2_ref[...]
    hid = jax.nn.silu(_dot(hb, wsg_ref[...])) * _dot(hb, wsu_ref[...])
    f = acc_ref[...] + _dot(hid.astype(BF16), wsd_ref[...])
    x2 = x + gate * f
    y = x2 * lax.rsqrt(jnp.mean(x2 * x2, axis=-1, keepdims=True) + EPS) * gf_ref[...]

    @pl.when(is_p)
    def _p():
        yp_ref[...] = y

    @pl.when(jnp.logical_not(is_p))
    def _s():
        ys_ref[...] = y


def _final(xp2d, xs2d, modp, modt, h2, rank, gates, pos, wts, ye, lay, ne, wsg, wsu, wsd, gf):
    tp, d = xp2d.shape
    ts = xs2d.shape[0]
    tb = TOKEN_BLOCK
    npb, nsb = tp // tb, ts // tb
    seq = tp // modp.shape[0]
    bpb = seq // tb
    buf_rows = _block_buffer_rows(tb, ne)

    def p_idx(i):
        return jnp.minimum(i, npb - 1)

    def s_idx(i):
        return jnp.maximum(i - npb, 0)

    def const(shape):
        zeros = (0,) * len(shape)
        return pl.BlockSpec(shape, lambda i, *_: zeros)

    return pl.pallas_call(
        functools.partial(_final_kernel, npb, ne),
        out_shape=(jax.ShapeDtypeStruct((tp, d), F32), jax.ShapeDtypeStruct((ts, d), F32)),
        grid_spec=pltpu.PrefetchScalarGridSpec(
            num_scalar_prefetch=3,
            grid=(npb + nsb,),
            in_specs=[
                pl.BlockSpec((tb, d), lambda i, *_: (p_idx(i), 0)),
                pl.BlockSpec((tb, d), lambda i, *_: (s_idx(i), 0)),
                pl.BlockSpec((1, N_MOD, d), lambda i, *_: (p_idx(i) // bpb, 0, 0)),
                pl.BlockSpec((N_MOD, tb, d), lambda i, *_: (0, s_idx(i), 0)),
                pl.BlockSpec((tb, d), lambda i, *_: (i, 0)),
                pl.BlockSpec((ne, tb), lambda i, *_: (0, i)),
                pl.BlockSpec((ne, tb), lambda i, *_: (0, i)),
                pl.BlockSpec((TOP_K, tb), lambda i, *_: (0, i)),
                pl.BlockSpec((TOP_K, tb), lambda i, *_: (0, i)),
                const(wsg.shape), const(wsu.shape), const(wsd.shape), const(gf.shape),
                pl.BlockSpec(memory_space=pl.ANY),
            ],
            out_specs=(pl.BlockSpec((tb, d), lambda i, *_: (p_idx(i), 0)),
                       pl.BlockSpec((tb, d), lambda i, *_: (s_idx(i), 0))),
            scratch_shapes=[
                pltpu.VMEM((2, buf_rows, d), BF16),
                pltpu.VMEM((tb, d), F32),
                pltpu.SemaphoreType.DMA((2,)),
            ],
        ),
        compiler_params=pltpu.CompilerParams(
            dimension_semantics=("arbitrary",),
            vmem_limit_bytes=_vmem_limit(4 * buf_rows * d + 20 * 1024 * 1024),
        ),
        name="final",
    )(lay.dst, lay.ovf_dst, lay.ovf_tot,
      xp2d, xs2d, modp, modt, h2, rank, gates, pos, wts, wsg, wsu, wsd, gf, ye)


def kernel(x_prompt, x_sample, c_prompt, c_sample, state_pool, state_ret, w_ada, b_ada, g_norm1,
           w_in, w_pool_group, pool_scale, w_branch_pool, w_branch_ret, w_out, g_norm2, w_router,
           b_router, w_exp_gate, w_exp_up, w_exp_down, w_sh_gate, w_sh_up, w_sh_down, g_final):
    b, s, d = x_prompt.shape
    db, ds, _ = x_sample.shape
    depth = w_ada.shape[0]
    heads, dk, dv = state_ret.shape[2:]
    pool_buf, d_pool = state_pool.shape[2:]
    dims = _Dims(d, d_pool, heads, dk, dv)
    assert dims.d_in == w_in.shape[2]
    assert pool_buf == max(POOL_WINDOWS) - 1 and pool_buf < POOL_PREV
    assert s % RET_CHUNK == 0 and s % TOKEN_BLOCK == 0 and (db * ds) % TOKEN_BLOCK == 0
    assert db % SAMPLE_SEQS_PER_STEP == 0 and SLOT_ROWS % DMA_ROWS == 0

    log_g = jnp.log(1.0 - jnp.power(2.0, -5.0 - jnp.arange(heads, dtype=F32)))
    ns = SAMPLE_SEQS_PER_STEP
    cos_p, sin_p = _rotary_tables(jnp.arange(s, dtype=jnp.int32), dk)
    cos_s, sin_s = _rotary_tables(PAST_LEN + jnp.arange(ds, dtype=jnp.int32), dk)
    cos_s, sin_s = jnp.tile(cos_s, (ns, 1)), jnp.tile(sin_s, (ns, 1))
    dec_p = _decay_tables(log_g, RET_CHUNK, 1, dk, dv)
    dec_s = _decay_tables(log_g, ds, ns, dk, dv)
    pool_tabs = _sample_pool_tables(ns, ds, dims.gd, PAST_LEN)

    assert depth == 1, "the final kernel fuses the output norm into the (single) layer"
    l = 0
    xs = x_sample.reshape(db * ds, d)
    mod = _ada(jnp.concatenate([c_prompt, c_sample], axis=0), w_ada[l], b_ada[l])
    modp = mod[:b].reshape(b, N_MOD, d)
    mods = mod[b:].reshape(db, N_MOD, d)
    modt = jnp.repeat(mods.transpose(1, 0, 2), ds, axis=1)
    g1 = g_norm1[l].reshape(1, d)
    weights = (w_in[l].astype(BF16), w_pool_group[l].astype(BF16),
               pool_scale[l].reshape(1, d_pool), w_branch_pool[l].astype(BF16),
               w_branch_ret[l].astype(BF16), w_out[l].astype(BF16))

    x1p, pool16, ret_p = _mixer_prompt(dims, x_prompt, modp, g1, weights,
                                       (cos_p, sin_p) + dec_p)
    pool_in = jnp.pad(state_pool[l], ((0, 0), (POOL_PREV - pool_buf, 0), (0, 0)))
    x1s, u_s, ret_s = _mixer_sample(dims, ds, xs, modt, g1, weights,
                                    (cos_s, sin_s) + dec_s + pool_tabs, pool_in,
                                    state_ret[l].astype(F32))
    pool_p = pool16[:, POOL_PREV - pool_buf:]
    pool_s = jnp.concatenate([state_pool[l], u_s.reshape(db, ds, d_pool)], axis=1)[:, -pool_buf:]

    x1p2d = x1p.reshape(b * s, d)
    ne = w_router.shape[2]
    h2, pos, wts, counts, rank, gates = _route(
        x1p2d, x1s, modp, modt, g_norm2[l].reshape(1, d), w_router[l].T,
        b_router[l].reshape(N_GROUPS, ne // N_GROUPS, 1))
    rows_cap = _expert_rows_cap(b * s + db * ds, TOKEN_BLOCK, ne)
    lay = _Layout(counts[:, :, 0], rows_cap // EXPERT_TILE)
    xe = _dispatch(h2, rank, pos, lay, ne, rows_cap)
    ye = _experts(xe, lay, w_exp_gate[l], w_exp_up[l], w_exp_down[l])
    yp, ys = _final(x1p2d, x1s, modp, modt, h2, rank, gates, pos, wts, ye, lay, ne,
                    w_sh_gate[l].astype(BF16), w_sh_up[l].astype(BF16),
                    w_sh_down[l].astype(BF16), g_final.reshape(1, d))

    return (yp.reshape(b, s, d), ys.reshape(db, ds, d), pool_p[None],
            ret_p.astype(x_prompt.dtype), pool_s[None], ret_s.astype(state_ret.dtype)[None])
```

```python
import functools

import jax
import jax.numpy as jnp
from jax import lax
from jax.experimental import pallas as pl
from jax.experimental.pallas import tpu as pltpu

F32 = jnp.float32
BF16 = jnp.bfloat16

PAST_LEN = 16384
POOL_WINDOWS = (2, 4, 8, 16)
RET_CHUNK = 128
ROPE_BASE = 10000.0
TOP_K = 8
N_GROUPS = 8
TOPK_GROUPS = 4
ROUTED_SCALE = 2.5
N_MOD = 6
EPS = 1e-6

V7X_VMEM_BYTES = 64 * 1024 * 1024
SUBLANES = 8
LANES = 128

SAMPLE_SEQS_PER_STEP = 8
TOKEN_BLOCK = 256
DMA_ROWS = 16
SLOT_ROWS = 3 * DMA_ROWS
ONEHOT_ROWS = 256
SLOT_CHUNK_EXPERTS = 16
EXPERT_TILE = 512
POOL_PREV = 16


def _vmem_limit(nbytes):
    return int(min(nbytes, V7X_VMEM_BYTES - 8 * 1024 * 1024))


def _dot(a, b):
    return jnp.dot(a, b, preferred_element_type=F32)


def _dot_nt(a, b):
    return lax.dot_general(a, b, (((1,), (1,)), ((), ())), preferred_element_type=F32)


def _dot_tn(a, b):
    return lax.dot_general(a, b, (((0,), (0,)), ((), ())), preferred_element_type=F32)


def _split_bf16(x, parts):
    out = []
    for _ in range(parts):
        p = x.astype(BF16)
        out.append(p)
        x = x - p.astype(F32)
    return out


def _rms_mod(x, g, shift, scale):
    y = x * lax.rsqrt(jnp.mean(x * x, axis=-1, keepdims=True) + EPS)
    return (y * g) * (1.0 + scale) + shift


def _rotary(x, cosb, sinb):
    return x * cosb + pltpu.roll(x, x.shape[-1] // 2, axis=1) * sinb


def _group_norm(o):
    return o * lax.rsqrt(jnp.mean(o * o, axis=-1, keepdims=True) + EPS)


def _ada_kernel(cp_ref, ct_ref, w_ref, b_ref, modp_ref, modt_ref):
    w = w_ref[...].astype(BF16)
    modp_ref[...] = _dot(cp_ref[...].astype(BF16), w) + b_ref[...]
    modt_ref[0] = _dot(ct_ref[...].astype(BF16), w) + b_ref[...]


def _ada(c_prompt, c_tokens, w_ada, b_ada):
    bp, d = c_prompt.shape
    ts = c_tokens.shape[0]
    n = w_ada.shape[1]
    return pl.pallas_call(
        _ada_kernel,
        out_shape=(jax.ShapeDtypeStruct((bp, n), F32), jax.ShapeDtypeStruct((n // d, ts, d), F32)),
        grid=(n // d,),
        in_specs=[
            pl.BlockSpec((bp, d), lambda j: (0, 0)),
            pl.BlockSpec((ts, d), lambda j: (0, 0)),
            pl.BlockSpec((d, d), lambda j: (0, j)),
            pl.BlockSpec((1, d), lambda j: (0, j)),
        ],
        out_specs=(pl.BlockSpec((bp, d), lambda j: (0, j)),
                   pl.BlockSpec((1, ts, d), lambda j: (j, 0, 0))),
        compiler_params=pltpu.CompilerParams(
            dimension_semantics=("arbitrary",),
            vmem_limit_bytes=_vmem_limit(32 * 1024 * 1024),
        ),
        name="ada",
    )(c_prompt, c_tokens, w_ada, b_ada.reshape(1, n))


class _Dims:
    def __init__(self, d_model, d_pool, heads, dk, dv):
        self.d = d_model
        self.d_pool = d_pool
        self.gd = d_pool // len(POOL_WINDOWS)
        self.heads = heads
        self.dk = dk
        self.dv = dv
        self.d_k = heads * dk
        self.d_v = heads * dv
        widths = (d_pool, self.d_k, self.d_k, self.d_v, self.d_v, d_model, d_model)
        offs = [0]
        for w in widths:
            offs.append(offs[-1] + w)
        self.cols = tuple((offs[i], offs[i + 1]) for i in range(len(widths)))
        self.d_in = offs[-1]


def _proj(hb, win_ref, cols):
    return _dot(hb, win_ref[:, cols[0]:cols[1]])


def _pool_branch(pm_parts, wpg_ref, ps_ref, wbp_ref):
    pg = [_dot(pm.astype(BF16), wpg_ref[gi]) for gi, pm in enumerate(pm_parts)]
    pm = jnp.concatenate(pg, axis=1) * ps_ref[...]
    return _dot(pm.astype(BF16), wbp_ref[...])


def _merge_out(x, gate, a, o, g, ga, gb, wbr_ref, wout_ref):
    r = _dot((jax.nn.silu(g) * o).astype(BF16), wbr_ref[...])
    merged = jax.nn.sigmoid(ga) * a + jax.nn.sigmoid(gb) * r
    m = _dot(merged.astype(BF16), wout_ref[...])
    return x + gate * m


def _mixer_prompt_kernel(dims, x_ref, mod_ref, g1_ref, win_ref, wpg_ref, ps_ref, wbp_ref,
                         wbr_ref, wout_ref, cos_ref, sin_ref, intra_ref, qdec_ref,
                         kdec_ref, sdec_ref, x1_ref, pool_ref, ret_ref, uext_ref):
    c = pl.program_id(1)
    L = x_ref.shape[1]

    @pl.when(c == 0)
    def _init():
        ret_ref[...] = jnp.zeros_like(ret_ref)
        uext_ref[0:POOL_PREV, :] = jnp.zeros((POOL_PREV, dims.d_pool), F32)

    x = x_ref[0]
    shift, scale, gate = mod_ref[0, 0:1, :], mod_ref[0, 1:2, :], mod_ref[0, 2:3, :]
    hb = _rms_mod(x, g1_ref[...], shift, scale).astype(BF16)

    u = _proj(hb, win_ref, dims.cols[0])
    uext_ref[POOL_PREV:POOL_PREV + L, :] = u
    pos = c * L + lax.broadcasted_iota(jnp.int32, (L, 1), 0)
    pm_parts = []
    for gi, w in enumerate(POOL_WINDOWS):
        s = uext_ref[:, gi * dims.gd:(gi + 1) * dims.gd]
        sh = 1
        while sh < w:
            s = s + pltpu.roll(s, sh, axis=0)
            sh *= 2
        cnt = jnp.minimum(w, pos + 1).astype(F32)
        pm_parts.append(s[POOL_PREV:, :] / cnt - u[:, gi * dims.gd:(gi + 1) * dims.gd])
    a = _pool_branch(pm_parts, wpg_ref, ps_ref, wbp_ref)
    uext_ref[0:POOL_PREV, :] = uext_ref[L:L + POOL_PREV, :]
    pool_ref[0] = uext_ref[0:POOL_PREV, :]

    q = _proj(hb, win_ref, dims.cols[1])
    k = _proj(hb, win_ref, dims.cols[2])
    v = _proj(hb, win_ref, dims.cols[3])
    cosb, sinb = cos_ref[...], sin_ref[...]
    outs = []
    for h in range(dims.heads):
        qr = _rotary(q[:, h * dims.dk:(h + 1) * dims.dk], cosb, sinb)
        kr = _rotary(k[:, h * dims.dk:(h + 1) * dims.dk], cosb, sinb) * (dims.dk ** -0.5)
        vb = v[:, h * dims.dv:(h + 1) * dims.dv].astype(BF16)
        scores = _dot_nt(qr.astype(BF16), kr.astype(BF16)) * intra_ref[h]
        s_prev = ret_ref[0, 0, h]
        o = _dot(scores.astype(BF16), vb) + _dot((qr * qdec_ref[h]).astype(BF16),
                                                  s_prev.astype(BF16))
        ret_ref[0, 0, h] = sdec_ref[h] * s_prev + _dot_tn((kr * kdec_ref[h]).astype(BF16), vb)
        outs.append(_group_norm(o))
    o = jnp.concatenate(outs, axis=1)

    g = _proj(hb, win_ref, dims.cols[4])
    ga = _proj(hb, win_ref, dims.cols[5])
    gb = _proj(hb, win_ref, dims.cols[6])
    x1_ref[0] = _merge_out(x, gate, a, o, g, ga, gb, wbr_ref, wout_ref)


def _const_spec(shape):
    zeros = (0,) * len(shape)
    return pl.BlockSpec(shape, lambda *_: zeros)


def _mixer_prompt(dims, x, mod, g1, weights, tables):
    b, s, d = x.shape
    L = RET_CHUNK
    nc = s // L
    win, wpg, ps, wbp, wbr, wout = weights
    cosb, sinb, intra, qdec, kdec, sdec = tables
    in_specs = [
        pl.BlockSpec((1, L, d), lambda i, c: (i, c, 0)),
        pl.BlockSpec((1, N_MOD, d), lambda i, c: (i, 0, 0)),
        _const_spec(g1.shape), _const_spec(win.shape), _const_spec(wpg.shape),
        _const_spec(ps.shape), _const_spec(wbp.shape), _const_spec(wbr.shape),
        _const_spec(wout.shape),
        pl.BlockSpec((L, dims.dk), lambda i, c: (c, 0)),
        pl.BlockSpec((L, dims.dk), lambda i, c: (c, 0)),
        _const_spec(intra.shape), _const_spec(qdec.shape), _const_spec(kdec.shape),
        _const_spec(sdec.shape),
    ]
    out_shape = (
        jax.ShapeDtypeStruct((b, s, d), F32),
        jax.ShapeDtypeStruct((b, POOL_PREV, dims.d_pool), F32),
        jax.ShapeDtypeStruct((1, b, dims.heads, dims.dk, dims.dv), F32),
    )
    out_specs = (
        pl.BlockSpec((1, L, d), lambda i, c: (i, c, 0)),
        pl.BlockSpec((1, POOL_PREV, dims.d_pool), lambda i, c: (i, 0, 0)),
        pl.BlockSpec((1, 1, dims.heads, dims.dk, dims.dv), lambda i, c: (0, i, 0, 0, 0)),
    )
    weight_bytes = 2 * (win.size + wpg.size + wbp.size + wbr.size + wout.size)
    return pl.pallas_call(
        functools.partial(_mixer_prompt_kernel, dims),
        out_shape=out_shape,
        grid=(b, nc),
        in_specs=in_specs,
        out_specs=out_specs,
        scratch_shapes=[pltpu.VMEM((POOL_PREV + L, dims.d_pool), F32)],
        compiler_params=pltpu.CompilerParams(
            dimension_semantics=("arbitrary", "arbitrary"),
            vmem_limit_bytes=_vmem_limit(2 * weight_bytes + 16 * 1024 * 1024),
        ),
        name="mixer_prompt",
    )(x, mod, g1, win, wpg, ps, wbp, wbr, wout, cosb, sinb, intra, qdec, kdec, sdec)


def _mixer_sample_kernel(dims, ds, x_ref, modt_ref, g1_ref, win_ref, wpg_ref, ps_ref, wbp_ref,
                         wbr_ref, wout_ref, cos_ref, sin_ref, intra_ref, qdec_ref, kdec_ref,
                         sdec_ref, bandp_ref, bandu_ref, cnt_ref, poolin_ref, sret_ref,
                         x1_ref, u_ref, ret_ref):
    R = x_ref.shape[0]
    ns = R // ds
    x = x_ref[...]
    shift, scale, gate = modt_ref[0], modt_ref[1], modt_ref[2]
    hb = _rms_mod(x, g1_ref[...], shift, scale).astype(BF16)

    u = _proj(hb, win_ref, dims.cols[0])
    u_ref[...] = u
    hist = poolin_ref[...].reshape(ns * POOL_PREV, dims.d_pool)
    pm_parts = []
    for gi in range(len(POOL_WINDOWS)):
        lo, hi = gi * dims.gd, (gi + 1) * dims.gd
        win_sum = jnp.zeros((R, dims.gd), F32)
        for part in _split_bf16(hist[:, lo:hi], 3):
            win_sum = win_sum + _dot(bandp_ref[gi], part)
        for part in _split_bf16(u[:, lo:hi], 3):
            win_sum = win_sum + _dot(bandu_ref[gi], part)
        pm_parts.append(win_sum / cnt_ref[gi] - u[:, lo:hi])
    a = _pool_branch(pm_parts, wpg_ref, ps_ref, wbp_ref)

    q = _proj(hb, win_ref, dims.cols[1])
    k = _proj(hb, win_ref, dims.cols[2])
    v = _proj(hb, win_ref, dims.cols[3])
    cosb, sinb = cos_ref[...], sin_ref[...]
    row_seq = lax.broadcasted_iota(jnp.int32, (R, ns * dims.dk), 0) // ds
    col_seq = lax.broadcasted_iota(jnp.int32, (R, ns * dims.dk), 1) // dims.dk
    own = row_seq == col_seq
    outs = []
    for h in range(dims.heads):
        qr = _rotary(q[:, h * dims.dk:(h + 1) * dims.dk], cosb, sinb)
        kr = _rotary(k[:, h * dims.dk:(h + 1) * dims.dk], cosb, sinb) * (dims.dk ** -0.5)
        vb = v[:, h * dims.dv:(h + 1) * dims.dv].astype(BF16)
        scores = _dot_nt(qr.astype(BF16), kr.astype(BF16)) * intra_ref[h]
        s_prev = sret_ref[:, h].reshape(ns * dims.dk, dims.dv)
        qd = jnp.concatenate([qr * qdec_ref[h]] * ns, axis=1)
        kd = jnp.concatenate([kr * kdec_ref[h]] * ns, axis=1)
        qexp = jnp.where(own, qd, 0.0).astype(BF16)
        kexp = jnp.where(own, kd, 0.0).astype(BF16)
        o = _dot(scores.astype(BF16), vb) + _dot(qexp, s_prev.astype(BF16))
        s_new = sdec_ref[h] * s_prev + _dot_tn(kexp, vb)
        ret_ref[:, h] = s_new.reshape(ns, dims.dk, dims.dv)
        outs.append(_group_norm(o))
    o = jnp.concatenate(outs, axis=1)

    g = _proj(hb, win_ref, dims.cols[4])
    ga = _proj(hb, win_ref, dims.cols[5])
    gb = _proj(hb, win_ref, dims.cols[6])
    x1_ref[...] = _merge_out(x, gate, a, o, g, ga, gb, wbr_ref, wout_ref)


def _mixer_sample(dims, ds, x2d, modt, g1, weights, tables, pool_in, state_ret):
    t, d = x2d.shape
    ns = SAMPLE_SEQS_PER_STEP
    R = ns * ds
    win, wpg, ps, wbp, wbr, wout = weights
    cosb, sinb, intra, qdec, kdec, sdec, bandp, bandu, cnt = tables
    db = state_ret.shape[0]
    in_specs = [
        pl.BlockSpec((R, d), lambda i: (i, 0)),
        pl.BlockSpec((N_MOD, R, d), lambda i: (0, i, 0)),
        _const_spec(g1.shape), _const_spec(win.shape), _const_spec(wpg.shape),
        _const_spec(ps.shape), _const_spec(wbp.shape), _const_spec(wbr.shape),
        _const_spec(wout.shape), _const_spec(cosb.shape), _const_spec(sinb.shape),
        _const_spec(intra.shape), _const_spec(qdec.shape), _const_spec(kdec.shape),
        _const_spec(sdec.shape), _const_spec(bandp.shape), _const_spec(bandu.shape),
        _const_spec(cnt.shape),
        pl.BlockSpec((ns, POOL_PREV, dims.d_pool), lambda i: (i, 0, 0)),
        pl.BlockSpec((ns, dims.heads, dims.dk, dims.dv), lambda i: (i, 0, 0, 0)),
    ]
    out_shape = (
        jax.ShapeDtypeStruct((t, d), F32),
        jax.ShapeDtypeStruct((t, dims.d_pool), F32),
        jax.ShapeDtypeStruct(state_ret.shape, state_ret.dtype),
    )
    out_specs = (
        pl.BlockSpec((R, d), lambda i: (i, 0)),
        pl.BlockSpec((R, dims.d_pool), lambda i: (i, 0)),
        pl.BlockSpec((ns, dims.heads, dims.dk, dims.dv), lambda i: (i, 0, 0, 0)),
    )
    weight_bytes = 2 * (win.size + wpg.size + wbp.size + wbr.size + wout.size)
    state_bytes = 4 * ns * dims.heads * dims.dk * dims.dv
    return pl.pallas_call(
        functools.partial(_mixer_sample_kernel, dims, ds),
        out_shape=out_shape,
        grid=(db // ns,),
        in_specs=in_specs,
        out_specs=out_specs,
        compiler_params=pltpu.CompilerParams(
            dimension_semantics=("arbitrary",),
            vmem_limit_bytes=_vmem_limit(2 * weight_bytes + 4 * state_bytes + 8 * 1024 * 1024),
        ),
        name="mixer_sample",
    )(x2d, modt, g1, win, wpg, ps, wbp, wbr, wout, cosb, sinb, intra, qdec, kdec, sdec,
      bandp, bandu, cnt, pool_in, state_ret)


def _rotary_tables(pos, dk):
    half = dk // 2
    inv = jnp.power(ROPE_BASE, -jnp.arange(half, dtype=F32) / half)
    ang = pos.astype(F32)[:, None] * inv[None, :]
    cos, sin = jnp.cos(ang), jnp.sin(ang)
    return jnp.concatenate([cos, cos], axis=1), jnp.concatenate([-sin, sin], axis=1)


def _decay_tables(log_g, chunk, reps, dk, dv):
    idx = jnp.arange(chunk, dtype=F32)
    diff = idx[:, None] - idx[None, :]
    intra = jnp.where((diff >= 0)[None],
                      jnp.exp(log_g[:, None, None] * jnp.maximum(diff, 0.0)[None]), 0.0)
    heads = log_g.shape[0]
    eye = jnp.eye(reps, dtype=F32)
    intra = jnp.einsum("ab,hij->haibj", eye, intra).reshape(heads, reps * chunk, reps * chunk)
    q_dec = jnp.exp(log_g[:, None] * (idx[None, :] + 1.0))
    k_dec = jnp.exp(log_g[:, None] * (chunk - 1.0 - idx[None, :]))
    s_dec = jnp.exp(log_g * chunk)
    q_dec = jnp.broadcast_to(jnp.tile(q_dec, (1, reps))[:, :, None], (heads, reps * chunk, dk))
    k_dec = jnp.broadcast_to(jnp.tile(k_dec, (1, reps))[:, :, None], (heads, reps * chunk, dk))
    s_dec = jnp.broadcast_to(s_dec[:, None, None], (heads, 1, dv))
    return intra, q_dec, k_dec, s_dec


def _sample_pool_tables(ns, ds, gd, start_pos):
    n = jnp.arange(ds)
    i = jnp.arange(POOL_PREV)
    eye = jnp.eye(ns, dtype=F32)
    bandp, bandu, cnt = [], [], []
    for w in POOL_WINDOWS:
        bp = ((i[None, :] - POOL_PREV) >= (n[:, None] - w + 1)).astype(F32)
        bu = ((n[None, :] <= n[:, None]) & (n[None, :] >= n[:, None] - w + 1)).astype(F32)
        bandp.append(jnp.einsum("ab,ni->anbi", eye, bp).reshape(ns * ds, ns * POOL_PREV))
        bandu.append(jnp.einsum("ab,nm->anbm", eye, bu).reshape(ns * ds, ns * ds))
        c = jnp.minimum(w, start_pos + n + 1).astype(F32)
        cnt.append(jnp.broadcast_to(jnp.tile(c, ns)[:, None], (ns * ds, gd)))
    return (jnp.stack(bandp).astype(BF16), jnp.stack(bandu).astype(BF16), jnp.stack(cnt))


def _first_max_onehot(v, idx, big, axes):
    m = jnp.max(v, axis=axes, keepdims=True)
    first = jnp.min(jnp.where(v == m, idx, big), axis=axes, keepdims=True)
    return idx == first


def _route_kernel(n_prompt_blocks, xp_ref, xs_ref, modp_ref, modt_ref, g2_ref, wr_ref, br_ref,
                  before_ref, below_ref, h2_ref, pos_ref, wts_ref, cnt_ref, rank_ref, gates_ref):
    i = pl.program_id(0)
    is_p = i < n_prompt_blocks
    x = jnp.where(is_p, xp_ref[...], xs_ref[...])
    shift = jnp.where(is_p, modp_ref[0, 3:4, :], modt_ref[3])
    scale = jnp.where(is_p, modp_ref[0, 4:5, :], modt_ref[4])
    h2 = _rms_mod(x, g2_ref[...], shift, scale)
    h2_ref[...] = h2.astype(BF16)

    hh, hl = _split_bf16(h2, 2)
    wh, wl = _split_bf16(wr_ref[...], 2)
    logits = _dot_nt(wh, hh) + (_dot_nt(wh, hl) + _dot_nt(wl, hh))
    s = jax.nn.sigmoid(logits)
    e, t = s.shape
    ge = e // N_GROUPS
    neg = -jnp.inf
    s3 = s.reshape(N_GROUPS, ge, t)
    sb3 = s3 + br_ref[...]

    j3 = lax.broadcasted_iota(jnp.int32, (N_GROUPS, ge, t), 1)
    top1 = _first_max_onehot(sb3, j3, ge, 1)
    m1 = jnp.max(sb3, axis=1, keepdims=True)
    m2 = jnp.max(jnp.where(top1, neg, sb3), axis=1, keepdims=True)
    gv = m1 + m2

    g3 = lax.broadcasted_iota(jnp.int32, (N_GROUPS, 1, t), 0)
    gsel = jnp.zeros((N_GROUPS, 1, t), F32)
    for _ in range(TOPK_GROUPS):
        hit = _first_max_onehot(gv, g3, N_GROUPS, 0)
        gsel = jnp.where(hit, 1.0, gsel)
        gv = jnp.where(hit, neg, gv)

    e3 = lax.broadcasted_iota(jnp.int32, (N_GROUPS, ge, t), 0) * ge + j3
    ev = jnp.where(gsel > 0.5, sb3, neg)
    sel = jnp.zeros((N_GROUPS, ge, t), F32)
    chosen = jnp.zeros((N_GROUPS, ge, t), F32)
    hits = []
    for _ in range(TOP_K):
        hit = _first_max_onehot(ev, e3, e, (0, 1))
        hits.append(hit)
        sel = jnp.where(hit, s3, sel)
        chosen = jnp.where(hit, 1.0, chosen)
        ev = jnp.where(hit, neg, ev)
    tot = jnp.sum(sel, axis=(0, 1), keepdims=True)
    gates = sel / tot * ROUTED_SCALE

    chosen2 = chosen.reshape(e, t)
    rank = _dot(chosen2.astype(BF16), before_ref[...])
    cnt = jnp.sum(chosen2, axis=1, keepdims=True)
    ovf_units = jnp.floor((jnp.maximum(cnt - SLOT_ROWS, 0.0) + (DMA_ROWS - 1)) / DMA_ROWS)
    ovf_off = _dot(below_ref[...], jnp.broadcast_to(ovf_units, (e, LANES)).astype(BF16))[:, 0:1]
    e_col = lax.broadcasted_iota(jnp.int32, (e, 1), 0).astype(F32)
    slot = jnp.where(rank < SLOT_ROWS, e_col * SLOT_ROWS + rank,
                     e * SLOT_ROWS + ovf_off * DMA_ROWS + (rank - SLOT_ROWS))
    slot3 = slot.reshape(N_GROUPS, ge, t)
    for k, hit in enumerate(hits):
        pos_k = jnp.sum(jnp.where(hit, slot3, 0.0), axis=(0, 1), keepdims=True)
        w_k = jnp.sum(jnp.where(hit, gates, 0.0), axis=(0, 1), keepdims=True)
        pos_ref[k:k + 1, :] = pos_k.reshape(1, t).astype(jnp.int32)
        wts_ref[k:k + 1, :] = w_k.reshape(1, t)
    cnt_ref[0] = jnp.broadcast_to(cnt, (e, LANES)).astype(jnp.int32)
    rank_ref[...] = jnp.where(chosen2 > 0.5, rank, -1.0).astype(jnp.int32)
    gates_ref[...] = gates.reshape(e, t)


def _route(xp2d, xs2d, modp, modt, g2, wr_t, br):
    tp, d = xp2d.shape
    ts = xs2d.shape[0]
    tb = TOKEN_BLOCK
    npb, nsb = tp // tb, ts // tb
    e = wr_t.shape[0]
    seq = tp // modp.shape[0]
    bpb = seq // tb

    def p_idx(i):
        return jnp.minimum(i, npb - 1)

    def s_idx(i):
        return jnp.maximum(i - npb, 0)

    t = tp + ts
    nb = npb + nsb
    tok = jnp.arange(tb)
    before = (tok[:, None] < tok[None, :]).astype(BF16)
    ex = jnp.arange(e)
    below = (ex[None, :] < ex[:, None]).astype(BF16)
    return pl.pallas_call(
        functools.partial(_route_kernel, npb),
        out_shape=(jax.ShapeDtypeStruct((t, d), BF16),
                   jax.ShapeDtypeStruct((TOP_K, t), jnp.int32),
                   jax.ShapeDtypeStruct((TOP_K, t), F32),
                   jax.ShapeDtypeStruct((nb, e, LANES), jnp.int32),
                   jax.ShapeDtypeStruct((e, t), jnp.int32),
                   jax.ShapeDtypeStruct((e, t), F32)),
        grid=(nb,),
        in_specs=[
            pl.BlockSpec((tb, d), lambda i: (p_idx(i), 0)),
            pl.BlockSpec((tb, d), lambda i: (s_idx(i), 0)),
            pl.BlockSpec((1, N_MOD, d), lambda i: (p_idx(i) // bpb, 0, 0)),
            pl.BlockSpec((N_MOD, tb, d), lambda i: (0, s_idx(i), 0)),
            _const_spec(g2.shape), _const_spec(wr_t.shape), _const_spec(br.shape),
            _const_spec(before.shape), _const_spec(below.shape),
        ],
        out_specs=(pl.BlockSpec((tb, d), lambda i: (i, 0)),
                   pl.BlockSpec((TOP_K, tb), lambda i: (0, i)),
                   pl.BlockSpec((TOP_K, tb), lambda i: (0, i)),
                   pl.BlockSpec((1, e, LANES), lambda i: (i, 0, 0)),
                   pl.BlockSpec((e, tb), lambda i: (0, i)),
                   pl.BlockSpec((e, tb), lambda i: (0, i))),
        compiler_params=pltpu.CompilerParams(
            dimension_semantics=("arbitrary",),
            vmem_limit_bytes=_vmem_limit(32 * 1024 * 1024),
        ),
        name="route",
    )(xp2d, xs2d, modp, modt, g2, wr_t, br, before, below)


class _Layout:
    def __init__(self, counts):
        nb, ne = counts.shape
        u, g, tm = SLOT_ROWS, DMA_ROWS, EXPERT_TILE
        pc = (counts + g - 1) // g * g
        ovf_units = (jnp.maximum(counts - u, 0) + g - 1) // g
        rows_e = jnp.sum(pc, axis=0)
        region = (rows_e + u + tm - 1) // tm * tm
        goff = jnp.cumsum(region) - region
        dst = goff[None, :] + jnp.cumsum(pc, axis=0) - pc
        self.dst = dst.reshape(-1)
        ovf_end = jnp.cumsum(ovf_units, axis=1)
        ovf_first = ovf_end - ovf_units
        unit = jnp.arange(_overflow_units_cap(ne))[None, :, None]
        owns = (ovf_first[:, None, :] <= unit) & (unit < ovf_end[:, None, :])
        row = dst[:, None, :] + u + (unit - ovf_first[:, None, :]) * g
        self.ovf_dst = jnp.sum(jnp.where(owns, row, 0), axis=2).reshape(-1)
        self.ovf_tot = ovf_end[:, -1]
        self.tail_start = goff + rows_e
        self.tail_units = (region - rows_e) // g
        self.region_start = goff
        self.region_tiles = region // tm


def _overflow_units_cap(ne):
    return TOKEN_BLOCK * TOP_K // DMA_ROWS + ne


def _block_buffer_rows(tb, ne):
    rows = ne * SLOT_ROWS + tb * TOP_K + ne * (DMA_ROWS - 1)
    return (rows + ONEHOT_ROWS - 1) // ONEHOT_ROWS * ONEHOT_ROWS


def _expert_rows_cap(t, tb, ne):
    nb = t // tb
    rows = t * TOP_K + nb * ne * (DMA_ROWS - 1) + ne * (SLOT_ROWS + EXPERT_TILE - 1)
    return (rows + EXPERT_TILE - 1) // EXPERT_TILE * EXPERT_TILE


def _slot_copy(buf_ref, slot, e, hbm_ref, row, sem, to_hbm):
    src = buf_ref.at[slot, pl.ds(pl.multiple_of(e * SLOT_ROWS, DMA_ROWS), SLOT_ROWS)]
    dst = hbm_ref.at[pl.ds(pl.multiple_of(row, DMA_ROWS), SLOT_ROWS)]
    return pltpu.make_async_copy(src, dst, sem) if to_hbm else pltpu.make_async_copy(dst, src, sem)


def _unit_copy(buf_ref, slot, buf_row, hbm_ref, row, sem, to_hbm):
    src = buf_ref.at[slot, pl.ds(pl.multiple_of(buf_row, DMA_ROWS), DMA_ROWS)]
    dst = hbm_ref.at[pl.ds(pl.multiple_of(row, DMA_ROWS), DMA_ROWS)]
    return pltpu.make_async_copy(src, dst, sem) if to_hbm else pltpu.make_async_copy(dst, src, sem)


def _slots_wait(ne, buf_ref, slot, hbm_ref, sem, to_hbm):
    src = buf_ref.at[slot, pl.ds(0, ne * SLOT_ROWS)]
    dst = hbm_ref.at[pl.ds(0, ne * SLOT_ROWS)]
    c = pltpu.make_async_copy(src, dst, sem) if to_hbm else pltpu.make_async_copy(dst, src, sem)
    c.wait()


def _overflow_copies(ne, blk, slot, odst_ref, otot_ref, buf_ref, hbm_ref, sem, to_hbm, wait):
    def per_unit(i, carry):
        row = odst_ref[blk * _overflow_units_cap(ne) + i]
        c = _unit_copy(buf_ref, slot, ne * SLOT_ROWS + i * DMA_ROWS, hbm_ref, row, sem, to_hbm)
        c.wait() if wait else c.start()
        return carry

    lax.fori_loop(0, otot_ref[blk], per_unit, 0)


def _slot_chunks(ne):
    return [range(c, c + SLOT_CHUNK_EXPERTS) for c in range(0, ne, SLOT_CHUNK_EXPERTS)]


def _overflow_chunks(n_units):
    return lax.shift_right_logical(n_units * DMA_ROWS + (ONEHOT_ROWS - 1),
                                   ONEHOT_ROWS.bit_length() - 1)


def _overflow_select(pos, vals, r0):
    rows = lax.broadcasted_iota(jnp.int32, (ONEHOT_ROWS, pos.shape[1]), 0) + r0
    sel = jnp.zeros(rows.shape, F32)
    for k in range(TOP_K):
        v = 1.0 if vals is None else vals[k:k + 1, :]
        sel = jnp.where(pos[k:k + 1, :] == rows, v, sel)
    return sel


def _dispatch_kernel(ne, dst_ref, odst_ref, otot_ref, tails_ref, tailu_ref,
                     h2_ref, rank_ref, pos_ref, xs_hbm, buf_ref, zero_ref, sem, zsem):
    b = pl.program_id(0)
    nb = pl.num_programs(0)
    slot = b % 2
    h2 = h2_ref[...]
    j = lax.broadcasted_iota(jnp.int32, (SLOT_ROWS, h2.shape[0]), 0)

    for ci, experts in enumerate(_slot_chunks(ne)):
        onehot = jnp.concatenate(
            [jnp.where(rank_ref[e:e + 1, :] == j, 1.0, 0.0) for e in experts], axis=0)
        r0 = experts[0] * SLOT_ROWS
        buf_ref[slot, r0:r0 + len(experts) * SLOT_ROWS, :] = (
            _dot(onehot.astype(BF16), h2).astype(BF16))
        if ci == 0:
            @pl.when(b > 0)
            def _prev():
                _slots_wait(ne, buf_ref, 1 - slot, xs_hbm, sem.at[1 - slot], True)
                _overflow_copies(ne, b - 1, 1 - slot, odst_ref, otot_ref, buf_ref, xs_hbm,
                                 sem.at[1 - slot], True, True)
        for e in experts:
            _slot_copy(buf_ref, slot, e, xs_hbm, dst_ref[b * ne + e], sem.at[slot], True).start()

    pos = pos_ref[...]

    def chunk(ci, carry):
        r0 = pl.multiple_of(ne * SLOT_ROWS + ci * ONEHOT_ROWS, ONEHOT_ROWS)
        onehot = _overflow_select(pos, None, r0).astype(BF16)
        buf_ref[slot, pl.ds(r0, ONEHOT_ROWS), :] = _dot(onehot, h2).astype(BF16)
        return carry

    lax.fori_loop(0, _overflow_chunks(otot_ref[b]), chunk, 0)
    _overflow_copies(ne, b, slot, odst_ref, otot_ref, buf_ref, xs_hbm, sem.at[slot], True, False)

    @pl.when(b == nb - 1)
    def _last():
        _slots_wait(ne, buf_ref, slot, xs_hbm, sem.at[slot], True)
        _overflow_copies(ne, b, slot, odst_ref, otot_ref, buf_ref, xs_hbm, sem.at[slot], True,
                         True)
        zero_ref[...] = jnp.zeros_like(zero_ref)

        def tail(wait):
            def per_expert(e, carry):
                def per_unit(i, carry2):
                    row = pl.multiple_of(tails_ref[e] + i * DMA_ROWS, DMA_ROWS)
                    c = pltpu.make_async_copy(zero_ref, xs_hbm.at[pl.ds(row, DMA_ROWS)], zsem)
                    c.wait() if wait else c.start()
                    return carry2
                lax.fori_loop(0, tailu_ref[e], per_unit, 0)
                return carry
            lax.fori_loop(0, ne, per_expert, 0)

        tail(False)
        tail(True)


def _dispatch(h2, rank, pos, lay, ne, rows_cap):
    t, d = h2.shape
    tb = TOKEN_BLOCK
    buf_rows = _block_buffer_rows(tb, ne)
    return pl.pallas_call(
        functools.partial(_dispatch_kernel, ne),
        out_shape=jax.ShapeDtypeStruct((rows_cap, d), BF16),
        grid_spec=pltpu.PrefetchScalarGridSpec(
            num_scalar_prefetch=5,
            grid=(t // tb,),
            in_specs=[
                pl.BlockSpec((tb, d), lambda i, *_: (i, 0)),
                pl.BlockSpec((ne, tb), lambda i, *_: (0, i)),
                pl.BlockSpec((TOP_K, tb), lambda i, *_: (0, i)),
            ],
            out_specs=pl.BlockSpec(memory_space=pl.ANY),
            scratch_shapes=[
                pltpu.VMEM((2, buf_rows, d), BF16),
                pltpu.VMEM((DMA_ROWS, d), BF16),
                pltpu.SemaphoreType.DMA((2,)),
                pltpu.SemaphoreType.DMA(()),
            ],
        ),
        compiler_params=pltpu.CompilerParams(
            dimension_semantics=("arbitrary",),
            vmem_limit_bytes=_vmem_limit(4 * buf_rows * d + 16 * 1024 * 1024),
        ),
        name="dispatch",
    )(lay.dst, lay.ovf_dst, lay.ovf_tot, lay.tail_start, lay.tail_units, h2, rank, pos)


def _experts_kernel(start_ref, tiles_ref, xe_hbm, wg_ref, wu_ref, wd_ref, ye_hbm,
                    xbuf_ref, ybuf_ref, wgb_ref, wub_ref, wdb_ref, done_ref, xsem, ysem):
    e = pl.program_id(0)
    ne = pl.num_programs(0)
    tm = xbuf_ref.shape[1]

    def x_copy(row, slot):
        return pltpu.make_async_copy(xe_hbm.at[pl.ds(pl.multiple_of(row, tm), tm)],
                                     xbuf_ref.at[slot], xsem.at[slot])

    def y_copy(row, slot):
        return pltpu.make_async_copy(ybuf_ref.at[slot],
                                     ye_hbm.at[pl.ds(pl.multiple_of(row, tm), tm)], ysem.at[slot])

    @pl.when(e == 0)
    def _first():
        done_ref[0] = 0
        x_copy(start_ref[0], 0).start()

    wgb_ref[...] = wg_ref[0].astype(BF16)
    wub_ref[...] = wu_ref[0].astype(BF16)
    wdb_ref[...] = wd_ref[0].astype(BF16)
    done = done_ref[0]
    n = tiles_ref[e]
    base = start_ref[e]
    next_start = start_ref[jnp.minimum(e + 1, ne - 1)]

    def tile(j, carry):
        g = done + j
        slot = g & 1
        row = base + j * tm
        x_copy(row, slot).wait()
        is_last = j == n - 1

        @pl.when(jnp.logical_not(jnp.logical_and(is_last, e == ne - 1)))
        def _prefetch():
            x_copy(jnp.where(is_last, next_start, row + tm), 1 - slot).start()

        x = xbuf_ref[slot]
        hid = jax.nn.silu(_dot(x, wgb_ref[...])) * _dot(x, wub_ref[...])
        y = _dot(hid.astype(BF16), wdb_ref[...]).astype(BF16)

        @pl.when(g >= 2)
        def _reuse():
            y_copy(0, slot).wait()

        ybuf_ref[slot] = y
        y_copy(row, slot).start()
        return carry

    lax.fori_loop(0, n, tile, 0)
    done_ref[0] = done + n

    @pl.when(e == ne - 1)
    def _drain():
        y_copy(0, 0).wait()
        y_copy(0, 1).wait()


def _experts(xs, lay, wg, wu, wd):
    rows_cap, d = xs.shape
    ne, _, de = wg.shape
    assert ne >= 2
    tm = EXPERT_TILE
    return pl.pallas_call(
        _experts_kernel,
        out_shape=jax.ShapeDtypeStruct((rows_cap, d), BF16),
        grid_spec=pltpu.PrefetchScalarGridSpec(
            num_scalar_prefetch=2,
            grid=(ne,),
            in_specs=[
                pl.BlockSpec(memory_space=pl.ANY),
                pl.BlockSpec((1, d, de), lambda e, *_: (e, 0, 0)),
                pl.BlockSpec((1, d, de), lambda e, *_: (e, 0, 0)),
                pl.BlockSpec((1, de, d), lambda e, *_: (e, 0, 0)),
            ],
            out_specs=pl.BlockSpec(memory_space=pl.ANY),
            scratch_shapes=[
                pltpu.VMEM((2, tm, d), BF16), pltpu.VMEM((2, tm, d), BF16),
                pltpu.VMEM((d, de), BF16), pltpu.VMEM((d, de), BF16), pltpu.VMEM((de, d), BF16),
                pltpu.SMEM((1,), jnp.int32),
                pltpu.SemaphoreType.DMA((2,)), pltpu.SemaphoreType.DMA((2,)),
            ],
        ),
        compiler_params=pltpu.CompilerParams(
            dimension_semantics=("arbitrary",),
            vmem_limit_bytes=_vmem_limit(32 * 1024 * 1024),
        ),
        name="experts",
    )(lay.region_start, lay.region_tiles, xs, wg, wu, wd)


def _final_kernel(n_prompt_blocks, ne, dst_ref, odst_ref, otot_ref,
                  xp_ref, xs_ref, modp_ref, modt_ref, h2_ref, rank_ref, gates_ref, pos_ref, wts_ref,
                  wsg_ref, wsu_ref, wsd_ref, gf_ref, ye_hbm, yp_ref, ys_ref,
                  buf_ref, acc_ref, sem):
    i = pl.program_id(0)
    nb = pl.num_programs(0)
    slot = i % 2

    def fetch(blk, sl):
        for e in range(ne):
            _slot_copy(buf_ref, sl, e, ye_hbm, dst_ref[blk * ne + e], sem.at[sl], False).start()
        _overflow_copies(ne, blk, sl, odst_ref, otot_ref, buf_ref, ye_hbm, sem.at[sl], False, False)

    @pl.when(i == 0)
    def _first():
        buf_ref[...] = jnp.zeros_like(buf_ref)
        fetch(0, 0)

    @pl.when(i + 1 < nb)
    def _prefetch():
        fetch(i + 1, 1 - slot)

    _slots_wait(ne, buf_ref, slot, ye_hbm, sem.at[slot], False)
    _overflow_copies(ne, i, slot, odst_ref, otot_ref, buf_ref, ye_hbm, sem.at[slot], False, True)

    tb = h2_ref.shape[0]
    j = lax.broadcasted_iota(jnp.int32, (SLOT_ROWS, tb), 0)
    acc = jnp.zeros((tb, h2_ref.shape[1]), F32)
    for experts in _slot_chunks(ne):
        sel = jnp.concatenate(
            [jnp.where(rank_ref[e:e + 1, :] == j, gates_ref[e:e + 1, :], 0.0) for e in experts],
            axis=0)
        r0 = experts[0] * SLOT_ROWS
        acc = acc + _dot_tn(sel.astype(BF16), buf_ref[slot, r0:r0 + len(experts) * SLOT_ROWS, :])
    acc_ref[...] = acc

    pos = pos_ref[...]
    wts = wts_ref[...]

    def chunk(ci, carry):
        r0 = pl.multiple_of(ne * SLOT_ROWS + ci * ONEHOT_ROWS, ONEHOT_ROWS)
        sel = _overflow_select(pos, wts, r0).astype(BF16)
        acc_ref[...] += _dot_tn(sel, buf_ref[slot, pl.ds(r0, ONEHOT_ROWS), :])
        return carry

    lax.fori_loop(0, _overflow_chunks(otot_ref[i]), chunk, 0)

    is_p = i < n_prompt_blocks
    x = jnp.where(is_p, xp_ref[...], xs_ref[...])
    gate = jnp.where(is_p, modp_ref[0, 5:6, :], modt_ref[5])
    hb = h2_ref[...]
    hid = jax.nn.silu(_dot(hb, wsg_ref[...])) * _dot(hb, wsu_ref[...])
    f = acc_ref[...] + _dot(hid.astype(BF16), wsd_ref[...])
    x2 = x + gate * f
    y = x2 * lax.rsqrt(jnp.mean(x2 * x2, axis=-1, keepdims=True) + EPS) * gf_ref[...]

    @pl.when(is_p)
    def _p():
        yp_ref[...] = y

    @pl.when(jnp.logical_not(is_p))
    def _s():
        ys_ref[...] = y


def _final(xp2d, xs2d, modp, modt, h2, rank, gates, pos, wts, ye, lay, ne, wsg, wsu, wsd, gf):
    tp, d = xp2d.shape
    ts = xs2d.shape[0]
    tb = TOKEN_BLOCK
    npb, nsb = tp // tb, ts // tb
    seq = tp // modp.shape[0]
    bpb = seq // tb
    buf_rows = _block_buffer_rows(tb, ne)

    def p_idx(i):
        return jnp.minimum(i, npb - 1)

    def s_idx(i):
        return jnp.maximum(i - npb, 0)

    def const(shape):
        zeros = (0,) * len(shape)
        return pl.BlockSpec(shape, lambda i, *_: zeros)

    return pl.pallas_call(
        functools.partial(_final_kernel, npb, ne),
        out_shape=(jax.ShapeDtypeStruct((tp, d), F32), jax.ShapeDtypeStruct((ts, d), F32)),
        grid_spec=pltpu.PrefetchScalarGridSpec(
            num_scalar_prefetch=3,
            grid=(npb + nsb,),
            in_specs=[
                pl.BlockSpec((tb, d), lambda i, *_: (p_idx(i), 0)),
                pl.BlockSpec((tb, d), lambda i, *_: (s_idx(i), 0)),
                pl.BlockSpec((1, N_MOD, d), lambda i, *_: (p_idx(i) // bpb, 0, 0)),
                pl.BlockSpec((N_MOD, tb, d), lambda i, *_: (0, s_idx(i), 0)),
                pl.BlockSpec((tb, d), lambda i, *_: (i, 0)),
                pl.BlockSpec((ne, tb), lambda i, *_: (0, i)),
                pl.BlockSpec((ne, tb), lambda i, *_: (0, i)),
                pl.BlockSpec((TOP_K, tb), lambda i, *_: (0, i)),
                pl.BlockSpec((TOP_K, tb), lambda i, *_: (0, i)),
                const(wsg.shape), const(wsu.shape), const(wsd.shape), const(gf.shape),
                pl.BlockSpec(memory_space=pl.ANY),
            ],
            out_specs=(pl.BlockSpec((tb, d), lambda i, *_: (p_idx(i), 0)),
                       pl.BlockSpec((tb, d), lambda i, *_: (s_idx(i), 0))),
            scratch_shapes=[
                pltpu.VMEM((2, buf_rows, d), BF16),
                pltpu.VMEM((tb, d), F32),
                pltpu.SemaphoreType.DMA((2,)),
            ],
        ),
        compiler_params=pltpu.CompilerParams(
            dimension_semantics=("arbitrary",),
            vmem_limit_bytes=_vmem_limit(4 * buf_rows * d + 20 * 1024 * 1024),
        ),
        name="final",
    )(lay.dst, lay.ovf_dst, lay.ovf_tot,
      xp2d, xs2d, modp, modt, h2, rank, gates, pos, wts, wsg, wsu, wsd, gf, ye)


def kernel(x_prompt, x_sample, c_prompt, c_sample, state_pool, state_ret, w_ada, b_ada, g_norm1,
           w_in, w_pool_group, pool_scale, w_branch_pool, w_branch_ret, w_out, g_norm2, w_router,
           b_router, w_exp_gate, w_exp_up, w_exp_down, w_sh_gate, w_sh_up, w_sh_down, g_final):
    b, s, d = x_prompt.shape
    db, ds, _ = x_sample.shape
    depth = w_ada.shape[0]
    heads, dk, dv = state_ret.shape[2:]
    pool_buf, d_pool = state_pool.shape[2:]
    dims = _Dims(d, d_pool, heads, dk, dv)
    assert dims.d_in == w_in.shape[2]
    assert pool_buf == max(POOL_WINDOWS) - 1 and pool_buf < POOL_PREV
    assert s % RET_CHUNK == 0 and s % TOKEN_BLOCK == 0 and (db * ds) % TOKEN_BLOCK == 0
    assert db % SAMPLE_SEQS_PER_STEP == 0 and SLOT_ROWS % DMA_ROWS == 0

    log_g = jnp.log(1.0 - jnp.power(2.0, -5.0 - jnp.arange(heads, dtype=F32)))
    ns = SAMPLE_SEQS_PER_STEP
    cos_p, sin_p = _rotary_tables(jnp.arange(s, dtype=jnp.int32), dk)
    cos_s, sin_s = _rotary_tables(PAST_LEN + jnp.arange(ds, dtype=jnp.int32), dk)
    cos_s, sin_s = jnp.tile(cos_s, (ns, 1)), jnp.tile(sin_s, (ns, 1))
    dec_p = _decay_tables(log_g, RET_CHUNK, 1, dk, dv)
    dec_s = _decay_tables(log_g, ds, ns, dk, dv)
    pool_tabs = _sample_pool_tables(ns, ds, dims.gd, PAST_LEN)

    assert depth == 1, "the final kernel fuses the output norm into the (single) layer"
    l = 0
    xs = x_sample.reshape(db * ds, d)
    modp, modt = _ada(c_prompt, jnp.repeat(c_sample, ds, axis=0), w_ada[l], b_ada[l])
    modp = modp.reshape(b, N_MOD, d)
    g1 = g_norm1[l].reshape(1, d)
    weights = (w_in[l].astype(BF16), w_pool_group[l].astype(BF16),
               pool_scale[l].reshape(1, d_pool), w_branch_pool[l].astype(BF16),
               w_branch_ret[l].astype(BF16), w_out[l].astype(BF16))

    x1p, pool16, ret_p = _mixer_prompt(dims, x_prompt, modp, g1, weights,
                                       (cos_p, sin_p) + dec_p)
    pool_in = jnp.pad(state_pool[l], ((0, 0), (POOL_PREV - pool_buf, 0), (0, 0)))
    x1s, u_s, ret_s = _mixer_sample(dims, ds, xs, modt, g1, weights,
                                    (cos_s, sin_s) + dec_s + pool_tabs, pool_in,
                                    state_ret[l].astype(F32))
    pool_p = pool16[:, POOL_PREV - pool_buf:]
    pool_s = jnp.concatenate([state_pool[l], u_s.reshape(db, ds, d_pool)], axis=1)[:, -pool_buf:]

    x1p2d = x1p.reshape(b * s, d)
    ne = w_router.shape[2]
    h2, pos, wts, counts, rank, gates = _route(
        x1p2d, x1s, modp, modt, g_norm2[l].reshape(1, d), w_router[l].T,
        b_router[l].reshape(N_GROUPS, ne // N_GROUPS, 1))
    rows_cap = _expert_rows_cap(b * s + db * ds, TOKEN_BLOCK, ne)
    lay = _Layout(counts[:, :, 0])
    xe = _dispatch(h2, rank, pos, lay, ne, rows_cap)
    ye = _experts(xe, lay, w_exp_gate[l], w_exp_up[l], w_exp_down[l])
    yp, ys = _final(x1p2d, x1s, modp, modt, h2, rank, gates, pos, wts, ye, lay, ne,
                    w_sh_gate[l].astype(BF16), w_sh_up[l].astype(BF16),
                    w_sh_down[l].astype(BF16), g_final.reshape(1, d))

    return (yp.reshape(b, s, d), ys.reshape(db, ds, d), pool_p[None],
            ret_p.astype(x_prompt.dtype), pool_s[None], ret_s.astype(state_ret.dtype)[None])
```

```python
import functools

import jax
import jax.numpy as jnp
from jax import lax
from jax.experimental import pallas as pl
from jax.experimental.pallas import tpu as pltpu

F32 = jnp.float32
BF16 = jnp.bfloat16

PAST_LEN = 16384
POOL_WINDOWS = (2, 4, 8, 16)
RET_CHUNK = 128
ROPE_BASE = 10000.0
TOP_K = 8
N_GROUPS = 8
TOPK_GROUPS = 4
ROUTED_SCALE = 2.5
N_MOD = 6
EPS = 1e-6

V7X_VMEM_BYTES = 64 * 1024 * 1024
SUBLANES = 8
LANES = 128

SAMPLE_SEQS_PER_STEP = 8
TOKEN_BLOCK = 256
DMA_ROWS = 16
SLOT_ROWS = 3 * DMA_ROWS
ONEHOT_ROWS = 256
SLOT_CHUNK_EXPERTS = 16
RING_SLOTS = 3
EXPERT_TILE = 512
POOL_PREV = 16


def _vmem_limit(nbytes):
    return int(min(nbytes, V7X_VMEM_BYTES - 8 * 1024 * 1024))


def _dot(a, b):
    return jnp.dot(a, b, preferred_element_type=F32)


def _dot_nt(a, b):
    return lax.dot_general(a, b, (((1,), (1,)), ((), ())), preferred_element_type=F32)


def _dot_tn(a, b):
    return lax.dot_general(a, b, (((0,), (0,)), ((), ())), preferred_element_type=F32)


def _split_bf16(x, parts):
    out = []
    for _ in range(parts):
        p = x.astype(BF16)
        out.append(p)
        x = x - p.astype(F32)
    return out


def _rms_mod(x, g, shift, scale):
    y = x * lax.rsqrt(jnp.mean(x * x, axis=-1, keepdims=True) + EPS)
    return (y * g) * (1.0 + scale) + shift


def _rotary(x, cosb, sinb):
    return x * cosb + pltpu.roll(x, x.shape[-1] // 2, axis=1) * sinb


def _group_norm(o):
    return o * lax.rsqrt(jnp.mean(o * o, axis=-1, keepdims=True) + EPS)


def _ada_kernel(cp_ref, ct_ref, w_ref, b_ref, modp_ref, modt_ref):
    w = w_ref[...].astype(BF16)
    modp_ref[...] = _dot(cp_ref[...].astype(BF16), w) + b_ref[...]
    modt_ref[0] = _dot(ct_ref[...].astype(BF16), w) + b_ref[...]


def _ada(c_prompt, c_tokens, w_ada, b_ada):
    bp, d = c_prompt.shape
    ts = c_tokens.shape[0]
    n = w_ada.shape[1]
    return pl.pallas_call(
        _ada_kernel,
        out_shape=(jax.ShapeDtypeStruct((bp, n), F32), jax.ShapeDtypeStruct((n // d, ts, d), F32)),
        grid=(n // d,),
        in_specs=[
            pl.BlockSpec((bp, d), lambda j: (0, 0)),
            pl.BlockSpec((ts, d), lambda j: (0, 0)),
            pl.BlockSpec((d, d), lambda j: (0, j)),
            pl.BlockSpec((1, d), lambda j: (0, j)),
        ],
        out_specs=(pl.BlockSpec((bp, d), lambda j: (0, j)),
                   pl.BlockSpec((1, ts, d), lambda j: (j, 0, 0))),
        compiler_params=pltpu.CompilerParams(
            dimension_semantics=("arbitrary",),
            vmem_limit_bytes=_vmem_limit(32 * 1024 * 1024),
        ),
        name="ada",
    )(c_prompt, c_tokens, w_ada, b_ada.reshape(1, n))


class _Dims:
    def __init__(self, d_model, d_pool, heads, dk, dv):
        self.d = d_model
        self.d_pool = d_pool
        self.gd = d_pool // len(POOL_WINDOWS)
        self.heads = heads
        self.dk = dk
        self.dv = dv
        self.d_k = heads * dk
        self.d_v = heads * dv
        widths = (d_pool, self.d_k, self.d_k, self.d_v, self.d_v, d_model, d_model)
        offs = [0]
        for w in widths:
            offs.append(offs[-1] + w)
        self.cols = tuple((offs[i], offs[i + 1]) for i in range(len(widths)))
        self.d_in = offs[-1]


def _proj(hb, win_ref, cols):
    return _dot(hb, win_ref[:, cols[0]:cols[1]])


def _pool_branch(pm_parts, wpg_ref, ps_ref, wbp_ref):
    pg = [_dot(pm.astype(BF16), wpg_ref[gi]) for gi, pm in enumerate(pm_parts)]
    pm = jnp.concatenate(pg, axis=1) * ps_ref[...]
    return _dot(pm.astype(BF16), wbp_ref[...])


def _merge_out(x, gate, a, o, g, ga, gb, wbr_ref, wout_ref):
    r = _dot((jax.nn.silu(g) * o).astype(BF16), wbr_ref[...])
    merged = jax.nn.sigmoid(ga) * a + jax.nn.sigmoid(gb) * r
    m = _dot(merged.astype(BF16), wout_ref[...])
    return x + gate * m


def _mixer_prompt_kernel(dims, x_ref, mod_ref, g1_ref, win_ref, wpg_ref, ps_ref, wbp_ref,
                         wbr_ref, wout_ref, cos_ref, sin_ref, intra_ref, qdec_ref,
                         kdec_ref, sdec_ref, x1_ref, pool_ref, ret_ref, uext_ref):
    c = pl.program_id(1)
    L = x_ref.shape[1]

    @pl.when(c == 0)
    def _init():
        ret_ref[...] = jnp.zeros_like(ret_ref)
        uext_ref[0:POOL_PREV, :] = jnp.zeros((POOL_PREV, dims.d_pool), F32)

    x = x_ref[0]
    shift, scale, gate = mod_ref[0, 0:1, :], mod_ref[0, 1:2, :], mod_ref[0, 2:3, :]
    hb = _rms_mod(x, g1_ref[...], shift, scale).astype(BF16)

    u = _proj(hb, win_ref, dims.cols[0])
    uext_ref[POOL_PREV:POOL_PREV + L, :] = u
    pos = c * L + lax.broadcasted_iota(jnp.int32, (L, 1), 0)
    pm_parts = []
    for gi, w in enumerate(POOL_WINDOWS):
        s = uext_ref[:, gi * dims.gd:(gi + 1) * dims.gd]
        sh = 1
        while sh < w:
            s = s + pltpu.roll(s, sh, axis=0)
            sh *= 2
        cnt = jnp.minimum(w, pos + 1).astype(F32)
        pm_parts.append(s[POOL_PREV:, :] / cnt - u[:, gi * dims.gd:(gi + 1) * dims.gd])
    a = _pool_branch(pm_parts, wpg_ref, ps_ref, wbp_ref)
    uext_ref[0:POOL_PREV, :] = uext_ref[L:L + POOL_PREV, :]
    pool_ref[0] = uext_ref[0:POOL_PREV, :]

    q = _proj(hb, win_ref, dims.cols[1])
    k = _proj(hb, win_ref, dims.cols[2])
    v = _proj(hb, win_ref, dims.cols[3])
    cosb, sinb = cos_ref[...], sin_ref[...]
    outs = []
    for h in range(dims.heads):
        qr = _rotary(q[:, h * dims.dk:(h + 1) * dims.dk], cosb, sinb)
        kr = _rotary(k[:, h * dims.dk:(h + 1) * dims.dk], cosb, sinb) * (dims.dk ** -0.5)
        vb = v[:, h * dims.dv:(h + 1) * dims.dv].astype(BF16)
        scores = _dot_nt(qr.astype(BF16), kr.astype(BF16)) * intra_ref[h]
        s_prev = ret_ref[0, 0, h]
        o = _dot(scores.astype(BF16), vb) + _dot((qr * qdec_ref[h]).astype(BF16),
                                                  s_prev.astype(BF16))
        ret_ref[0, 0, h] = sdec_ref[h] * s_prev + _dot_tn((kr * kdec_ref[h]).astype(BF16), vb)
        outs.append(_group_norm(o))
    o = jnp.concatenate(outs, axis=1)

    g = _proj(hb, win_ref, dims.cols[4])
    ga = _proj(hb, win_ref, dims.cols[5])
    gb = _proj(hb, win_ref, dims.cols[6])
    x1_ref[0] = _merge_out(x, gate, a, o, g, ga, gb, wbr_ref, wout_ref)


def _const_spec(shape):
    zeros = (0,) * len(shape)
    return pl.BlockSpec(shape, lambda *_: zeros)


def _mixer_prompt(dims, x, mod, g1, weights, tables):
    b, s, d = x.shape
    L = RET_CHUNK
    nc = s // L
    win, wpg, ps, wbp, wbr, wout = weights
    cosb, sinb, intra, qdec, kdec, sdec = tables
    in_specs = [
        pl.BlockSpec((1, L, d), lambda i, c: (i, c, 0)),
        pl.BlockSpec((1, N_MOD, d), lambda i, c: (i, 0, 0)),
        _const_spec(g1.shape), _const_spec(win.shape), _const_spec(wpg.shape),
        _const_spec(ps.shape), _const_spec(wbp.shape), _const_spec(wbr.shape),
        _const_spec(wout.shape),
        pl.BlockSpec((L, dims.dk), lambda i, c: (c, 0)),
        pl.BlockSpec((L, dims.dk), lambda i, c: (c, 0)),
        _const_spec(intra.shape), _const_spec(qdec.shape), _const_spec(kdec.shape),
        _const_spec(sdec.shape),
    ]
    out_shape = (
        jax.ShapeDtypeStruct((b, s, d), F32),
        jax.ShapeDtypeStruct((b, POOL_PREV, dims.d_pool), F32),
        jax.ShapeDtypeStruct((1, b, dims.heads, dims.dk, dims.dv), F32),
    )
    out_specs = (
        pl.BlockSpec((1, L, d), lambda i, c: (i, c, 0)),
        pl.BlockSpec((1, POOL_PREV, dims.d_pool), lambda i, c: (i, 0, 0)),
        pl.BlockSpec((1, 1, dims.heads, dims.dk, dims.dv), lambda i, c: (0, i, 0, 0, 0)),
    )
    weight_bytes = 2 * (win.size + wpg.size + wbp.size + wbr.size + wout.size)
    return pl.pallas_call(
        functools.partial(_mixer_prompt_kernel, dims),
        out_shape=out_shape,
        grid=(b, nc),
        in_specs=in_specs,
        out_specs=out_specs,
        scratch_shapes=[pltpu.VMEM((POOL_PREV + L, dims.d_pool), F32)],
        compiler_params=pltpu.CompilerParams(
            dimension_semantics=("arbitrary", "arbitrary"),
            vmem_limit_bytes=_vmem_limit(2 * weight_bytes + 16 * 1024 * 1024),
        ),
        name="mixer_prompt",
    )(x, mod, g1, win, wpg, ps, wbp, wbr, wout, cosb, sinb, intra, qdec, kdec, sdec)


def _mixer_sample_kernel(dims, ds, x_ref, modt_ref, g1_ref, win_ref, wpg_ref, ps_ref, wbp_ref,
                         wbr_ref, wout_ref, cos_ref, sin_ref, intra_ref, qdec_ref, kdec_ref,
                         sdec_ref, bandp_ref, bandu_ref, cnt_ref, poolin_ref, sret_ref,
                         x1_ref, u_ref, ret_ref):
    R = x_ref.shape[0]
    ns = R // ds
    x = x_ref[...]
    shift, scale, gate = modt_ref[0], modt_ref[1], modt_ref[2]
    hb = _rms_mod(x, g1_ref[...], shift, scale).astype(BF16)

    u = _proj(hb, win_ref, dims.cols[0])
    u_ref[...] = u
    hist = poolin_ref[...].reshape(ns * POOL_PREV, dims.d_pool)
    pm_parts = []
    for gi in range(len(POOL_WINDOWS)):
        lo, hi = gi * dims.gd, (gi + 1) * dims.gd
        win_sum = jnp.zeros((R, dims.gd), F32)
        for part in _split_bf16(hist[:, lo:hi], 3):
            win_sum = win_sum + _dot(bandp_ref[gi], part)
        for part in _split_bf16(u[:, lo:hi], 3):
            win_sum = win_sum + _dot(bandu_ref[gi], part)
        pm_parts.append(win_sum / cnt_ref[gi] - u[:, lo:hi])
    a = _pool_branch(pm_parts, wpg_ref, ps_ref, wbp_ref)

    q = _proj(hb, win_ref, dims.cols[1])
    k = _proj(hb, win_ref, dims.cols[2])
    v = _proj(hb, win_ref, dims.cols[3])
    cosb, sinb = cos_ref[...], sin_ref[...]
    row_seq = lax.broadcasted_iota(jnp.int32, (R, ns * dims.dk), 0) // ds
    col_seq = lax.broadcasted_iota(jnp.int32, (R, ns * dims.dk), 1) // dims.dk
    own = row_seq == col_seq
    outs = []
    for h in range(dims.heads):
        qr = _rotary(q[:, h * dims.dk:(h + 1) * dims.dk], cosb, sinb)
        kr = _rotary(k[:, h * dims.dk:(h + 1) * dims.dk], cosb, sinb) * (dims.dk ** -0.5)
        vb = v[:, h * dims.dv:(h + 1) * dims.dv].astype(BF16)
        scores = _dot_nt(qr.astype(BF16), kr.astype(BF16)) * intra_ref[h]
        s_prev = sret_ref[:, h].reshape(ns * dims.dk, dims.dv)
        qd = jnp.concatenate([qr * qdec_ref[h]] * ns, axis=1)
        kd = jnp.concatenate([kr * kdec_ref[h]] * ns, axis=1)
        qexp = jnp.where(own, qd, 0.0).astype(BF16)
        kexp = jnp.where(own, kd, 0.0).astype(BF16)
        o = _dot(scores.astype(BF16), vb) + _dot(qexp, s_prev.astype(BF16))
        s_new = sdec_ref[h] * s_prev + _dot_tn(kexp, vb)
        ret_ref[:, h] = s_new.reshape(ns, dims.dk, dims.dv)
        outs.append(_group_norm(o))
    o = jnp.concatenate(outs, axis=1)

    g = _proj(hb, win_ref, dims.cols[4])
    ga = _proj(hb, win_ref, dims.cols[5])
    gb = _proj(hb, win_ref, dims.cols[6])
    x1_ref[...] = _merge_out(x, gate, a, o, g, ga, gb, wbr_ref, wout_ref)


def _mixer_sample(dims, ds, x2d, modt, g1, weights, tables, pool_in, state_ret):
    t, d = x2d.shape
    ns = SAMPLE_SEQS_PER_STEP
    R = ns * ds
    win, wpg, ps, wbp, wbr, wout = weights
    cosb, sinb, intra, qdec, kdec, sdec, bandp, bandu, cnt = tables
    db = state_ret.shape[0]
    in_specs = [
        pl.BlockSpec((R, d), lambda i: (i, 0)),
        pl.BlockSpec((N_MOD, R, d), lambda i: (0, i, 0)),
        _const_spec(g1.shape), _const_spec(win.shape), _const_spec(wpg.shape),
        _const_spec(ps.shape), _const_spec(wbp.shape), _const_spec(wbr.shape),
        _const_spec(wout.shape), _const_spec(cosb.shape), _const_spec(sinb.shape),
        _const_spec(intra.shape), _const_spec(qdec.shape), _const_spec(kdec.shape),
        _const_spec(sdec.shape), _const_spec(bandp.shape), _const_spec(bandu.shape),
        _const_spec(cnt.shape),
        pl.BlockSpec((ns, POOL_PREV, dims.d_pool), lambda i: (i, 0, 0)),
        pl.BlockSpec((ns, dims.heads, dims.dk, dims.dv), lambda i: (i, 0, 0, 0)),
    ]
    out_shape = (
        jax.ShapeDtypeStruct((t, d), F32),
        jax.ShapeDtypeStruct((t, dims.d_pool), F32),
        jax.ShapeDtypeStruct(state_ret.shape, state_ret.dtype),
    )
    out_specs = (
        pl.BlockSpec((R, d), lambda i: (i, 0)),
        pl.BlockSpec((R, dims.d_pool), lambda i: (i, 0)),
        pl.BlockSpec((ns, dims.heads, dims.dk, dims.dv), lambda i: (i, 0, 0, 0)),
    )
    weight_bytes = 2 * (win.size + wpg.size + wbp.size + wbr.size + wout.size)
    state_bytes = 4 * ns * dims.heads * dims.dk * dims.dv
    return pl.pallas_call(
        functools.partial(_mixer_sample_kernel, dims, ds),
        out_shape=out_shape,
        grid=(db // ns,),
        in_specs=in_specs,
        out_specs=out_specs,
        compiler_params=pltpu.CompilerParams(
            dimension_semantics=("arbitrary",),
            vmem_limit_bytes=_vmem_limit(2 * weight_bytes + 4 * state_bytes + 8 * 1024 * 1024),
        ),
        name="mixer_sample",
    )(x2d, modt, g1, win, wpg, ps, wbp, wbr, wout, cosb, sinb, intra, qdec, kdec, sdec,
      bandp, bandu, cnt, pool_in, state_ret)


def _rotary_tables(pos, dk):
    half = dk // 2
    inv = jnp.power(ROPE_BASE, -jnp.arange(half, dtype=F32) / half)
    ang = pos.astype(F32)[:, None] * inv[None, :]
    cos, sin = jnp.cos(ang), jnp.sin(ang)
    return jnp.concatenate([cos, cos], axis=1), jnp.concatenate([-sin, sin], axis=1)


def _decay_tables(log_g, chunk, reps, dk, dv):
    idx = jnp.arange(chunk, dtype=F32)
    diff = idx[:, None] - idx[None, :]
    intra = jnp.where((diff >= 0)[None],
                      jnp.exp(log_g[:, None, None] * jnp.maximum(diff, 0.0)[None]), 0.0)
    heads = log_g.shape[0]
    eye = jnp.eye(reps, dtype=F32)
    intra = jnp.einsum("ab,hij->haibj", eye, intra).reshape(heads, reps * chunk, reps * chunk)
    q_dec = jnp.exp(log_g[:, None] * (idx[None, :] + 1.0))
    k_dec = jnp.exp(log_g[:, None] * (chunk - 1.0 - idx[None, :]))
    s_dec = jnp.exp(log_g * chunk)
    q_dec = jnp.broadcast_to(jnp.tile(q_dec, (1, reps))[:, :, None], (heads, reps * chunk, dk))
    k_dec = jnp.broadcast_to(jnp.tile(k_dec, (1, reps))[:, :, None], (heads, reps * chunk, dk))
    s_dec = jnp.broadcast_to(s_dec[:, None, None], (heads, 1, dv))
    return intra, q_dec, k_dec, s_dec


def _sample_pool_tables(ns, ds, gd, start_pos):
    n = jnp.arange(ds)
    i = jnp.arange(POOL_PREV)
    eye = jnp.eye(ns, dtype=F32)
    bandp, bandu, cnt = [], [], []
    for w in POOL_WINDOWS:
        bp = ((i[None, :] - POOL_PREV) >= (n[:, None] - w + 1)).astype(F32)
        bu = ((n[None, :] <= n[:, None]) & (n[None, :] >= n[:, None] - w + 1)).astype(F32)
        bandp.append(jnp.einsum("ab,ni->anbi", eye, bp).reshape(ns * ds, ns * POOL_PREV))
        bandu.append(jnp.einsum("ab,nm->anbm", eye, bu).reshape(ns * ds, ns * ds))
        c = jnp.minimum(w, start_pos + n + 1).astype(F32)
        cnt.append(jnp.broadcast_to(jnp.tile(c, ns)[:, None], (ns * ds, gd)))
    return (jnp.stack(bandp).astype(BF16), jnp.stack(bandu).astype(BF16), jnp.stack(cnt))


def _first_max_onehot(v, idx, big, axes):
    m = jnp.max(v, axis=axes, keepdims=True)
    first = jnp.min(jnp.where(v == m, idx, big), axis=axes, keepdims=True)
    return idx == first


def _route_kernel(n_prompt_blocks, xp_ref, xs_ref, modp_ref, modt_ref, g2_ref, wr_ref, br_ref,
                  before_ref, below_ref, h2_ref, pos_ref, wts_ref, cnt_ref, rank_ref, gates_ref):
    i = pl.program_id(0)
    is_p = i < n_prompt_blocks
    x = jnp.where(is_p, xp_ref[...], xs_ref[...])
    shift = jnp.where(is_p, modp_ref[0, 3:4, :], modt_ref[3])
    scale = jnp.where(is_p, modp_ref[0, 4:5, :], modt_ref[4])
    h2 = _rms_mod(x, g2_ref[...], shift, scale)
    h2_ref[...] = h2.astype(BF16)

    hh, hl = _split_bf16(h2, 2)
    wh, wl = _split_bf16(wr_ref[...], 2)
    logits = _dot_nt(wh, hh) + (_dot_nt(wh, hl) + _dot_nt(wl, hh))
    s = jax.nn.sigmoid(logits)
    e, t = s.shape
    ge = e // N_GROUPS
    neg = -jnp.inf
    s3 = s.reshape(N_GROUPS, ge, t)
    sb3 = s3 + br_ref[...]

    j3 = lax.broadcasted_iota(jnp.int32, (N_GROUPS, ge, t), 1)
    top1 = _first_max_onehot(sb3, j3, ge, 1)
    m1 = jnp.max(sb3, axis=1, keepdims=True)
    m2 = jnp.max(jnp.where(top1, neg, sb3), axis=1, keepdims=True)
    gv = m1 + m2

    g3 = lax.broadcasted_iota(jnp.int32, (N_GROUPS, 1, t), 0)
    gsel = jnp.zeros((N_GROUPS, 1, t), F32)
    for _ in range(TOPK_GROUPS):
        hit = _first_max_onehot(gv, g3, N_GROUPS, 0)
        gsel = jnp.where(hit, 1.0, gsel)
        gv = jnp.where(hit, neg, gv)

    e3 = lax.broadcasted_iota(jnp.int32, (N_GROUPS, ge, t), 0) * ge + j3
    ev = jnp.where(gsel > 0.5, sb3, neg)
    sel = jnp.zeros((N_GROUPS, ge, t), F32)
    chosen = jnp.zeros((N_GROUPS, ge, t), F32)
    hits = []
    for _ in range(TOP_K):
        hit = _first_max_onehot(ev, e3, e, (0, 1))
        hits.append(hit)
        sel = jnp.where(hit, s3, sel)
        chosen = jnp.where(hit, 1.0, chosen)
        ev = jnp.where(hit, neg, ev)
    tot = jnp.sum(sel, axis=(0, 1), keepdims=True)
    gates = sel / tot * ROUTED_SCALE

    chosen2 = chosen.reshape(e, t)
    rank = _dot(chosen2.astype(BF16), before_ref[...])
    cnt = jnp.sum(chosen2, axis=1, keepdims=True)
    ovf_units = jnp.floor((jnp.maximum(cnt - SLOT_ROWS, 0.0) + (DMA_ROWS - 1)) / DMA_ROWS)
    ovf_off = _dot(below_ref[...], jnp.broadcast_to(ovf_units, (e, LANES)).astype(BF16))[:, 0:1]
    e_col = lax.broadcasted_iota(jnp.int32, (e, 1), 0).astype(F32)
    slot = jnp.where(rank < SLOT_ROWS, e_col * SLOT_ROWS + rank,
                     e * SLOT_ROWS + ovf_off * DMA_ROWS + (rank - SLOT_ROWS))
    slot3 = slot.reshape(N_GROUPS, ge, t)
    for k, hit in enumerate(hits):
        pos_k = jnp.sum(jnp.where(hit, slot3, 0.0), axis=(0, 1), keepdims=True)
        w_k = jnp.sum(jnp.where(hit, gates, 0.0), axis=(0, 1), keepdims=True)
        pos_ref[k:k + 1, :] = pos_k.reshape(1, t).astype(jnp.int32)
        wts_ref[k:k + 1, :] = w_k.reshape(1, t)
    cnt_ref[0] = jnp.broadcast_to(cnt, (e, LANES)).astype(jnp.int32)
    rank_ref[...] = jnp.where(chosen2 > 0.5, rank, -1.0).astype(jnp.int32)
    gates_ref[...] = gates.reshape(e, t)


def _route(xp2d, xs2d, modp, modt, g2, wr_t, br):
    tp, d = xp2d.shape
    ts = xs2d.shape[0]
    tb = TOKEN_BLOCK
    npb, nsb = tp // tb, ts // tb
    e = wr_t.shape[0]
    seq = tp // modp.shape[0]
    bpb = seq // tb

    def p_idx(i):
        return jnp.minimum(i, npb - 1)

    def s_idx(i):
        return jnp.maximum(i - npb, 0)

    t = tp + ts
    nb = npb + nsb
    tok = jnp.arange(tb)
    before = (tok[:, None] < tok[None, :]).astype(BF16)
    ex = jnp.arange(e)
    below = (ex[None, :] < ex[:, None]).astype(BF16)
    return pl.pallas_call(
        functools.partial(_route_kernel, npb),
        out_shape=(jax.ShapeDtypeStruct((t, d), BF16),
                   jax.ShapeDtypeStruct((TOP_K, t), jnp.int32),
                   jax.ShapeDtypeStruct((TOP_K, t), F32),
                   jax.ShapeDtypeStruct((nb, e, LANES), jnp.int32),
                   jax.ShapeDtypeStruct((e, t), jnp.int32),
                   jax.ShapeDtypeStruct((e, t), F32)),
        grid=(nb,),
        in_specs=[
            pl.BlockSpec((tb, d), lambda i: (p_idx(i), 0)),
            pl.BlockSpec((tb, d), lambda i: (s_idx(i), 0)),
            pl.BlockSpec((1, N_MOD, d), lambda i: (p_idx(i) // bpb, 0, 0)),
            pl.BlockSpec((N_MOD, tb, d), lambda i: (0, s_idx(i), 0)),
            _const_spec(g2.shape), _const_spec(wr_t.shape), _const_spec(br.shape),
            _const_spec(before.shape), _const_spec(below.shape),
        ],
        out_specs=(pl.BlockSpec((tb, d), lambda i: (i, 0)),
                   pl.BlockSpec((TOP_K, tb), lambda i: (0, i)),
                   pl.BlockSpec((TOP_K, tb), lambda i: (0, i)),
                   pl.BlockSpec((1, e, LANES), lambda i: (i, 0, 0)),
                   pl.BlockSpec((e, tb), lambda i: (0, i)),
                   pl.BlockSpec((e, tb), lambda i: (0, i))),
        compiler_params=pltpu.CompilerParams(
            dimension_semantics=("arbitrary",),
            vmem_limit_bytes=_vmem_limit(32 * 1024 * 1024),
        ),
        name="route",
    )(xp2d, xs2d, modp, modt, g2, wr_t, br, before, below)


class _Layout:
    def __init__(self, counts):
        nb, ne = counts.shape
        u, g, tm = SLOT_ROWS, DMA_ROWS, EXPERT_TILE
        pc = (counts + g - 1) // g * g
        ovf_units = (jnp.maximum(counts - u, 0) + g - 1) // g
        rows_e = jnp.sum(pc, axis=0)
        region = (rows_e + u + tm - 1) // tm * tm
        goff = jnp.cumsum(region) - region
        dst = goff[None, :] + jnp.cumsum(pc, axis=0) - pc
        self.dst = dst.reshape(-1)
        ovf_end = jnp.cumsum(ovf_units, axis=1)
        ovf_first = ovf_end - ovf_units
        unit = jnp.arange(_overflow_units_cap(ne))[None, :, None]
        owns = (ovf_first[:, None, :] <= unit) & (unit < ovf_end[:, None, :])
        row = dst[:, None, :] + u + (unit - ovf_first[:, None, :]) * g
        self.ovf_dst = jnp.sum(jnp.where(owns, row, 0), axis=2).reshape(-1)
        self.ovf_tot = ovf_end[:, -1]
        self.tail_start = goff + rows_e
        self.tail_units = (region - rows_e) // g
        tiles = region // tm
        self.region_tiles = jnp.concatenate([tiles, jnp.sum(tiles, keepdims=True)])


def _overflow_units_cap(ne):
    return TOKEN_BLOCK * TOP_K // DMA_ROWS + ne


def _block_buffer_rows(tb, ne):
    rows = ne * SLOT_ROWS + tb * TOP_K + ne * (DMA_ROWS - 1)
    return (rows + ONEHOT_ROWS - 1) // ONEHOT_ROWS * ONEHOT_ROWS


def _expert_rows_cap(t, tb, ne):
    nb = t // tb
    rows = t * TOP_K + nb * ne * (DMA_ROWS - 1) + ne * (SLOT_ROWS + EXPERT_TILE - 1)
    return (rows + EXPERT_TILE - 1) // EXPERT_TILE * EXPERT_TILE


def _slot_copy(buf_ref, slot, e, hbm_ref, row, sem, to_hbm):
    src = buf_ref.at[slot, pl.ds(pl.multiple_of(e * SLOT_ROWS, DMA_ROWS), SLOT_ROWS)]
    dst = hbm_ref.at[pl.ds(pl.multiple_of(row, DMA_ROWS), SLOT_ROWS)]
    return pltpu.make_async_copy(src, dst, sem) if to_hbm else pltpu.make_async_copy(dst, src, sem)


def _unit_copy(buf_ref, slot, buf_row, hbm_ref, row, sem, to_hbm):
    src = buf_ref.at[slot, pl.ds(pl.multiple_of(buf_row, DMA_ROWS), DMA_ROWS)]
    dst = hbm_ref.at[pl.ds(pl.multiple_of(row, DMA_ROWS), DMA_ROWS)]
    return pltpu.make_async_copy(src, dst, sem) if to_hbm else pltpu.make_async_copy(dst, src, sem)


def _slots_wait(ne, buf_ref, slot, hbm_ref, sem, to_hbm):
    src = buf_ref.at[slot, pl.ds(0, ne * SLOT_ROWS)]
    dst = hbm_ref.at[pl.ds(0, ne * SLOT_ROWS)]
    c = pltpu.make_async_copy(src, dst, sem) if to_hbm else pltpu.make_async_copy(dst, src, sem)
    c.wait()


def _overflow_copies(ne, blk, slot, odst_ref, otot_ref, buf_ref, hbm_ref, sem, to_hbm, wait):
    def per_unit(i, carry):
        row = odst_ref[blk * _overflow_units_cap(ne) + i]
        c = _unit_copy(buf_ref, slot, ne * SLOT_ROWS + i * DMA_ROWS, hbm_ref, row, sem, to_hbm)
        c.wait() if wait else c.start()
        return carry

    lax.fori_loop(0, otot_ref[blk], per_unit, 0)


def _slot_chunks(ne):
    return [range(c, c + SLOT_CHUNK_EXPERTS) for c in range(0, ne, SLOT_CHUNK_EXPERTS)]


def _overflow_chunks(n_units):
    return lax.shift_right_logical(n_units * DMA_ROWS + (ONEHOT_ROWS - 1),
                                   ONEHOT_ROWS.bit_length() - 1)


def _overflow_select(pos, vals, r0):
    rows = lax.broadcasted_iota(jnp.int32, (ONEHOT_ROWS, pos.shape[1]), 0) + r0
    sel = jnp.zeros(rows.shape, F32)
    for k in range(TOP_K):
        v = 1.0 if vals is None else vals[k:k + 1, :]
        sel = jnp.where(pos[k:k + 1, :] == rows, v, sel)
    return sel


def _dispatch_kernel(ne, dst_ref, odst_ref, otot_ref, tails_ref, tailu_ref,
                     h2_ref, rank_ref, pos_ref, xs_hbm, buf_ref, zero_ref, sem, zsem):
    b = pl.program_id(0)
    nb = pl.num_programs(0)
    slot = b % 2
    h2 = h2_ref[...]
    j = lax.broadcasted_iota(jnp.int32, (SLOT_ROWS, h2.shape[0]), 0)

    for ci, experts in enumerate(_slot_chunks(ne)):
        onehot = jnp.concatenate(
            [jnp.where(rank_ref[e:e + 1, :] == j, 1.0, 0.0) for e in experts], axis=0)
        r0 = experts[0] * SLOT_ROWS
        buf_ref[slot, r0:r0 + len(experts) * SLOT_ROWS, :] = (
            _dot(onehot.astype(BF16), h2).astype(BF16))
        if ci == 0:
            @pl.when(b > 0)
            def _prev():
                _slots_wait(ne, buf_ref, 1 - slot, xs_hbm, sem.at[1 - slot], True)
                _overflow_copies(ne, b - 1, 1 - slot, odst_ref, otot_ref, buf_ref, xs_hbm,
                                 sem.at[1 - slot], True, True)
        for e in experts:
            _slot_copy(buf_ref, slot, e, xs_hbm, dst_ref[b * ne + e], sem.at[slot], True).start()

    pos = pos_ref[...]

    def chunk(ci, carry):
        r0 = pl.multiple_of(ne * SLOT_ROWS + ci * ONEHOT_ROWS, ONEHOT_ROWS)
        onehot = _overflow_select(pos, None, r0).astype(BF16)
        buf_ref[slot, pl.ds(r0, ONEHOT_ROWS), :] = _dot(onehot, h2).astype(BF16)
        return carry

    lax.fori_loop(0, _overflow_chunks(otot_ref[b]), chunk, 0)
    _overflow_copies(ne, b, slot, odst_ref, otot_ref, buf_ref, xs_hbm, sem.at[slot], True, False)

    @pl.when(b == nb - 1)
    def _last():
        _slots_wait(ne, buf_ref, slot, xs_hbm, sem.at[slot], True)
        _overflow_copies(ne, b, slot, odst_ref, otot_ref, buf_ref, xs_hbm, sem.at[slot], True,
                         True)
        zero_ref[...] = jnp.zeros_like(zero_ref)

        def tail(wait):
            def per_expert(e, carry):
                def per_unit(i, carry2):
                    row = pl.multiple_of(tails_ref[e] + i * DMA_ROWS, DMA_ROWS)
                    c = pltpu.make_async_copy(zero_ref, xs_hbm.at[pl.ds(row, DMA_ROWS)], zsem)
                    c.wait() if wait else c.start()
                    return carry2
                lax.fori_loop(0, tailu_ref[e], per_unit, 0)
                return carry
            lax.fori_loop(0, ne, per_expert, 0)

        tail(False)
        tail(True)


def _dispatch(h2, rank, pos, lay, ne, rows_cap):
    t, d = h2.shape
    tb = TOKEN_BLOCK
    buf_rows = _block_buffer_rows(tb, ne)
    return pl.pallas_call(
        functools.partial(_dispatch_kernel, ne),
        out_shape=jax.ShapeDtypeStruct((rows_cap, d), BF16),
        grid_spec=pltpu.PrefetchScalarGridSpec(
            num_scalar_prefetch=5,
            grid=(t // tb,),
            in_specs=[
                pl.BlockSpec((tb, d), lambda i, *_: (i, 0)),
                pl.BlockSpec((ne, tb), lambda i, *_: (0, i)),
                pl.BlockSpec((TOP_K, tb), lambda i, *_: (0, i)),
            ],
            out_specs=pl.BlockSpec(memory_space=pl.ANY),
            scratch_shapes=[
                pltpu.VMEM((2, buf_rows, d), BF16),
                pltpu.VMEM((DMA_ROWS, d), BF16),
                pltpu.SemaphoreType.DMA((2,)),
                pltpu.SemaphoreType.DMA(()),
            ],
        ),
        compiler_params=pltpu.CompilerParams(
            dimension_semantics=("arbitrary",),
            vmem_limit_bytes=_vmem_limit(4 * buf_rows * d + 16 * 1024 * 1024),
        ),
        name="dispatch",
    )(lay.dst, lay.ovf_dst, lay.ovf_tot, lay.tail_start, lay.tail_units, h2, rank, pos)


def _experts_kernel(tiles_ref, xe_hbm, wg_ref, wu_ref, wd_ref, ye_hbm,
                    xbuf_ref, ybuf_ref, wgb_ref, wub_ref, wdb_ref, done_ref, xsem, ysem):
    e = pl.program_id(0)
    ne = pl.num_programs(0)
    tm = xbuf_ref.shape[1]

    def x_copy(row, slot):
        return pltpu.make_async_copy(xe_hbm.at[pl.ds(pl.multiple_of(row, tm), tm)],
                                     xbuf_ref.at[slot], xsem.at[slot])

    def y_copy(row, slot):
        return pltpu.make_async_copy(ybuf_ref.at[slot],
                                     ye_hbm.at[pl.ds(pl.multiple_of(row, tm), tm)], ysem.at[slot])

    total = tiles_ref[ne]

    @pl.when(e == 0)
    def _first():
        done_ref[0] = 0
        for g in range(RING_SLOTS - 1):
            x_copy(g * tm, g).start()

    wgb_ref[...] = wg_ref[0].astype(BF16)
    wub_ref[...] = wu_ref[0].astype(BF16)
    wdb_ref[...] = wd_ref[0].astype(BF16)
    done = done_ref[0]
    n = tiles_ref[e]

    def tile(j, carry):
        g = done + j
        slot = lax.rem(g, RING_SLOTS)
        x_copy(0, slot).wait()
        ahead = g + (RING_SLOTS - 1)

        @pl.when(ahead < total)
        def _prefetch():
            x_copy(ahead * tm, lax.rem(ahead, RING_SLOTS)).start()

        x = xbuf_ref[slot]
        hid = jax.nn.silu(_dot(x, wgb_ref[...])) * _dot(x, wub_ref[...])
        y = _dot(hid.astype(BF16), wdb_ref[...]).astype(BF16)

        @pl.when(g >= RING_SLOTS)
        def _reuse():
            y_copy(0, slot).wait()

        ybuf_ref[slot] = y
        y_copy(g * tm, slot).start()
        return carry

    lax.fori_loop(0, n, tile, 0)
    done_ref[0] = done + n

    @pl.when(e == ne - 1)
    def _drain():
        for s in range(RING_SLOTS):
            y_copy(0, s).wait()


def _experts(xs, lay, wg, wu, wd):
    rows_cap, d = xs.shape
    ne, _, de = wg.shape
    assert ne >= RING_SLOTS
    tm = EXPERT_TILE
    ring = RING_SLOTS
    return pl.pallas_call(
        _experts_kernel,
        out_shape=jax.ShapeDtypeStruct((rows_cap, d), BF16),
        grid_spec=pltpu.PrefetchScalarGridSpec(
            num_scalar_prefetch=1,
            grid=(ne,),
            in_specs=[
                pl.BlockSpec(memory_space=pl.ANY),
                pl.BlockSpec((1, d, de), lambda e, *_: (e, 0, 0)),
                pl.BlockSpec((1, d, de), lambda e, *_: (e, 0, 0)),
                pl.BlockSpec((1, de, d), lambda e, *_: (e, 0, 0)),
            ],
            out_specs=pl.BlockSpec(memory_space=pl.ANY),
            scratch_shapes=[
                pltpu.VMEM((ring, tm, d), BF16), pltpu.VMEM((ring, tm, d), BF16),
                pltpu.VMEM((d, de), BF16), pltpu.VMEM((d, de), BF16), pltpu.VMEM((de, d), BF16),
                pltpu.SMEM((1,), jnp.int32),
                pltpu.SemaphoreType.DMA((ring,)), pltpu.SemaphoreType.DMA((ring,)),
            ],
        ),
        compiler_params=pltpu.CompilerParams(
            dimension_semantics=("arbitrary",),
            vmem_limit_bytes=_vmem_limit(32 * 1024 * 1024),
        ),
        name="experts",
    )(lay.region_tiles, xs, wg, wu, wd)


def _final_kernel(n_prompt_blocks, ne, dst_ref, odst_ref, otot_ref,
                  xp_ref, xs_ref, modp_ref, modt_ref, h2_ref, rank_ref, gates_ref, pos_ref, wts_ref,
                  wsg_ref, wsu_ref, wsd_ref, gf_ref, ye_hbm, yp_ref, ys_ref,
                  buf_ref, acc_ref, sem):
    i = pl.program_id(0)
    nb = pl.num_programs(0)
    slot = i % 2

    def fetch(blk, sl):
        for e in range(ne):
            _slot_copy(buf_ref, sl, e, ye_hbm, dst_ref[blk * ne + e], sem.at[sl], False).start()
        _overflow_copies(ne, blk, sl, odst_ref, otot_ref, buf_ref, ye_hbm, sem.at[sl], False, False)

    @pl.when(i == 0)
    def _first():
        buf_ref[...] = jnp.zeros_like(buf_ref)
        fetch(0, 0)

    @pl.when(i + 1 < nb)
    def _prefetch():
        fetch(i + 1, 1 - slot)

    _slots_wait(ne, buf_ref, slot, ye_hbm, sem.at[slot], False)
    _overflow_copies(ne, i, slot, odst_ref, otot_ref, buf_ref, ye_hbm, sem.at[slot], False, True)

    tb = h2_ref.shape[0]
    j = lax.broadcasted_iota(jnp.int32, (SLOT_ROWS, tb), 0)
    acc = jnp.zeros((tb, h2_ref.shape[1]), F32)
    for experts in _slot_chunks(ne):
        sel = jnp.concatenate(
            [jnp.where(rank_ref[e:e + 1, :] == j, gates_ref[e:e + 1, :], 0.0) for e in experts],
            axis=0)
        r0 = experts[0] * SLOT_ROWS
        acc = acc + _dot_tn(sel.astype(BF16), buf_ref[slot, r0:r0 + len(experts) * SLOT_ROWS, :])
    acc_ref[...] = acc

    pos = pos_ref[...]
    wts = wts_ref[...]

    def chunk(ci, carry):
        r0 = pl.multiple_of(ne * SLOT_ROWS + ci * ONEHOT_ROWS, ONEHOT_ROWS)
        sel = _overflow_select(pos, wts, r0).astype(BF16)
        acc_ref[...] += _dot_tn(sel, buf_ref[slot, pl.ds(r0, ONEHOT_ROWS), :])
        return carry

    lax.fori_loop(0, _overflow_chunks(otot_ref[i]), chunk, 0)

    is_p = i < n_prompt_blocks
    x = jnp.where(is_p, xp_ref[...], xs_ref[...])
    gate = jnp.where(is_p, modp_ref[0, 5:6, :], modt_ref[5])
    hb = h2_ref[...]
    hid = jax.nn.silu(_dot(hb, wsg_ref[...])) * _dot(hb, wsu_ref[...])
    f = acc_ref[...] + _dot(hid.astype(BF16), wsd_ref[...])
    x2 = x + gate * f
    y = x2 * lax.rsqrt(jnp.mean(x2 * x2, axis=-1, keepdims=True) + EPS) * gf_ref[...]

    @pl.when(is_p)
    def _p():
        yp_ref[...] = y

    @pl.when(jnp.logical_not(is_p))
    def _s():
        ys_ref[...] = y


def _final(xp2d, xs2d, modp, modt, h2, rank, gates, pos, wts, ye, lay, ne, wsg, wsu, wsd, gf):
    tp, d = xp2d.shape
    ts = xs2d.shape[0]
    tb = TOKEN_BLOCK
    npb, nsb = tp // tb, ts // tb
    seq = tp // modp.shape[0]
    bpb = seq // tb
    buf_rows = _block_buffer_rows(tb, ne)

    def p_idx(i):
        return jnp.minimum(i, npb - 1)

    def s_idx(i):
        return jnp.maximum(i - npb, 0)

    def const(shape):
        zeros = (0,) * len(shape)
        return pl.BlockSpec(shape, lambda i, *_: zeros)

    return pl.pallas_call(
        functools.partial(_final_kernel, npb, ne),
        out_shape=(jax.ShapeDtypeStruct((tp, d), F32), jax.ShapeDtypeStruct((ts, d), F32)),
        grid_spec=pltpu.PrefetchScalarGridSpec(
            num_scalar_prefetch=3,
            grid=(npb + nsb,),
            in_specs=[
                pl.BlockSpec((tb, d), lambda i, *_: (p_idx(i), 0)),
                pl.BlockSpec((tb, d), lambda i, *_: (s_idx(i), 0)),
                pl.BlockSpec((1, N_MOD, d), lambda i, *_: (p_idx(i) // bpb, 0, 0)),
                pl.BlockSpec((N_MOD, tb, d), lambda i, *_: (0, s_idx(i), 0)),
                pl.BlockSpec((tb, d), lambda i, *_: (i, 0)),
                pl.BlockSpec((ne, tb), lambda i, *_: (0, i)),
                pl.BlockSpec((ne, tb), lambda i, *_: (0, i)),
                pl.BlockSpec((TOP_K, tb), lambda i, *_: (0, i)),
                pl.BlockSpec((TOP_K, tb), lambda i, *_: (0, i)),
                const(wsg.shape), const(wsu.shape), const(wsd.shape), const(gf.shape),
                pl.BlockSpec(memory_space=pl.ANY),
            ],
            out_specs=(pl.BlockSpec((tb, d), lambda i, *_: (p_idx(i), 0)),
                       pl.BlockSpec((tb, d), lambda i, *_: (s_idx(i), 0))),
            scratch_shapes=[
                pltpu.VMEM((2, buf_rows, d), BF16),
                pltpu.VMEM((tb, d), F32),
                pltpu.SemaphoreType.DMA((2,)),
            ],
        ),
        compiler_params=pltpu.CompilerParams(
            dimension_semantics=("arbitrary",),
            vmem_limit_bytes=_vmem_limit(4 * buf_rows * d + 20 * 1024 * 1024),
        ),
        name="final",
    )(lay.dst, lay.ovf_dst, lay.ovf_tot,
      xp2d, xs2d, modp, modt, h2, rank, gates, pos, wts, wsg, wsu, wsd, gf, ye)


def kernel(x_prompt, x_sample, c_prompt, c_sample, state_pool, state_ret, w_ada, b_ada, g_norm1,
           w_in, w_pool_group, pool_scale, w_branch_pool, w_branch_ret, w_out, g_norm2, w_router,
           b_router, w_exp_gate, w_exp_up, w_exp_down, w_sh_gate, w_sh_up, w_sh_down, g_final):
    b, s, d = x_prompt.shape
    db, ds, _ = x_sample.shape
    depth = w_ada.shape[0]
    heads, dk, dv = state_ret.shape[2:]
    pool_buf, d_pool = state_pool.shape[2:]
    dims = _Dims(d, d_pool, heads, dk, dv)
    assert dims.d_in == w_in.shape[2]
    assert pool_buf == max(POOL_WINDOWS) - 1 and pool_buf < POOL_PREV
    assert s % RET_CHUNK == 0 and s % TOKEN_BLOCK == 0 and (db * ds) % TOKEN_BLOCK == 0
    assert db % SAMPLE_SEQS_PER_STEP == 0 and SLOT_ROWS % DMA_ROWS == 0

    log_g = jnp.log(1.0 - jnp.power(2.0, -5.0 - jnp.arange(heads, dtype=F32)))
    ns = SAMPLE_SEQS_PER_STEP
    cos_p, sin_p = _rotary_tables(jnp.arange(s, dtype=jnp.int32), dk)
    cos_s, sin_s = _rotary_tables(PAST_LEN + jnp.arange(ds, dtype=jnp.int32), dk)
    cos_s, sin_s = jnp.tile(cos_s, (ns, 1)), jnp.tile(sin_s, (ns, 1))
    dec_p = _decay_tables(log_g, RET_CHUNK, 1, dk, dv)
    dec_s = _decay_tables(log_g, ds, ns, dk, dv)
    pool_tabs = _sample_pool_tables(ns, ds, dims.gd, PAST_LEN)

    assert depth == 1, "the final kernel fuses the output norm into the (single) layer"
    l = 0
    xs = x_sample.reshape(db * ds, d)
    modp, modt = _ada(c_prompt, jnp.repeat(c_sample, ds, axis=0), w_ada[l], b_ada[l])
    modp = modp.reshape(b, N_MOD, d)
    g1 = g_norm1[l].reshape(1, d)
    weights = (w_in[l].astype(BF16), w_pool_group[l].astype(BF16),
               pool_scale[l].reshape(1, d_pool), w_branch_pool[l].astype(BF16),
               w_branch_ret[l].astype(BF16), w_out[l].astype(BF16))

    x1p, pool16, ret_p = _mixer_prompt(dims, x_prompt, modp, g1, weights,
                                       (cos_p, sin_p) + dec_p)
    pool_in = jnp.pad(state_pool[l], ((0, 0), (POOL_PREV - pool_buf, 0), (0, 0)))
    x1s, u_s, ret_s = _mixer_sample(dims, ds, xs, modt, g1, weights,
                                    (cos_s, sin_s) + dec_s + pool_tabs, pool_in,
                                    state_ret[l].astype(F32))
    pool_p = pool16[:, POOL_PREV - pool_buf:]
    pool_s = jnp.concatenate([state_pool[l], u_s.reshape(db, ds, d_pool)], axis=1)[:, -pool_buf:]

    x1p2d = x1p.reshape(b * s, d)
    ne = w_router.shape[2]
    h2, pos, wts, counts, rank, gates = _route(
        x1p2d, x1s, modp, modt, g_norm2[l].reshape(1, d), w_router[l].T,
        b_router[l].reshape(N_GROUPS, ne // N_GROUPS, 1))
    rows_cap = _expert_rows_cap(b * s + db * ds, TOKEN_BLOCK, ne)
    lay = _Layout(counts[:, :, 0])
    xe = _dispatch(h2, rank, pos, lay, ne, rows_cap)
    ye = _experts(xe, lay, w_exp_gate[l], w_exp_up[l], w_exp_down[l])
    yp, ys = _final(x1p2d, x1s, modp, modt, h2, rank, gates, pos, wts, ye, lay, ne,
                    w_sh_gate[l].astype(BF16), w_sh_up[l].astype(BF16),
                    w_sh_down[l].astype(BF16), g_final.reshape(1, d))

    return (yp.reshape(b, s, d), ys.reshape(db, ds, d), pool_p[None],
            ret_p.astype(x_prompt.dtype), pool_s[None], ret_s.astype(state_ret.dtype)[None])
```

```python
import functools

import jax
import jax.numpy as jnp
from jax import lax
from jax.experimental import pallas as pl
from jax.experimental.pallas import tpu as pltpu

F32 = jnp.float32
BF16 = jnp.bfloat16

PAST_LEN = 16384
POOL_WINDOWS = (2, 4, 8, 16)
RET_CHUNK = 128
ROPE_BASE = 10000.0
TOP_K = 8
N_GROUPS = 8
TOPK_GROUPS = 4
ROUTED_SCALE = 2.5
N_MOD = 6
EPS = 1e-6

V7X_VMEM_BYTES = 64 * 1024 * 1024
SUBLANES = 8
LANES = 128

PROMPT_STEP_ROWS = 512
SAMPLE_SEQS_PER_STEP = 8
TOKEN_BLOCK = 256
DMA_ROWS = 16
SLOT_ROWS = 3 * DMA_ROWS
ONEHOT_ROWS = 256
SLOT_CHUNK_EXPERTS = 16
EXPERT_TILE_PARTS = 2
RING_SLOTS = 3
EXPERT_TILE = 512
POOL_PREV = 16


def _vmem_limit(nbytes):
    return int(min(nbytes, V7X_VMEM_BYTES - 8 * 1024 * 1024))


def _dot(a, b):
    return jnp.dot(a, b, preferred_element_type=F32)


def _dot_nt(a, b):
    return lax.dot_general(a, b, (((1,), (1,)), ((), ())), preferred_element_type=F32)


def _dot_tn(a, b):
    return lax.dot_general(a, b, (((0,), (0,)), ((), ())), preferred_element_type=F32)


def _split_bf16(x, parts):
    out = []
    for _ in range(parts):
        p = x.astype(BF16)
        out.append(p)
        x = x - p.astype(F32)
    return out


def _rms_mod(x, g, shift, scale):
    y = x * lax.rsqrt(jnp.mean(x * x, axis=-1, keepdims=True) + EPS)
    return (y * g) * (1.0 + scale) + shift


def _rotary(x, cosb, sinb):
    return x * cosb + pltpu.roll(x, x.shape[-1] // 2, axis=1) * sinb


def _group_norm(o):
    return o * lax.rsqrt(jnp.mean(o * o, axis=-1, keepdims=True) + EPS)


def _ada_kernel(cp_ref, ct_ref, w_ref, b_ref, modp_ref, modt_ref):
    w = w_ref[...].astype(BF16)
    modp_ref[...] = _dot(cp_ref[...].astype(BF16), w) + b_ref[...]
    modt_ref[0] = _dot(ct_ref[...].astype(BF16), w) + b_ref[...]


def _ada(c_prompt, c_tokens, w_ada, b_ada):
    bp, d = c_prompt.shape
    ts = c_tokens.shape[0]
    n = w_ada.shape[1]
    return pl.pallas_call(
        _ada_kernel,
        out_shape=(jax.ShapeDtypeStruct((bp, n), F32), jax.ShapeDtypeStruct((n // d, ts, d), F32)),
        grid=(n // d,),
        in_specs=[
            pl.BlockSpec((bp, d), lambda j: (0, 0)),
            pl.BlockSpec((ts, d), lambda j: (0, 0)),
            pl.BlockSpec((d, d), lambda j: (0, j)),
            pl.BlockSpec((1, d), lambda j: (0, j)),
        ],
        out_specs=(pl.BlockSpec((bp, d), lambda j: (0, j)),
                   pl.BlockSpec((1, ts, d), lambda j: (j, 0, 0))),
        compiler_params=pltpu.CompilerParams(
            dimension_semantics=("arbitrary",),
            vmem_limit_bytes=_vmem_limit(32 * 1024 * 1024),
        ),
        name="ada",
    )(c_prompt, c_tokens, w_ada, b_ada.reshape(1, n))


class _Dims:
    def __init__(self, d_model, d_pool, heads, dk, dv):
        self.d = d_model
        self.d_pool = d_pool
        self.gd = d_pool // len(POOL_WINDOWS)
        self.heads = heads
        self.dk = dk
        self.dv = dv
        self.d_k = heads * dk
        self.d_v = heads * dv
        widths = (d_pool, self.d_k, self.d_k, self.d_v, self.d_v, d_model, d_model)
        offs = [0]
        for w in widths:
            offs.append(offs[-1] + w)
        self.cols = tuple((offs[i], offs[i + 1]) for i in range(len(widths)))
        self.d_in = offs[-1]


def _proj(hb, win_ref, cols):
    return _dot(hb, win_ref[:, cols[0]:cols[1]])


def _pool_branch(pm_parts, wpg_ref, ps_ref, wbp_ref):
    pg = [_dot(pm.astype(BF16), wpg_ref[gi]) for gi, pm in enumerate(pm_parts)]
    pm = jnp.concatenate(pg, axis=1) * ps_ref[...]
    return _dot(pm.astype(BF16), wbp_ref[...])


def _merge_out(x, gate, a, o, g, ga, gb, wbr_ref, wout_ref):
    r = _dot((jax.nn.silu(g) * o).astype(BF16), wbr_ref[...])
    merged = jax.nn.sigmoid(ga) * a + jax.nn.sigmoid(gb) * r
    m = _dot(merged.astype(BF16), wout_ref[...])
    return x + gate * m


def _mixer_prompt_kernel(dims, x_ref, mod_ref, g1_ref, win_ref, wpg_ref, ps_ref, wbp_ref,
                         wbr_ref, wout_ref, cos_ref, sin_ref, intra_ref, qdec_ref,
                         kdec_ref, sdec_ref, x1_ref, pool_ref, ret_ref, uext_ref):
    c = pl.program_id(1)
    L = x_ref.shape[1]

    @pl.when(c == 0)
    def _init():
        ret_ref[...] = jnp.zeros_like(ret_ref)
        uext_ref[0:POOL_PREV, :] = jnp.zeros((POOL_PREV, dims.d_pool), F32)

    x = x_ref[0]
    shift, scale, gate = mod_ref[0, 0:1, :], mod_ref[0, 1:2, :], mod_ref[0, 2:3, :]
    hb = _rms_mod(x, g1_ref[...], shift, scale).astype(BF16)

    u = _proj(hb, win_ref, dims.cols[0])
    uext_ref[POOL_PREV:POOL_PREV + L, :] = u
    pos = c * L + lax.broadcasted_iota(jnp.int32, (L, 1), 0)
    pm_parts = []
    for gi, w in enumerate(POOL_WINDOWS):
        s = uext_ref[:, gi * dims.gd:(gi + 1) * dims.gd]
        sh = 1
        while sh < w:
            s = s + pltpu.roll(s, sh, axis=0)
            sh *= 2
        cnt = jnp.minimum(w, pos + 1).astype(F32)
        pm_parts.append(s[POOL_PREV:, :] / cnt - u[:, gi * dims.gd:(gi + 1) * dims.gd])
    a = _pool_branch(pm_parts, wpg_ref, ps_ref, wbp_ref)
    uext_ref[0:POOL_PREV, :] = uext_ref[L:L + POOL_PREV, :]
    pool_ref[0] = uext_ref[0:POOL_PREV, :]

    q = _proj(hb, win_ref, dims.cols[1])
    k = _proj(hb, win_ref, dims.cols[2])
    v = _proj(hb, win_ref, dims.cols[3])
    chunk_outs = []
    for r0 in range(0, L, RET_CHUNK):
        rows = slice(r0, r0 + RET_CHUNK)
        cosb, sinb = cos_ref[rows, :], sin_ref[rows, :]
        outs = []
        for h in range(dims.heads):
            qr = _rotary(q[rows, h * dims.dk:(h + 1) * dims.dk], cosb, sinb)
            kr = _rotary(k[rows, h * dims.dk:(h + 1) * dims.dk], cosb, sinb) * (dims.dk ** -0.5)
            vb = v[rows, h * dims.dv:(h + 1) * dims.dv].astype(BF16)
            scores = _dot_nt(qr.astype(BF16), kr.astype(BF16)) * intra_ref[h]
            s_prev = ret_ref[0, 0, h]
            o = _dot(scores.astype(BF16), vb) + _dot((qr * qdec_ref[h]).astype(BF16),
                                                      s_prev.astype(BF16))
            ret_ref[0, 0, h] = (sdec_ref[h] * s_prev
                                + _dot_tn((kr * kdec_ref[h]).astype(BF16), vb))
            outs.append(_group_norm(o))
        chunk_outs.append(jnp.concatenate(outs, axis=1))
    o = jnp.concatenate(chunk_outs, axis=0)

    g = _proj(hb, win_ref, dims.cols[4])
    ga = _proj(hb, win_ref, dims.cols[5])
    gb = _proj(hb, win_ref, dims.cols[6])
    x1_ref[0] = _merge_out(x, gate, a, o, g, ga, gb, wbr_ref, wout_ref)


def _const_spec(shape):
    zeros = (0,) * len(shape)
    return pl.BlockSpec(shape, lambda *_: zeros)


def _mixer_prompt(dims, x, mod, g1, weights, tables):
    b, s, d = x.shape
    L = PROMPT_STEP_ROWS
    nc = s // L
    win, wpg, ps, wbp, wbr, wout = weights
    cosb, sinb, intra, qdec, kdec, sdec = tables
    in_specs = [
        pl.BlockSpec((1, L, d), lambda i, c: (i, c, 0)),
        pl.BlockSpec((1, N_MOD, d), lambda i, c: (i, 0, 0)),
        _const_spec(g1.shape), _const_spec(win.shape), _const_spec(wpg.shape),
        _const_spec(ps.shape), _const_spec(wbp.shape), _const_spec(wbr.shape),
        _const_spec(wout.shape),
        pl.BlockSpec((L, dims.dk), lambda i, c: (c, 0)),
        pl.BlockSpec((L, dims.dk), lambda i, c: (c, 0)),
        _const_spec(intra.shape), _const_spec(qdec.shape), _const_spec(kdec.shape),
        _const_spec(sdec.shape),
    ]
    out_shape = (
        jax.ShapeDtypeStruct((b, s, d), F32),
        jax.ShapeDtypeStruct((b, POOL_PREV, dims.d_pool), F32),
        jax.ShapeDtypeStruct((1, b, dims.heads, dims.dk, dims.dv), F32),
    )
    out_specs = (
        pl.BlockSpec((1, L, d), lambda i, c: (i, c, 0)),
        pl.BlockSpec((1, POOL_PREV, dims.d_pool), lambda i, c: (i, 0, 0)),
        pl.BlockSpec((1, 1, dims.heads, dims.dk, dims.dv), lambda i, c: (0, i, 0, 0, 0)),
    )
    weight_bytes = 2 * (win.size + wpg.size + wbp.size + wbr.size + wout.size)
    return pl.pallas_call(
        functools.partial(_mixer_prompt_kernel, dims),
        out_shape=out_shape,
        grid=(b, nc),
        in_specs=in_specs,
        out_specs=out_specs,
        scratch_shapes=[pltpu.VMEM((POOL_PREV + L, dims.d_pool), F32)],
        compiler_params=pltpu.CompilerParams(
            dimension_semantics=("arbitrary", "arbitrary"),
            vmem_limit_bytes=_vmem_limit(2 * weight_bytes + 16 * 1024 * 1024),
        ),
        name="mixer_prompt",
    )(x, mod, g1, win, wpg, ps, wbp, wbr, wout, cosb, sinb, intra, qdec, kdec, sdec)


def _mixer_sample_kernel(dims, ds, x_ref, modt_ref, g1_ref, win_ref, wpg_ref, ps_ref, wbp_ref,
                         wbr_ref, wout_ref, cos_ref, sin_ref, intra_ref, qdec_ref, kdec_ref,
                         sdec_ref, bandp_ref, bandu_ref, cnt_ref, poolin_ref, sret_ref,
                         x1_ref, u_ref, ret_ref):
    R = x_ref.shape[0]
    ns = R // ds
    x = x_ref[...]
    shift, scale, gate = modt_ref[0], modt_ref[1], modt_ref[2]
    hb = _rms_mod(x, g1_ref[...], shift, scale).astype(BF16)

    u = _proj(hb, win_ref, dims.cols[0])
    u_ref[...] = u
    hist = poolin_ref[...].reshape(ns * POOL_PREV, dims.d_pool)
    pm_parts = []
    for gi in range(len(POOL_WINDOWS)):
        lo, hi = gi * dims.gd, (gi + 1) * dims.gd
        win_sum = jnp.zeros((R, dims.gd), F32)
        for part in _split_bf16(hist[:, lo:hi], 3):
            win_sum = win_sum + _dot(bandp_ref[gi], part)
        for part in _split_bf16(u[:, lo:hi], 3):
            win_sum = win_sum + _dot(bandu_ref[gi], part)
        pm_parts.append(win_sum / cnt_ref[gi] - u[:, lo:hi])
    a = _pool_branch(pm_parts, wpg_ref, ps_ref, wbp_ref)

    q = _proj(hb, win_ref, dims.cols[1])
    k = _proj(hb, win_ref, dims.cols[2])
    v = _proj(hb, win_ref, dims.cols[3])
    cosb, sinb = cos_ref[...], sin_ref[...]
    row_seq = lax.broadcasted_iota(jnp.int32, (R, ns * dims.dk), 0) // ds
    col_seq = lax.broadcasted_iota(jnp.int32, (R, ns * dims.dk), 1) // dims.dk
    own = row_seq == col_seq
    outs = []
    for h in range(dims.heads):
        qr = _rotary(q[:, h * dims.dk:(h + 1) * dims.dk], cosb, sinb)
        kr = _rotary(k[:, h * dims.dk:(h + 1) * dims.dk], cosb, sinb) * (dims.dk ** -0.5)
        vb = v[:, h * dims.dv:(h + 1) * dims.dv].astype(BF16)
        scores = _dot_nt(qr.astype(BF16), kr.astype(BF16)) * intra_ref[h]
        s_prev = sret_ref[:, h].reshape(ns * dims.dk, dims.dv)
        qd = jnp.concatenate([qr * qdec_ref[h]] * ns, axis=1)
        kd = jnp.concatenate([kr * kdec_ref[h]] * ns, axis=1)
        qexp = jnp.where(own, qd, 0.0).astype(BF16)
        kexp = jnp.where(own, kd, 0.0).astype(BF16)
        o = _dot(scores.astype(BF16), vb) + _dot(qexp, s_prev.astype(BF16))
        s_new = sdec_ref[h] * s_prev + _dot_tn(kexp, vb)
        ret_ref[:, h] = s_new.reshape(ns, dims.dk, dims.dv)
        outs.append(_group_norm(o))
    o = jnp.concatenate(outs, axis=1)

    g = _proj(hb, win_ref, dims.cols[4])
    ga = _proj(hb, win_ref, dims.cols[5])
    gb = _proj(hb, win_ref, dims.cols[6])
    x1_ref[...] = _merge_out(x, gate, a, o, g, ga, gb, wbr_ref, wout_ref)


def _mixer_sample(dims, ds, x2d, modt, g1, weights, tables, pool_in, state_ret):
    t, d = x2d.shape
    ns = SAMPLE_SEQS_PER_STEP
    R = ns * ds
    win, wpg, ps, wbp, wbr, wout = weights
    cosb, sinb, intra, qdec, kdec, sdec, bandp, bandu, cnt = tables
    db = state_ret.shape[0]
    in_specs = [
        pl.BlockSpec((R, d), lambda i: (i, 0)),
        pl.BlockSpec((N_MOD, R, d), lambda i: (0, i, 0)),
        _const_spec(g1.shape), _const_spec(win.shape), _const_spec(wpg.shape),
        _const_spec(ps.shape), _const_spec(wbp.shape), _const_spec(wbr.shape),
        _const_spec(wout.shape), _const_spec(cosb.shape), _const_spec(sinb.shape),
        _const_spec(intra.shape), _const_spec(qdec.shape), _const_spec(kdec.shape),
        _const_spec(sdec.shape), _const_spec(bandp.shape), _const_spec(bandu.shape),
        _const_spec(cnt.shape),
        pl.BlockSpec((ns, POOL_PREV, dims.d_pool), lambda i: (i, 0, 0)),
        pl.BlockSpec((ns, dims.heads, dims.dk, dims.dv), lambda i: (i, 0, 0, 0)),
    ]
    out_shape = (
        jax.ShapeDtypeStruct((t, d), F32),
        jax.ShapeDtypeStruct((t, dims.d_pool), F32),
        jax.ShapeDtypeStruct(state_ret.shape, state_ret.dtype),
    )
    out_specs = (
        pl.BlockSpec((R, d), lambda i: (i, 0)),
        pl.BlockSpec((R, dims.d_pool), lambda i: (i, 0)),
        pl.BlockSpec((ns, dims.heads, dims.dk, dims.dv), lambda i: (i, 0, 0, 0)),
    )
    weight_bytes = 2 * (win.size + wpg.size + wbp.size + wbr.size + wout.size)
    state_bytes = 4 * ns * dims.heads * dims.dk * dims.dv
    return pl.pallas_call(
        functools.partial(_mixer_sample_kernel, dims, ds),
        out_shape=out_shape,
        grid=(db // ns,),
        in_specs=in_specs,
        out_specs=out_specs,
        compiler_params=pltpu.CompilerParams(
            dimension_semantics=("arbitrary",),
            vmem_limit_bytes=_vmem_limit(2 * weight_bytes + 4 * state_bytes + 8 * 1024 * 1024),
        ),
        name="mixer_sample",
    )(x2d, modt, g1, win, wpg, ps, wbp, wbr, wout, cosb, sinb, intra, qdec, kdec, sdec,
      bandp, bandu, cnt, pool_in, state_ret)


def _rotary_tables(pos, dk):
    half = dk // 2
    inv = jnp.power(ROPE_BASE, -jnp.arange(half, dtype=F32) / half)
    ang = pos.astype(F32)[:, None] * inv[None, :]
    cos, sin = jnp.cos(ang), jnp.sin(ang)
    return jnp.concatenate([cos, cos], axis=1), jnp.concatenate([-sin, sin], axis=1)


def _decay_tables(log_g, chunk, reps, dk, dv):
    idx = jnp.arange(chunk, dtype=F32)
    diff = idx[:, None] - idx[None, :]
    intra = jnp.where((diff >= 0)[None],
                      jnp.exp(log_g[:, None, None] * jnp.maximum(diff, 0.0)[None]), 0.0)
    heads = log_g.shape[0]
    eye = jnp.eye(reps, dtype=F32)
    intra = jnp.einsum("ab,hij->haibj", eye, intra).reshape(heads, reps * chunk, reps * chunk)
    q_dec = jnp.exp(log_g[:, None] * (idx[None, :] + 1.0))
    k_dec = jnp.exp(log_g[:, None] * (chunk - 1.0 - idx[None, :]))
    s_dec = jnp.exp(log_g * chunk)
    q_dec = jnp.broadcast_to(jnp.tile(q_dec, (1, reps))[:, :, None], (heads, reps * chunk, dk))
    k_dec = jnp.broadcast_to(jnp.tile(k_dec, (1, reps))[:, :, None], (heads, reps * chunk, dk))
    s_dec = jnp.broadcast_to(s_dec[:, None, None], (heads, 1, dv))
    return intra, q_dec, k_dec, s_dec


def _sample_pool_tables(ns, ds, gd, start_pos):
    n = jnp.arange(ds)
    i = jnp.arange(POOL_PREV)
    eye = jnp.eye(ns, dtype=F32)
    bandp, bandu, cnt = [], [], []
    for w in POOL_WINDOWS:
        bp = ((i[None, :] - POOL_PREV) >= (n[:, None] - w + 1)).astype(F32)
        bu = ((n[None, :] <= n[:, None]) & (n[None, :] >= n[:, None] - w + 1)).astype(F32)
        bandp.append(jnp.einsum("ab,ni->anbi", eye, bp).reshape(ns * ds, ns * POOL_PREV))
        bandu.append(jnp.einsum("ab,nm->anbm", eye, bu).reshape(ns * ds, ns * ds))
        c = jnp.minimum(w, start_pos + n + 1).astype(F32)
        cnt.append(jnp.broadcast_to(jnp.tile(c, ns)[:, None], (ns * ds, gd)))
    return (jnp.stack(bandp).astype(BF16), jnp.stack(bandu).astype(BF16), jnp.stack(cnt))


def _first_max_onehot(v, idx, big, axes):
    m = jnp.max(v, axis=axes, keepdims=True)
    first = jnp.min(jnp.where(v == m, idx, big), axis=axes, keepdims=True)
    return idx == first


def _route_kernel(n_prompt_blocks, xp_ref, xs_ref, modp_ref, modt_ref, g2_ref, wr_ref, br_ref,
                  before_ref, below_ref, h2_ref, pos_ref, wts_ref, cnt_ref, rank_ref, gates_ref):
    i = pl.program_id(0)
    is_p = i < n_prompt_blocks
    x = jnp.where(is_p, xp_ref[...], xs_ref[...])
    shift = jnp.where(is_p, modp_ref[0, 3:4, :], modt_ref[3])
    scale = jnp.where(is_p, modp_ref[0, 4:5, :], modt_ref[4])
    h2 = _rms_mod(x, g2_ref[...], shift, scale)
    h2_ref[...] = h2.astype(BF16)

    hh, hl = _split_bf16(h2, 2)
    wh, wl = _split_bf16(wr_ref[...], 2)
    logits = _dot_nt(wh, hh) + (_dot_nt(wh, hl) + _dot_nt(wl, hh))
    s = jax.nn.sigmoid(logits)
    e, t = s.shape
    ge = e // N_GROUPS
    neg = -jnp.inf
    s3 = s.reshape(N_GROUPS, ge, t)
    sb3 = s3 + br_ref[...]

    j3 = lax.broadcasted_iota(jnp.int32, (N_GROUPS, ge, t), 1)
    top1 = _first_max_onehot(sb3, j3, ge, 1)
    m1 = jnp.max(sb3, axis=1, keepdims=True)
    m2 = jnp.max(jnp.where(top1, neg, sb3), axis=1, keepdims=True)
    gv = m1 + m2

    g3 = lax.broadcasted_iota(jnp.int32, (N_GROUPS, 1, t), 0)
    gsel = jnp.zeros((N_GROUPS, 1, t), F32)
    for _ in range(TOPK_GROUPS):
        hit = _first_max_onehot(gv, g3, N_GROUPS, 0)
        gsel = jnp.where(hit, 1.0, gsel)
        gv = jnp.where(hit, neg, gv)

    e3 = lax.broadcasted_iota(jnp.int32, (N_GROUPS, ge, t), 0) * ge + j3
    ev = jnp.where(gsel > 0.5, sb3, neg)
    sel = jnp.zeros((N_GROUPS, ge, t), F32)
    chosen = jnp.zeros((N_GROUPS, ge, t), F32)
    hits = []
    for _ in range(TOP_K):
        hit = _first_max_onehot(ev, e3, e, (0, 1))
        hits.append(hit)
        sel = jnp.where(hit, s3, sel)
        chosen = jnp.where(hit, 1.0, chosen)
        ev = jnp.where(hit, neg, ev)
    tot = jnp.sum(sel, axis=(0, 1), keepdims=True)
    gates = sel / tot * ROUTED_SCALE

    chosen2 = chosen.reshape(e, t)
    rank = _dot(chosen2.astype(BF16), before_ref[...])
    cnt = jnp.sum(chosen2, axis=1, keepdims=True)
    ovf_units = jnp.floor((jnp.maximum(cnt - SLOT_ROWS, 0.0) + (DMA_ROWS - 1)) / DMA_ROWS)
    ovf_off = _dot(below_ref[...], jnp.broadcast_to(ovf_units, (e, LANES)).astype(BF16))[:, 0:1]
    e_col = lax.broadcasted_iota(jnp.int32, (e, 1), 0).astype(F32)
    slot = jnp.where(rank < SLOT_ROWS, e_col * SLOT_ROWS + rank,
                     e * SLOT_ROWS + ovf_off * DMA_ROWS + (rank - SLOT_ROWS))
    slot3 = slot.reshape(N_GROUPS, ge, t)
    for k, hit in enumerate(hits):
        pos_k = jnp.sum(jnp.where(hit, slot3, 0.0), axis=(0, 1), keepdims=True)
        w_k = jnp.sum(jnp.where(hit, gates, 0.0), axis=(0, 1), keepdims=True)
        pos_ref[k:k + 1, :] = pos_k.reshape(1, t).astype(jnp.int32)
        wts_ref[k:k + 1, :] = w_k.reshape(1, t)
    cnt_ref[0] = jnp.broadcast_to(cnt, (e, LANES)).astype(jnp.int32)
    rank_ref[...] = jnp.where(chosen2 > 0.5, rank, -1.0).astype(jnp.int32)
    gates_ref[...] = gates.reshape(e, t)


def _route(xp2d, xs2d, modp, modt, g2, wr_t, br):
    tp, d = xp2d.shape
    ts = xs2d.shape[0]
    tb = TOKEN_BLOCK
    npb, nsb = tp // tb, ts // tb
    e = wr_t.shape[0]
    seq = tp // modp.shape[0]
    bpb = seq // tb

    def p_idx(i):
        return jnp.minimum(i, npb - 1)

    def s_idx(i):
        return jnp.maximum(i - npb, 0)

    t = tp + ts
    nb = npb + nsb
    tok = jnp.arange(tb)
    before = (tok[:, None] < tok[None, :]).astype(BF16)
    ex = jnp.arange(e)
    below = (ex[None, :] < ex[:, None]).astype(BF16)
    return pl.pallas_call(
        functools.partial(_route_kernel, npb),
        out_shape=(jax.ShapeDtypeStruct((t, d), BF16),
                   jax.ShapeDtypeStruct((TOP_K, t), jnp.int32),
                   jax.ShapeDtypeStruct((TOP_K, t), F32),
                   jax.ShapeDtypeStruct((nb, e, LANES), jnp.int32),
                   jax.ShapeDtypeStruct((e, t), jnp.int32),
                   jax.ShapeDtypeStruct((e, t), F32)),
        grid=(nb,),
        in_specs=[
            pl.BlockSpec((tb, d), lambda i: (p_idx(i), 0)),
            pl.BlockSpec((tb, d), lambda i: (s_idx(i), 0)),
            pl.BlockSpec((1, N_MOD, d), lambda i: (p_idx(i) // bpb, 0, 0)),
            pl.BlockSpec((N_MOD, tb, d), lambda i: (0, s_idx(i), 0)),
            _const_spec(g2.shape), _const_spec(wr_t.shape), _const_spec(br.shape),
            _const_spec(before.shape), _const_spec(below.shape),
        ],
        out_specs=(pl.BlockSpec((tb, d), lambda i: (i, 0)),
                   pl.BlockSpec((TOP_K, tb), lambda i: (0, i)),
                   pl.BlockSpec((TOP_K, tb), lambda i: (0, i)),
                   pl.BlockSpec((1, e, LANES), lambda i: (i, 0, 0)),
                   pl.BlockSpec((e, tb), lambda i: (0, i)),
                   pl.BlockSpec((e, tb), lambda i: (0, i))),
        compiler_params=pltpu.CompilerParams(
            dimension_semantics=("arbitrary",),
            vmem_limit_bytes=_vmem_limit(32 * 1024 * 1024),
        ),
        name="route",
    )(xp2d, xs2d, modp, modt, g2, wr_t, br, before, below)


class _Layout:
    def __init__(self, counts):
        nb, ne = counts.shape
        u, g, tm = SLOT_ROWS, DMA_ROWS, EXPERT_TILE
        pc = (counts + g - 1) // g * g
        ovf_units = (jnp.maximum(counts - u, 0) + g - 1) // g
        rows_e = jnp.sum(pc, axis=0)
        region = (rows_e + u + tm - 1) // tm * tm
        goff = jnp.cumsum(region) - region
        dst = goff[None, :] + jnp.cumsum(pc, axis=0) - pc
        self.dst = dst.reshape(-1)
        ovf_end = jnp.cumsum(ovf_units, axis=1)
        ovf_first = ovf_end - ovf_units
        unit = jnp.arange(_overflow_units_cap(ne))[None, :, None]
        owns = (ovf_first[:, None, :] <= unit) & (unit < ovf_end[:, None, :])
        row = dst[:, None, :] + u + (unit - ovf_first[:, None, :]) * g
        self.ovf_dst = jnp.sum(jnp.where(owns, row, 0), axis=2).reshape(-1)
        self.ovf_tot = ovf_end[:, -1]
        self.tail_start = goff + rows_e
        self.tail_units = (region - rows_e) // g
        tiles = region // tm
        self.region_tiles = jnp.concatenate([tiles, jnp.sum(tiles, keepdims=True)])


def _overflow_units_cap(ne):
    return TOKEN_BLOCK * TOP_K // DMA_ROWS + ne


def _block_buffer_rows(tb, ne):
    rows = ne * SLOT_ROWS + tb * TOP_K + ne * (DMA_ROWS - 1)
    return (rows + ONEHOT_ROWS - 1) // ONEHOT_ROWS * ONEHOT_ROWS


def _expert_rows_cap(t, tb, ne):
    nb = t // tb
    rows = t * TOP_K + nb * ne * (DMA_ROWS - 1) + ne * (SLOT_ROWS + EXPERT_TILE - 1)
    return (rows + EXPERT_TILE - 1) // EXPERT_TILE * EXPERT_TILE


def _slot_copy(buf_ref, slot, e, hbm_ref, row, sem, to_hbm):
    src = buf_ref.at[slot, pl.ds(pl.multiple_of(e * SLOT_ROWS, DMA_ROWS), SLOT_ROWS)]
    dst = hbm_ref.at[pl.ds(pl.multiple_of(row, DMA_ROWS), SLOT_ROWS)]
    return pltpu.make_async_copy(src, dst, sem) if to_hbm else pltpu.make_async_copy(dst, src, sem)


def _unit_copy(buf_ref, slot, buf_row, hbm_ref, row, sem, to_hbm):
    src = buf_ref.at[slot, pl.ds(pl.multiple_of(buf_row, DMA_ROWS), DMA_ROWS)]
    dst = hbm_ref.at[pl.ds(pl.multiple_of(row, DMA_ROWS), DMA_ROWS)]
    return pltpu.make_async_copy(src, dst, sem) if to_hbm else pltpu.make_async_copy(dst, src, sem)


def _slots_wait(ne, buf_ref, slot, hbm_ref, sem, to_hbm):
    src = buf_ref.at[slot, pl.ds(0, ne * SLOT_ROWS)]
    dst = hbm_ref.at[pl.ds(0, ne * SLOT_ROWS)]
    c = pltpu.make_async_copy(src, dst, sem) if to_hbm else pltpu.make_async_copy(dst, src, sem)
    c.wait()


def _overflow_copies(ne, blk, slot, odst_ref, otot_ref, buf_ref, hbm_ref, sem, to_hbm, wait):
    def per_unit(i, carry):
        row = odst_ref[blk * _overflow_units_cap(ne) + i]
        c = _unit_copy(buf_ref, slot, ne * SLOT_ROWS + i * DMA_ROWS, hbm_ref, row, sem, to_hbm)
        c.wait() if wait else c.start()
        return carry

    lax.fori_loop(0, otot_ref[blk], per_unit, 0)


def _slot_chunks(ne):
    return [range(c, c + SLOT_CHUNK_EXPERTS) for c in range(0, ne, SLOT_CHUNK_EXPERTS)]


def _overflow_chunks(n_units):
    return lax.shift_right_logical(n_units * DMA_ROWS + (ONEHOT_ROWS - 1),
                                   ONEHOT_ROWS.bit_length() - 1)


def _overflow_select(pos, vals, r0):
    rows = lax.broadcasted_iota(jnp.int32, (ONEHOT_ROWS, pos.shape[1]), 0) + r0
    sel = jnp.zeros(rows.shape, F32)
    for k in range(TOP_K):
        v = 1.0 if vals is None else vals[k:k + 1, :]
        sel = jnp.where(pos[k:k + 1, :] == rows, v, sel)
    return sel


def _dispatch_kernel(ne, dst_ref, odst_ref, otot_ref, tails_ref, tailu_ref,
                     h2_ref, rank_ref, pos_ref, xs_hbm, buf_ref, zero_ref, sem, zsem):
    b = pl.program_id(0)
    nb = pl.num_programs(0)
    slot = b % 2
    h2 = h2_ref[...]
    j = lax.broadcasted_iota(jnp.int32, (SLOT_ROWS, h2.shape[0]), 0)

    for ci, experts in enumerate(_slot_chunks(ne)):
        onehot = jnp.concatenate(
            [jnp.where(rank_ref[e:e + 1, :] == j, 1.0, 0.0) for e in experts], axis=0)
        r0 = experts[0] * SLOT_ROWS
        buf_ref[slot, r0:r0 + len(experts) * SLOT_ROWS, :] = (
            _dot(onehot.astype(BF16), h2).astype(BF16))
        if ci == 0:
            @pl.when(b > 0)
            def _prev():
                _slots_wait(ne, buf_ref, 1 - slot, xs_hbm, sem.at[1 - slot], True)
                _overflow_copies(ne, b - 1, 1 - slot, odst_ref, otot_ref, buf_ref, xs_hbm,
                                 sem.at[1 - slot], True, True)
        for e in experts:
            _slot_copy(buf_ref, slot, e, xs_hbm, dst_ref[b * ne + e], sem.at[slot], True).start()

    pos = pos_ref[...]

    def chunk(ci, carry):
        r0 = pl.multiple_of(ne * SLOT_ROWS + ci * ONEHOT_ROWS, ONEHOT_ROWS)
        onehot = _overflow_select(pos, None, r0).astype(BF16)
        buf_ref[slot, pl.ds(r0, ONEHOT_ROWS), :] = _dot(onehot, h2).astype(BF16)
        return carry

    lax.fori_loop(0, _overflow_chunks(otot_ref[b]), chunk, 0)
    _overflow_copies(ne, b, slot, odst_ref, otot_ref, buf_ref, xs_hbm, sem.at[slot], True, False)

    @pl.when(b == nb - 1)
    def _last():
        _slots_wait(ne, buf_ref, slot, xs_hbm, sem.at[slot], True)
        _overflow_copies(ne, b, slot, odst_ref, otot_ref, buf_ref, xs_hbm, sem.at[slot], True,
                         True)
        zero_ref[...] = jnp.zeros_like(zero_ref)

        def tail(wait):
            def per_expert(e, carry):
                def per_unit(i, carry2):
                    row = pl.multiple_of(tails_ref[e] + i * DMA_ROWS, DMA_ROWS)
                    c = pltpu.make_async_copy(zero_ref, xs_hbm.at[pl.ds(row, DMA_ROWS)], zsem)
                    c.wait() if wait else c.start()
                    return carry2
                lax.fori_loop(0, tailu_ref[e], per_unit, 0)
                return carry
            lax.fori_loop(0, ne, per_expert, 0)

        tail(False)
        tail(True)


def _dispatch(h2, rank, pos, lay, ne, rows_cap):
    t, d = h2.shape
    tb = TOKEN_BLOCK
    buf_rows = _block_buffer_rows(tb, ne)
    return pl.pallas_call(
        functools.partial(_dispatch_kernel, ne),
        out_shape=jax.ShapeDtypeStruct((rows_cap, d), BF16),
        grid_spec=pltpu.PrefetchScalarGridSpec(
            num_scalar_prefetch=5,
            grid=(t // tb,),
            in_specs=[
                pl.BlockSpec((tb, d), lambda i, *_: (i, 0)),
                pl.BlockSpec((ne, tb), lambda i, *_: (0, i)),
                pl.BlockSpec((TOP_K, tb), lambda i, *_: (0, i)),
            ],
            out_specs=pl.BlockSpec(memory_space=pl.ANY),
            scratch_shapes=[
                pltpu.VMEM((2, buf_rows, d), BF16),
                pltpu.VMEM((DMA_ROWS, d), BF16),
                pltpu.SemaphoreType.DMA((2,)),
                pltpu.SemaphoreType.DMA(()),
            ],
        ),
        compiler_params=pltpu.CompilerParams(
            dimension_semantics=("arbitrary",),
            vmem_limit_bytes=_vmem_limit(4 * buf_rows * d + 16 * 1024 * 1024),
        ),
        name="dispatch",
    )(lay.dst, lay.ovf_dst, lay.ovf_tot, lay.tail_start, lay.tail_units, h2, rank, pos)


def _experts_kernel(tiles_ref, xe_hbm, wg_ref, wu_ref, wd_ref, ye_hbm,
                    xbuf_ref, ybuf_ref, wgb_ref, wub_ref, wdb_ref, done_ref, xsem, ysem):
    e = pl.program_id(0)
    ne = pl.num_programs(0)
    tm = xbuf_ref.shape[1]

    def x_copy(row, slot):
        return pltpu.make_async_copy(xe_hbm.at[pl.ds(pl.multiple_of(row, tm), tm)],
                                     xbuf_ref.at[slot], xsem.at[slot])

    def y_copy(row, slot):
        return pltpu.make_async_copy(ybuf_ref.at[slot],
                                     ye_hbm.at[pl.ds(pl.multiple_of(row, tm), tm)], ysem.at[slot])

    total = tiles_ref[ne]

    @pl.when(e == 0)
    def _first():
        done_ref[0] = 0
        for g in range(RING_SLOTS - 1):
            x_copy(g * tm, g).start()

    wgb_ref[...] = wg_ref[0].astype(BF16)
    wub_ref[...] = wu_ref[0].astype(BF16)
    wdb_ref[...] = wd_ref[0].astype(BF16)
    done = done_ref[0]
    n = tiles_ref[e]

    def tile(j, carry):
        g = done + j
        slot = lax.rem(g, RING_SLOTS)
        x_copy(0, slot).wait()
        ahead = g + (RING_SLOTS - 1)

        @pl.when(ahead < total)
        def _prefetch():
            x_copy(ahead * tm, lax.rem(ahead, RING_SLOTS)).start()

        @pl.when(g >= RING_SLOTS)
        def _reuse():
            y_copy(0, slot).wait()

        part = tm // EXPERT_TILE_PARTS
        starts = range(0, tm, part)
        hids = []
        for r0 in starts:
            x = xbuf_ref[slot, r0:r0 + part, :]
            hids.append((jax.nn.silu(_dot(x, wgb_ref[...])) * _dot(x, wub_ref[...])).astype(BF16))
        for r0, hid in zip(starts, hids):
            ybuf_ref[slot, r0:r0 + part, :] = _dot(hid, wdb_ref[...]).astype(BF16)
        y_copy(g * tm, slot).start()
        return carry

    lax.fori_loop(0, n, tile, 0)
    done_ref[0] = done + n

    @pl.when(e == ne - 1)
    def _drain():
        for s in range(RING_SLOTS):
            y_copy(0, s).wait()


def _experts(xs, lay, wg, wu, wd):
    rows_cap, d = xs.shape
    ne, _, de = wg.shape
    assert ne >= RING_SLOTS
    tm = EXPERT_TILE
    ring = RING_SLOTS
    return pl.pallas_call(
        _experts_kernel,
        out_shape=jax.ShapeDtypeStruct((rows_cap, d), BF16),
        grid_spec=pltpu.PrefetchScalarGridSpec(
            num_scalar_prefetch=1,
            grid=(ne,),
            in_specs=[
                pl.BlockSpec(memory_space=pl.ANY),
                pl.BlockSpec((1, d, de), lambda e, *_: (e, 0, 0)),
                pl.BlockSpec((1, d, de), lambda e, *_: (e, 0, 0)),
                pl.BlockSpec((1, de, d), lambda e, *_: (e, 0, 0)),
            ],
            out_specs=pl.BlockSpec(memory_space=pl.ANY),
            scratch_shapes=[
                pltpu.VMEM((ring, tm, d), BF16), pltpu.VMEM((ring, tm, d), BF16),
                pltpu.VMEM((d, de), BF16), pltpu.VMEM((d, de), BF16), pltpu.VMEM((de, d), BF16),
                pltpu.SMEM((1,), jnp.int32),
                pltpu.SemaphoreType.DMA((ring,)), pltpu.SemaphoreType.DMA((ring,)),
            ],
        ),
        compiler_params=pltpu.CompilerParams(
            dimension_semantics=("arbitrary",),
            vmem_limit_bytes=_vmem_limit(32 * 1024 * 1024),
        ),
        name="experts",
    )(lay.region_tiles, xs, wg, wu, wd)


def _final_kernel(n_prompt_blocks, ne, dst_ref, odst_ref, otot_ref,
                  xp_ref, xs_ref, modp_ref, modt_ref, h2_ref, rank_ref, gates_ref, pos_ref, wts_ref,
                  wsg_ref, wsu_ref, wsd_ref, gf_ref, ye_hbm, yp_ref, ys_ref,
                  buf_ref, acc_ref, sem):
    i = pl.program_id(0)
    nb = pl.num_programs(0)
    slot = i % 2

    def fetch(blk, sl):
        for e in range(ne):
            _slot_copy(buf_ref, sl, e, ye_hbm, dst_ref[blk * ne + e], sem.at[sl], False).start()
        _overflow_copies(ne, blk, sl, odst_ref, otot_ref, buf_ref, ye_hbm, sem.at[sl], False, False)

    @pl.when(i == 0)
    def _first():
        buf_ref[...] = jnp.zeros_like(buf_ref)
        fetch(0, 0)

    @pl.when(i + 1 < nb)
    def _prefetch():
        fetch(i + 1, 1 - slot)

    _slots_wait(ne, buf_ref, slot, ye_hbm, sem.at[slot], False)
    _overflow_copies(ne, i, slot, odst_ref, otot_ref, buf_ref, ye_hbm, sem.at[slot], False, True)

    tb = h2_ref.shape[0]
    j = lax.broadcasted_iota(jnp.int32, (SLOT_ROWS, tb), 0)
    acc = jnp.zeros((tb, h2_ref.shape[1]), F32)
    for experts in _slot_chunks(ne):
        sel = jnp.concatenate(
            [jnp.where(rank_ref[e:e + 1, :] == j, gates_ref[e:e + 1, :], 0.0) for e in experts],
            axis=0)
        r0 = experts[0] * SLOT_ROWS
        acc = acc + _dot_tn(sel.astype(BF16), buf_ref[slot, r0:r0 + len(experts) * SLOT_ROWS, :])
    acc_ref[...] = acc

    pos = pos_ref[...]
    wts = wts_ref[...]

    def chunk(ci, carry):
        r0 = pl.multiple_of(ne * SLOT_ROWS + ci * ONEHOT_ROWS, ONEHOT_ROWS)
        sel = _overflow_select(pos, wts, r0).astype(BF16)
        acc_ref[...] += _dot_tn(sel, buf_ref[slot, pl.ds(r0, ONEHOT_ROWS), :])
        return carry

    lax.fori_loop(0, _overflow_chunks(otot_ref[i]), chunk, 0)

    is_p = i < n_prompt_blocks
    x = jnp.where(is_p, xp_ref[...], xs_ref[...])
    gate = jnp.where(is_p, modp_ref[0, 5:6, :], modt_ref[5])
    hb = h2_ref[...]
    hid = jax.nn.silu(_dot(hb, wsg_ref[...])) * _dot(hb, wsu_ref[...])
    f = acc_ref[...] + _dot(hid.astype(BF16), wsd_ref[...])
    x2 = x + gate * f
    y = x2 * lax.rsqrt(jnp.mean(x2 * x2, axis=-1, keepdims=True) + EPS) * gf_ref[...]

    @pl.when(is_p)
    def _p():
        yp_ref[...] = y

    @pl.when(jnp.logical_not(is_p))
    def _s():
        ys_ref[...] = y


def _final(xp2d, xs2d, modp, modt, h2, rank, gates, pos, wts, ye, lay, ne, wsg, wsu, wsd, gf):
    tp, d = xp2d.shape
    ts = xs2d.shape[0]
    tb = TOKEN_BLOCK
    npb, nsb = tp // tb, ts // tb
    seq = tp // modp.shape[0]
    bpb = seq // tb
    buf_rows = _block_buffer_rows(tb, ne)

    def p_idx(i):
        return jnp.minimum(i, npb - 1)

    def s_idx(i):
        return jnp.maximum(i - npb, 0)

    def const(shape):
        zeros = (0,) * len(shape)
        return pl.BlockSpec(shape, lambda i, *_: zeros)

    return pl.pallas_call(
        functools.partial(_final_kernel, npb, ne),
        out_shape=(jax.ShapeDtypeStruct((tp, d), F32), jax.ShapeDtypeStruct((ts, d), F32)),
        grid_spec=pltpu.PrefetchScalarGridSpec(
            num_scalar_prefetch=3,
            grid=(npb + nsb,),
            in_specs=[
                pl.BlockSpec((tb, d), lambda i, *_: (p_idx(i), 0)),
                pl.BlockSpec((tb, d), lambda i, *_: (s_idx(i), 0)),
                pl.BlockSpec((1, N_MOD, d), lambda i, *_: (p_idx(i) // bpb, 0, 0)),
                pl.BlockSpec((N_MOD, tb, d), lambda i, *_: (0, s_idx(i), 0)),
                pl.BlockSpec((tb, d), lambda i, *_: (i, 0)),
                pl.BlockSpec((ne, tb), lambda i, *_: (0, i)),
                pl.BlockSpec((ne, tb), lambda i, *_: (0, i)),
                pl.BlockSpec((TOP_K, tb), lambda i, *_: (0, i)),
                pl.BlockSpec((TOP_K, tb), lambda i, *_: (0, i)),
                const(wsg.shape), const(wsu.shape), const(wsd.shape), const(gf.shape),
                pl.BlockSpec(memory_space=pl.ANY),
            ],
            out_specs=(pl.BlockSpec((tb, d), lambda i, *_: (p_idx(i), 0)),
                       pl.BlockSpec((tb, d), lambda i, *_: (s_idx(i), 0))),
            scratch_shapes=[
                pltpu.VMEM((2, buf_rows, d), BF16),
                pltpu.VMEM((tb, d), F32),
                pltpu.SemaphoreType.DMA((2,)),
            ],
        ),
        compiler_params=pltpu.CompilerParams(
            dimension_semantics=("arbitrary",),
            vmem_limit_bytes=_vmem_limit(4 * buf_rows * d + 20 * 1024 * 1024),
        ),
        name="final",
    )(lay.dst, lay.ovf_dst, lay.ovf_tot,
      xp2d, xs2d, modp, modt, h2, rank, gates, pos, wts, wsg, wsu, wsd, gf, ye)


def kernel(x_prompt, x_sample, c_prompt, c_sample, state_pool, state_ret, w_ada, b_ada, g_norm1,
           w_in, w_pool_group, pool_scale, w_branch_pool, w_branch_ret, w_out, g_norm2, w_router,
           b_router, w_exp_gate, w_exp_up, w_exp_down, w_sh_gate, w_sh_up, w_sh_down, g_final):
    b, s, d = x_prompt.shape
    db, ds, _ = x_sample.shape
    depth = w_ada.shape[0]
    heads, dk, dv = state_ret.shape[2:]
    pool_buf, d_pool = state_pool.shape[2:]
    dims = _Dims(d, d_pool, heads, dk, dv)
    assert dims.d_in == w_in.shape[2]
    assert pool_buf == max(POOL_WINDOWS) - 1 and pool_buf < POOL_PREV
    assert s % PROMPT_STEP_ROWS == 0 and PROMPT_STEP_ROWS % RET_CHUNK == 0
    assert s % TOKEN_BLOCK == 0 and (db * ds) % TOKEN_BLOCK == 0
    assert db % SAMPLE_SEQS_PER_STEP == 0 and SLOT_ROWS % DMA_ROWS == 0

    log_g = jnp.log(1.0 - jnp.power(2.0, -5.0 - jnp.arange(heads, dtype=F32)))
    ns = SAMPLE_SEQS_PER_STEP
    cos_p, sin_p = _rotary_tables(jnp.arange(s, dtype=jnp.int32), dk)
    cos_s, sin_s = _rotary_tables(PAST_LEN + jnp.arange(ds, dtype=jnp.int32), dk)
    cos_s, sin_s = jnp.tile(cos_s, (ns, 1)), jnp.tile(sin_s, (ns, 1))
    dec_p = _decay_tables(log_g, RET_CHUNK, 1, dk, dv)
    dec_s = _decay_tables(log_g, ds, ns, dk, dv)
    pool_tabs = _sample_pool_tables(ns, ds, dims.gd, PAST_LEN)

    assert depth == 1, "the final kernel fuses the output norm into the (single) layer"
    l = 0
    xs = x_sample.reshape(db * ds, d)
    modp, modt = _ada(c_prompt, jnp.repeat(c_sample, ds, axis=0), w_ada[l], b_ada[l])
    modp = modp.reshape(b, N_MOD, d)
    g1 = g_norm1[l].reshape(1, d)
    weights = (w_in[l].astype(BF16), w_pool_group[l].astype(BF16),
               pool_scale[l].reshape(1, d_pool), w_branch_pool[l].astype(BF16),
               w_branch_ret[l].astype(BF16), w_out[l].astype(BF16))

    x1p, pool16, ret_p = _mixer_prompt(dims, x_prompt, modp, g1, weights,
                                       (cos_p, sin_p) + dec_p)
    pool_in = jnp.pad(state_pool[l], ((0, 0), (POOL_PREV - pool_buf, 0), (0, 0)))
    x1s, u_s, ret_s = _mixer_sample(dims, ds, xs, modt, g1, weights,
                                    (cos_s, sin_s) + dec_s + pool_tabs, pool_in,
                                    state_ret[l].astype(F32))
    pool_p = pool16[:, POOL_PREV - pool_buf:]
    pool_s = jnp.concatenate([state_pool[l], u_s.reshape(db, ds, d_pool)], axis=1)[:, -pool_buf:]

    x1p2d = x1p.reshape(b * s, d)
    ne = w_router.shape[2]
    h2, pos, wts, counts, rank, gates = _route(
        x1p2d, x1s, modp, modt, g_norm2[l].reshape(1, d), w_router[l].T,
        b_router[l].reshape(N_GROUPS, ne // N_GROUPS, 1))
    rows_cap = _expert_rows_cap(b * s + db * ds, TOKEN_BLOCK, ne)
    lay = _Layout(counts[:, :, 0])
    xe = _dispatch(h2, rank, pos, lay, ne, rows_cap)
    ye = _experts(xe, lay, w_exp_gate[l], w_exp_up[l], w_exp_down[l])
    yp, ys = _final(x1p2d, x1s, modp, modt, h2, rank, gates, pos, wts, ye, lay, ne,
                    w_sh_gate[l].astype(BF16), w_sh_up[l].astype(BF16),
                    w_sh_down[l].astype(BF16), g_final.reshape(1, d))

    return (yp.reshape(b, s, d), ys.reshape(db, ds, d), pool_p[None],
            ret_p.astype(x_prompt.dtype), pool_s[None], ret_s.astype(state_ret.dtype)[None])
```

```python
import functools

import jax
import jax.numpy as jnp
from jax import lax
from jax.experimental import pallas as pl
from jax.experimental.pallas import tpu as pltpu

F32 = jnp.float32
BF16 = jnp.bfloat16

PAST_LEN = 16384
POOL_WINDOWS = (2, 4, 8, 16)
RET_CHUNK = 128
ROPE_BASE = 10000.0
TOP_K = 8
N_GROUPS = 8
TOPK_GROUPS = 4
ROUTED_SCALE = 2.5
N_MOD = 6
EPS = 1e-6

V7X_VMEM_BYTES = 64 * 1024 * 1024
SUBLANES = 8
LANES = 128

PROMPT_STEP_ROWS = 512
SAMPLE_SEQS_PER_STEP = 8
TOKEN_BLOCK = 256
DMA_ROWS = 16
SLOT_ROWS = 3 * DMA_ROWS
ONEHOT_ROWS = 256
SLOT_CHUNK_EXPERTS = 16
EXPERT_TILE_PARTS = 2
RING_SLOTS = 3
EXPERT_TILE = 512
POOL_PREV = 16


def _vmem_limit(nbytes):
    return int(min(nbytes, V7X_VMEM_BYTES - 8 * 1024 * 1024))


def _dot(a, b):
    return jnp.dot(a, b, preferred_element_type=F32)


def _dot_nt(a, b):
    return lax.dot_general(a, b, (((1,), (1,)), ((), ())), preferred_element_type=F32)


def _dot_tn(a, b):
    return lax.dot_general(a, b, (((0,), (0,)), ((), ())), preferred_element_type=F32)


def _split_bf16(x, parts):
    out = []
    for _ in range(parts):
        p = x.astype(BF16)
        out.append(p)
        x = x - p.astype(F32)
    return out


def _rms_mod(x, g, shift, scale):
    y = x * lax.rsqrt(jnp.mean(x * x, axis=-1, keepdims=True) + EPS)
    return (y * g) * (1.0 + scale) + shift


def _rotary(x, cosb, sinb):
    return x * cosb + pltpu.roll(x, x.shape[-1] // 2, axis=1) * sinb


def _group_norm(o):
    return o * lax.rsqrt(jnp.mean(o * o, axis=-1, keepdims=True) + EPS)


def _ada_kernel(cp_ref, ct_ref, w_ref, b_ref, modp_ref, modt_ref):
    w = w_ref[...].astype(BF16)
    modp_ref[...] = _dot(cp_ref[...].astype(BF16), w) + b_ref[...]
    modt_ref[0] = _dot(ct_ref[...].astype(BF16), w) + b_ref[...]


def _ada(c_prompt, c_tokens, w_ada, b_ada):
    bp, d = c_prompt.shape
    ts = c_tokens.shape[0]
    n = w_ada.shape[1]
    return pl.pallas_call(
        _ada_kernel,
        out_shape=(jax.ShapeDtypeStruct((bp, n), F32), jax.ShapeDtypeStruct((n // d, ts, d), F32)),
        grid=(n // d,),
        in_specs=[
            pl.BlockSpec((bp, d), lambda j: (0, 0)),
            pl.BlockSpec((ts, d), lambda j: (0, 0)),
            pl.BlockSpec((d, d), lambda j: (0, j)),
            pl.BlockSpec((1, d), lambda j: (0, j)),
        ],
        out_specs=(pl.BlockSpec((bp, d), lambda j: (0, j)),
                   pl.BlockSpec((1, ts, d), lambda j: (j, 0, 0))),
        compiler_params=pltpu.CompilerParams(
            dimension_semantics=("arbitrary",),
            vmem_limit_bytes=_vmem_limit(32 * 1024 * 1024),
        ),
        name="ada",
    )(c_prompt, c_tokens, w_ada, b_ada.reshape(1, n))


class _Dims:
    def __init__(self, d_model, d_pool, heads, dk, dv):
        self.d = d_model
        self.d_pool = d_pool
        self.gd = d_pool // len(POOL_WINDOWS)
        self.heads = heads
        self.dk = dk
        self.dv = dv
        self.d_k = heads * dk
        self.d_v = heads * dv
        widths = (d_pool, self.d_k, self.d_k, self.d_v, self.d_v, d_model, d_model)
        offs = [0]
        for w in widths:
            offs.append(offs[-1] + w)
        self.cols = tuple((offs[i], offs[i + 1]) for i in range(len(widths)))
        self.d_in = offs[-1]


def _proj(hb, win_ref, cols):
    return _dot(hb, win_ref[:, cols[0]:cols[1]])


def _pool_branch(pm_parts, wpg_ref, ps_ref, wbp_ref):
    pg = [_dot(pm.astype(BF16), wpg_ref[gi]) for gi, pm in enumerate(pm_parts)]
    pm = jnp.concatenate(pg, axis=1) * ps_ref[...]
    return _dot(pm.astype(BF16), wbp_ref[...])


def _merge_out(x, gate, a, o, g, ga, gb, wbr_ref, wout_ref):
    r = _dot((jax.nn.silu(g) * o).astype(BF16), wbr_ref[...])
    merged = jax.nn.sigmoid(ga) * a + jax.nn.sigmoid(gb) * r
    m = _dot(merged.astype(BF16), wout_ref[...])
    return x + gate * m


def _mixer_prompt_kernel(dims, x_ref, mod_ref, g1_ref, win_ref, wpg_ref, ps_ref, wbp_ref,
                         wbr_ref, wout_ref, cos_ref, sin_ref, intra_ref, qdec_ref,
                         kdec_ref, sdec_ref, x1_ref, pool_ref, ret_ref, uext_ref):
    c = pl.program_id(1)
    L = x_ref.shape[1]

    @pl.when(c == 0)
    def _init():
        ret_ref[...] = jnp.zeros_like(ret_ref)
        uext_ref[0:POOL_PREV, :] = jnp.zeros((POOL_PREV, dims.d_pool), F32)

    x = x_ref[0]
    shift, scale, gate = mod_ref[0, 0:1, :], mod_ref[0, 1:2, :], mod_ref[0, 2:3, :]
    hb = _rms_mod(x, g1_ref[...], shift, scale).astype(BF16)

    u = _proj(hb, win_ref, dims.cols[0])
    uext_ref[POOL_PREV:POOL_PREV + L, :] = u
    pos = c * L + lax.broadcasted_iota(jnp.int32, (L, 1), 0)
    pm_parts = []
    for gi, w in enumerate(POOL_WINDOWS):
        s = uext_ref[:, gi * dims.gd:(gi + 1) * dims.gd]
        sh = 1
        while sh < w:
            s = s + pltpu.roll(s, sh, axis=0)
            sh *= 2
        cnt = jnp.minimum(w, pos + 1).astype(F32)
        pm_parts.append(s[POOL_PREV:, :] / cnt - u[:, gi * dims.gd:(gi + 1) * dims.gd])
    a = _pool_branch(pm_parts, wpg_ref, ps_ref, wbp_ref)
    uext_ref[0:POOL_PREV, :] = uext_ref[L:L + POOL_PREV, :]
    pool_ref[0] = uext_ref[0:POOL_PREV, :]

    q = _proj(hb, win_ref, dims.cols[1])
    k = _proj(hb, win_ref, dims.cols[2])
    v = _proj(hb, win_ref, dims.cols[3])
    chunk_outs = []
    for r0 in range(0, L, RET_CHUNK):
        rows = slice(r0, r0 + RET_CHUNK)
        cosb, sinb = cos_ref[rows, :], sin_ref[rows, :]
        outs = []
        for h in range(dims.heads):
            qr = _rotary(q[rows, h * dims.dk:(h + 1) * dims.dk], cosb, sinb)
            kr = _rotary(k[rows, h * dims.dk:(h + 1) * dims.dk], cosb, sinb) * (dims.dk ** -0.5)
            vb = v[rows, h * dims.dv:(h + 1) * dims.dv].astype(BF16)
            scores = _dot_nt(qr.astype(BF16), kr.astype(BF16)) * intra_ref[h]
            s_prev = ret_ref[0, 0, h]
            o = _dot(scores.astype(BF16), vb) + _dot((qr * qdec_ref[h]).astype(BF16),
                                                      s_prev.astype(BF16))
            ret_ref[0, 0, h] = (sdec_ref[h] * s_prev
                                + _dot_tn((kr * kdec_ref[h]).astype(BF16), vb))
            outs.append(_group_norm(o))
        chunk_outs.append(jnp.concatenate(outs, axis=1))
    o = jnp.concatenate(chunk_outs, axis=0)

    g = _proj(hb, win_ref, dims.cols[4])
    ga = _proj(hb, win_ref, dims.cols[5])
    gb = _proj(hb, win_ref, dims.cols[6])
    x1_ref[0] = _merge_out(x, gate, a, o, g, ga, gb, wbr_ref, wout_ref)


def _const_spec(shape):
    zeros = (0,) * len(shape)
    return pl.BlockSpec(shape, lambda *_: zeros)


def _resident_spec(shape):
    zeros = (0,) * len(shape)
    return pl.BlockSpec(shape, lambda *_: zeros, pipeline_mode=pl.Buffered(1))


def _mixer_prompt(dims, x, mod, g1, weights, tables):
    b, s, d = x.shape
    L = PROMPT_STEP_ROWS
    nc = s // L
    win, wpg, ps, wbp, wbr, wout = weights
    cosb, sinb, intra, qdec, kdec, sdec = tables
    in_specs = [
        pl.BlockSpec((1, L, d), lambda i, c: (i, c, 0)),
        pl.BlockSpec((1, N_MOD, d), lambda i, c: (i, 0, 0)),
        _const_spec(g1.shape), _resident_spec(win.shape), _const_spec(wpg.shape),
        _const_spec(ps.shape), _resident_spec(wbp.shape), _resident_spec(wbr.shape),
        _resident_spec(wout.shape),
        pl.BlockSpec((L, dims.dk), lambda i, c: (c, 0)),
        pl.BlockSpec((L, dims.dk), lambda i, c: (c, 0)),
        _const_spec(intra.shape), _const_spec(qdec.shape), _const_spec(kdec.shape),
        _const_spec(sdec.shape),
    ]
    out_shape = (
        jax.ShapeDtypeStruct((b, s, d), F32),
        jax.ShapeDtypeStruct((b, POOL_PREV, dims.d_pool), F32),
        jax.ShapeDtypeStruct((1, b, dims.heads, dims.dk, dims.dv), F32),
    )
    out_specs = (
        pl.BlockSpec((1, L, d), lambda i, c: (i, c, 0)),
        pl.BlockSpec((1, POOL_PREV, dims.d_pool), lambda i, c: (i, 0, 0)),
        pl.BlockSpec((1, 1, dims.heads, dims.dk, dims.dv), lambda i, c: (0, i, 0, 0, 0)),
    )
    weight_bytes = 2 * (win.size + wpg.size + wbp.size + wbr.size + wout.size)
    return pl.pallas_call(
        functools.partial(_mixer_prompt_kernel, dims),
        out_shape=out_shape,
        grid=(b, nc),
        in_specs=in_specs,
        out_specs=out_specs,
        scratch_shapes=[pltpu.VMEM((POOL_PREV + L, dims.d_pool), F32)],
        compiler_params=pltpu.CompilerParams(
            dimension_semantics=("arbitrary", "arbitrary"),
            vmem_limit_bytes=_vmem_limit(2 * weight_bytes + 16 * 1024 * 1024),
        ),
        name="mixer_prompt",
    )(x, mod, g1, win, wpg, ps, wbp, wbr, wout, cosb, sinb, intra, qdec, kdec, sdec)


def _mixer_sample_kernel(dims, ds, x_ref, modt_ref, g1_ref, win_ref, wpg_ref, ps_ref, wbp_ref,
                         wbr_ref, wout_ref, cos_ref, sin_ref, intra_ref, qdec_ref, kdec_ref,
                         sdec_ref, bandp_ref, bandu_ref, cnt_ref, poolin_ref, sret_ref,
                         x1_ref, u_ref, ret_ref):
    R = x_ref.shape[0]
    ns = R // ds
    x = x_ref[...]
    shift, scale, gate = modt_ref[0], modt_ref[1], modt_ref[2]
    hb = _rms_mod(x, g1_ref[...], shift, scale).astype(BF16)

    u = _proj(hb, win_ref, dims.cols[0])
    u_ref[...] = u
    hist = poolin_ref[...].reshape(ns * POOL_PREV, dims.d_pool)
    pm_parts = []
    for gi in range(len(POOL_WINDOWS)):
        lo, hi = gi * dims.gd, (gi + 1) * dims.gd
        win_sum = jnp.zeros((R, dims.gd), F32)
        for part in _split_bf16(hist[:, lo:hi], 3):
            win_sum = win_sum + _dot(bandp_ref[gi], part)
        for part in _split_bf16(u[:, lo:hi], 3):
            win_sum = win_sum + _dot(bandu_ref[gi], part)
        pm_parts.append(win_sum / cnt_ref[gi] - u[:, lo:hi])
    a = _pool_branch(pm_parts, wpg_ref, ps_ref, wbp_ref)

    q = _proj(hb, win_ref, dims.cols[1])
    k = _proj(hb, win_ref, dims.cols[2])
    v = _proj(hb, win_ref, dims.cols[3])
    cosb, sinb = cos_ref[...], sin_ref[...]
    row_seq = lax.broadcasted_iota(jnp.int32, (R, ns * dims.dk), 0) // ds
    col_seq = lax.broadcasted_iota(jnp.int32, (R, ns * dims.dk), 1) // dims.dk
    own = row_seq == col_seq
    outs = []
    for h in range(dims.heads):
        qr = _rotary(q[:, h * dims.dk:(h + 1) * dims.dk], cosb, sinb)
        kr = _rotary(k[:, h * dims.dk:(h + 1) * dims.dk], cosb, sinb) * (dims.dk ** -0.5)
        vb = v[:, h * dims.dv:(h + 1) * dims.dv].astype(BF16)
        scores = _dot_nt(qr.astype(BF16), kr.astype(BF16)) * intra_ref[h]
        s_prev = sret_ref[:, h].reshape(ns * dims.dk, dims.dv)
        qd = jnp.concatenate([qr * qdec_ref[h]] * ns, axis=1)
        kd = jnp.concatenate([kr * kdec_ref[h]] * ns, axis=1)
        qexp = jnp.where(own, qd, 0.0).astype(BF16)
        kexp = jnp.where(own, kd, 0.0).astype(BF16)
        o = _dot(scores.astype(BF16), vb) + _dot(qexp, s_prev.astype(BF16))
        s_new = sdec_ref[h] * s_prev + _dot_tn(kexp, vb)
        ret_ref[:, h] = s_new.reshape(ns, dims.dk, dims.dv)
        outs.append(_group_norm(o))
    o = jnp.concatenate(outs, axis=1)

    g = _proj(hb, win_ref, dims.cols[4])
    ga = _proj(hb, win_ref, dims.cols[5])
    gb = _proj(hb, win_ref, dims.cols[6])
    x1_ref[...] = _merge_out(x, gate, a, o, g, ga, gb, wbr_ref, wout_ref)


def _mixer_sample(dims, ds, x2d, modt, g1, weights, tables, pool_in, state_ret):
    t, d = x2d.shape
    ns = SAMPLE_SEQS_PER_STEP
    R = ns * ds
    win, wpg, ps, wbp, wbr, wout = weights
    cosb, sinb, intra, qdec, kdec, sdec, bandp, bandu, cnt = tables
    db = state_ret.shape[0]
    in_specs = [
        pl.BlockSpec((R, d), lambda i: (i, 0)),
        pl.BlockSpec((N_MOD, R, d), lambda i: (0, i, 0)),
        _const_spec(g1.shape), _resident_spec(win.shape), _const_spec(wpg.shape),
        _const_spec(ps.shape), _resident_spec(wbp.shape), _resident_spec(wbr.shape),
        _resident_spec(wout.shape), _const_spec(cosb.shape), _const_spec(sinb.shape),
        _const_spec(intra.shape), _const_spec(qdec.shape), _const_spec(kdec.shape),
        _const_spec(sdec.shape), _const_spec(bandp.shape), _const_spec(bandu.shape),
        _const_spec(cnt.shape),
        pl.BlockSpec((ns, POOL_PREV, dims.d_pool), lambda i: (i, 0, 0)),
        pl.BlockSpec((ns, dims.heads, dims.dk, dims.dv), lambda i: (i, 0, 0, 0)),
    ]
    out_shape = (
        jax.ShapeDtypeStruct((t, d), F32),
        jax.ShapeDtypeStruct((t, dims.d_pool), F32),
        jax.ShapeDtypeStruct(state_ret.shape, state_ret.dtype),
    )
    out_specs = (
        pl.BlockSpec((R, d), lambda i: (i, 0)),
        pl.BlockSpec((R, dims.d_pool), lambda i: (i, 0)),
        pl.BlockSpec((ns, dims.heads, dims.dk, dims.dv), lambda i: (i, 0, 0, 0)),
    )
    weight_bytes = 2 * (win.size + wpg.size + wbp.size + wbr.size + wout.size)
    state_bytes = 4 * ns * dims.heads * dims.dk * dims.dv
    return pl.pallas_call(
        functools.partial(_mixer_sample_kernel, dims, ds),
        out_shape=out_shape,
        grid=(db // ns,),
        in_specs=in_specs,
        out_specs=out_specs,
        compiler_params=pltpu.CompilerParams(
            dimension_semantics=("arbitrary",),
            vmem_limit_bytes=_vmem_limit(2 * weight_bytes + 4 * state_bytes + 8 * 1024 * 1024),
        ),
        name="mixer_sample",
    )(x2d, modt, g1, win, wpg, ps, wbp, wbr, wout, cosb, sinb, intra, qdec, kdec, sdec,
      bandp, bandu, cnt, pool_in, state_ret)


def _rotary_tables(pos, dk):
    half = dk // 2
    inv = jnp.power(ROPE_BASE, -jnp.arange(half, dtype=F32) / half)
    ang = pos.astype(F32)[:, None] * inv[None, :]
    cos, sin = jnp.cos(ang), jnp.sin(ang)
    return jnp.concatenate([cos, cos], axis=1), jnp.concatenate([-sin, sin], axis=1)


def _decay_tables(log_g, chunk, reps, dk, dv):
    idx = jnp.arange(chunk, dtype=F32)
    diff = idx[:, None] - idx[None, :]
    intra = jnp.where((diff >= 0)[None],
                      jnp.exp(log_g[:, None, None] * jnp.maximum(diff, 0.0)[None]), 0.0)
    heads = log_g.shape[0]
    eye = jnp.eye(reps, dtype=F32)
    intra = jnp.einsum("ab,hij->haibj", eye, intra).reshape(heads, reps * chunk, reps * chunk)
    q_dec = jnp.exp(log_g[:, None] * (idx[None, :] + 1.0))
    k_dec = jnp.exp(log_g[:, None] * (chunk - 1.0 - idx[None, :]))
    s_dec = jnp.exp(log_g * chunk)
    q_dec = jnp.broadcast_to(jnp.tile(q_dec, (1, reps))[:, :, None], (heads, reps * chunk, dk))
    k_dec = jnp.broadcast_to(jnp.tile(k_dec, (1, reps))[:, :, None], (heads, reps * chunk, dk))
    s_dec = jnp.broadcast_to(s_dec[:, None, None], (heads, 1, dv))
    return intra, q_dec, k_dec, s_dec


def _sample_pool_tables(ns, ds, gd, start_pos):
    n = jnp.arange(ds)
    i = jnp.arange(POOL_PREV)
    eye = jnp.eye(ns, dtype=F32)
    bandp, bandu, cnt = [], [], []
    for w in POOL_WINDOWS:
        bp = ((i[None, :] - POOL_PREV) >= (n[:, None] - w + 1)).astype(F32)
        bu = ((n[None, :] <= n[:, None]) & (n[None, :] >= n[:, None] - w + 1)).astype(F32)
        bandp.append(jnp.einsum("ab,ni->anbi", eye, bp).reshape(ns * ds, ns * POOL_PREV))
        bandu.append(jnp.einsum("ab,nm->anbm", eye, bu).reshape(ns * ds, ns * ds))
        c = jnp.minimum(w, start_pos + n + 1).astype(F32)
        cnt.append(jnp.broadcast_to(jnp.tile(c, ns)[:, None], (ns * ds, gd)))
    return (jnp.stack(bandp).astype(BF16), jnp.stack(bandu).astype(BF16), jnp.stack(cnt))


def _first_max_onehot(v, idx, big, axes):
    m = jnp.max(v, axis=axes, keepdims=True)
    first = jnp.min(jnp.where(v == m, idx, big), axis=axes, keepdims=True)
    return idx == first


def _route_kernel(n_prompt_blocks, xp_ref, xs_ref, modp_ref, modt_ref, g2_ref, wr_ref, br_ref,
                  before_ref, below_ref, h2_ref, pos_ref, wts_ref, cnt_ref, rank_ref, gates_ref):
    i = pl.program_id(0)
    is_p = i < n_prompt_blocks
    x = jnp.where(is_p, xp_ref[...], xs_ref[...])
    shift = jnp.where(is_p, modp_ref[0, 3:4, :], modt_ref[3])
    scale = jnp.where(is_p, modp_ref[0, 4:5, :], modt_ref[4])
    h2 = _rms_mod(x, g2_ref[...], shift, scale)
    h2_ref[...] = h2.astype(BF16)

    hh, hl = _split_bf16(h2, 2)
    wh, wl = _split_bf16(wr_ref[...], 2)
    logits = _dot_nt(wh, hh) + (_dot_nt(wh, hl) + _dot_nt(wl, hh))
    s = jax.nn.sigmoid(logits)
    e, t = s.shape
    ge = e // N_GROUPS
    neg = -jnp.inf
    s3 = s.reshape(N_GROUPS, ge, t)
    sb3 = s3 + br_ref[...]

    j3 = lax.broadcasted_iota(jnp.int32, (N_GROUPS, ge, t), 1)
    top1 = _first_max_onehot(sb3, j3, ge, 1)
    m1 = jnp.max(sb3, axis=1, keepdims=True)
    m2 = jnp.max(jnp.where(top1, neg, sb3), axis=1, keepdims=True)
    gv = m1 + m2

    g3 = lax.broadcasted_iota(jnp.int32, (N_GROUPS, 1, t), 0)
    gsel = jnp.zeros((N_GROUPS, 1, t), F32)
    for _ in range(TOPK_GROUPS):
        hit = _first_max_onehot(gv, g3, N_GROUPS, 0)
        gsel = jnp.where(hit, 1.0, gsel)
        gv = jnp.where(hit, neg, gv)

    e3 = lax.broadcasted_iota(jnp.int32, (N_GROUPS, ge, t), 0) * ge + j3
    ev = jnp.where(gsel > 0.5, sb3, neg)
    sel = jnp.zeros((N_GROUPS, ge, t), F32)
    chosen = jnp.zeros((N_GROUPS, ge, t), F32)
    hits = []
    for _ in range(TOP_K):
        hit = _first_max_onehot(ev, e3, e, (0, 1))
        hits.append(hit)
        sel = jnp.where(hit, s3, sel)
        chosen = jnp.where(hit, 1.0, chosen)
        ev = jnp.where(hit, neg, ev)
    tot = jnp.sum(sel, axis=(0, 1), keepdims=True)
    gates = sel / tot * ROUTED_SCALE

    chosen2 = chosen.reshape(e, t)
    rank = _dot(chosen2.astype(BF16), before_ref[...])
    cnt = jnp.sum(chosen2, axis=1, keepdims=True)
    ovf_units = jnp.floor((jnp.maximum(cnt - SLOT_ROWS, 0.0) + (DMA_ROWS - 1)) / DMA_ROWS)
    ovf_off = _dot(below_ref[...], jnp.broadcast_to(ovf_units, (e, LANES)).astype(BF16))[:, 0:1]
    e_col = lax.broadcasted_iota(jnp.int32, (e, 1), 0).astype(F32)
    slot = jnp.where(rank < SLOT_ROWS, e_col * SLOT_ROWS + rank,
                     e * SLOT_ROWS + ovf_off * DMA_ROWS + (rank - SLOT_ROWS))
    slot3 = slot.reshape(N_GROUPS, ge, t)
    for k, hit in enumerate(hits):
        pos_k = jnp.sum(jnp.where(hit, slot3, 0.0), axis=(0, 1), keepdims=True)
        w_k = jnp.sum(jnp.where(hit, gates, 0.0), axis=(0, 1), keepdims=True)
        pos_ref[k:k + 1, :] = pos_k.reshape(1, t).astype(jnp.int32)
        wts_ref[k:k + 1, :] = w_k.reshape(1, t)
    cnt_ref[0] = jnp.broadcast_to(cnt, (e, LANES)).astype(jnp.int32)
    rank_ref[...] = jnp.where(chosen2 > 0.5, rank, -1.0).astype(jnp.int32)
    gates_ref[...] = gates.reshape(e, t)


def _route(xp2d, xs2d, modp, modt, g2, wr_t, br):
    tp, d = xp2d.shape
    ts = xs2d.shape[0]
    tb = TOKEN_BLOCK
    npb, nsb = tp // tb, ts // tb
    e = wr_t.shape[0]
    seq = tp // modp.shape[0]
    bpb = seq // tb

    def p_idx(i):
        return jnp.minimum(i, npb - 1)

    def s_idx(i):
        return jnp.maximum(i - npb, 0)

    t = tp + ts
    nb = npb + nsb
    tok = jnp.arange(tb)
    before = (tok[:, None] < tok[None, :]).astype(BF16)
    ex = jnp.arange(e)
    below = (ex[None, :] < ex[:, None]).astype(BF16)
    return pl.pallas_call(
        functools.partial(_route_kernel, npb),
        out_shape=(jax.ShapeDtypeStruct((t, d), BF16),
                   jax.ShapeDtypeStruct((TOP_K, t), jnp.int32),
                   jax.ShapeDtypeStruct((TOP_K, t), F32),
                   jax.ShapeDtypeStruct((nb, e, LANES), jnp.int32),
                   jax.ShapeDtypeStruct((e, t), jnp.int32),
                   jax.ShapeDtypeStruct((e, t), F32)),
        grid=(nb,),
        in_specs=[
            pl.BlockSpec((tb, d), lambda i: (p_idx(i), 0)),
            pl.BlockSpec((tb, d), lambda i: (s_idx(i), 0)),
            pl.BlockSpec((1, N_MOD, d), lambda i: (p_idx(i) // bpb, 0, 0)),
            pl.BlockSpec((N_MOD, tb, d), lambda i: (0, s_idx(i), 0)),
            _const_spec(g2.shape), _const_spec(wr_t.shape), _const_spec(br.shape),
            _const_spec(before.shape), _const_spec(below.shape),
        ],
        out_specs=(pl.BlockSpec((tb, d), lambda i: (i, 0)),
                   pl.BlockSpec((TOP_K, tb), lambda i: (0, i)),
                   pl.BlockSpec((TOP_K, tb), lambda i: (0, i)),
                   pl.BlockSpec((1, e, LANES), lambda i: (i, 0, 0)),
                   pl.BlockSpec((e, tb), lambda i: (0, i)),
                   pl.BlockSpec((e, tb), lambda i: (0, i))),
        compiler_params=pltpu.CompilerParams(
            dimension_semantics=("arbitrary",),
            vmem_limit_bytes=_vmem_limit(32 * 1024 * 1024),
        ),
        name="route",
    )(xp2d, xs2d, modp, modt, g2, wr_t, br, before, below)


class _Layout:
    def __init__(self, counts):
        nb, ne = counts.shape
        u, g, tm = SLOT_ROWS, DMA_ROWS, EXPERT_TILE
        pc = (counts + g - 1) // g * g
        ovf_units = (jnp.maximum(counts - u, 0) + g - 1) // g
        rows_e = jnp.sum(pc, axis=0)
        region = (rows_e + u + tm - 1) // tm * tm
        goff = jnp.cumsum(region) - region
        dst = goff[None, :] + jnp.cumsum(pc, axis=0) - pc
        self.dst = dst.reshape(-1)
        ovf_end = jnp.cumsum(ovf_units, axis=1)
        ovf_first = ovf_end - ovf_units
        unit = jnp.arange(_overflow_units_cap(ne))[None, :, None]
        owns = (ovf_first[:, None, :] <= unit) & (unit < ovf_end[:, None, :])
        row = dst[:, None, :] + u + (unit - ovf_first[:, None, :]) * g
        self.ovf_dst = jnp.sum(jnp.where(owns, row, 0), axis=2).reshape(-1)
        self.ovf_tot = ovf_end[:, -1]
        self.tail_start = goff + rows_e
        self.tail_units = (region - rows_e) // g
        tiles = region // tm
        self.region_tiles = jnp.concatenate([tiles, jnp.sum(tiles, keepdims=True)])


def _overflow_units_cap(ne):
    return TOKEN_BLOCK * TOP_K // DMA_ROWS + ne


def _block_buffer_rows(tb, ne):
    rows = ne * SLOT_ROWS + tb * TOP_K + ne * (DMA_ROWS - 1)
    return (rows + ONEHOT_ROWS - 1) // ONEHOT_ROWS * ONEHOT_ROWS


def _expert_rows_cap(t, tb, ne):
    nb = t // tb
    rows = t * TOP_K + nb * ne * (DMA_ROWS - 1) + ne * (SLOT_ROWS + EXPERT_TILE - 1)
    return (rows + EXPERT_TILE - 1) // EXPERT_TILE * EXPERT_TILE


def _slot_copy(buf_ref, slot, e, hbm_ref, row, sem, to_hbm):
    src = buf_ref.at[slot, pl.ds(pl.multiple_of(e * SLOT_ROWS, DMA_ROWS), SLOT_ROWS)]
    dst = hbm_ref.at[pl.ds(pl.multiple_of(row, DMA_ROWS), SLOT_ROWS)]
    return pltpu.make_async_copy(src, dst, sem) if to_hbm else pltpu.make_async_copy(dst, src, sem)


def _unit_copy(buf_ref, slot, buf_row, hbm_ref, row, sem, to_hbm):
    src = buf_ref.at[slot, pl.ds(pl.multiple_of(buf_row, DMA_ROWS), DMA_ROWS)]
    dst = hbm_ref.at[pl.ds(pl.multiple_of(row, DMA_ROWS), DMA_ROWS)]
    return pltpu.make_async_copy(src, dst, sem) if to_hbm else pltpu.make_async_copy(dst, src, sem)


def _slots_wait(ne, buf_ref, slot, hbm_ref, sem, to_hbm):
    src = buf_ref.at[slot, pl.ds(0, ne * SLOT_ROWS)]
    dst = hbm_ref.at[pl.ds(0, ne * SLOT_ROWS)]
    c = pltpu.make_async_copy(src, dst, sem) if to_hbm else pltpu.make_async_copy(dst, src, sem)
    c.wait()


def _overflow_copies(ne, blk, slot, odst_ref, otot_ref, buf_ref, hbm_ref, sem, to_hbm, wait):
    def per_unit(i, carry):
        row = odst_ref[blk * _overflow_units_cap(ne) + i]
        c = _unit_copy(buf_ref, slot, ne * SLOT_ROWS + i * DMA_ROWS, hbm_ref, row, sem, to_hbm)
        c.wait() if wait else c.start()
        return carry

    lax.fori_loop(0, otot_ref[blk], per_unit, 0)


def _slot_chunks(ne):
    return [range(c, c + SLOT_CHUNK_EXPERTS) for c in range(0, ne, SLOT_CHUNK_EXPERTS)]


def _overflow_chunks(n_units):
    return lax.shift_right_logical(n_units * DMA_ROWS + (ONEHOT_ROWS - 1),
                                   ONEHOT_ROWS.bit_length() - 1)


def _overflow_select(pos, vals, r0):
    rows = lax.broadcasted_iota(jnp.int32, (ONEHOT_ROWS, pos.shape[1]), 0) + r0
    sel = jnp.zeros(rows.shape, F32)
    for k in range(TOP_K):
        v = 1.0 if vals is None else vals[k:k + 1, :]
        sel = jnp.where(pos[k:k + 1, :] == rows, v, sel)
    return sel


def _dispatch_kernel(ne, dst_ref, odst_ref, otot_ref, tails_ref, tailu_ref,
                     h2_ref, rank_ref, pos_ref, xs_hbm, buf_ref, zero_ref, sem, zsem):
    b = pl.program_id(0)
    nb = pl.num_programs(0)
    slot = b % 2
    h2 = h2_ref[...]
    j = lax.broadcasted_iota(jnp.int32, (SLOT_ROWS, h2.shape[0]), 0)

    for ci, experts in enumerate(_slot_chunks(ne)):
        onehot = jnp.concatenate(
            [jnp.where(rank_ref[e:e + 1, :] == j, 1.0, 0.0) for e in experts], axis=0)
        r0 = experts[0] * SLOT_ROWS
        buf_ref[slot, r0:r0 + len(experts) * SLOT_ROWS, :] = (
            _dot(onehot.astype(BF16), h2).astype(BF16))
        if ci == 0:
            @pl.when(b > 0)
            def _prev():
                _slots_wait(ne, buf_ref, 1 - slot, xs_hbm, sem.at[1 - slot], True)
                _overflow_copies(ne, b - 1, 1 - slot, odst_ref, otot_ref, buf_ref, xs_hbm,
                                 sem.at[1 - slot], True, True)
        for e in experts:
            _slot_copy(buf_ref, slot, e, xs_hbm, dst_ref[b * ne + e], sem.at[slot], True).start()

    pos = pos_ref[...]

    def chunk(ci, carry):
        r0 = pl.multiple_of(ne * SLOT_ROWS + ci * ONEHOT_ROWS, ONEHOT_ROWS)
        onehot = _overflow_select(pos, None, r0).astype(BF16)
        buf_ref[slot, pl.ds(r0, ONEHOT_ROWS), :] = _dot(onehot, h2).astype(BF16)
        return carry

    lax.fori_loop(0, _overflow_chunks(otot_ref[b]), chunk, 0)
    _overflow_copies(ne, b, slot, odst_ref, otot_ref, buf_ref, xs_hbm, sem.at[slot], True, False)

    @pl.when(b == nb - 1)
    def _last():
        _slots_wait(ne, buf_ref, slot, xs_hbm, sem.at[slot], True)
        _overflow_copies(ne, b, slot, odst_ref, otot_ref, buf_ref, xs_hbm, sem.at[slot], True,
                         True)
        zero_ref[...] = jnp.zeros_like(zero_ref)

        def tail(wait):
            def per_expert(e, carry):
                def per_unit(i, carry2):
                    row = pl.multiple_of(tails_ref[e] + i * DMA_ROWS, DMA_ROWS)
                    c = pltpu.make_async_copy(zero_ref, xs_hbm.at[pl.ds(row, DMA_ROWS)], zsem)
                    c.wait() if wait else c.start()
                    return carry2
                lax.fori_loop(0, tailu_ref[e], per_unit, 0)
                return carry
            lax.fori_loop(0, ne, per_expert, 0)

        tail(False)
        tail(True)


def _dispatch(h2, rank, pos, lay, ne, rows_cap):
    t, d = h2.shape
    tb = TOKEN_BLOCK
    buf_rows = _block_buffer_rows(tb, ne)
    return pl.pallas_call(
        functools.partial(_dispatch_kernel, ne),
        out_shape=jax.ShapeDtypeStruct((rows_cap, d), BF16),
        grid_spec=pltpu.PrefetchScalarGridSpec(
            num_scalar_prefetch=5,
            grid=(t // tb,),
            in_specs=[
                pl.BlockSpec((tb, d), lambda i, *_: (i, 0)),
                pl.BlockSpec((ne, tb), lambda i, *_: (0, i)),
                pl.BlockSpec((TOP_K, tb), lambda i, *_: (0, i)),
            ],
            out_specs=pl.BlockSpec(memory_space=pl.ANY),
            scratch_shapes=[
                pltpu.VMEM((2, buf_rows, d), BF16),
                pltpu.VMEM((DMA_ROWS, d), BF16),
                pltpu.SemaphoreType.DMA((2,)),
                pltpu.SemaphoreType.DMA(()),
            ],
        ),
        compiler_params=pltpu.CompilerParams(
            dimension_semantics=("arbitrary",),
            vmem_limit_bytes=_vmem_limit(4 * buf_rows * d + 16 * 1024 * 1024),
        ),
        name="dispatch",
    )(lay.dst, lay.ovf_dst, lay.ovf_tot, lay.tail_start, lay.tail_units, h2, rank, pos)


def _experts_kernel(tiles_ref, xe_hbm, wg_ref, wu_ref, wd_ref, ye_hbm,
                    xbuf_ref, ybuf_ref, wgb_ref, wub_ref, wdb_ref, done_ref, xsem, ysem):
    e = pl.program_id(0)
    ne = pl.num_programs(0)
    tm = xbuf_ref.shape[1]

    def x_copy(row, slot):
        return pltpu.make_async_copy(xe_hbm.at[pl.ds(pl.multiple_of(row, tm), tm)],
                                     xbuf_ref.at[slot], xsem.at[slot])

    def y_copy(row, slot):
        return pltpu.make_async_copy(ybuf_ref.at[slot],
                                     ye_hbm.at[pl.ds(pl.multiple_of(row, tm), tm)], ysem.at[slot])

    total = tiles_ref[ne]

    @pl.when(e == 0)
    def _first():
        done_ref[0] = 0
        for g in range(RING_SLOTS - 1):
            x_copy(g * tm, g).start()

    wgb_ref[...] = wg_ref[0].astype(BF16)
    wub_ref[...] = wu_ref[0].astype(BF16)
    wdb_ref[...] = wd_ref[0].astype(BF16)
    done = done_ref[0]
    n = tiles_ref[e]

    def tile(j, carry):
        g = done + j
        slot = lax.rem(g, RING_SLOTS)
        x_copy(0, slot).wait()
        ahead = g + (RING_SLOTS - 1)

        @pl.when(ahead < total)
        def _prefetch():
            x_copy(ahead * tm, lax.rem(ahead, RING_SLOTS)).start()

        @pl.when(g >= RING_SLOTS)
        def _reuse():
            y_copy(0, slot).wait()

        part = tm // EXPERT_TILE_PARTS
        starts = range(0, tm, part)
        hids = []
        for r0 in starts:
            x = xbuf_ref[slot, r0:r0 + part, :]
            hids.append((jax.nn.silu(_dot(x, wgb_ref[...])) * _dot(x, wub_ref[...])).astype(BF16))
        for r0, hid in zip(starts, hids):
            ybuf_ref[slot, r0:r0 + part, :] = _dot(hid, wdb_ref[...]).astype(BF16)
        y_copy(g * tm, slot).start()
        return carry

    lax.fori_loop(0, n, tile, 0)
    done_ref[0] = done + n

    @pl.when(e == ne - 1)
    def _drain():
        for s in range(RING_SLOTS):
            y_copy(0, s).wait()


def _experts(xs, lay, wg, wu, wd):
    rows_cap, d = xs.shape
    ne, _, de = wg.shape
    assert ne >= RING_SLOTS
    tm = EXPERT_TILE
    ring = RING_SLOTS
    return pl.pallas_call(
        _experts_kernel,
        out_shape=jax.ShapeDtypeStruct((rows_cap, d), BF16),
        grid_spec=pltpu.PrefetchScalarGridSpec(
            num_scalar_prefetch=1,
            grid=(ne,),
            in_specs=[
                pl.BlockSpec(memory_space=pl.ANY),
                pl.BlockSpec((1, d, de), lambda e, *_: (e, 0, 0)),
                pl.BlockSpec((1, d, de), lambda e, *_: (e, 0, 0)),
                pl.BlockSpec((1, de, d), lambda e, *_: (e, 0, 0)),
            ],
            out_specs=pl.BlockSpec(memory_space=pl.ANY),
            scratch_shapes=[
                pltpu.VMEM((ring, tm, d), BF16), pltpu.VMEM((ring, tm, d), BF16),
                pltpu.VMEM((d, de), BF16), pltpu.VMEM((d, de), BF16), pltpu.VMEM((de, d), BF16),
                pltpu.SMEM((1,), jnp.int32),
                pltpu.SemaphoreType.DMA((ring,)), pltpu.SemaphoreType.DMA((ring,)),
            ],
        ),
        compiler_params=pltpu.CompilerParams(
            dimension_semantics=("arbitrary",),
            vmem_limit_bytes=_vmem_limit(32 * 1024 * 1024),
        ),
        name="experts",
    )(lay.region_tiles, xs, wg, wu, wd)


def _final_kernel(n_prompt_blocks, ne, dst_ref, odst_ref, otot_ref,
                  xp_ref, xs_ref, modp_ref, modt_ref, h2_ref, rank_ref, gates_ref, pos_ref, wts_ref,
                  wsg_ref, wsu_ref, wsd_ref, gf_ref, ye_hbm, yp_ref, ys_ref,
                  buf_ref, acc_ref, sem):
    i = pl.program_id(0)
    nb = pl.num_programs(0)
    slot = i % 2

    def fetch_slots(blk, sl, experts):
        for e in experts:
            _slot_copy(buf_ref, sl, e, ye_hbm, dst_ref[blk * ne + e], sem.at[sl], False).start()

    def overflow(blk, sl, wait):
        _overflow_copies(ne, blk, sl, odst_ref, otot_ref, buf_ref, ye_hbm, sem.at[sl], False, wait)

    @pl.when(i == 0)
    def _first():
        buf_ref[...] = jnp.zeros_like(buf_ref)
        fetch_slots(0, 0, range(ne))
        overflow(0, 0, False)

    _slots_wait(ne, buf_ref, slot, ye_hbm, sem.at[slot], False)
    overflow(i, slot, True)

    nxt = jnp.minimum(i + 1, nb - 1)
    tb = h2_ref.shape[0]
    j = lax.broadcasted_iota(jnp.int32, (SLOT_ROWS, tb), 0)
    acc = jnp.zeros((tb, h2_ref.shape[1]), F32)
    for experts in _slot_chunks(ne):
        sel = jnp.concatenate(
            [jnp.where(rank_ref[e:e + 1, :] == j, gates_ref[e:e + 1, :], 0.0) for e in experts],
            axis=0)
        r0 = experts[0] * SLOT_ROWS
        acc = acc + _dot_tn(sel.astype(BF16), buf_ref[slot, r0:r0 + len(experts) * SLOT_ROWS, :])
        fetch_slots(nxt, 1 - slot, experts)
    acc_ref[...] = acc
    overflow(nxt, 1 - slot, False)

    pos = pos_ref[...]
    wts = wts_ref[...]

    def chunk(ci, carry):
        r0 = pl.multiple_of(ne * SLOT_ROWS + ci * ONEHOT_ROWS, ONEHOT_ROWS)
        sel = _overflow_select(pos, wts, r0).astype(BF16)
        acc_ref[...] += _dot_tn(sel, buf_ref[slot, pl.ds(r0, ONEHOT_ROWS), :])
        return carry

    lax.fori_loop(0, _overflow_chunks(otot_ref[i]), chunk, 0)

    is_p = i < n_prompt_blocks
    x = jnp.where(is_p, xp_ref[...], xs_ref[...])
    gate = jnp.where(is_p, modp_ref[0, 5:6, :], modt_ref[5])
    hb = h2_ref[...]
    hid = jax.nn.silu(_dot(hb, wsg_ref[...])) * _dot(hb, wsu_ref[...])
    f = acc_ref[...] + _dot(hid.astype(BF16), wsd_ref[...])
    x2 = x + gate * f
    y = x2 * lax.rsqrt(jnp.mean(x2 * x2, axis=-1, keepdims=True) + EPS) * gf_ref[...]

    @pl.when(is_p)
    def _p():
        yp_ref[...] = y

    @pl.when(jnp.logical_not(is_p))
    def _s():
        ys_ref[...] = y

    @pl.when(i == nb - 1)
    def _drain():
        _slots_wait(ne, buf_ref, 1 - slot, ye_hbm, sem.at[1 - slot], False)
        overflow(nxt, 1 - slot, True)


def _final(xp2d, xs2d, modp, modt, h2, rank, gates, pos, wts, ye, lay, ne, wsg, wsu, wsd, gf):
    tp, d = xp2d.shape
    ts = xs2d.shape[0]
    tb = TOKEN_BLOCK
    npb, nsb = tp // tb, ts // tb
    seq = tp // modp.shape[0]
    bpb = seq // tb
    buf_rows = _block_buffer_rows(tb, ne)

    def p_idx(i):
        return jnp.minimum(i, npb - 1)

    def s_idx(i):
        return jnp.maximum(i - npb, 0)

    def const(shape):
        zeros = (0,) * len(shape)
        return pl.BlockSpec(shape, lambda i, *_: zeros)

    return pl.pallas_call(
        functools.partial(_final_kernel, npb, ne),
        out_shape=(jax.ShapeDtypeStruct((tp, d), F32), jax.ShapeDtypeStruct((ts, d), F32)),
        grid_spec=pltpu.PrefetchScalarGridSpec(
            num_scalar_prefetch=3,
            grid=(npb + nsb,),
            in_specs=[
                pl.BlockSpec((tb, d), lambda i, *_: (p_idx(i), 0)),
                pl.BlockSpec((tb, d), lambda i, *_: (s_idx(i), 0)),
                pl.BlockSpec((1, N_MOD, d), lambda i, *_: (p_idx(i) // bpb, 0, 0)),
                pl.BlockSpec((N_MOD, tb, d), lambda i, *_: (0, s_idx(i), 0)),
                pl.BlockSpec((tb, d), lambda i, *_: (i, 0)),
                pl.BlockSpec((ne, tb), lambda i, *_: (0, i)),
                pl.BlockSpec((ne, tb), lambda i, *_: (0, i)),
                pl.BlockSpec((TOP_K, tb), lambda i, *_: (0, i)),
                pl.BlockSpec((TOP_K, tb), lambda i, *_: (0, i)),
                const(wsg.shape), const(wsu.shape), const(wsd.shape), const(gf.shape),
                pl.BlockSpec(memory_space=pl.ANY),
            ],
            out_specs=(pl.BlockSpec((tb, d), lambda i, *_: (p_idx(i), 0)),
                       pl.BlockSpec((tb, d), lambda i, *_: (s_idx(i), 0))),
            scratch_shapes=[
                pltpu.VMEM((2, buf_rows, d), BF16),
                pltpu.VMEM((tb, d), F32),
                pltpu.SemaphoreType.DMA((2,)),
            ],
        ),
        compiler_params=pltpu.CompilerParams(
            dimension_semantics=("arbitrary",),
            vmem_limit_bytes=_vmem_limit(4 * buf_rows * d + 32 * 1024 * 1024),
        ),
        name="final",
    )(lay.dst, lay.ovf_dst, lay.ovf_tot,
      xp2d, xs2d, modp, modt, h2, rank, gates, pos, wts, wsg, wsu, wsd, gf, ye)


def kernel(x_prompt, x_sample, c_prompt, c_sample, state_pool, state_ret, w_ada, b_ada, g_norm1,
           w_in, w_pool_group, pool_scale, w_branch_pool, w_branch_ret, w_out, g_norm2, w_router,
           b_router, w_exp_gate, w_exp_up, w_exp_down, w_sh_gate, w_sh_up, w_sh_down, g_final):
    b, s, d = x_prompt.shape
    db, ds, _ = x_sample.shape
    depth = w_ada.shape[0]
    heads, dk, dv = state_ret.shape[2:]
    pool_buf, d_pool = state_pool.shape[2:]
    dims = _Dims(d, d_pool, heads, dk, dv)
    assert dims.d_in == w_in.shape[2]
    assert pool_buf == max(POOL_WINDOWS) - 1 and pool_buf < POOL_PREV
    assert s % PROMPT_STEP_ROWS == 0 and PROMPT_STEP_ROWS % RET_CHUNK == 0
    assert s % TOKEN_BLOCK == 0 and (db * ds) % TOKEN_BLOCK == 0
    assert db % SAMPLE_SEQS_PER_STEP == 0 and SLOT_ROWS % DMA_ROWS == 0

    log_g = jnp.log(1.0 - jnp.power(2.0, -5.0 - jnp.arange(heads, dtype=F32)))
    ns = SAMPLE_SEQS_PER_STEP
    cos_p, sin_p = _rotary_tables(jnp.arange(s, dtype=jnp.int32), dk)
    cos_s, sin_s = _rotary_tables(PAST_LEN + jnp.arange(ds, dtype=jnp.int32), dk)
    cos_s, sin_s = jnp.tile(cos_s, (ns, 1)), jnp.tile(sin_s, (ns, 1))
    dec_p = _decay_tables(log_g, RET_CHUNK, 1, dk, dv)
    dec_s = _decay_tables(log_g, ds, ns, dk, dv)
    pool_tabs = _sample_pool_tables(ns, ds, dims.gd, PAST_LEN)

    assert depth == 1, "the final kernel fuses the output norm into the (single) layer"
    l = 0
    xs = x_sample.reshape(db * ds, d)
    modp, modt = _ada(c_prompt, jnp.repeat(c_sample, ds, axis=0), w_ada[l], b_ada[l])
    modp = modp.reshape(b, N_MOD, d)
    g1 = g_norm1[l].reshape(1, d)
    weights = (w_in[l].astype(BF16), w_pool_group[l].astype(BF16),
               pool_scale[l].reshape(1, d_pool), w_branch_pool[l].astype(BF16),
               w_branch_ret[l].astype(BF16), w_out[l].astype(BF16))

    x1p, pool16, ret_p = _mixer_prompt(dims, x_prompt, modp, g1, weights,
                                       (cos_p, sin_p) + dec_p)
    pool_in = jnp.pad(state_pool[l], ((0, 0), (POOL_PREV - pool_buf, 0), (0, 0)))
    x1s, u_s, ret_s = _mixer_sample(dims, ds, xs, modt, g1, weights,
                                    (cos_s, sin_s) + dec_s + pool_tabs, pool_in,
                                    state_ret[l].astype(F32))
    pool_p = pool16[:, POOL_PREV - pool_buf:]
    pool_s = jnp.concatenate([state_pool[l], u_s.reshape(db, ds, d_pool)], axis=1)[:, -pool_buf:]

    x1p2d = x1p.reshape(b * s, d)
    ne = w_router.shape[2]
    h2, pos, wts, counts, rank, gates = _route(
        x1p2d, x1s, modp, modt, g_norm2[l].reshape(1, d), w_router[l].T,
        b_router[l].reshape(N_GROUPS, ne // N_GROUPS, 1))
    rows_cap = _expert_rows_cap(b * s + db * ds, TOKEN_BLOCK, ne)
    lay = _Layout(counts[:, :, 0])
    xe = _dispatch(h2, rank, pos, lay, ne, rows_cap)
    ye = _experts(xe, lay, w_exp_gate[l], w_exp_up[l], w_exp_down[l])
    yp, ys = _final(x1p2d, x1s, modp, modt, h2, rank, gates, pos, wts, ye, lay, ne,
                    w_sh_gate[l].astype(BF16), w_sh_up[l].astype(BF16),
                    w_sh_down[l].astype(BF16), g_final.reshape(1, d))

    return (yp.reshape(b, s, d), ys.reshape(db, ds, d), pool_p[None],
            ret_p.astype(x_prompt.dtype), pool_s[None], ret_s.astype(state_ret.dtype)[None])
```

```python
import functools

import jax
import jax.numpy as jnp
import numpy as np
from jax import lax
from jax.experimental import pallas as pl
from jax.experimental.pallas import tpu as pltpu

F32 = jnp.float32
BF16 = jnp.bfloat16

PAST_LEN = 16384
POOL_WINDOWS = (2, 4, 8, 16)
RET_CHUNK = 128
ROPE_BASE = 10000.0
TOP_K = 8
N_GROUPS = 8
TOPK_GROUPS = 4
ROUTED_SCALE = 2.5
N_MOD = 6
EPS = 1e-6

V7X_VMEM_BYTES = 64 * 1024 * 1024
SUBLANES = 8
LANES = 128

PROMPT_STEP_ROWS = 512
SAMPLE_SEQS_PER_STEP = 8
SAMPLE_STEPS_PER_GROUP = 4
TOKEN_BLOCK = 256
DMA_ROWS = 16
SLOT_ROWS = 3 * DMA_ROWS
ONEHOT_ROWS = 256
SLOT_CHUNK_EXPERTS = 16
EXPERT_TILE_PARTS = 2
RING_SLOTS = 3
EXPERT_TILE = 512
POOL_PREV = 16


def _vmem_limit(nbytes):
    return int(min(nbytes, V7X_VMEM_BYTES - 8 * 1024 * 1024))


def _dot(a, b):
    return jnp.dot(a, b, preferred_element_type=F32)


def _dot_nt(a, b):
    return lax.dot_general(a, b, (((1,), (1,)), ((), ())), preferred_element_type=F32)


def _dot_tn(a, b):
    return lax.dot_general(a, b, (((0,), (0,)), ((), ())), preferred_element_type=F32)


def _split_bf16(x, parts):
    out = []
    for _ in range(parts):
        p = x.astype(BF16)
        out.append(p)
        x = x - p.astype(F32)
    return out


def _rms_mod(x, g, shift, scale):
    y = x * lax.rsqrt(jnp.mean(x * x, axis=-1, keepdims=True) + EPS)
    return (y * g) * (1.0 + scale) + shift


def _rotary(x, cosb, sinb):
    return x * cosb + pltpu.roll(x, x.shape[-1] // 2, axis=1) * sinb


def _group_norm(o):
    return o * lax.rsqrt(jnp.mean(o * o, axis=-1, keepdims=True) + EPS)


def _ada_kernel(cp_ref, ct_ref, w_ref, b_ref, modp_ref, modt_ref):
    w = w_ref[...].astype(BF16)
    modp_ref[...] = _dot(cp_ref[...].astype(BF16), w) + b_ref[...]
    modt_ref[0] = _dot(ct_ref[...].astype(BF16), w) + b_ref[...]


def _ada(c_prompt, c_tokens, w_ada, b_ada):
    bp, d = c_prompt.shape
    ts = c_tokens.shape[0]
    n = w_ada.shape[1]
    return pl.pallas_call(
        _ada_kernel,
        out_shape=(jax.ShapeDtypeStruct((bp, n), F32), jax.ShapeDtypeStruct((n // d, ts, d), F32)),
        grid=(n // d,),
        in_specs=[
            pl.BlockSpec((bp, d), lambda j: (0, 0)),
            pl.BlockSpec((ts, d), lambda j: (0, 0)),
            pl.BlockSpec((d, d), lambda j: (0, j)),
            pl.BlockSpec((1, d), lambda j: (0, j)),
        ],
        out_specs=(pl.BlockSpec((bp, d), lambda j: (0, j)),
                   pl.BlockSpec((1, ts, d), lambda j: (j, 0, 0))),
        compiler_params=pltpu.CompilerParams(
            dimension_semantics=("arbitrary",),
            vmem_limit_bytes=_vmem_limit(32 * 1024 * 1024),
        ),
        name="ada",
    )(c_prompt, c_tokens, w_ada, b_ada.reshape(1, n))


class _Dims:
    def __init__(self, d_model, d_pool, heads, dk, dv):
        self.d = d_model
        self.d_pool = d_pool
        self.gd = d_pool // len(POOL_WINDOWS)
        self.heads = heads
        self.dk = dk
        self.dv = dv
        self.d_k = heads * dk
        self.d_v = heads * dv
        widths = (d_pool, self.d_k, self.d_k, self.d_v, self.d_v, d_model, d_model)
        offs = [0]
        for w in widths:
            offs.append(offs[-1] + w)
        self.cols = tuple((offs[i], offs[i + 1]) for i in range(len(widths)))
        self.d_in = offs[-1]


def _proj(hb, win_ref, cols):
    return _dot(hb, win_ref[:, cols[0]:cols[1]])


def _pool_branch(pm_parts, wpg_ref, ps_ref, wbp_ref):
    pg = [_dot(pm.astype(BF16), wpg_ref[gi]) for gi, pm in enumerate(pm_parts)]
    pm = jnp.concatenate(pg, axis=1) * ps_ref[...]
    return _dot(pm.astype(BF16), wbp_ref[...])


def _merge_out(x, gate, a, o, g, ga, gb, wbr_ref, wout_ref):
    r = _dot((jax.nn.silu(g) * o).astype(BF16), wbr_ref[...])
    merged = jax.nn.sigmoid(ga) * a + jax.nn.sigmoid(gb) * r
    m = _dot(merged.astype(BF16), wout_ref[...])
    return x + gate * m


def _mixer_prompt_kernel(dims, x_ref, mod_ref, g1_ref, win_ref, wpg_ref, ps_ref, wbp_ref,
                         wbr_ref, wout_ref, cos_ref, sin_ref, intra_ref, qdec_ref,
                         kdec_ref, sdec_ref, x1_ref, pool_ref, ret_ref, uext_ref):
    c = pl.program_id(1)
    L = x_ref.shape[1]

    @pl.when(c == 0)
    def _init():
        ret_ref[...] = jnp.zeros_like(ret_ref)
        uext_ref[0:POOL_PREV, :] = jnp.zeros((POOL_PREV, dims.d_pool), F32)

    x = x_ref[0]
    shift, scale, gate = mod_ref[0, 0:1, :], mod_ref[0, 1:2, :], mod_ref[0, 2:3, :]
    hb = _rms_mod(x, g1_ref[...], shift, scale).astype(BF16)

    u = _proj(hb, win_ref, dims.cols[0])
    uext_ref[POOL_PREV:POOL_PREV + L, :] = u
    pos = c * L + lax.broadcasted_iota(jnp.int32, (L, 1), 0)
    pm_parts = []
    for gi, w in enumerate(POOL_WINDOWS):
        s = uext_ref[:, gi * dims.gd:(gi + 1) * dims.gd]
        sh = 1
        while sh < w:
            s = s + pltpu.roll(s, sh, axis=0)
            sh *= 2
        cnt = jnp.minimum(w, pos + 1).astype(F32)
        pm_parts.append(s[POOL_PREV:, :] / cnt - u[:, gi * dims.gd:(gi + 1) * dims.gd])
    a = _pool_branch(pm_parts, wpg_ref, ps_ref, wbp_ref)
    uext_ref[0:POOL_PREV, :] = uext_ref[L:L + POOL_PREV, :]
    pool_ref[0] = uext_ref[0:POOL_PREV, :]

    q = _proj(hb, win_ref, dims.cols[1])
    k = _proj(hb, win_ref, dims.cols[2])
    v = _proj(hb, win_ref, dims.cols[3])
    chunk_outs = []
    for r0 in range(0, L, RET_CHUNK):
        rows = slice(r0, r0 + RET_CHUNK)
        cosb, sinb = cos_ref[rows, :], sin_ref[rows, :]
        outs = []
        for h in range(dims.heads):
            qr = _rotary(q[rows, h * dims.dk:(h + 1) * dims.dk], cosb, sinb)
            kr = _rotary(k[rows, h * dims.dk:(h + 1) * dims.dk], cosb, sinb) * (dims.dk ** -0.5)
            vb = v[rows, h * dims.dv:(h + 1) * dims.dv].astype(BF16)
            scores = _dot_nt(qr.astype(BF16), kr.astype(BF16)) * intra_ref[h]
            s_prev = ret_ref[0, 0, h]
            o = _dot(scores.astype(BF16), vb) + _dot((qr * qdec_ref[h]).astype(BF16),
                                                      s_prev.astype(BF16))
            ret_ref[0, 0, h] = (sdec_ref[h] * s_prev
                                + _dot_tn((kr * kdec_ref[h]).astype(BF16), vb))
            outs.append(_group_norm(o))
        chunk_outs.append(jnp.concatenate(outs, axis=1))
    o = jnp.concatenate(chunk_outs, axis=0)

    g = _proj(hb, win_ref, dims.cols[4])
    ga = _proj(hb, win_ref, dims.cols[5])
    gb = _proj(hb, win_ref, dims.cols[6])
    x1_ref[0] = _merge_out(x, gate, a, o, g, ga, gb, wbr_ref, wout_ref)


def _const_spec(shape):
    zeros = (0,) * len(shape)
    return pl.BlockSpec(shape, lambda *_: zeros)


def _resident_spec(shape):
    zeros = (0,) * len(shape)
    return pl.BlockSpec(shape, lambda *_: zeros, pipeline_mode=pl.Buffered(1))


def _mixer_prompt(dims, x, mod, g1, weights, tables):
    b, s, d = x.shape
    L = PROMPT_STEP_ROWS
    nc = s // L
    win, wpg, ps, wbp, wbr, wout = weights
    cosb, sinb, intra, qdec, kdec, sdec = tables
    in_specs = [
        pl.BlockSpec((1, L, d), lambda i, c: (i, c, 0)),
        pl.BlockSpec((1, N_MOD, d), lambda i, c: (i, 0, 0)),
        _const_spec(g1.shape), _resident_spec(win.shape), _const_spec(wpg.shape),
        _const_spec(ps.shape), _resident_spec(wbp.shape), _resident_spec(wbr.shape),
        _resident_spec(wout.shape),
        pl.BlockSpec((L, dims.dk), lambda i, c: (c, 0)),
        pl.BlockSpec((L, dims.dk), lambda i, c: (c, 0)),
        _const_spec(intra.shape), _const_spec(qdec.shape), _const_spec(kdec.shape),
        _const_spec(sdec.shape),
    ]
    out_shape = (
        jax.ShapeDtypeStruct((b, s, d), F32),
        jax.ShapeDtypeStruct((b, POOL_PREV, dims.d_pool), F32),
        jax.ShapeDtypeStruct((1, b, dims.heads, dims.dk, dims.dv), F32),
    )
    out_specs = (
        pl.BlockSpec((1, L, d), lambda i, c: (i, c, 0)),
        pl.BlockSpec((1, POOL_PREV, dims.d_pool), lambda i, c: (i, 0, 0)),
        pl.BlockSpec((1, 1, dims.heads, dims.dk, dims.dv), lambda i, c: (0, i, 0, 0, 0)),
    )
    weight_bytes = 2 * (win.size + wpg.size + wbp.size + wbr.size + wout.size)
    return pl.pallas_call(
        functools.partial(_mixer_prompt_kernel, dims),
        out_shape=out_shape,
        grid=(b, nc),
        in_specs=in_specs,
        out_specs=out_specs,
        scratch_shapes=[pltpu.VMEM((POOL_PREV + L, dims.d_pool), F32)],
        compiler_params=pltpu.CompilerParams(
            dimension_semantics=("arbitrary", "arbitrary"),
            vmem_limit_bytes=_vmem_limit(2 * weight_bytes + 16 * 1024 * 1024),
        ),
        name="mixer_prompt",
    )(x, mod, g1, win, wpg, ps, wbp, wbr, wout, cosb, sinb, intra, qdec, kdec, sdec)


def _mixer_sample_kernel(dims, ds, x_ref, modt_ref, g1_ref, win_ref, wpg_ref, ps_ref, wbp_ref,
                         wbr_ref, wout_ref, cos_ref, sin_ref, intra_ref, qdec_ref, kdec_ref,
                         sdec_ref, bandp_ref, bandu_ref, cnt_ref, poolin_ref, sret_ref,
                         x1_ref, u_ref, ret_ref, hb_ref, q_ref, k_ref, v_ref, pm_ref, o_ref):
    j = pl.program_id(1)
    R = cos_ref.shape[0]
    ns = R // ds

    @pl.when(j == 0)
    def _project():
        hb = _rms_mod(x_ref[...], g1_ref[...], modt_ref[0], modt_ref[1]).astype(BF16)
        hb_ref[...] = hb
        u_ref[...] = _proj(hb, win_ref, dims.cols[0])
        q_ref[...] = _proj(hb, win_ref, dims.cols[1])
        k_ref[...] = _proj(hb, win_ref, dims.cols[2])
        v_ref[...] = _proj(hb, win_ref, dims.cols[3]).astype(BF16)

    rows = pl.ds(pl.multiple_of(j * R, R), R)

    u = u_ref[rows, :]
    hist = poolin_ref[...].reshape(ns * POOL_PREV, dims.d_pool)
    pm_parts = []
    for gi in range(len(POOL_WINDOWS)):
        lo, hi = gi * dims.gd, (gi + 1) * dims.gd
        win_sum = jnp.zeros((R, dims.gd), F32)
        for part in _split_bf16(hist[:, lo:hi], 3):
            win_sum = win_sum + _dot(bandp_ref[gi], part)
        for part in _split_bf16(u[:, lo:hi], 3):
            win_sum = win_sum + _dot(bandu_ref[gi], part)
        pm_parts.append(win_sum / cnt_ref[gi] - u[:, lo:hi])
    pm_ref[rows, :] = jnp.concatenate(pm_parts, axis=1)

    q = q_ref[rows, :]
    k = k_ref[rows, :]
    v = v_ref[rows, :]
    cosb, sinb = cos_ref[...], sin_ref[...]
    row_seq = lax.broadcasted_iota(jnp.int32, (R, ns * dims.dk), 0) // ds
    col_seq = lax.broadcasted_iota(jnp.int32, (R, ns * dims.dk), 1) // dims.dk
    own = row_seq == col_seq
    outs = []
    for h in range(dims.heads):
        qr = _rotary(q[:, h * dims.dk:(h + 1) * dims.dk], cosb, sinb)
        kr = _rotary(k[:, h * dims.dk:(h + 1) * dims.dk], cosb, sinb) * (dims.dk ** -0.5)
        vb = v[:, h * dims.dv:(h + 1) * dims.dv]
        scores = _dot_nt(qr.astype(BF16), kr.astype(BF16)) * intra_ref[h]
        s_prev = sret_ref[:, h].reshape(ns * dims.dk, dims.dv)
        qd = jnp.concatenate([qr * qdec_ref[h]] * ns, axis=1)
        kd = jnp.concatenate([kr * kdec_ref[h]] * ns, axis=1)
        qexp = jnp.where(own, qd, 0.0).astype(BF16)
        kexp = jnp.where(own, kd, 0.0).astype(BF16)
        o = _dot(scores.astype(BF16), vb) + _dot(qexp, s_prev.astype(BF16))
        s_new = sdec_ref[h] * s_prev + _dot_tn(kexp, vb)
        ret_ref[:, h] = s_new.reshape(ns, dims.dk, dims.dv)
        outs.append(_group_norm(o))
    o_ref[rows, :] = jnp.concatenate(outs, axis=1)

    @pl.when(j == pl.num_programs(1) - 1)
    def _merge():
        hb = hb_ref[...]
        pm = pm_ref[...]
        pm_parts = [pm[:, gi * dims.gd:(gi + 1) * dims.gd] for gi in range(len(POOL_WINDOWS))]
        a = _pool_branch(pm_parts, wpg_ref, ps_ref, wbp_ref)
        g = _proj(hb, win_ref, dims.cols[4])
        ga = _proj(hb, win_ref, dims.cols[5])
        gb = _proj(hb, win_ref, dims.cols[6])
        x1_ref[...] = _merge_out(x_ref[...], modt_ref[2], a, o_ref[...], g, ga, gb,
                                 wbr_ref, wout_ref)


def _mixer_sample(dims, ds, x2d, modt, g1, weights, tables, pool_in, state_ret):
    t, d = x2d.shape
    ns = SAMPLE_SEQS_PER_STEP
    steps = SAMPLE_STEPS_PER_GROUP
    R = ns * ds
    RG = R * steps
    win, wpg, ps, wbp, wbr, wout = weights
    cosb, sinb, intra, qdec, kdec, sdec, bandp, bandu, cnt = tables
    db = state_ret.shape[0]
    in_specs = [
        pl.BlockSpec((RG, d), lambda i, j: (i, 0)),
        pl.BlockSpec((N_MOD, RG, d), lambda i, j: (0, i, 0)),
        _const_spec(g1.shape), _resident_spec(win.shape), _const_spec(wpg.shape),
        _const_spec(ps.shape), _resident_spec(wbp.shape), _resident_spec(wbr.shape),
        _resident_spec(wout.shape), _const_spec(cosb.shape), _const_spec(sinb.shape),
        _const_spec(intra.shape), _const_spec(qdec.shape), _const_spec(kdec.shape),
        _const_spec(sdec.shape), _const_spec(bandp.shape), _const_spec(bandu.shape),
        _const_spec(cnt.shape),
        pl.BlockSpec((ns, POOL_PREV, dims.d_pool), lambda i, j: (i * steps + j, 0, 0)),
        pl.BlockSpec((ns, dims.heads, dims.dk, dims.dv), lambda i, j: (i * steps + j, 0, 0, 0)),
    ]
    out_shape = (
        jax.ShapeDtypeStruct((t, d), F32),
        jax.ShapeDtypeStruct((t, dims.d_pool), F32),
        jax.ShapeDtypeStruct(state_ret.shape, state_ret.dtype),
    )
    out_specs = (
        pl.BlockSpec((RG, d), lambda i, j: (i, 0)),
        pl.BlockSpec((RG, dims.d_pool), lambda i, j: (i, 0)),
        pl.BlockSpec((ns, dims.heads, dims.dk, dims.dv), lambda i, j: (i * steps + j, 0, 0, 0)),
    )
    weight_bytes = 2 * (win.size + wpg.size + wbp.size + wbr.size + wout.size)
    state_bytes = 4 * ns * dims.heads * dims.dk * dims.dv
    return pl.pallas_call(
        functools.partial(_mixer_sample_kernel, dims, ds),
        out_shape=out_shape,
        grid=(db // (ns * steps), steps),
        in_specs=in_specs,
        out_specs=out_specs,
        scratch_shapes=[
            pltpu.VMEM((RG, d), BF16), pltpu.VMEM((RG, dims.d_k), F32),
            pltpu.VMEM((RG, dims.d_k), F32), pltpu.VMEM((RG, dims.d_v), BF16),
            pltpu.VMEM((RG, dims.d_pool), F32), pltpu.VMEM((RG, dims.d_v), F32),
        ],
        compiler_params=pltpu.CompilerParams(
            dimension_semantics=("arbitrary", "arbitrary"),
            vmem_limit_bytes=_vmem_limit(2 * weight_bytes + 4 * state_bytes + 8 * 1024 * 1024),
        ),
        name="mixer_sample",
    )(x2d, modt, g1, win, wpg, ps, wbp, wbr, wout, cosb, sinb, intra, qdec, kdec, sdec,
      bandp, bandu, cnt, pool_in, state_ret)


def _rotary_tables(pos, dk):
    half = dk // 2
    inv = np.power(ROPE_BASE, -np.arange(half, dtype=np.float64) / half)
    ang = pos.astype(np.float64)[:, None] * inv[None, :]
    cos, sin = np.cos(ang), np.sin(ang)
    return (np.concatenate([cos, cos], axis=1).astype(np.float32),
            np.concatenate([-sin, sin], axis=1).astype(np.float32))


def _decay_tables(log_g, chunk, reps, dk, dv):
    idx = np.arange(chunk, dtype=np.float64)
    diff = idx[:, None] - idx[None, :]
    intra = np.where((diff >= 0)[None],
                     np.exp(log_g[:, None, None] * np.maximum(diff, 0.0)[None]), 0.0)
    heads = log_g.shape[0]
    eye = np.eye(reps)
    intra = np.einsum("ab,hij->haibj", eye, intra).reshape(heads, reps * chunk, reps * chunk)
    q_dec = np.exp(log_g[:, None] * (idx[None, :] + 1.0))
    k_dec = np.exp(log_g[:, None] * (chunk - 1.0 - idx[None, :]))
    s_dec = np.exp(log_g * chunk)
    q_dec = np.broadcast_to(np.tile(q_dec, (1, reps))[:, :, None], (heads, reps * chunk, dk))
    k_dec = np.broadcast_to(np.tile(k_dec, (1, reps))[:, :, None], (heads, reps * chunk, dk))
    s_dec = np.broadcast_to(s_dec[:, None, None], (heads, 1, dv))
    return tuple(np.ascontiguousarray(t, dtype=np.float32) for t in (intra, q_dec, k_dec, s_dec))


def _sample_pool_tables(ns, ds, gd, start_pos):
    n = np.arange(ds)
    i = np.arange(POOL_PREV)
    eye = np.eye(ns, dtype=np.float32)
    bandp, bandu, cnt = [], [], []
    for w in POOL_WINDOWS:
        bp = ((i[None, :] - POOL_PREV) >= (n[:, None] - w + 1)).astype(np.float32)
        bu = ((n[None, :] <= n[:, None]) & (n[None, :] >= n[:, None] - w + 1)).astype(np.float32)
        bandp.append(np.einsum("ab,ni->anbi", eye, bp).reshape(ns * ds, ns * POOL_PREV))
        bandu.append(np.einsum("ab,nm->anbm", eye, bu).reshape(ns * ds, ns * ds))
        c = np.minimum(w, start_pos + n + 1).astype(np.float32)
        cnt.append(np.broadcast_to(np.tile(c, ns)[:, None], (ns * ds, gd)))
    return (np.stack(bandp).astype(BF16), np.stack(bandu).astype(BF16),
            np.ascontiguousarray(np.stack(cnt)))


def _first_max_onehot(v, idx, big, axes):
    m = jnp.max(v, axis=axes, keepdims=True)
    first = jnp.min(jnp.where(v == m, idx, big), axis=axes, keepdims=True)
    return idx == first


def _route_kernel(n_prompt_blocks, xp_ref, xs_ref, modp_ref, modt_ref, g2_ref, wr_ref, br_ref,
                  before_ref, below_ref, h2_ref, pos_ref, wts_ref, cnt_ref, rank_ref, gates_ref):
    i = pl.program_id(0)
    is_p = i < n_prompt_blocks
    x = jnp.where(is_p, xp_ref[...], xs_ref[...])
    shift = jnp.where(is_p, modp_ref[0, 3:4, :], modt_ref[3])
    scale = jnp.where(is_p, modp_ref[0, 4:5, :], modt_ref[4])
    h2 = _rms_mod(x, g2_ref[...], shift, scale)
    h2_ref[...] = h2.astype(BF16)

    hh, hl = _split_bf16(h2, 2)
    wh, wl = _split_bf16(wr_ref[...], 2)
    logits = _dot_nt(wh, hh) + (_dot_nt(wh, hl) + _dot_nt(wl, hh))
    s = jax.nn.sigmoid(logits)
    e, t = s.shape
    ge = e // N_GROUPS
    neg = -jnp.inf
    s3 = s.reshape(N_GROUPS, ge, t)
    sb3 = s3 + br_ref[...]

    j3 = lax.broadcasted_iota(jnp.int32, (N_GROUPS, ge, t), 1)
    top1 = _first_max_onehot(sb3, j3, ge, 1)
    m1 = jnp.max(sb3, axis=1, keepdims=True)
    m2 = jnp.max(jnp.where(top1, neg, sb3), axis=1, keepdims=True)
    gv = m1 + m2

    g3 = lax.broadcasted_iota(jnp.int32, (N_GROUPS, 1, t), 0)
    gsel = jnp.zeros((N_GROUPS, 1, t), F32)
    for _ in range(TOPK_GROUPS):
        hit = _first_max_onehot(gv, g3, N_GROUPS, 0)
        gsel = jnp.where(hit, 1.0, gsel)
        gv = jnp.where(hit, neg, gv)

    e3 = lax.broadcasted_iota(jnp.int32, (N_GROUPS, ge, t), 0) * ge + j3
    ev = jnp.where(gsel > 0.5, sb3, neg)
    sel = jnp.zeros((N_GROUPS, ge, t), F32)
    chosen = jnp.zeros((N_GROUPS, ge, t), F32)
    hits = []
    for _ in range(TOP_K):
        hit = _first_max_onehot(ev, e3, e, (0, 1))
        hits.append(hit)
        sel = jnp.where(hit, s3, sel)
        chosen = jnp.where(hit, 1.0, chosen)
        ev = jnp.where(hit, neg, ev)
    tot = jnp.sum(sel, axis=(0, 1), keepdims=True)
    gates = sel / tot * ROUTED_SCALE

    chosen2 = chosen.reshape(e, t)
    rank = _dot(chosen2.astype(BF16), before_ref[...])
    cnt = jnp.sum(chosen2, axis=1, keepdims=True)
    ovf_units = jnp.floor((jnp.maximum(cnt - SLOT_ROWS, 0.0) + (DMA_ROWS - 1)) / DMA_ROWS)
    ovf_off = _dot(below_ref[...], jnp.broadcast_to(ovf_units, (e, LANES)).astype(BF16))[:, 0:1]
    e_col = lax.broadcasted_iota(jnp.int32, (e, 1), 0).astype(F32)
    slot = jnp.where(rank < SLOT_ROWS, e_col * SLOT_ROWS + rank,
                     e * SLOT_ROWS + ovf_off * DMA_ROWS + (rank - SLOT_ROWS))
    slot3 = slot.reshape(N_GROUPS, ge, t)
    for k, hit in enumerate(hits):
        pos_k = jnp.sum(jnp.where(hit, slot3, 0.0), axis=(0, 1), keepdims=True)
        w_k = jnp.sum(jnp.where(hit, gates, 0.0), axis=(0, 1), keepdims=True)
        pos_ref[k:k + 1, :] = pos_k.reshape(1, t).astype(jnp.int32)
        wts_ref[k:k + 1, :] = w_k.reshape(1, t)
    cnt_ref[0] = jnp.broadcast_to(cnt, (e, LANES)).astype(jnp.int32)
    rank_ref[...] = jnp.where(chosen2 > 0.5, rank, -1.0).astype(jnp.int32)
    gates_ref[...] = gates.reshape(e, t)


def _route(xp2d, xs2d, modp, modt, g2, wr_t, br):
    tp, d = xp2d.shape
    ts = xs2d.shape[0]
    tb = TOKEN_BLOCK
    npb, nsb = tp // tb, ts // tb
    e = wr_t.shape[0]
    seq = tp // modp.shape[0]
    bpb = seq // tb

    def p_idx(i):
        return jnp.minimum(i, npb - 1)

    def s_idx(i):
        return jnp.maximum(i - npb, 0)

    t = tp + ts
    nb = npb + nsb
    tok = np.arange(tb)
    before = (tok[:, None] < tok[None, :]).astype(BF16)
    ex = np.arange(e)
    below = (ex[None, :] < ex[:, None]).astype(BF16)
    return pl.pallas_call(
        functools.partial(_route_kernel, npb),
        out_shape=(jax.ShapeDtypeStruct((t, d), BF16),
                   jax.ShapeDtypeStruct((TOP_K, t), jnp.int32),
                   jax.ShapeDtypeStruct((TOP_K, t), F32),
                   jax.ShapeDtypeStruct((nb, e, LANES), jnp.int32),
                   jax.ShapeDtypeStruct((e, t), jnp.int32),
                   jax.ShapeDtypeStruct((e, t), F32)),
        grid=(nb,),
        in_specs=[
            pl.BlockSpec((tb, d), lambda i: (p_idx(i), 0)),
            pl.BlockSpec((tb, d), lambda i: (s_idx(i), 0)),
            pl.BlockSpec((1, N_MOD, d), lambda i: (p_idx(i) // bpb, 0, 0)),
            pl.BlockSpec((N_MOD, tb, d), lambda i: (0, s_idx(i), 0)),
            _const_spec(g2.shape), _const_spec(wr_t.shape), _const_spec(br.shape),
            _const_spec(before.shape), _const_spec(below.shape),
        ],
        out_specs=(pl.BlockSpec((tb, d), lambda i: (i, 0)),
                   pl.BlockSpec((TOP_K, tb), lambda i: (0, i)),
                   pl.BlockSpec((TOP_K, tb), lambda i: (0, i)),
                   pl.BlockSpec((1, e, LANES), lambda i: (i, 0, 0)),
                   pl.BlockSpec((e, tb), lambda i: (0, i)),
                   pl.BlockSpec((e, tb), lambda i: (0, i))),
        compiler_params=pltpu.CompilerParams(
            dimension_semantics=("arbitrary",),
            vmem_limit_bytes=_vmem_limit(32 * 1024 * 1024),
        ),
        name="route",
    )(xp2d, xs2d, modp, modt, g2, wr_t, br, before, below)


class _Layout:
    def __init__(self, counts):
        nb, ne = counts.shape
        u, g, tm = SLOT_ROWS, DMA_ROWS, EXPERT_TILE
        pc = (counts + g - 1) // g * g
        ovf_units = (jnp.maximum(counts - u, 0) + g - 1) // g
        rows_e = jnp.sum(pc, axis=0)
        region = (rows_e + u + tm - 1) // tm * tm
        goff = jnp.cumsum(region) - region
        dst = goff[None, :] + jnp.cumsum(pc, axis=0) - pc
        self.dst = dst.reshape(-1)
        ovf_end = jnp.cumsum(ovf_units, axis=1)
        ovf_first = ovf_end - ovf_units
        unit = jnp.arange(_overflow_units_cap(ne))[None, :, None]
        owns = (ovf_first[:, None, :] <= unit) & (unit < ovf_end[:, None, :])
        row = dst[:, None, :] + u + (unit - ovf_first[:, None, :]) * g
        self.ovf_dst = jnp.sum(jnp.where(owns, row, 0), axis=2).reshape(-1)
        self.ovf_tot = ovf_end[:, -1]
        self.tail_start = goff + rows_e
        self.tail_units = (region - rows_e) // g
        tiles = region // tm
        self.region_tiles = jnp.concatenate([tiles, jnp.sum(tiles, keepdims=True)])


def _overflow_units_cap(ne):
    return TOKEN_BLOCK * TOP_K // DMA_ROWS + ne


def _block_buffer_rows(tb, ne):
    rows = ne * SLOT_ROWS + tb * TOP_K + ne * (DMA_ROWS - 1)
    return (rows + ONEHOT_ROWS - 1) // ONEHOT_ROWS * ONEHOT_ROWS


def _expert_rows_cap(t, tb, ne):
    nb = t // tb
    rows = t * TOP_K + nb * ne * (DMA_ROWS - 1) + ne * (SLOT_ROWS + EXPERT_TILE - 1)
    return (rows + EXPERT_TILE - 1) // EXPERT_TILE * EXPERT_TILE


def _slot_copy(buf_ref, slot, e, hbm_ref, row, sem, to_hbm):
    src = buf_ref.at[slot, pl.ds(pl.multiple_of(e * SLOT_ROWS, DMA_ROWS), SLOT_ROWS)]
    dst = hbm_ref.at[pl.ds(pl.multiple_of(row, DMA_ROWS), SLOT_ROWS)]
    return pltpu.make_async_copy(src, dst, sem) if to_hbm else pltpu.make_async_copy(dst, src, sem)


def _unit_copy(buf_ref, slot, buf_row, hbm_ref, row, sem, to_hbm):
    src = buf_ref.at[slot, pl.ds(pl.multiple_of(buf_row, DMA_ROWS), DMA_ROWS)]
    dst = hbm_ref.at[pl.ds(pl.multiple_of(row, DMA_ROWS), DMA_ROWS)]
    return pltpu.make_async_copy(src, dst, sem) if to_hbm else pltpu.make_async_copy(dst, src, sem)


def _slots_wait(ne, buf_ref, slot, hbm_ref, sem, to_hbm):
    src = buf_ref.at[slot, pl.ds(0, ne * SLOT_ROWS)]
    dst = hbm_ref.at[pl.ds(0, ne * SLOT_ROWS)]
    c = pltpu.make_async_copy(src, dst, sem) if to_hbm else pltpu.make_async_copy(dst, src, sem)
    c.wait()


def _overflow_copies(ne, blk, slot, odst_ref, otot_ref, buf_ref, hbm_ref, sem, to_hbm, wait):
    def per_unit(i, carry):
        row = odst_ref[blk * _overflow_units_cap(ne) + i]
        c = _unit_copy(buf_ref, slot, ne * SLOT_ROWS + i * DMA_ROWS, hbm_ref, row, sem, to_hbm)
        c.wait() if wait else c.start()
        return carry

    lax.fori_loop(0, otot_ref[blk], per_unit, 0)


def _slot_chunks(ne):
    return [range(c, c + SLOT_CHUNK_EXPERTS) for c in range(0, ne, SLOT_CHUNK_EXPERTS)]


def _overflow_chunks(n_units):
    return lax.shift_right_logical(n_units * DMA_ROWS + (ONEHOT_ROWS - 1),
                                   ONEHOT_ROWS.bit_length() - 1)


def _overflow_select(pos, vals, r0):
    rows = lax.broadcasted_iota(jnp.int32, (ONEHOT_ROWS, pos.shape[1]), 0) + r0
    sel = jnp.zeros(rows.shape, F32)
    for k in range(TOP_K):
        v = 1.0 if vals is None else vals[k:k + 1, :]
        sel = jnp.where(pos[k:k + 1, :] == rows, v, sel)
    return sel


def _dispatch_kernel(ne, dst_ref, odst_ref, otot_ref, tails_ref, tailu_ref,
                     h2_ref, rank_ref, pos_ref, xs_hbm, buf_ref, zero_ref, sem, zsem):
    b = pl.program_id(0)
    nb = pl.num_programs(0)
    slot = b % 2
    h2 = h2_ref[...]
    j = lax.broadcasted_iota(jnp.int32, (SLOT_ROWS, h2.shape[0]), 0)

    for ci, experts in enumerate(_slot_chunks(ne)):
        onehot = jnp.concatenate(
            [jnp.where(rank_ref[e:e + 1, :] == j, 1.0, 0.0) for e in experts], axis=0)
        r0 = experts[0] * SLOT_ROWS
        buf_ref[slot, r0:r0 + len(experts) * SLOT_ROWS, :] = (
            _dot(onehot.astype(BF16), h2).astype(BF16))
        if ci == 0:
            @pl.when(b > 0)
            def _prev():
                _slots_wait(ne, buf_ref, 1 - slot, xs_hbm, sem.at[1 - slot], True)
                _overflow_copies(ne, b - 1, 1 - slot, odst_ref, otot_ref, buf_ref, xs_hbm,
                                 sem.at[1 - slot], True, True)
        for e in experts:
            _slot_copy(buf_ref, slot, e, xs_hbm, dst_ref[b * ne + e], sem.at[slot], True).start()

    pos = pos_ref[...]

    def chunk(ci, carry):
        r0 = pl.multiple_of(ne * SLOT_ROWS + ci * ONEHOT_ROWS, ONEHOT_ROWS)
        onehot = _overflow_select(pos, None, r0).astype(BF16)
        buf_ref[slot, pl.ds(r0, ONEHOT_ROWS), :] = _dot(onehot, h2).astype(BF16)
        return carry

    lax.fori_loop(0, _overflow_chunks(otot_ref[b]), chunk, 0)
    _overflow_copies(ne, b, slot, odst_ref, otot_ref, buf_ref, xs_hbm, sem.at[slot], True, False)

    @pl.when(b == nb - 1)
    def _last():
        _slots_wait(ne, buf_ref, slot, xs_hbm, sem.at[slot], True)
        _overflow_copies(ne, b, slot, odst_ref, otot_ref, buf_ref, xs_hbm, sem.at[slot], True,
                         True)
        zero_ref[...] = jnp.zeros_like(zero_ref)

        def tail(wait):
            def per_expert(e, carry):
                def per_unit(i, carry2):
                    row = pl.multiple_of(tails_ref[e] + i * DMA_ROWS, DMA_ROWS)
                    c = pltpu.make_async_copy(zero_ref, xs_hbm.at[pl.ds(row, DMA_ROWS)], zsem)
                    c.wait() if wait else c.start()
                    return carry2
                lax.fori_loop(0, tailu_ref[e], per_unit, 0)
                return carry
            lax.fori_loop(0, ne, per_expert, 0)

        tail(False)
        tail(True)


def _dispatch(h2, rank, pos, lay, ne, rows_cap):
    t, d = h2.shape
    tb = TOKEN_BLOCK
    buf_rows = _block_buffer_rows(tb, ne)
    return pl.pallas_call(
        functools.partial(_dispatch_kernel, ne),
        out_shape=jax.ShapeDtypeStruct((rows_cap, d), BF16),
        grid_spec=pltpu.PrefetchScalarGridSpec(
            num_scalar_prefetch=5,
            grid=(t // tb,),
            in_specs=[
                pl.BlockSpec((tb, d), lambda i, *_: (i, 0)),
                pl.BlockSpec((ne, tb), lambda i, *_: (0, i)),
                pl.BlockSpec((TOP_K, tb), lambda i, *_: (0, i)),
            ],
            out_specs=pl.BlockSpec(memory_space=pl.ANY),
            scratch_shapes=[
                pltpu.VMEM((2, buf_rows, d), BF16),
                pltpu.VMEM((DMA_ROWS, d), BF16),
                pltpu.SemaphoreType.DMA((2,)),
                pltpu.SemaphoreType.DMA(()),
            ],
        ),
        compiler_params=pltpu.CompilerParams(
            dimension_semantics=("arbitrary",),
            vmem_limit_bytes=_vmem_limit(4 * buf_rows * d + 16 * 1024 * 1024),
        ),
        name="dispatch",
    )(lay.dst, lay.ovf_dst, lay.ovf_tot, lay.tail_start, lay.tail_units, h2, rank, pos)


def _experts_kernel(tiles_ref, xe_hbm, wg_ref, wu_ref, wd_ref, ye_hbm,
                    xbuf_ref, ybuf_ref, wgb_ref, wub_ref, wdb_ref, done_ref, xsem, ysem):
    e = pl.program_id(0)
    ne = pl.num_programs(0)
    tm = xbuf_ref.shape[1]

    def x_copy(row, slot):
        return pltpu.make_async_copy(xe_hbm.at[pl.ds(pl.multiple_of(row, tm), tm)],
                                     xbuf_ref.at[slot], xsem.at[slot])

    def y_copy(row, slot):
        return pltpu.make_async_copy(ybuf_ref.at[slot],
                                     ye_hbm.at[pl.ds(pl.multiple_of(row, tm), tm)], ysem.at[slot])

    total = tiles_ref[ne]

    @pl.when(e == 0)
    def _first():
        done_ref[0] = 0
        for g in range(RING_SLOTS - 1):
            x_copy(g * tm, g).start()

    wgb_ref[...] = wg_ref[0].astype(BF16)
    wub_ref[...] = wu_ref[0].astype(BF16)
    wdb_ref[...] = wd_ref[0].astype(BF16)
    done = done_ref[0]
    n = tiles_ref[e]

    def tile(j, carry):
        g = done + j
        slot = lax.rem(g, RING_SLOTS)
        x_copy(0, slot).wait()
        ahead = g + (RING_SLOTS - 1)

        @pl.when(ahead < total)
        def _prefetch():
            x_copy(ahead * tm, lax.rem(ahead, RING_SLOTS)).start()

        @pl.when(g >= RING_SLOTS)
        def _reuse():
            y_copy(0, slot).wait()

        part = tm // EXPERT_TILE_PARTS
        starts = range(0, tm, part)
        hids = []
        for r0 in starts:
            x = xbuf_ref[slot, r0:r0 + part, :]
            hids.append((jax.nn.silu(_dot(x, wgb_ref[...])) * _dot(x, wub_ref[...])).astype(BF16))
        for r0, hid in zip(starts, hids):
            ybuf_ref[slot, r0:r0 + part, :] = _dot(hid, wdb_ref[...]).astype(BF16)
        y_copy(g * tm, slot).start()
        return carry

    lax.fori_loop(0, n, tile, 0)
    done_ref[0] = done + n

    @pl.when(e == ne - 1)
    def _drain():
        for s in range(RING_SLOTS):
            y_copy(0, s).wait()


def _experts(xs, lay, wg, wu, wd):
    rows_cap, d = xs.shape
    ne, _, de = wg.shape
    assert ne >= RING_SLOTS
    tm = EXPERT_TILE
    ring = RING_SLOTS
    return pl.pallas_call(
        _experts_kernel,
        out_shape=jax.ShapeDtypeStruct((rows_cap, d), BF16),
        grid_spec=pltpu.PrefetchScalarGridSpec(
            num_scalar_prefetch=1,
            grid=(ne,),
            in_specs=[
                pl.BlockSpec(memory_space=pl.ANY),
                pl.BlockSpec((1, d, de), lambda e, *_: (e, 0, 0)),
                pl.BlockSpec((1, d, de), lambda e, *_: (e, 0, 0)),
                pl.BlockSpec((1, de, d), lambda e, *_: (e, 0, 0)),
            ],
            out_specs=pl.BlockSpec(memory_space=pl.ANY),
            scratch_shapes=[
                pltpu.VMEM((ring, tm, d), BF16), pltpu.VMEM((ring, tm, d), BF16),
                pltpu.VMEM((d, de), BF16), pltpu.VMEM((d, de), BF16), pltpu.VMEM((de, d), BF16),
                pltpu.SMEM((1,), jnp.int32),
                pltpu.SemaphoreType.DMA((ring,)), pltpu.SemaphoreType.DMA((ring,)),
            ],
        ),
        compiler_params=pltpu.CompilerParams(
            dimension_semantics=("arbitrary",),
            vmem_limit_bytes=_vmem_limit(32 * 1024 * 1024),
        ),
        name="experts",
    )(lay.region_tiles, xs, wg, wu, wd)


def _final_kernel(n_prompt_blocks, ne, dst_ref, odst_ref, otot_ref,
                  xp_ref, xs_ref, modp_ref, modt_ref, h2_ref, rank_ref, gates_ref, pos_ref, wts_ref,
                  wsg_ref, wsu_ref, wsd_ref, gf_ref, ye_hbm, yp_ref, ys_ref,
                  buf_ref, acc_ref, sem):
    i = pl.program_id(0)
    nb = pl.num_programs(0)
    slot = i % 2

    def fetch_slots(blk, sl, experts):
        for e in experts:
            _slot_copy(buf_ref, sl, e, ye_hbm, dst_ref[blk * ne + e], sem.at[sl], False).start()

    def overflow(blk, sl, wait):
        _overflow_copies(ne, blk, sl, odst_ref, otot_ref, buf_ref, ye_hbm, sem.at[sl], False, wait)

    @pl.when(i == 0)
    def _first():
        buf_ref[...] = jnp.zeros_like(buf_ref)
        fetch_slots(0, 0, range(ne))
        overflow(0, 0, False)

    @pl.when(i + 1 < nb)
    def _prefetch():
        fetch_slots(i + 1, 1 - slot, range(ne))
        overflow(i + 1, 1 - slot, False)

    _slots_wait(ne, buf_ref, slot, ye_hbm, sem.at[slot], False)
    overflow(i, slot, True)

    tb = h2_ref.shape[0]
    j = lax.broadcasted_iota(jnp.int32, (SLOT_ROWS, tb), 0)
    acc = jnp.zeros((tb, h2_ref.shape[1]), F32)
    for experts in _slot_chunks(ne):
        sel = jnp.concatenate(
            [jnp.where(rank_ref[e:e + 1, :] == j, gates_ref[e:e + 1, :], 0.0) for e in experts],
            axis=0)
        r0 = experts[0] * SLOT_ROWS
        acc = acc + _dot_tn(sel.astype(BF16), buf_ref[slot, r0:r0 + len(experts) * SLOT_ROWS, :])
    acc_ref[...] = acc

    pos = pos_ref[...]
    wts = wts_ref[...]

    def chunk(ci, carry):
        r0 = pl.multiple_of(ne * SLOT_ROWS + ci * ONEHOT_ROWS, ONEHOT_ROWS)
        sel = _overflow_select(pos, wts, r0).astype(BF16)
        acc_ref[...] += _dot_tn(sel, buf_ref[slot, pl.ds(r0, ONEHOT_ROWS), :])
        return carry

    lax.fori_loop(0, _overflow_chunks(otot_ref[i]), chunk, 0)

    is_p = i < n_prompt_blocks
    x = jnp.where(is_p, xp_ref[...], xs_ref[...])
    gate = jnp.where(is_p, modp_ref[0, 5:6, :], modt_ref[5])
    hb = h2_ref[...]
    hid = jax.nn.silu(_dot(hb, wsg_ref[...])) * _dot(hb, wsu_ref[...])
    f = acc_ref[...] + _dot(hid.astype(BF16), wsd_ref[...])
    x2 = x + gate * f
    y = x2 * lax.rsqrt(jnp.mean(x2 * x2, axis=-1, keepdims=True) + EPS) * gf_ref[...]

    @pl.when(is_p)
    def _p():
        yp_ref[...] = y

    @pl.when(jnp.logical_not(is_p))
    def _s():
        ys_ref[...] = y


def _final(xp2d, xs2d, modp, modt, h2, rank, gates, pos, wts, ye, lay, ne, wsg, wsu, wsd, gf):
    tp, d = xp2d.shape
    ts = xs2d.shape[0]
    tb = TOKEN_BLOCK
    npb, nsb = tp // tb, ts // tb
    seq = tp // modp.shape[0]
    bpb = seq // tb
    buf_rows = _block_buffer_rows(tb, ne)

    def p_idx(i):
        return jnp.minimum(i, npb - 1)

    def s_idx(i):
        return jnp.maximum(i - npb, 0)

    def const(shape):
        zeros = (0,) * len(shape)
        return pl.BlockSpec(shape, lambda i, *_: zeros)

    return pl.pallas_call(
        functools.partial(_final_kernel, npb, ne),
        out_shape=(jax.ShapeDtypeStruct((tp, d), F32), jax.ShapeDtypeStruct((ts, d), F32)),
        grid_spec=pltpu.PrefetchScalarGridSpec(
            num_scalar_prefetch=3,
            grid=(npb + nsb,),
            in_specs=[
                pl.BlockSpec((tb, d), lambda i, *_: (p_idx(i), 0)),
                pl.BlockSpec((tb, d), lambda i, *_: (s_idx(i), 0)),
                pl.BlockSpec((1, N_MOD, d), lambda i, *_: (p_idx(i) // bpb, 0, 0)),
                pl.BlockSpec((N_MOD, tb, d), lambda i, *_: (0, s_idx(i), 0)),
                pl.BlockSpec((tb, d), lambda i, *_: (i, 0)),
                pl.BlockSpec((ne, tb), lambda i, *_: (0, i)),
                pl.BlockSpec((ne, tb), lambda i, *_: (0, i)),
                pl.BlockSpec((TOP_K, tb), lambda i, *_: (0, i)),
                pl.BlockSpec((TOP_K, tb), lambda i, *_: (0, i)),
                const(wsg.shape), const(wsu.shape), const(wsd.shape), const(gf.shape),
                pl.BlockSpec(memory_space=pl.ANY),
            ],
            out_specs=(pl.BlockSpec((tb, d), lambda i, *_: (p_idx(i), 0)),
                       pl.BlockSpec((tb, d), lambda i, *_: (s_idx(i), 0))),
            scratch_shapes=[
                pltpu.VMEM((2, buf_rows, d), BF16),
                pltpu.VMEM((tb, d), F32),
                pltpu.SemaphoreType.DMA((2,)),
            ],
        ),
        compiler_params=pltpu.CompilerParams(
            dimension_semantics=("arbitrary",),
            vmem_limit_bytes=_vmem_limit(4 * buf_rows * d + 32 * 1024 * 1024),
        ),
        name="final",
    )(lay.dst, lay.ovf_dst, lay.ovf_tot,
      xp2d, xs2d, modp, modt, h2, rank, gates, pos, wts, wsg, wsu, wsd, gf, ye)


def kernel(x_prompt, x_sample, c_prompt, c_sample, state_pool, state_ret, w_ada, b_ada, g_norm1,
           w_in, w_pool_group, pool_scale, w_branch_pool, w_branch_ret, w_out, g_norm2, w_router,
           b_router, w_exp_gate, w_exp_up, w_exp_down, w_sh_gate, w_sh_up, w_sh_down, g_final):
    b, s, d = x_prompt.shape
    db, ds, _ = x_sample.shape
    depth = w_ada.shape[0]
    heads, dk, dv = state_ret.shape[2:]
    pool_buf, d_pool = state_pool.shape[2:]
    dims = _Dims(d, d_pool, heads, dk, dv)
    assert dims.d_in == w_in.shape[2]
    assert pool_buf == max(POOL_WINDOWS) - 1 and pool_buf < POOL_PREV
    assert s % PROMPT_STEP_ROWS == 0 and PROMPT_STEP_ROWS % RET_CHUNK == 0
    assert s % TOKEN_BLOCK == 0 and (db * ds) % TOKEN_BLOCK == 0
    assert db % (SAMPLE_SEQS_PER_STEP * SAMPLE_STEPS_PER_GROUP) == 0
    assert SLOT_ROWS % DMA_ROWS == 0 and (w_router.shape[2] * SLOT_ROWS) % ONEHOT_ROWS == 0

    log_g = np.log(1.0 - np.power(2.0, -5.0 - np.arange(heads, dtype=np.float64)))
    ns = SAMPLE_SEQS_PER_STEP
    cos_p, sin_p = _rotary_tables(np.arange(s), dk)
    cos_s, sin_s = _rotary_tables(PAST_LEN + np.arange(ds), dk)
    cos_s, sin_s = np.tile(cos_s, (ns, 1)), np.tile(sin_s, (ns, 1))
    dec_p = _decay_tables(log_g, RET_CHUNK, 1, dk, dv)
    dec_s = _decay_tables(log_g, ds, ns, dk, dv)
    pool_tabs = _sample_pool_tables(ns, ds, dims.gd, PAST_LEN)

    assert depth == 1, "the final kernel fuses the output norm into the (single) layer"
    l = 0
    xs = x_sample.reshape(db * ds, d)
    modp, modt = _ada(c_prompt, jnp.repeat(c_sample, ds, axis=0), w_ada[l], b_ada[l])
    modp = modp.reshape(b, N_MOD, d)
    g1 = g_norm1[l].reshape(1, d)
    weights = (w_in[l].astype(BF16), w_pool_group[l].astype(BF16),
               pool_scale[l].reshape(1, d_pool), w_branch_pool[l].astype(BF16),
               w_branch_ret[l].astype(BF16), w_out[l].astype(BF16))

    x1p, pool16, ret_p = _mixer_prompt(dims, x_prompt, modp, g1, weights,
                                       (cos_p, sin_p) + dec_p)
    pool_in = jnp.pad(state_pool[l], ((0, 0), (POOL_PREV - pool_buf, 0), (0, 0)))
    x1s, u_s, ret_s = _mixer_sample(dims, ds, xs, modt, g1, weights,
                                    (cos_s, sin_s) + dec_s + pool_tabs, pool_in,
                                    state_ret[l].astype(F32))
    pool_p = pool16[:, POOL_PREV - pool_buf:]
    pool_s = jnp.concatenate([state_pool[l], u_s.reshape(db, ds, d_pool)], axis=1)[:, -pool_buf:]

    x1p2d = x1p.reshape(b * s, d)
    ne = w_router.shape[2]
    h2, pos, wts, counts, rank, gates = _route(
        x1p2d, x1s, modp, modt, g_norm2[l].reshape(1, d), w_router[l].T,
        b_router[l].reshape(N_GROUPS, ne // N_GROUPS, 1))
    rows_cap = _expert_rows_cap(b * s + db * ds, TOKEN_BLOCK, ne)
    lay = _Layout(counts[:, :, 0])
    xe = _dispatch(h2, rank, pos, lay, ne, rows_cap)
    ye = _experts(xe, lay, w_exp_gate[l], w_exp_up[l], w_exp_down[l])
    yp, ys = _final(x1p2d, x1s, modp, modt, h2, rank, gates, pos, wts, ye, lay, ne,
                    w_sh_gate[l].astype(BF16), w_sh_up[l].astype(BF16),
                    w_sh_down[l].astype(BF16), g_final.reshape(1, d))

    return (yp.reshape(b, s, d), ys.reshape(db, ds, d), pool_p[None],
            ret_p.astype(x_prompt.dtype), pool_s[None], ret_s.astype(state_ret.dtype)[None])
```

```python
import functools

import jax
import jax.numpy as jnp
import numpy as np
from jax import lax
from jax.experimental import pallas as pl
from jax.experimental.pallas import tpu as pltpu

F32 = jnp.float32
BF16 = jnp.bfloat16

PAST_LEN = 16384
POOL_WINDOWS = (2, 4, 8, 16)
RET_CHUNK = 128
ROPE_BASE = 10000.0
TOP_K = 8
N_GROUPS = 8
TOPK_GROUPS = 4
ROUTED_SCALE = 2.5
N_MOD = 6
EPS = 1e-6

V7X_VMEM_BYTES = 64 * 1024 * 1024
SUBLANES = 8
LANES = 128

PROMPT_STEP_ROWS = 512
SAMPLE_SEQS_PER_STEP = 8
SAMPLE_STEPS_PER_GROUP = 4
TOKEN_BLOCK = 256
DMA_ROWS = 16
SLOT_ROWS = 3 * DMA_ROWS
ONEHOT_ROWS = 256
SLOT_CHUNK_EXPERTS = 16
EXPERT_TILE = 512
EXPERT_TILE_PARTS = 2
EXPERT_TILES_PER_ITER = 2
RING_SLOTS = 2 * EXPERT_TILES_PER_ITER
POOL_PREV = 16


def _vmem_limit(nbytes):
    return int(min(nbytes, V7X_VMEM_BYTES - 8 * 1024 * 1024))


def _dot(a, b):
    return jnp.dot(a, b, preferred_element_type=F32)


def _dot_nt(a, b):
    return lax.dot_general(a, b, (((1,), (1,)), ((), ())), preferred_element_type=F32)


def _dot_tn(a, b):
    return lax.dot_general(a, b, (((0,), (0,)), ((), ())), preferred_element_type=F32)


def _split_bf16(x, parts):
    out = []
    for _ in range(parts):
        p = x.astype(BF16)
        out.append(p)
        x = x - p.astype(F32)
    return out


def _rms_mod(x, g, shift, scale):
    y = x * lax.rsqrt(jnp.mean(x * x, axis=-1, keepdims=True) + EPS)
    return (y * g) * (1.0 + scale) + shift


def _rotary(x, cosb, sinb):
    return x * cosb + pltpu.roll(x, x.shape[-1] // 2, axis=1) * sinb


def _group_norm(o):
    return o * lax.rsqrt(jnp.mean(o * o, axis=-1, keepdims=True) + EPS)


def _ada_kernel(cp_ref, ct_ref, w_ref, b_ref, modp_ref, modt_ref):
    w = w_ref[...].astype(BF16)
    modp_ref[...] = _dot(cp_ref[...].astype(BF16), w) + b_ref[...]
    modt_ref[0] = _dot(ct_ref[...].astype(BF16), w) + b_ref[...]


def _ada(c_prompt, c_tokens, w_ada, b_ada):
    bp, d = c_prompt.shape
    ts = c_tokens.shape[0]
    n = w_ada.shape[1]
    return pl.pallas_call(
        _ada_kernel,
        out_shape=(jax.ShapeDtypeStruct((bp, n), F32), jax.ShapeDtypeStruct((n // d, ts, d), F32)),
        grid=(n // d,),
        in_specs=[
            pl.BlockSpec((bp, d), lambda j: (0, 0)),
            pl.BlockSpec((ts, d), lambda j: (0, 0)),
            pl.BlockSpec((d, d), lambda j: (0, j)),
            pl.BlockSpec((1, d), lambda j: (0, j)),
        ],
        out_specs=(pl.BlockSpec((bp, d), lambda j: (0, j)),
                   pl.BlockSpec((1, ts, d), lambda j: (j, 0, 0))),
        compiler_params=pltpu.CompilerParams(
            dimension_semantics=("arbitrary",),
            vmem_limit_bytes=_vmem_limit(32 * 1024 * 1024),
        ),
        name="ada",
    )(c_prompt, c_tokens, w_ada, b_ada.reshape(1, n))


class _Dims:
    def __init__(self, d_model, d_pool, heads, dk, dv):
        self.d = d_model
        self.d_pool = d_pool
        self.gd = d_pool // len(POOL_WINDOWS)
        self.heads = heads
        self.dk = dk
        self.dv = dv
        self.d_k = heads * dk
        self.d_v = heads * dv
        widths = (d_pool, self.d_k, self.d_k, self.d_v, self.d_v, d_model, d_model)
        offs = [0]
        for w in widths:
            offs.append(offs[-1] + w)
        self.cols = tuple((offs[i], offs[i + 1]) for i in range(len(widths)))
        self.d_in = offs[-1]


def _proj(hb, win_ref, cols):
    return _dot(hb, win_ref[:, cols[0]:cols[1]])


def _pool_branch(pm_parts, wpg_ref, ps_ref, wbp_ref):
    pg = [_dot(pm.astype(BF16), wpg_ref[gi]) for gi, pm in enumerate(pm_parts)]
    pm = jnp.concatenate(pg, axis=1) * ps_ref[...]
    return _dot(pm.astype(BF16), wbp_ref[...])


def _merge_out(x, gate, a, o, g, ga, gb, wbr_ref, wout_ref):
    r = _dot((jax.nn.silu(g) * o).astype(BF16), wbr_ref[...])
    merged = jax.nn.sigmoid(ga) * a + jax.nn.sigmoid(gb) * r
    m = _dot(merged.astype(BF16), wout_ref[...])
    return x + gate * m


def _mixer_prompt_kernel(dims, x_ref, mod_ref, g1_ref, win_ref, wpg_ref, ps_ref, wbp_ref,
                         wbr_ref, wout_ref, cos_ref, sin_ref, intra_ref, qdec_ref,
                         kdec_ref, sdec_ref, x1_ref, pool_ref, ret_ref, uext_ref):
    c = pl.program_id(1)
    L = x_ref.shape[1]

    @pl.when(c == 0)
    def _init():
        ret_ref[...] = jnp.zeros_like(ret_ref)
        uext_ref[0:POOL_PREV, :] = jnp.zeros((POOL_PREV, dims.d_pool), F32)

    x = x_ref[0]
    shift, scale, gate = mod_ref[0, 0:1, :], mod_ref[0, 1:2, :], mod_ref[0, 2:3, :]
    hb = _rms_mod(x, g1_ref[...], shift, scale).astype(BF16)

    u = _proj(hb, win_ref, dims.cols[0])
    uext_ref[POOL_PREV:POOL_PREV + L, :] = u
    pos = c * L + lax.broadcasted_iota(jnp.int32, (L, 1), 0)
    pm_parts = []
    for gi, w in enumerate(POOL_WINDOWS):
        s = uext_ref[:, gi * dims.gd:(gi + 1) * dims.gd]
        sh = 1
        while sh < w:
            s = s + pltpu.roll(s, sh, axis=0)
            sh *= 2
        cnt = jnp.minimum(w, pos + 1).astype(F32)
        pm_parts.append(s[POOL_PREV:, :] / cnt - u[:, gi * dims.gd:(gi + 1) * dims.gd])
    a = _pool_branch(pm_parts, wpg_ref, ps_ref, wbp_ref)
    uext_ref[0:POOL_PREV, :] = uext_ref[L:L + POOL_PREV, :]
    pool_ref[0] = uext_ref[0:POOL_PREV, :]

    q = _proj(hb, win_ref, dims.cols[1])
    k = _proj(hb, win_ref, dims.cols[2])
    v = _proj(hb, win_ref, dims.cols[3])
    chunk_outs = []
    for r0 in range(0, L, RET_CHUNK):
        rows = slice(r0, r0 + RET_CHUNK)
        cosb, sinb = cos_ref[rows, :], sin_ref[rows, :]
        outs = []
        for h in range(dims.heads):
            qr = _rotary(q[rows, h * dims.dk:(h + 1) * dims.dk], cosb, sinb)
            kr = _rotary(k[rows, h * dims.dk:(h + 1) * dims.dk], cosb, sinb) * (dims.dk ** -0.5)
            vb = v[rows, h * dims.dv:(h + 1) * dims.dv].astype(BF16)
            scores = _dot_nt(qr.astype(BF16), kr.astype(BF16)) * intra_ref[h]
            s_prev = ret_ref[0, 0, h]
            o = _dot(scores.astype(BF16), vb) + _dot((qr * qdec_ref[h]).astype(BF16),
                                                      s_prev.astype(BF16))
            ret_ref[0, 0, h] = (sdec_ref[h] * s_prev
                                + _dot_tn((kr * kdec_ref[h]).astype(BF16), vb))
            outs.append(_group_norm(o))
        chunk_outs.append(jnp.concatenate(outs, axis=1))
    o = jnp.concatenate(chunk_outs, axis=0)

    g = _proj(hb, win_ref, dims.cols[4])
    ga = _proj(hb, win_ref, dims.cols[5])
    gb = _proj(hb, win_ref, dims.cols[6])
    x1_ref[0] = _merge_out(x, gate, a, o, g, ga, gb, wbr_ref, wout_ref)


def _const_spec(shape):
    zeros = (0,) * len(shape)
    return pl.BlockSpec(shape, lambda *_: zeros)


def _resident_spec(shape):
    zeros = (0,) * len(shape)
    return pl.BlockSpec(shape, lambda *_: zeros, pipeline_mode=pl.Buffered(1))


def _mixer_prompt(dims, x, mod, g1, weights, tables):
    b, s, d = x.shape
    L = PROMPT_STEP_ROWS
    nc = s // L
    win, wpg, ps, wbp, wbr, wout = weights
    cosb, sinb, intra, qdec, kdec, sdec = tables
    in_specs = [
        pl.BlockSpec((1, L, d), lambda i, c: (i, c, 0)),
        pl.BlockSpec((1, N_MOD, d), lambda i, c: (i, 0, 0)),
        _const_spec(g1.shape), _resident_spec(win.shape), _const_spec(wpg.shape),
        _const_spec(ps.shape), _resident_spec(wbp.shape), _resident_spec(wbr.shape),
        _resident_spec(wout.shape),
        pl.BlockSpec((L, dims.dk), lambda i, c: (c, 0)),
        pl.BlockSpec((L, dims.dk), lambda i, c: (c, 0)),
        _const_spec(intra.shape), _const_spec(qdec.shape), _const_spec(kdec.shape),
        _const_spec(sdec.shape),
    ]
    out_shape = (
        jax.ShapeDtypeStruct((b, s, d), F32),
        jax.ShapeDtypeStruct((b, POOL_PREV, dims.d_pool), F32),
        jax.ShapeDtypeStruct((1, b, dims.heads, dims.dk, dims.dv), F32),
    )
    out_specs = (
        pl.BlockSpec((1, L, d), lambda i, c: (i, c, 0)),
        pl.BlockSpec((1, POOL_PREV, dims.d_pool), lambda i, c: (i, 0, 0)),
        pl.BlockSpec((1, 1, dims.heads, dims.dk, dims.dv), lambda i, c: (0, i, 0, 0, 0)),
    )
    weight_bytes = 2 * (win.size + wpg.size + wbp.size + wbr.size + wout.size)
    return pl.pallas_call(
        functools.partial(_mixer_prompt_kernel, dims),
        out_shape=out_shape,
        grid=(b, nc),
        in_specs=in_specs,
        out_specs=out_specs,
        scratch_shapes=[pltpu.VMEM((POOL_PREV + L, dims.d_pool), F32)],
        compiler_params=pltpu.CompilerParams(
            dimension_semantics=("arbitrary", "arbitrary"),
            vmem_limit_bytes=_vmem_limit(2 * weight_bytes + 16 * 1024 * 1024),
        ),
        name="mixer_prompt",
    )(x, mod, g1, win, wpg, ps, wbp, wbr, wout, cosb, sinb, intra, qdec, kdec, sdec)


def _mixer_sample_kernel(dims, ds, x_ref, modt_ref, g1_ref, win_ref, wpg_ref, ps_ref, wbp_ref,
                         wbr_ref, wout_ref, cos_ref, sin_ref, intra_ref, qdec_ref, kdec_ref,
                         sdec_ref, bandp_ref, bandu_ref, cnt_ref, poolin_ref, sret_ref,
                         x1_ref, u_ref, ret_ref, hb_ref, q_ref, k_ref, v_ref, pm_ref, o_ref):
    j = pl.program_id(1)
    R = cos_ref.shape[0]
    ns = R // ds

    @pl.when(j == 0)
    def _project():
        hb = _rms_mod(x_ref[...], g1_ref[...], modt_ref[0], modt_ref[1]).astype(BF16)
        hb_ref[...] = hb
        u_ref[...] = _proj(hb, win_ref, dims.cols[0])
        q_ref[...] = _proj(hb, win_ref, dims.cols[1])
        k_ref[...] = _proj(hb, win_ref, dims.cols[2])
        v_ref[...] = _proj(hb, win_ref, dims.cols[3]).astype(BF16)

    rows = pl.ds(pl.multiple_of(j * R, R), R)

    u = u_ref[rows, :]
    hist = poolin_ref[...].reshape(ns * POOL_PREV, dims.d_pool)
    pm_parts = []
    for gi in range(len(POOL_WINDOWS)):
        lo, hi = gi * dims.gd, (gi + 1) * dims.gd
        win_sum = jnp.zeros((R, dims.gd), F32)
        for part in _split_bf16(hist[:, lo:hi], 3):
            win_sum = win_sum + _dot(bandp_ref[gi], part)
        for part in _split_bf16(u[:, lo:hi], 3):
            win_sum = win_sum + _dot(bandu_ref[gi], part)
        pm_parts.append(win_sum / cnt_ref[gi] - u[:, lo:hi])
    pm_ref[rows, :] = jnp.concatenate(pm_parts, axis=1)

    q = q_ref[rows, :]
    k = k_ref[rows, :]
    v = v_ref[rows, :]
    cosb, sinb = cos_ref[...], sin_ref[...]
    row_seq = lax.broadcasted_iota(jnp.int32, (R, ns * dims.dk), 0) // ds
    col_seq = lax.broadcasted_iota(jnp.int32, (R, ns * dims.dk), 1) // dims.dk
    own = row_seq == col_seq
    outs = []
    for h in range(dims.heads):
        qr = _rotary(q[:, h * dims.dk:(h + 1) * dims.dk], cosb, sinb)
        kr = _rotary(k[:, h * dims.dk:(h + 1) * dims.dk], cosb, sinb) * (dims.dk ** -0.5)
        vb = v[:, h * dims.dv:(h + 1) * dims.dv]
        scores = _dot_nt(qr.astype(BF16), kr.astype(BF16)) * intra_ref[h]
        s_prev = sret_ref[:, h].reshape(ns * dims.dk, dims.dv)
        qd = jnp.concatenate([qr * qdec_ref[h]] * ns, axis=1)
        kd = jnp.concatenate([kr * kdec_ref[h]] * ns, axis=1)
        qexp = jnp.where(own, qd, 0.0).astype(BF16)
        kexp = jnp.where(own, kd, 0.0).astype(BF16)
        o = _dot(scores.astype(BF16), vb) + _dot(qexp, s_prev.astype(BF16))
        s_new = sdec_ref[h] * s_prev + _dot_tn(kexp, vb)
        ret_ref[:, h] = s_new.reshape(ns, dims.dk, dims.dv)
        outs.append(_group_norm(o))
    o_ref[rows, :] = jnp.concatenate(outs, axis=1)

    @pl.when(j == pl.num_programs(1) - 1)
    def _merge():
        hb = hb_ref[...]
        pm = pm_ref[...]
        pm_parts = [pm[:, gi * dims.gd:(gi + 1) * dims.gd] for gi in range(len(POOL_WINDOWS))]
        a = _pool_branch(pm_parts, wpg_ref, ps_ref, wbp_ref)
        g = _proj(hb, win_ref, dims.cols[4])
        ga = _proj(hb, win_ref, dims.cols[5])
        gb = _proj(hb, win_ref, dims.cols[6])
        x1_ref[...] = _merge_out(x_ref[...], modt_ref[2], a, o_ref[...], g, ga, gb,
                                 wbr_ref, wout_ref)


def _mixer_sample(dims, ds, x2d, modt, g1, weights, tables, pool_in, state_ret):
    t, d = x2d.shape
    ns = SAMPLE_SEQS_PER_STEP
    steps = SAMPLE_STEPS_PER_GROUP
    R = ns * ds
    RG = R * steps
    win, wpg, ps, wbp, wbr, wout = weights
    cosb, sinb, intra, qdec, kdec, sdec, bandp, bandu, cnt = tables
    db = state_ret.shape[0]
    in_specs = [
        pl.BlockSpec((RG, d), lambda i, j: (i, 0)),
        pl.BlockSpec((N_MOD, RG, d), lambda i, j: (0, i, 0)),
        _const_spec(g1.shape), _resident_spec(win.shape), _const_spec(wpg.shape),
        _const_spec(ps.shape), _resident_spec(wbp.shape), _resident_spec(wbr.shape),
        _resident_spec(wout.shape), _const_spec(cosb.shape), _const_spec(sinb.shape),
        _const_spec(intra.shape), _const_spec(qdec.shape), _const_spec(kdec.shape),
        _const_spec(sdec.shape), _const_spec(bandp.shape), _const_spec(bandu.shape),
        _const_spec(cnt.shape),
        pl.BlockSpec((ns, POOL_PREV, dims.d_pool), lambda i, j: (i * steps + j, 0, 0)),
        pl.BlockSpec((ns, dims.heads, dims.dk, dims.dv), lambda i, j: (i * steps + j, 0, 0, 0)),
    ]
    out_shape = (
        jax.ShapeDtypeStruct((t, d), F32),
        jax.ShapeDtypeStruct((t, dims.d_pool), F32),
        jax.ShapeDtypeStruct(state_ret.shape, state_ret.dtype),
    )
    out_specs = (
        pl.BlockSpec((RG, d), lambda i, j: (i, 0)),
        pl.BlockSpec((RG, dims.d_pool), lambda i, j: (i, 0)),
        pl.BlockSpec((ns, dims.heads, dims.dk, dims.dv), lambda i, j: (i * steps + j, 0, 0, 0)),
    )
    weight_bytes = 2 * (win.size + wpg.size + wbp.size + wbr.size + wout.size)
    state_bytes = 4 * ns * dims.heads * dims.dk * dims.dv
    return pl.pallas_call(
        functools.partial(_mixer_sample_kernel, dims, ds),
        out_shape=out_shape,
        grid=(db // (ns * steps), steps),
        in_specs=in_specs,
        out_specs=out_specs,
        scratch_shapes=[
            pltpu.VMEM((RG, d), BF16), pltpu.VMEM((RG, dims.d_k), F32),
            pltpu.VMEM((RG, dims.d_k), F32), pltpu.VMEM((RG, dims.d_v), BF16),
            pltpu.VMEM((RG, dims.d_pool), F32), pltpu.VMEM((RG, dims.d_v), F32),
        ],
        compiler_params=pltpu.CompilerParams(
            dimension_semantics=("arbitrary", "arbitrary"),
            vmem_limit_bytes=_vmem_limit(2 * weight_bytes + 4 * state_bytes + 8 * 1024 * 1024),
        ),
        name="mixer_sample",
    )(x2d, modt, g1, win, wpg, ps, wbp, wbr, wout, cosb, sinb, intra, qdec, kdec, sdec,
      bandp, bandu, cnt, pool_in, state_ret)


def _rotary_tables(pos, dk):
    half = dk // 2
    inv = np.power(ROPE_BASE, -np.arange(half, dtype=np.float64) / half)
    ang = pos.astype(np.float64)[:, None] * inv[None, :]
    cos, sin = np.cos(ang), np.sin(ang)
    return (np.concatenate([cos, cos], axis=1).astype(np.float32),
            np.concatenate([-sin, sin], axis=1).astype(np.float32))


def _decay_tables(log_g, chunk, reps, dk, dv):
    idx = np.arange(chunk, dtype=np.float64)
    diff = idx[:, None] - idx[None, :]
    intra = np.where((diff >= 0)[None],
                     np.exp(log_g[:, None, None] * np.maximum(diff, 0.0)[None]), 0.0)
    heads = log_g.shape[0]
    eye = np.eye(reps)
    intra = np.einsum("ab,hij->haibj", eye, intra).reshape(heads, reps * chunk, reps * chunk)
    q_dec = np.exp(log_g[:, None] * (idx[None, :] + 1.0))
    k_dec = np.exp(log_g[:, None] * (chunk - 1.0 - idx[None, :]))
    s_dec = np.exp(log_g * chunk)
    q_dec = np.broadcast_to(np.tile(q_dec, (1, reps))[:, :, None], (heads, reps * chunk, dk))
    k_dec = np.broadcast_to(np.tile(k_dec, (1, reps))[:, :, None], (heads, reps * chunk, dk))
    s_dec = np.broadcast_to(s_dec[:, None, None], (heads, 1, dv))
    return tuple(np.ascontiguousarray(t, dtype=np.float32) for t in (intra, q_dec, k_dec, s_dec))


def _sample_pool_tables(ns, ds, gd, start_pos):
    n = np.arange(ds)
    i = np.arange(POOL_PREV)
    eye = np.eye(ns, dtype=np.float32)
    bandp, bandu, cnt = [], [], []
    for w in POOL_WINDOWS:
        bp = ((i[None, :] - POOL_PREV) >= (n[:, None] - w + 1)).astype(np.float32)
        bu = ((n[None, :] <= n[:, None]) & (n[None, :] >= n[:, None] - w + 1)).astype(np.float32)
        bandp.append(np.einsum("ab,ni->anbi", eye, bp).reshape(ns * ds, ns * POOL_PREV))
        bandu.append(np.einsum("ab,nm->anbm", eye, bu).reshape(ns * ds, ns * ds))
        c = np.minimum(w, start_pos + n + 1).astype(np.float32)
        cnt.append(np.broadcast_to(np.tile(c, ns)[:, None], (ns * ds, gd)))
    return (np.stack(bandp).astype(BF16), np.stack(bandu).astype(BF16),
            np.ascontiguousarray(np.stack(cnt)))


def _first_max_onehot(v, idx, big, axes):
    m = jnp.max(v, axis=axes, keepdims=True)
    first = jnp.min(jnp.where(v == m, idx, big), axis=axes, keepdims=True)
    return idx == first


def _route_kernel(n_prompt_blocks, xp_ref, xs_ref, modp_ref, modt_ref, g2_ref, wr_ref, br_ref,
                  before_ref, below_ref, h2_ref, pos_ref, wts_ref, cnt_ref, rank_ref, gates_ref,
                  h_ref):
    i = pl.program_id(0)
    is_p = i < n_prompt_blocks

    @pl.when(is_p)
    def _prompt():
        h_ref[...] = _rms_mod(xp_ref[...], g2_ref[...], modp_ref[0, 3:4, :], modp_ref[0, 4:5, :])

    @pl.when(jnp.logical_not(is_p))
    def _sample():
        h_ref[...] = _rms_mod(xs_ref[...], g2_ref[...], modt_ref[3], modt_ref[4])

    h2 = h_ref[...]
    hh = h2.astype(BF16)
    h2_ref[...] = hh

    hl = (h2 - hh.astype(F32)).astype(BF16)
    wh, wl = _split_bf16(wr_ref[...], 2)
    logits = _dot_nt(wh, hh) + (_dot_nt(wh, hl) + _dot_nt(wl, hh))
    s = jax.nn.sigmoid(logits)
    e, t = s.shape
    ge = e // N_GROUPS
    neg = -jnp.inf
    s3 = s.reshape(N_GROUPS, ge, t)
    sb3 = s3 + br_ref[...]

    j3 = lax.broadcasted_iota(jnp.int32, (N_GROUPS, ge, t), 1)
    top1 = _first_max_onehot(sb3, j3, ge, 1)
    m1 = jnp.max(sb3, axis=1, keepdims=True)
    m2 = jnp.max(jnp.where(top1, neg, sb3), axis=1, keepdims=True)
    gv = m1 + m2

    g3 = lax.broadcasted_iota(jnp.int32, (N_GROUPS, 1, t), 0)
    gsel = jnp.zeros((N_GROUPS, 1, t), F32)
    for _ in range(TOPK_GROUPS):
        hit = _first_max_onehot(gv, g3, N_GROUPS, 0)
        gsel = jnp.where(hit, 1.0, gsel)
        gv = jnp.where(hit, neg, gv)

    e3 = lax.broadcasted_iota(jnp.int32, (N_GROUPS, ge, t), 0) * ge + j3
    ev = jnp.where(gsel > 0.5, sb3, neg)
    sel = jnp.zeros((N_GROUPS, ge, t), F32)
    chosen = jnp.zeros((N_GROUPS, ge, t), F32)
    hits = []
    for _ in range(TOP_K):
        hit = _first_max_onehot(ev, e3, e, (0, 1))
        hits.append(hit)
        sel = jnp.where(hit, s3, sel)
        chosen = jnp.where(hit, 1.0, chosen)
        ev = jnp.where(hit, neg, ev)
    tot = jnp.sum(sel, axis=(0, 1), keepdims=True)
    gates = sel / tot * ROUTED_SCALE

    chosen2 = chosen.reshape(e, t)
    rank = _dot(chosen2.astype(BF16), before_ref[...])
    cnt = jnp.sum(chosen2, axis=1, keepdims=True)
    ovf_units = jnp.floor((jnp.maximum(cnt - SLOT_ROWS, 0.0) + (DMA_ROWS - 1)) / DMA_ROWS)
    ovf_off = _dot(below_ref[...], jnp.broadcast_to(ovf_units, (e, LANES)).astype(BF16))[:, 0:1]
    e_col = lax.broadcasted_iota(jnp.int32, (e, 1), 0).astype(F32)
    slot = jnp.where(rank < SLOT_ROWS, e_col * SLOT_ROWS + rank,
                     e * SLOT_ROWS + ovf_off * DMA_ROWS + (rank - SLOT_ROWS))
    slot3 = slot.reshape(N_GROUPS, ge, t)
    for k, hit in enumerate(hits):
        pos_k = jnp.sum(jnp.where(hit, slot3, 0.0), axis=(0, 1), keepdims=True)
        w_k = jnp.sum(jnp.where(hit, gates, 0.0), axis=(0, 1), keepdims=True)
        pos_ref[k:k + 1, :] = pos_k.reshape(1, t).astype(jnp.int32)
        wts_ref[k:k + 1, :] = w_k.reshape(1, t)
    cnt_ref[0] = jnp.broadcast_to(cnt, (e, LANES)).astype(jnp.int32)
    rank_ref[...] = jnp.where(chosen2 > 0.5, rank, -1.0).astype(jnp.int32)
    gates_ref[...] = gates.reshape(e, t)


def _route(xp2d, xs2d, modp, modt, g2, wr_t, br):
    tp, d = xp2d.shape
    ts = xs2d.shape[0]
    tb = TOKEN_BLOCK
    npb, nsb = tp // tb, ts // tb
    e = wr_t.shape[0]
    seq = tp // modp.shape[0]
    bpb = seq // tb

    def p_idx(i):
        return jnp.minimum(i, npb - 1)

    def s_idx(i):
        return jnp.maximum(i - npb, 0)

    t = tp + ts
    nb = npb + nsb
    tok = np.arange(tb)
    before = (tok[:, None] < tok[None, :]).astype(BF16)
    ex = np.arange(e)
    below = (ex[None, :] < ex[:, None]).astype(BF16)
    return pl.pallas_call(
        functools.partial(_route_kernel, npb),
        out_shape=(jax.ShapeDtypeStruct((t, d), BF16),
                   jax.ShapeDtypeStruct((TOP_K, t), jnp.int32),
                   jax.ShapeDtypeStruct((TOP_K, t), F32),
                   jax.ShapeDtypeStruct((nb, e, LANES), jnp.int32),
                   jax.ShapeDtypeStruct((e, t), jnp.int32),
                   jax.ShapeDtypeStruct((e, t), F32)),
        grid=(nb,),
        in_specs=[
            pl.BlockSpec((tb, d), lambda i: (p_idx(i), 0)),
            pl.BlockSpec((tb, d), lambda i: (s_idx(i), 0)),
            pl.BlockSpec((1, N_MOD, d), lambda i: (p_idx(i) // bpb, 0, 0)),
            pl.BlockSpec((N_MOD, tb, d), lambda i: (0, s_idx(i), 0)),
            _const_spec(g2.shape), _const_spec(wr_t.shape), _const_spec(br.shape),
            _const_spec(before.shape), _const_spec(below.shape),
        ],
        out_specs=(pl.BlockSpec((tb, d), lambda i: (i, 0)),
                   pl.BlockSpec((TOP_K, tb), lambda i: (0, i)),
                   pl.BlockSpec((TOP_K, tb), lambda i: (0, i)),
                   pl.BlockSpec((1, e, LANES), lambda i: (i, 0, 0)),
                   pl.BlockSpec((e, tb), lambda i: (0, i)),
                   pl.BlockSpec((e, tb), lambda i: (0, i))),
        scratch_shapes=[pltpu.VMEM((tb, d), F32)],
        compiler_params=pltpu.CompilerParams(
            dimension_semantics=("arbitrary",),
            vmem_limit_bytes=_vmem_limit(32 * 1024 * 1024),
        ),
        name="route",
    )(xp2d, xs2d, modp, modt, g2, wr_t, br, before, below)


class _Layout:
    def __init__(self, counts):
        nb, ne = counts.shape
        u, g, tm = SLOT_ROWS, DMA_ROWS, EXPERT_TILE
        pc = (counts + g - 1) // g * g
        ovf_units = (jnp.maximum(counts - u, 0) + g - 1) // g
        rows_e = jnp.sum(pc, axis=0)
        region = (rows_e + u + tm - 1) // tm * tm
        goff = jnp.cumsum(region) - region
        dst = goff[None, :] + jnp.cumsum(pc, axis=0) - pc
        self.dst = dst.reshape(-1)
        ovf_end = jnp.cumsum(ovf_units, axis=1)
        ovf_first = ovf_end - ovf_units
        unit = jnp.arange(_overflow_units_cap(ne))[None, :, None]
        owns = (ovf_first[:, None, :] <= unit) & (unit < ovf_end[:, None, :])
        row = dst[:, None, :] + u + (unit - ovf_first[:, None, :]) * g
        self.ovf_dst = jnp.sum(jnp.where(owns, row, 0), axis=2).reshape(-1)
        self.ovf_tot = ovf_end[:, -1]
        self.tail_start = goff + rows_e
        self.tail_units = (region - rows_e) // g
        tiles = region // tm
        self.region_tiles = jnp.concatenate([tiles, jnp.sum(tiles, keepdims=True)])


def _overflow_units_cap(ne):
    return TOKEN_BLOCK * TOP_K // DMA_ROWS + ne


def _block_buffer_rows(tb, ne):
    rows = ne * SLOT_ROWS + tb * TOP_K + ne * (DMA_ROWS - 1)
    return (rows + ONEHOT_ROWS - 1) // ONEHOT_ROWS * ONEHOT_ROWS


def _expert_rows_cap(t, tb, ne):
    nb = t // tb
    rows = t * TOP_K + nb * ne * (DMA_ROWS - 1) + ne * (SLOT_ROWS + EXPERT_TILE - 1)
    return (rows + EXPERT_TILE - 1) // EXPERT_TILE * EXPERT_TILE


def _slot_copy(buf_ref, slot, e, hbm_ref, row, sem, to_hbm):
    src = buf_ref.at[slot, pl.ds(pl.multiple_of(e * SLOT_ROWS, DMA_ROWS), SLOT_ROWS)]
    dst = hbm_ref.at[pl.ds(pl.multiple_of(row, DMA_ROWS), SLOT_ROWS)]
    return pltpu.make_async_copy(src, dst, sem) if to_hbm else pltpu.make_async_copy(dst, src, sem)


def _unit_copy(buf_ref, slot, buf_row, hbm_ref, row, sem, to_hbm):
    src = buf_ref.at[slot, pl.ds(pl.multiple_of(buf_row, DMA_ROWS), DMA_ROWS)]
    dst = hbm_ref.at[pl.ds(pl.multiple_of(row, DMA_ROWS), DMA_ROWS)]
    return pltpu.make_async_copy(src, dst, sem) if to_hbm else pltpu.make_async_copy(dst, src, sem)


def _slots_wait(ne, buf_ref, slot, hbm_ref, sem, to_hbm):
    src = buf_ref.at[slot, pl.ds(0, ne * SLOT_ROWS)]
    dst = hbm_ref.at[pl.ds(0, ne * SLOT_ROWS)]
    c = pltpu.make_async_copy(src, dst, sem) if to_hbm else pltpu.make_async_copy(dst, src, sem)
    c.wait()


def _overflow_copies(ne, blk, slot, odst_ref, otot_ref, buf_ref, hbm_ref, sem, to_hbm, wait):
    def per_unit(i, carry):
        row = odst_ref[blk * _overflow_units_cap(ne) + i]
        c = _unit_copy(buf_ref, slot, ne * SLOT_ROWS + i * DMA_ROWS, hbm_ref, row, sem, to_hbm)
        c.wait() if wait else c.start()
        return carry

    lax.fori_loop(0, otot_ref[blk], per_unit, 0)


def _slot_chunks(ne):
    return [range(c, c + SLOT_CHUNK_EXPERTS) for c in range(0, ne, SLOT_CHUNK_EXPERTS)]


def _overflow_chunks(n_units):
    return lax.shift_right_logical(n_units * DMA_ROWS + (ONEHOT_ROWS - 1),
                                   ONEHOT_ROWS.bit_length() - 1)


def _overflow_select(pos, vals, r0):
    rows = lax.broadcasted_iota(jnp.int32, (ONEHOT_ROWS, pos.shape[1]), 0) + r0
    sel = jnp.zeros(rows.shape, F32)
    for k in range(TOP_K):
        v = 1.0 if vals is None else vals[k:k + 1, :]
        sel = jnp.where(pos[k:k + 1, :] == rows, v, sel)
    return sel


def _dispatch_kernel(ne, dst_ref, odst_ref, otot_ref, tails_ref, tailu_ref,
                     h2_ref, rank_ref, pos_ref, xs_hbm, buf_ref, zero_ref, sem, zsem):
    b = pl.program_id(0)
    nb = pl.num_programs(0)
    slot = b % 2
    h2 = h2_ref[...]
    j = lax.broadcasted_iota(jnp.int32, (SLOT_ROWS, h2.shape[0]), 0)

    for ci, experts in enumerate(_slot_chunks(ne)):
        onehot = jnp.concatenate(
            [jnp.where(rank_ref[e:e + 1, :] == j, 1.0, 0.0) for e in experts], axis=0)
        r0 = experts[0] * SLOT_ROWS
        buf_ref[slot, r0:r0 + len(experts) * SLOT_ROWS, :] = (
            _dot(onehot.astype(BF16), h2).astype(BF16))
        if ci == 0:
            @pl.when(b > 0)
            def _prev():
                _slots_wait(ne, buf_ref, 1 - slot, xs_hbm, sem.at[1 - slot], True)
                _overflow_copies(ne, b - 1, 1 - slot, odst_ref, otot_ref, buf_ref, xs_hbm,
                                 sem.at[1 - slot], True, True)
        for e in experts:
            _slot_copy(buf_ref, slot, e, xs_hbm, dst_ref[b * ne + e], sem.at[slot], True).start()

    pos = pos_ref[...]

    def chunk(ci, carry):
        r0 = pl.multiple_of(ne * SLOT_ROWS + ci * ONEHOT_ROWS, ONEHOT_ROWS)
        onehot = _overflow_select(pos, None, r0).astype(BF16)
        buf_ref[slot, pl.ds(r0, ONEHOT_ROWS), :] = _dot(onehot, h2).astype(BF16)
        return carry

    lax.fori_loop(0, _overflow_chunks(otot_ref[b]), chunk, 0)
    _overflow_copies(ne, b, slot, odst_ref, otot_ref, buf_ref, xs_hbm, sem.at[slot], True, False)

    @pl.when(b == nb - 1)
    def _last():
        _slots_wait(ne, buf_ref, slot, xs_hbm, sem.at[slot], True)
        _overflow_copies(ne, b, slot, odst_ref, otot_ref, buf_ref, xs_hbm, sem.at[slot], True,
                         True)
        zero_ref[...] = jnp.zeros_like(zero_ref)

        def tail(wait):
            def per_expert(e, carry):
                def per_unit(i, carry2):
                    row = pl.multiple_of(tails_ref[e] + i * DMA_ROWS, DMA_ROWS)
                    c = pltpu.make_async_copy(zero_ref, xs_hbm.at[pl.ds(row, DMA_ROWS)], zsem)
                    c.wait() if wait else c.start()
                    return carry2
                lax.fori_loop(0, tailu_ref[e], per_unit, 0)
                return carry
            lax.fori_loop(0, ne, per_expert, 0)

        tail(False)
        tail(True)


def _dispatch(h2, rank, pos, lay, ne, rows_cap):
    t, d = h2.shape
    tb = TOKEN_BLOCK
    buf_rows = _block_buffer_rows(tb, ne)
    return pl.pallas_call(
        functools.partial(_dispatch_kernel, ne),
        out_shape=jax.ShapeDtypeStruct((rows_cap, d), BF16),
        grid_spec=pltpu.PrefetchScalarGridSpec(
            num_scalar_prefetch=5,
            grid=(t // tb,),
            in_specs=[
                pl.BlockSpec((tb, d), lambda i, *_: (i, 0)),
                pl.BlockSpec((ne, tb), lambda i, *_: (0, i)),
                pl.BlockSpec((TOP_K, tb), lambda i, *_: (0, i)),
            ],
            out_specs=pl.BlockSpec(memory_space=pl.ANY),
            scratch_shapes=[
                pltpu.VMEM((2, buf_rows, d), BF16),
                pltpu.VMEM((DMA_ROWS, d), BF16),
                pltpu.SemaphoreType.DMA((2,)),
                pltpu.SemaphoreType.DMA(()),
            ],
        ),
        compiler_params=pltpu.CompilerParams(
            dimension_semantics=("arbitrary",),
            vmem_limit_bytes=_vmem_limit(4 * buf_rows * d + 16 * 1024 * 1024),
        ),
        name="dispatch",
    )(lay.dst, lay.ovf_dst, lay.ovf_tot, lay.tail_start, lay.tail_units, h2, rank, pos)


def _experts_kernel(tiles_ref, xe_hbm, wg_ref, wu_ref, wd_ref, ye_hbm,
                    xbuf_ref, ybuf_ref, wgb_ref, wub_ref, wdb_ref, done_ref, xsem, ysem):
    e = pl.program_id(0)
    ne = pl.num_programs(0)
    tm = xbuf_ref.shape[1]

    def x_copy(row, slot):
        return pltpu.make_async_copy(xe_hbm.at[pl.ds(pl.multiple_of(row, tm), tm)],
                                     xbuf_ref.at[slot], xsem.at[slot])

    def y_copy(row, slot):
        return pltpu.make_async_copy(ybuf_ref.at[slot],
                                     ye_hbm.at[pl.ds(pl.multiple_of(row, tm), tm)], ysem.at[slot])

    total = tiles_ref[ne]

    def request(limit):
        for _ in range(EXPERT_TILES_PER_ITER + 1):
            r = done_ref[1]

            @pl.when(r < jnp.minimum(limit, total))
            def _start():
                x_copy(r * tm, lax.rem(r, RING_SLOTS)).start()
                done_ref[1] = r + 1

    @pl.when(e == 0)
    def _first():
        done_ref[0] = 0
        done_ref[1] = 0
        request(EXPERT_TILES_PER_ITER)

    wgb_ref[...] = wg_ref[0].astype(BF16)
    wub_ref[...] = wu_ref[0].astype(BF16)
    wdb_ref[...] = wd_ref[0].astype(BF16)
    done = done_ref[0]
    n = tiles_ref[e]

    def run_tiles(g, count):
        request(g + count + EXPERT_TILES_PER_ITER)
        slots = [lax.rem(g + t, RING_SLOTS) for t in range(count)]
        for t in range(count):
            x_copy(0, slots[t]).wait()

            @pl.when(g + t >= RING_SLOTS)
            def _reuse():
                y_copy(0, slots[t]).wait()

        part = tm // EXPERT_TILE_PARTS
        pieces = [(slots[t], r0) for t in range(count) for r0 in range(0, tm, part)]
        hids = []
        for slot, r0 in pieces:
            x = xbuf_ref[slot, r0:r0 + part, :]
            hids.append((jax.nn.silu(_dot(x, wgb_ref[...])) * _dot(x, wub_ref[...])).astype(BF16))
        for (slot, r0), hid in zip(pieces, hids):
            ybuf_ref[slot, r0:r0 + part, :] = _dot(hid, wdb_ref[...]).astype(BF16)
        for t in range(count):
            y_copy((g + t) * tm, slots[t]).start()

    per = EXPERT_TILES_PER_ITER
    full = n // per

    def full_iter(j, carry):
        run_tiles(done + j * per, per)
        return carry

    lax.fori_loop(0, full, full_iter, 0)
    for rest in range(1, per):
        @pl.when(n - full * per == rest)
        def _rest():
            run_tiles(done + full * per, rest)

    done_ref[0] = done + n

    @pl.when(e == ne - 1)
    def _drain():
        for s in range(RING_SLOTS):
            y_copy(0, s).wait()


def _experts(xs, lay, wg, wu, wd):
    rows_cap, d = xs.shape
    ne, _, de = wg.shape
    assert ne >= RING_SLOTS
    tm = EXPERT_TILE
    ring = RING_SLOTS
    return pl.pallas_call(
        _experts_kernel,
        out_shape=jax.ShapeDtypeStruct((rows_cap, d), BF16),
        grid_spec=pltpu.PrefetchScalarGridSpec(
            num_scalar_prefetch=1,
            grid=(ne,),
            in_specs=[
                pl.BlockSpec(memory_space=pl.ANY),
                pl.BlockSpec((1, d, de), lambda e, *_: (e, 0, 0)),
                pl.BlockSpec((1, d, de), lambda e, *_: (e, 0, 0)),
                pl.BlockSpec((1, de, d), lambda e, *_: (e, 0, 0)),
            ],
            out_specs=pl.BlockSpec(memory_space=pl.ANY),
            scratch_shapes=[
                pltpu.VMEM((ring, tm, d), BF16), pltpu.VMEM((ring, tm, d), BF16),
                pltpu.VMEM((d, de), BF16), pltpu.VMEM((d, de), BF16), pltpu.VMEM((de, d), BF16),
                pltpu.SMEM((2,), jnp.int32),
                pltpu.SemaphoreType.DMA((ring,)), pltpu.SemaphoreType.DMA((ring,)),
            ],
        ),
        compiler_params=pltpu.CompilerParams(
            dimension_semantics=("arbitrary",),
            vmem_limit_bytes=_vmem_limit(32 * 1024 * 1024),
        ),
        name="experts",
    )(lay.region_tiles, xs, wg, wu, wd)


def _final_kernel(n_prompt_blocks, ne, dst_ref, odst_ref, otot_ref,
                  xp_ref, xs_ref, modp_ref, modt_ref, h2_ref, rank_ref, gates_ref, pos_ref, wts_ref,
                  wsg_ref, wsu_ref, wsd_ref, gf_ref, ye_hbm, yp_ref, ys_ref,
                  buf_ref, acc_ref, sem):
    i = pl.program_id(0)
    nb = pl.num_programs(0)
    slot = i % 2

    def fetch_slots(blk, sl, experts):
        for e in experts:
            _slot_copy(buf_ref, sl, e, ye_hbm, dst_ref[blk * ne + e], sem.at[sl], False).start()

    def overflow(blk, sl, wait):
        _overflow_copies(ne, blk, sl, odst_ref, otot_ref, buf_ref, ye_hbm, sem.at[sl], False, wait)

    @pl.when(i == 0)
    def _first():
        buf_ref[...] = jnp.zeros_like(buf_ref)
        fetch_slots(0, 0, range(ne))
        overflow(0, 0, False)

    @pl.when(i + 1 < nb)
    def _prefetch():
        fetch_slots(i + 1, 1 - slot, range(ne))
        overflow(i + 1, 1 - slot, False)

    _slots_wait(ne, buf_ref, slot, ye_hbm, sem.at[slot], False)
    overflow(i, slot, True)

    tb = h2_ref.shape[0]
    j = lax.broadcasted_iota(jnp.int32, (SLOT_ROWS, tb), 0)
    acc = jnp.zeros((tb, h2_ref.shape[1]), F32)
    for experts in _slot_chunks(ne):
        sel = jnp.concatenate(
            [jnp.where(rank_ref[e:e + 1, :] == j, gates_ref[e:e + 1, :], 0.0) for e in experts],
            axis=0)
        r0 = experts[0] * SLOT_ROWS
        acc = acc + _dot_tn(sel.astype(BF16), buf_ref[slot, r0:r0 + len(experts) * SLOT_ROWS, :])
    acc_ref[...] = acc

    pos = pos_ref[...]
    wts = wts_ref[...]

    def chunk(ci, carry):
        r0 = pl.multiple_of(ne * SLOT_ROWS + ci * ONEHOT_ROWS, ONEHOT_ROWS)
        sel = _overflow_select(pos, wts, r0).astype(BF16)
        acc_ref[...] += _dot_tn(sel, buf_ref[slot, pl.ds(r0, ONEHOT_ROWS), :])
        return carry

    lax.fori_loop(0, _overflow_chunks(otot_ref[i]), chunk, 0)

    is_p = i < n_prompt_blocks
    x = jnp.where(is_p, xp_ref[...], xs_ref[...])
    gate = jnp.where(is_p, modp_ref[0, 5:6, :], modt_ref[5])
    hb = h2_ref[...]
    hid = jax.nn.silu(_dot(hb, wsg_ref[...])) * _dot(hb, wsu_ref[...])
    f = acc_ref[...] + _dot(hid.astype(BF16), wsd_ref[...])
    x2 = x + gate * f
    y = x2 * lax.rsqrt(jnp.mean(x2 * x2, axis=-1, keepdims=True) + EPS) * gf_ref[...]

    @pl.when(is_p)
    def _p():
        yp_ref[...] = y

    @pl.when(jnp.logical_not(is_p))
    def _s():
        ys_ref[...] = y


def _final(xp2d, xs2d, modp, modt, h2, rank, gates, pos, wts, ye, lay, ne, wsg, wsu, wsd, gf):
    tp, d = xp2d.shape
    ts = xs2d.shape[0]
    tb = TOKEN_BLOCK
    npb, nsb = tp // tb, ts // tb
    seq = tp // modp.shape[0]
    bpb = seq // tb
    buf_rows = _block_buffer_rows(tb, ne)

    def p_idx(i):
        return jnp.minimum(i, npb - 1)

    def s_idx(i):
        return jnp.maximum(i - npb, 0)

    def const(shape):
        zeros = (0,) * len(shape)
        return pl.BlockSpec(shape, lambda i, *_: zeros)

    return pl.pallas_call(
        functools.partial(_final_kernel, npb, ne),
        out_shape=(jax.ShapeDtypeStruct((tp, d), F32), jax.ShapeDtypeStruct((ts, d), F32)),
        grid_spec=pltpu.PrefetchScalarGridSpec(
            num_scalar_prefetch=3,
            grid=(npb + nsb,),
            in_specs=[
                pl.BlockSpec((tb, d), lambda i, *_: (p_idx(i), 0)),
                pl.BlockSpec((tb, d), lambda i, *_: (s_idx(i), 0)),
                pl.BlockSpec((1, N_MOD, d), lambda i, *_: (p_idx(i) // bpb, 0, 0)),
                pl.BlockSpec((N_MOD, tb, d), lambda i, *_: (0, s_idx(i), 0)),
                pl.BlockSpec((tb, d), lambda i, *_: (i, 0)),
                pl.BlockSpec((ne, tb), lambda i, *_: (0, i)),
                pl.BlockSpec((ne, tb), lambda i, *_: (0, i)),
                pl.BlockSpec((TOP_K, tb), lambda i, *_: (0, i)),
                pl.BlockSpec((TOP_K, tb), lambda i, *_: (0, i)),
                const(wsg.shape), const(wsu.shape), const(wsd.shape), const(gf.shape),
                pl.BlockSpec(memory_space=pl.ANY),
            ],
            out_specs=(pl.BlockSpec((tb, d), lambda i, *_: (p_idx(i), 0)),
                       pl.BlockSpec((tb, d), lambda i, *_: (s_idx(i), 0))),
            scratch_shapes=[
                pltpu.VMEM((2, buf_rows, d), BF16),
                pltpu.VMEM((tb, d), F32),
                pltpu.SemaphoreType.DMA((2,)),
            ],
        ),
        compiler_params=pltpu.CompilerParams(
            dimension_semantics=("arbitrary",),
            vmem_limit_bytes=_vmem_limit(4 * buf_rows * d + 32 * 1024 * 1024),
        ),
        name="final",
    )(lay.dst, lay.ovf_dst, lay.ovf_tot,
      xp2d, xs2d, modp, modt, h2, rank, gates, pos, wts, wsg, wsu, wsd, gf, ye)


def kernel(x_prompt, x_sample, c_prompt, c_sample, state_pool, state_ret, w_ada, b_ada, g_norm1,
           w_in, w_pool_group, pool_scale, w_branch_pool, w_branch_ret, w_out, g_norm2, w_router,
           b_router, w_exp_gate, w_exp_up, w_exp_down, w_sh_gate, w_sh_up, w_sh_down, g_final):
    b, s, d = x_prompt.shape
    db, ds, _ = x_sample.shape
    depth = w_ada.shape[0]
    heads, dk, dv = state_ret.shape[2:]
    pool_buf, d_pool = state_pool.shape[2:]
    dims = _Dims(d, d_pool, heads, dk, dv)
    assert dims.d_in == w_in.shape[2]
    assert pool_buf == max(POOL_WINDOWS) - 1 and pool_buf < POOL_PREV
    assert s % PROMPT_STEP_ROWS == 0 and PROMPT_STEP_ROWS % RET_CHUNK == 0
    assert s % TOKEN_BLOCK == 0 and (db * ds) % TOKEN_BLOCK == 0
    assert db % (SAMPLE_SEQS_PER_STEP * SAMPLE_STEPS_PER_GROUP) == 0
    assert SLOT_ROWS % DMA_ROWS == 0 and (w_router.shape[2] * SLOT_ROWS) % ONEHOT_ROWS == 0

    log_g = np.log(1.0 - np.power(2.0, -5.0 - np.arange(heads, dtype=np.float64)))
    ns = SAMPLE_SEQS_PER_STEP
    cos_p, sin_p = _rotary_tables(np.arange(s), dk)
    cos_s, sin_s = _rotary_tables(PAST_LEN + np.arange(ds), dk)
    cos_s, sin_s = np.tile(cos_s, (ns, 1)), np.tile(sin_s, (ns, 1))
    dec_p = _decay_tables(log_g, RET_CHUNK, 1, dk, dv)
    dec_s = _decay_tables(log_g, ds, ns, dk, dv)
    pool_tabs = _sample_pool_tables(ns, ds, dims.gd, PAST_LEN)

    assert depth == 1, "the final kernel fuses the output norm into the (single) layer"
    l = 0
    xs = x_sample.reshape(db * ds, d)
    modp, modt = _ada(c_prompt, jnp.repeat(c_sample, ds, axis=0), w_ada[l], b_ada[l])
    modp = modp.reshape(b, N_MOD, d)
    g1 = g_norm1[l].reshape(1, d)
    weights = (w_in[l].astype(BF16), w_pool_group[l].astype(BF16),
               pool_scale[l].reshape(1, d_pool), w_branch_pool[l].astype(BF16),
               w_branch_ret[l].astype(BF16), w_out[l].astype(BF16))

    x1p, pool16, ret_p = _mixer_prompt(dims, x_prompt, modp, g1, weights,
                                       (cos_p, sin_p) + dec_p)
    pool_in = jnp.pad(state_pool[l], ((0, 0), (POOL_PREV - pool_buf, 0), (0, 0)))
    x1s, u_s, ret_s = _mixer_sample(dims, ds, xs, modt, g1, weights,
                                    (cos_s, sin_s) + dec_s + pool_tabs, pool_in,
                                    state_ret[l].astype(F32))
    pool_p = pool16[:, POOL_PREV - pool_buf:]
    pool_s = jnp.concatenate([state_pool[l], u_s.reshape(db, ds, d_pool)], axis=1)[:, -pool_buf:]

    x1p2d = x1p.reshape(b * s, d)
    ne = w_router.shape[2]
    h2, pos, wts, counts, rank, gates = _route(
        x1p2d, x1s, modp, modt, g_norm2[l].reshape(1, d), w_router[l].T,
        b_router[l].reshape(N_GROUPS, ne // N_GROUPS, 1))
    rows_cap = _expert_rows_cap(b * s + db * ds, TOKEN_BLOCK, ne)
    lay = _Layout(counts[:, :, 0])
    xe = _dispatch(h2, rank, pos, lay, ne, rows_cap)
    ye = _experts(xe, lay, w_exp_gate[l], w_exp_up[l], w_exp_down[l])
    yp, ys = _final(x1p2d, x1s, modp, modt, h2, rank, gates, pos, wts, ye, lay, ne,
                    w_sh_gate[l].astype(BF16), w_sh_up[l].astype(BF16),
                    w_sh_down[l].astype(BF16), g_final.reshape(1, d))

    return (yp.reshape(b, s, d), ys.reshape(db, ds, d), pool_p[None],
            ret_p.astype(x_prompt.dtype), pool_s[None], ret_s.astype(state_ret.dtype)[None])
```

```python
import functools

import jax
import jax.numpy as jnp
import numpy as np
from jax import lax
from jax.experimental import pallas as pl
from jax.experimental.pallas import tpu as pltpu

F32 = jnp.float32
BF16 = jnp.bfloat16

PAST_LEN = 16384
POOL_WINDOWS = (2, 4, 8, 16)
RET_CHUNK = 128
ROPE_BASE = 10000.0
TOP_K = 8
N_GROUPS = 8
TOPK_GROUPS = 4
ROUTED_SCALE = 2.5
N_MOD = 6
EPS = 1e-6

V7X_VMEM_BYTES = 64 * 1024 * 1024
SUBLANES = 8
LANES = 128

PROMPT_STEP_ROWS = 512
SAMPLE_SEQS_PER_STEP = 8
SAMPLE_STEPS_PER_GROUP = 4
TOKEN_BLOCK = 256
DMA_ROWS = 16
SLOT_ROWS = 3 * DMA_ROWS
ONEHOT_ROWS = 256
SLOT_CHUNK_EXPERTS = 16
EXPERT_TILE = 512
EXPERT_TILE_PARTS = 2
RING_SLOTS = 3
POOL_PREV = 16


def _vmem_limit(nbytes):
    return int(min(nbytes, V7X_VMEM_BYTES - 8 * 1024 * 1024))


def _dot(a, b):
    return jnp.dot(a, b, preferred_element_type=F32)


def _dot_nt(a, b):
    return lax.dot_general(a, b, (((1,), (1,)), ((), ())), preferred_element_type=F32)


def _dot_tn(a, b):
    return lax.dot_general(a, b, (((0,), (0,)), ((), ())), preferred_element_type=F32)


def _split_bf16(x, parts):
    out = []
    for _ in range(parts):
        p = x.astype(BF16)
        out.append(p)
        x = x - p.astype(F32)
    return out


def _rms_mod(x, g, shift, scale):
    y = x * lax.rsqrt(jnp.mean(x * x, axis=-1, keepdims=True) + EPS)
    return (y * g) * (1.0 + scale) + shift


def _rotary(x, cosb, sinb):
    return x * cosb + pltpu.roll(x, x.shape[-1] // 2, axis=1) * sinb


def _group_norm(o):
    return o * lax.rsqrt(jnp.mean(o * o, axis=-1, keepdims=True) + EPS)


def _ada_kernel(cp_ref, ct_ref, w_ref, b_ref, modp_ref, modt_ref):
    w = w_ref[...].astype(BF16)
    modp_ref[...] = _dot(cp_ref[...].astype(BF16), w) + b_ref[...]
    modt_ref[0] = _dot(ct_ref[...].astype(BF16), w) + b_ref[...]


def _ada(c_prompt, c_tokens, w_ada, b_ada):
    bp, d = c_prompt.shape
    ts = c_tokens.shape[0]
    n = w_ada.shape[1]
    return pl.pallas_call(
        _ada_kernel,
        out_shape=(jax.ShapeDtypeStruct((bp, n), F32), jax.ShapeDtypeStruct((n // d, ts, d), F32)),
        grid=(n // d,),
        in_specs=[
            pl.BlockSpec((bp, d), lambda j: (0, 0)),
            pl.BlockSpec((ts, d), lambda j: (0, 0)),
            pl.BlockSpec((d, d), lambda j: (0, j)),
            pl.BlockSpec((1, d), lambda j: (0, j)),
        ],
        out_specs=(pl.BlockSpec((bp, d), lambda j: (0, j)),
                   pl.BlockSpec((1, ts, d), lambda j: (j, 0, 0))),
        compiler_params=pltpu.CompilerParams(
            dimension_semantics=("arbitrary",),
            vmem_limit_bytes=_vmem_limit(32 * 1024 * 1024),
        ),
        name="ada",
    )(c_prompt, c_tokens, w_ada, b_ada.reshape(1, n))


class _Dims:
    def __init__(self, d_model, d_pool, heads, dk, dv):
        self.d = d_model
        self.d_pool = d_pool
        self.gd = d_pool // len(POOL_WINDOWS)
        self.heads = heads
        self.dk = dk
        self.dv = dv
        self.d_k = heads * dk
        self.d_v = heads * dv
        widths = (d_pool, self.d_k, self.d_k, self.d_v, self.d_v, d_model, d_model)
        offs = [0]
        for w in widths:
            offs.append(offs[-1] + w)
        self.cols = tuple((offs[i], offs[i + 1]) for i in range(len(widths)))
        self.d_in = offs[-1]


def _proj(hb, win_ref, cols):
    return _dot(hb, win_ref[:, cols[0]:cols[1]])


def _pool_branch(pm_parts, wpg_ref, ps_ref, wbp_ref):
    pg = [_dot(pm.astype(BF16), wpg_ref[gi]) for gi, pm in enumerate(pm_parts)]
    pm = jnp.concatenate(pg, axis=1) * ps_ref[...]
    return _dot(pm.astype(BF16), wbp_ref[...])


def _merge_out(x, gate, a, o, g, ga, gb, wbr_ref, wout_ref):
    r = _dot((jax.nn.silu(g) * o).astype(BF16), wbr_ref[...])
    merged = jax.nn.sigmoid(ga) * a + jax.nn.sigmoid(gb) * r
    m = _dot(merged.astype(BF16), wout_ref[...])
    return x + gate * m


def _mixer_prompt_kernel(dims, x_ref, mod_ref, g1_ref, win_ref, wpg_ref, ps_ref, wbp_ref,
                         wbr_ref, wout_ref, cos_ref, sin_ref, intra_ref, qdec_ref,
                         kdec_ref, sdec_ref, x1_ref, pool_ref, ret_ref, uext_ref):
    c = pl.program_id(1)
    L = x_ref.shape[1]

    @pl.when(c == 0)
    def _init():
        ret_ref[...] = jnp.zeros_like(ret_ref)
        uext_ref[0:POOL_PREV, :] = jnp.zeros((POOL_PREV, dims.d_pool), F32)

    x = x_ref[0]
    shift, scale, gate = mod_ref[0, 0:1, :], mod_ref[0, 1:2, :], mod_ref[0, 2:3, :]
    hb = _rms_mod(x, g1_ref[...], shift, scale).astype(BF16)

    u = _proj(hb, win_ref, dims.cols[0])
    uext_ref[POOL_PREV:POOL_PREV + L, :] = u
    pos = c * L + lax.broadcasted_iota(jnp.int32, (L, 1), 0)
    pm_parts = []
    for gi, w in enumerate(POOL_WINDOWS):
        s = uext_ref[:, gi * dims.gd:(gi + 1) * dims.gd]
        sh = 1
        while sh < w:
            s = s + pltpu.roll(s, sh, axis=0)
            sh *= 2
        cnt = jnp.minimum(w, pos + 1).astype(F32)
        pm_parts.append(s[POOL_PREV:, :] / cnt - u[:, gi * dims.gd:(gi + 1) * dims.gd])
    a = _pool_branch(pm_parts, wpg_ref, ps_ref, wbp_ref)
    uext_ref[0:POOL_PREV, :] = uext_ref[L:L + POOL_PREV, :]
    pool_ref[0] = uext_ref[0:POOL_PREV, :]

    q = _proj(hb, win_ref, dims.cols[1])
    k = _proj(hb, win_ref, dims.cols[2])
    v = _proj(hb, win_ref, dims.cols[3])
    chunk_outs = []
    for r0 in range(0, L, RET_CHUNK):
        rows = slice(r0, r0 + RET_CHUNK)
        cosb, sinb = cos_ref[rows, :], sin_ref[rows, :]
        outs = []
        for h in range(dims.heads):
            qr = _rotary(q[rows, h * dims.dk:(h + 1) * dims.dk], cosb, sinb)
            kr = _rotary(k[rows, h * dims.dk:(h + 1) * dims.dk], cosb, sinb) * (dims.dk ** -0.5)
            vb = v[rows, h * dims.dv:(h + 1) * dims.dv].astype(BF16)
            scores = _dot_nt(qr.astype(BF16), kr.astype(BF16)) * intra_ref[h]
            s_prev = ret_ref[0, 0, h]
            o = _dot(scores.astype(BF16), vb) + _dot((qr * qdec_ref[h]).astype(BF16),
                                                      s_prev.astype(BF16))
            ret_ref[0, 0, h] = (sdec_ref[h] * s_prev
                                + _dot_tn((kr * kdec_ref[h]).astype(BF16), vb))
            outs.append(_group_norm(o))
        chunk_outs.append(jnp.concatenate(outs, axis=1))
    o = jnp.concatenate(chunk_outs, axis=0)

    g = _proj(hb, win_ref, dims.cols[4])
    ga = _proj(hb, win_ref, dims.cols[5])
    gb = _proj(hb, win_ref, dims.cols[6])
    x1_ref[0] = _merge_out(x, gate, a, o, g, ga, gb, wbr_ref, wout_ref)


def _const_spec(shape):
    zeros = (0,) * len(shape)
    return pl.BlockSpec(shape, lambda *_: zeros)


def _resident_spec(shape):
    zeros = (0,) * len(shape)
    return pl.BlockSpec(shape, lambda *_: zeros, pipeline_mode=pl.Buffered(1))


def _mixer_prompt(dims, x, mod, g1, weights, tables):
    b, s, d = x.shape
    L = PROMPT_STEP_ROWS
    nc = s // L
    win, wpg, ps, wbp, wbr, wout = weights
    cosb, sinb, intra, qdec, kdec, sdec = tables
    in_specs = [
        pl.BlockSpec((1, L, d), lambda i, c: (i, c, 0)),
        pl.BlockSpec((1, N_MOD, d), lambda i, c: (i, 0, 0)),
        _const_spec(g1.shape), _resident_spec(win.shape), _const_spec(wpg.shape),
        _const_spec(ps.shape), _resident_spec(wbp.shape), _resident_spec(wbr.shape),
        _resident_spec(wout.shape),
        pl.BlockSpec((L, dims.dk), lambda i, c: (c, 0)),
        pl.BlockSpec((L, dims.dk), lambda i, c: (c, 0)),
        _const_spec(intra.shape), _const_spec(qdec.shape), _const_spec(kdec.shape),
        _const_spec(sdec.shape),
    ]
    out_shape = (
        jax.ShapeDtypeStruct((b, s, d), F32),
        jax.ShapeDtypeStruct((b, POOL_PREV, dims.d_pool), F32),
        jax.ShapeDtypeStruct((1, b, dims.heads, dims.dk, dims.dv), F32),
    )
    out_specs = (
        pl.BlockSpec((1, L, d), lambda i, c: (i, c, 0)),
        pl.BlockSpec((1, POOL_PREV, dims.d_pool), lambda i, c: (i, 0, 0)),
        pl.BlockSpec((1, 1, dims.heads, dims.dk, dims.dv), lambda i, c: (0, i, 0, 0, 0)),
    )
    weight_bytes = 2 * (win.size + wpg.size + wbp.size + wbr.size + wout.size)
    return pl.pallas_call(
        functools.partial(_mixer_prompt_kernel, dims),
        out_shape=out_shape,
        grid=(b, nc),
        in_specs=in_specs,
        out_specs=out_specs,
        scratch_shapes=[pltpu.VMEM((POOL_PREV + L, dims.d_pool), F32)],
        compiler_params=pltpu.CompilerParams(
            dimension_semantics=("arbitrary", "arbitrary"),
            vmem_limit_bytes=_vmem_limit(2 * weight_bytes + 16 * 1024 * 1024),
        ),
        name="mixer_prompt",
    )(x, mod, g1, win, wpg, ps, wbp, wbr, wout, cosb, sinb, intra, qdec, kdec, sdec)


def _mixer_sample_kernel(dims, ds, x_ref, modt_ref, g1_ref, win_ref, wpg_ref, ps_ref, wbp_ref,
                         wbr_ref, wout_ref, cos_ref, sin_ref, intra_ref, qdec_ref, kdec_ref,
                         sdec_ref, bandp_ref, bandu_ref, cnt_ref, poolin_ref, sret_ref,
                         x1_ref, u_ref, ret_ref, hb_ref, q_ref, k_ref, v_ref, pm_ref, o_ref):
    j = pl.program_id(1)
    R = cos_ref.shape[0]
    ns = R // ds

    @pl.when(j == 0)
    def _project():
        hb = _rms_mod(x_ref[...], g1_ref[...], modt_ref[0], modt_ref[1]).astype(BF16)
        hb_ref[...] = hb
        u_ref[...] = _proj(hb, win_ref, dims.cols[0])
        q_ref[...] = _proj(hb, win_ref, dims.cols[1])
        k_ref[...] = _proj(hb, win_ref, dims.cols[2])
        v_ref[...] = _proj(hb, win_ref, dims.cols[3]).astype(BF16)

    rows = pl.ds(pl.multiple_of(j * R, R), R)

    u = u_ref[rows, :]
    hist = poolin_ref[...].reshape(ns * POOL_PREV, dims.d_pool)
    pm_parts = []
    for gi in range(len(POOL_WINDOWS)):
        lo, hi = gi * dims.gd, (gi + 1) * dims.gd
        win_sum = jnp.zeros((R, dims.gd), F32)
        for part in _split_bf16(hist[:, lo:hi], 3):
            win_sum = win_sum + _dot(bandp_ref[gi], part)
        for part in _split_bf16(u[:, lo:hi], 3):
            win_sum = win_sum + _dot(bandu_ref[gi], part)
        pm_parts.append(win_sum / cnt_ref[gi] - u[:, lo:hi])
    pm_ref[rows, :] = jnp.concatenate(pm_parts, axis=1)

    q = q_ref[rows, :]
    k = k_ref[rows, :]
    v = v_ref[rows, :]
    cosb, sinb = cos_ref[...], sin_ref[...]
    row_seq = lax.broadcasted_iota(jnp.int32, (R, ns * dims.dk), 0) // ds
    col_seq = lax.broadcasted_iota(jnp.int32, (R, ns * dims.dk), 1) // dims.dk
    own = row_seq == col_seq
    outs = []
    for h in range(dims.heads):
        qr = _rotary(q[:, h * dims.dk:(h + 1) * dims.dk], cosb, sinb)
        kr = _rotary(k[:, h * dims.dk:(h + 1) * dims.dk], cosb, sinb) * (dims.dk ** -0.5)
        vb = v[:, h * dims.dv:(h + 1) * dims.dv]
        scores = _dot_nt(qr.astype(BF16), kr.astype(BF16)) * intra_ref[h]
        s_prev = sret_ref[:, h].reshape(ns * dims.dk, dims.dv)
        qd = jnp.concatenate([qr * qdec_ref[h]] * ns, axis=1)
        kd = jnp.concatenate([kr * kdec_ref[h]] * ns, axis=1)
        qexp = jnp.where(own, qd, 0.0).astype(BF16)
        kexp = jnp.where(own, kd, 0.0).astype(BF16)
        o = _dot(scores.astype(BF16), vb) + _dot(qexp, s_prev.astype(BF16))
        s_new = sdec_ref[h] * s_prev + _dot_tn(kexp, vb)
        ret_ref[:, h] = s_new.reshape(ns, dims.dk, dims.dv)
        outs.append(_group_norm(o))
    o_ref[rows, :] = jnp.concatenate(outs, axis=1)

    @pl.when(j == pl.num_programs(1) - 1)
    def _merge():
        hb = hb_ref[...]
        pm = pm_ref[...]
        pm_parts = [pm[:, gi * dims.gd:(gi + 1) * dims.gd] for gi in range(len(POOL_WINDOWS))]
        a = _pool_branch(pm_parts, wpg_ref, ps_ref, wbp_ref)
        g = _proj(hb, win_ref, dims.cols[4])
        ga = _proj(hb, win_ref, dims.cols[5])
        gb = _proj(hb, win_ref, dims.cols[6])
        x1_ref[...] = _merge_out(x_ref[...], modt_ref[2], a, o_ref[...], g, ga, gb,
                                 wbr_ref, wout_ref)


def _mixer_sample(dims, ds, x2d, modt, g1, weights, tables, pool_in, state_ret):
    t, d = x2d.shape
    ns = SAMPLE_SEQS_PER_STEP
    steps = SAMPLE_STEPS_PER_GROUP
    R = ns * ds
    RG = R * steps
    win, wpg, ps, wbp, wbr, wout = weights
    cosb, sinb, intra, qdec, kdec, sdec, bandp, bandu, cnt = tables
    db = state_ret.shape[0]
    in_specs = [
        pl.BlockSpec((RG, d), lambda i, j: (i, 0)),
        pl.BlockSpec((N_MOD, RG, d), lambda i, j: (0, i, 0)),
        _const_spec(g1.shape), _resident_spec(win.shape), _const_spec(wpg.shape),
        _const_spec(ps.shape), _resident_spec(wbp.shape), _resident_spec(wbr.shape),
        _resident_spec(wout.shape), _const_spec(cosb.shape), _const_spec(sinb.shape),
        _const_spec(intra.shape), _const_spec(qdec.shape), _const_spec(kdec.shape),
        _const_spec(sdec.shape), _const_spec(bandp.shape), _const_spec(bandu.shape),
        _const_spec(cnt.shape),
        pl.BlockSpec((ns, POOL_PREV, dims.d_pool), lambda i, j: (i * steps + j, 0, 0)),
        pl.BlockSpec((ns, dims.heads, dims.dk, dims.dv), lambda i, j: (i * steps + j, 0, 0, 0)),
    ]
    out_shape = (
        jax.ShapeDtypeStruct((t, d), F32),
        jax.ShapeDtypeStruct((t, dims.d_pool), F32),
        jax.ShapeDtypeStruct(state_ret.shape, state_ret.dtype),
    )
    out_specs = (
        pl.BlockSpec((RG, d), lambda i, j: (i, 0)),
        pl.BlockSpec((RG, dims.d_pool), lambda i, j: (i, 0)),
        pl.BlockSpec((ns, dims.heads, dims.dk, dims.dv), lambda i, j: (i * steps + j, 0, 0, 0)),
    )
    weight_bytes = 2 * (win.size + wpg.size + wbp.size + wbr.size + wout.size)
    state_bytes = 4 * ns * dims.heads * dims.dk * dims.dv
    return pl.pallas_call(
        functools.partial(_mixer_sample_kernel, dims, ds),
        out_shape=out_shape,
        grid=(db // (ns * steps), steps),
        in_specs=in_specs,
        out_specs=out_specs,
        scratch_shapes=[
            pltpu.VMEM((RG, d), BF16), pltpu.VMEM((RG, dims.d_k), F32),
            pltpu.VMEM((RG, dims.d_k), F32), pltpu.VMEM((RG, dims.d_v), BF16),
            pltpu.VMEM((RG, dims.d_pool), F32), pltpu.VMEM((RG, dims.d_v), F32),
        ],
        compiler_params=pltpu.CompilerParams(
            dimension_semantics=("arbitrary", "arbitrary"),
            vmem_limit_bytes=_vmem_limit(2 * weight_bytes + 4 * state_bytes + 8 * 1024 * 1024),
        ),
        name="mixer_sample",
    )(x2d, modt, g1, win, wpg, ps, wbp, wbr, wout, cosb, sinb, intra, qdec, kdec, sdec,
      bandp, bandu, cnt, pool_in, state_ret)


def _rotary_tables(pos, dk):
    half = dk // 2
    inv = np.power(ROPE_BASE, -np.arange(half, dtype=np.float64) / half)
    ang = pos.astype(np.float64)[:, None] * inv[None, :]
    cos, sin = np.cos(ang), np.sin(ang)
    return (np.concatenate([cos, cos], axis=1).astype(np.float32),
            np.concatenate([-sin, sin], axis=1).astype(np.float32))


def _decay_tables(log_g, chunk, reps, dk, dv):
    idx = np.arange(chunk, dtype=np.float64)
    diff = idx[:, None] - idx[None, :]
    intra = np.where((diff >= 0)[None],
                     np.exp(log_g[:, None, None] * np.maximum(diff, 0.0)[None]), 0.0)
    heads = log_g.shape[0]
    eye = np.eye(reps)
    intra = np.einsum("ab,hij->haibj", eye, intra).reshape(heads, reps * chunk, reps * chunk)
    q_dec = np.exp(log_g[:, None] * (idx[None, :] + 1.0))
    k_dec = np.exp(log_g[:, None] * (chunk - 1.0 - idx[None, :]))
    s_dec = np.exp(log_g * chunk)
    q_dec = np.broadcast_to(np.tile(q_dec, (1, reps))[:, :, None], (heads, reps * chunk, dk))
    k_dec = np.broadcast_to(np.tile(k_dec, (1, reps))[:, :, None], (heads, reps * chunk, dk))
    s_dec = np.broadcast_to(s_dec[:, None, None], (heads, 1, dv))
    return tuple(np.ascontiguousarray(t, dtype=np.float32) for t in (intra, q_dec, k_dec, s_dec))


def _sample_pool_tables(ns, ds, gd, start_pos):
    n = np.arange(ds)
    i = np.arange(POOL_PREV)
    eye = np.eye(ns, dtype=np.float32)
    bandp, bandu, cnt = [], [], []
    for w in POOL_WINDOWS:
        bp = ((i[None, :] - POOL_PREV) >= (n[:, None] - w + 1)).astype(np.float32)
        bu = ((n[None, :] <= n[:, None]) & (n[None, :] >= n[:, None] - w + 1)).astype(np.float32)
        bandp.append(np.einsum("ab,ni->anbi", eye, bp).reshape(ns * ds, ns * POOL_PREV))
        bandu.append(np.einsum("ab,nm->anbm", eye, bu).reshape(ns * ds, ns * ds))
        c = np.minimum(w, start_pos + n + 1).astype(np.float32)
        cnt.append(np.broadcast_to(np.tile(c, ns)[:, None], (ns * ds, gd)))
    return (np.stack(bandp).astype(BF16), np.stack(bandu).astype(BF16),
            np.ascontiguousarray(np.stack(cnt)))


def _first_max_onehot(v, idx, big, axes):
    m = jnp.max(v, axis=axes, keepdims=True)
    first = jnp.min(jnp.where(v == m, idx, big), axis=axes, keepdims=True)
    return idx == first


def _route_kernel(n_prompt_blocks, xp_ref, xs_ref, modp_ref, modt_ref, g2_ref, wr_ref, br_ref,
                  before_ref, below_ref, h2_ref, pos_ref, wts_ref, cnt_ref, rank_ref, gates_ref,
                  h_ref):
    i = pl.program_id(0)
    is_p = i < n_prompt_blocks

    @pl.when(is_p)
    def _prompt():
        h_ref[...] = _rms_mod(xp_ref[...], g2_ref[...], modp_ref[0, 3:4, :], modp_ref[0, 4:5, :])

    @pl.when(jnp.logical_not(is_p))
    def _sample():
        h_ref[...] = _rms_mod(xs_ref[...], g2_ref[...], modt_ref[3], modt_ref[4])

    h2 = h_ref[...]
    hh = h2.astype(BF16)
    h2_ref[...] = hh

    hl = (h2 - hh.astype(F32)).astype(BF16)
    wh, wl = _split_bf16(wr_ref[...], 2)
    logits = _dot_nt(wh, hh) + (_dot_nt(wh, hl) + _dot_nt(wl, hh))
    s = jax.nn.sigmoid(logits)
    e, t = s.shape
    ge = e // N_GROUPS
    neg = -jnp.inf
    s3 = s.reshape(N_GROUPS, ge, t)
    sb3 = s3 + br_ref[...]

    j3 = lax.broadcasted_iota(jnp.int32, (N_GROUPS, ge, t), 1)
    top1 = _first_max_onehot(sb3, j3, ge, 1)
    m1 = jnp.max(sb3, axis=1, keepdims=True)
    m2 = jnp.max(jnp.where(top1, neg, sb3), axis=1, keepdims=True)
    gv = m1 + m2

    g3 = lax.broadcasted_iota(jnp.int32, (N_GROUPS, 1, t), 0)
    gsel = jnp.zeros((N_GROUPS, 1, t), F32)
    for _ in range(TOPK_GROUPS):
        hit = _first_max_onehot(gv, g3, N_GROUPS, 0)
        gsel = jnp.where(hit, 1.0, gsel)
        gv = jnp.where(hit, neg, gv)

    e3 = lax.broadcasted_iota(jnp.int32, (N_GROUPS, ge, t), 0) * ge + j3
    ev = jnp.where(gsel > 0.5, sb3, neg)
    sel = jnp.zeros((N_GROUPS, ge, t), F32)
    chosen = jnp.zeros((N_GROUPS, ge, t), F32)
    hits = []
    for _ in range(TOP_K):
        hit = _first_max_onehot(ev, e3, e, (0, 1))
        hits.append(hit)
        sel = jnp.where(hit, s3, sel)
        chosen = jnp.where(hit, 1.0, chosen)
        ev = jnp.where(hit, neg, ev)
    tot = jnp.sum(sel, axis=(0, 1), keepdims=True)
    gates = sel / tot * ROUTED_SCALE

    chosen2 = chosen.reshape(e, t)
    rank = _dot(chosen2.astype(BF16), before_ref[...])
    cnt = jnp.sum(chosen2, axis=1, keepdims=True)
    ovf_units = jnp.floor((jnp.maximum(cnt - SLOT_ROWS, 0.0) + (DMA_ROWS - 1)) / DMA_ROWS)
    ovf_off = _dot(below_ref[...], jnp.broadcast_to(ovf_units, (e, LANES)).astype(BF16))[:, 0:1]
    e_col = lax.broadcasted_iota(jnp.int32, (e, 1), 0).astype(F32)
    slot = jnp.where(rank < SLOT_ROWS, e_col * SLOT_ROWS + rank,
                     e * SLOT_ROWS + ovf_off * DMA_ROWS + (rank - SLOT_ROWS))
    slot3 = slot.reshape(N_GROUPS, ge, t)
    for k, hit in enumerate(hits):
        pos_k = jnp.sum(jnp.where(hit, slot3, 0.0), axis=(0, 1), keepdims=True)
        w_k = jnp.sum(jnp.where(hit, gates, 0.0), axis=(0, 1), keepdims=True)
        pos_ref[k:k + 1, :] = pos_k.reshape(1, t).astype(jnp.int32)
        wts_ref[k:k + 1, :] = w_k.reshape(1, t)
    cnt_ref[0] = jnp.broadcast_to(cnt, (e, LANES)).astype(jnp.int32)
    rank_ref[...] = jnp.where(chosen2 > 0.5, rank, -1.0).astype(jnp.int32)
    gates_ref[...] = gates.reshape(e, t)


def _route(xp2d, xs2d, modp, modt, g2, wr_t, br):
    tp, d = xp2d.shape
    ts = xs2d.shape[0]
    tb = TOKEN_BLOCK
    npb, nsb = tp // tb, ts // tb
    e = wr_t.shape[0]
    seq = tp // modp.shape[0]
    bpb = seq // tb

    def p_idx(i):
        return jnp.minimum(i, npb - 1)

    def s_idx(i):
        return jnp.maximum(i - npb, 0)

    t = tp + ts
    nb = npb + nsb
    tok = np.arange(tb)
    before = (tok[:, None] < tok[None, :]).astype(BF16)
    ex = np.arange(e)
    below = (ex[None, :] < ex[:, None]).astype(BF16)
    return pl.pallas_call(
        functools.partial(_route_kernel, npb),
        out_shape=(jax.ShapeDtypeStruct((t, d), BF16),
                   jax.ShapeDtypeStruct((TOP_K, t), jnp.int32),
                   jax.ShapeDtypeStruct((TOP_K, t), F32),
                   jax.ShapeDtypeStruct((nb, e, LANES), jnp.int32),
                   jax.ShapeDtypeStruct((e, t), jnp.int32),
                   jax.ShapeDtypeStruct((e, t), F32)),
        grid=(nb,),
        in_specs=[
            pl.BlockSpec((tb, d), lambda i: (p_idx(i), 0)),
            pl.BlockSpec((tb, d), lambda i: (s_idx(i), 0)),
            pl.BlockSpec((1, N_MOD, d), lambda i: (p_idx(i) // bpb, 0, 0)),
            pl.BlockSpec((N_MOD, tb, d), lambda i: (0, s_idx(i), 0)),
            _const_spec(g2.shape), _const_spec(wr_t.shape), _const_spec(br.shape),
            _const_spec(before.shape), _const_spec(below.shape),
        ],
        out_specs=(pl.BlockSpec((tb, d), lambda i: (i, 0)),
                   pl.BlockSpec((TOP_K, tb), lambda i: (0, i)),
                   pl.BlockSpec((TOP_K, tb), lambda i: (0, i)),
                   pl.BlockSpec((1, e, LANES), lambda i: (i, 0, 0)),
                   pl.BlockSpec((e, tb), lambda i: (0, i)),
                   pl.BlockSpec((e, tb), lambda i: (0, i))),
        scratch_shapes=[pltpu.VMEM((tb, d), F32)],
        compiler_params=pltpu.CompilerParams(
            dimension_semantics=("arbitrary",),
            vmem_limit_bytes=_vmem_limit(32 * 1024 * 1024),
        ),
        name="route",
    )(xp2d, xs2d, modp, modt, g2, wr_t, br, before, below)


class _Layout:
    def __init__(self, counts):
        nb, ne = counts.shape
        u, g, tm = SLOT_ROWS, DMA_ROWS, EXPERT_TILE
        pc = (counts + g - 1) // g * g
        ovf_units = (jnp.maximum(counts - u, 0) + g - 1) // g
        rows_e = jnp.sum(pc, axis=0)
        region = (rows_e + u + tm - 1) // tm * tm
        goff = jnp.cumsum(region) - region
        dst = goff[None, :] + jnp.cumsum(pc, axis=0) - pc
        self.dst = dst.reshape(-1)
        ovf_end = jnp.cumsum(ovf_units, axis=1)
        ovf_first = ovf_end - ovf_units
        unit = jnp.arange(_overflow_units_cap(ne))[None, :, None]
        owns = (ovf_first[:, None, :] <= unit) & (unit < ovf_end[:, None, :])
        row = dst[:, None, :] + u + (unit - ovf_first[:, None, :]) * g
        self.ovf_dst = jnp.sum(jnp.where(owns, row, 0), axis=2).reshape(-1)
        self.ovf_tot = ovf_end[:, -1]
        self.tail_start = goff + rows_e
        self.tail_units = (region - rows_e) // g
        tiles = region // tm
        self.region_tiles = jnp.concatenate([tiles, jnp.sum(tiles, keepdims=True)])


def _overflow_units_cap(ne):
    return TOKEN_BLOCK * TOP_K // DMA_ROWS + ne


def _block_buffer_rows(tb, ne):
    rows = ne * SLOT_ROWS + tb * TOP_K + ne * (DMA_ROWS - 1)
    return (rows + ONEHOT_ROWS - 1) // ONEHOT_ROWS * ONEHOT_ROWS


def _expert_rows_cap(t, tb, ne):
    nb = t // tb
    rows = t * TOP_K + nb * ne * (DMA_ROWS - 1) + ne * (SLOT_ROWS + EXPERT_TILE - 1)
    return (rows + EXPERT_TILE - 1) // EXPERT_TILE * EXPERT_TILE


def _slot_copy(buf_ref, slot, e, hbm_ref, row, sem, to_hbm):
    src = buf_ref.at[slot, pl.ds(pl.multiple_of(e * SLOT_ROWS, DMA_ROWS), SLOT_ROWS)]
    dst = hbm_ref.at[pl.ds(pl.multiple_of(row, DMA_ROWS), SLOT_ROWS)]
    return pltpu.make_async_copy(src, dst, sem) if to_hbm else pltpu.make_async_copy(dst, src, sem)


def _unit_copy(buf_ref, slot, buf_row, hbm_ref, row, sem, to_hbm):
    src = buf_ref.at[slot, pl.ds(pl.multiple_of(buf_row, DMA_ROWS), DMA_ROWS)]
    dst = hbm_ref.at[pl.ds(pl.multiple_of(row, DMA_ROWS), DMA_ROWS)]
    return pltpu.make_async_copy(src, dst, sem) if to_hbm else pltpu.make_async_copy(dst, src, sem)


def _slots_wait(ne, buf_ref, slot, hbm_ref, sem, to_hbm):
    src = buf_ref.at[slot, pl.ds(0, ne * SLOT_ROWS)]
    dst = hbm_ref.at[pl.ds(0, ne * SLOT_ROWS)]
    c = pltpu.make_async_copy(src, dst, sem) if to_hbm else pltpu.make_async_copy(dst, src, sem)
    c.wait()


def _overflow_copies(ne, blk, slot, odst_ref, otot_ref, buf_ref, hbm_ref, sem, to_hbm, wait):
    def per_unit(i, carry):
        row = odst_ref[blk * _overflow_units_cap(ne) + i]
        c = _unit_copy(buf_ref, slot, ne * SLOT_ROWS + i * DMA_ROWS, hbm_ref, row, sem, to_hbm)
        c.wait() if wait else c.start()
        return carry

    lax.fori_loop(0, otot_ref[blk], per_unit, 0)


def _slot_chunks(ne):
    return [range(c, c + SLOT_CHUNK_EXPERTS) for c in range(0, ne, SLOT_CHUNK_EXPERTS)]


def _overflow_chunks(n_units):
    return lax.shift_right_logical(n_units * DMA_ROWS + (ONEHOT_ROWS - 1),
                                   ONEHOT_ROWS.bit_length() - 1)


def _overflow_select(pos, vals, r0):
    rows = lax.broadcasted_iota(jnp.int32, (ONEHOT_ROWS, pos.shape[1]), 0) + r0
    sel = jnp.zeros(rows.shape, F32)
    for k in range(TOP_K):
        v = 1.0 if vals is None else vals[k:k + 1, :]
        sel = jnp.where(pos[k:k + 1, :] == rows, v, sel)
    return sel


def _dispatch_kernel(ne, dst_ref, odst_ref, otot_ref, tails_ref, tailu_ref,
                     h2_ref, rank_ref, pos_ref, xs_hbm, buf_ref, zero_ref, sem, zsem):
    b = pl.program_id(0)
    nb = pl.num_programs(0)
    slot = b % 2
    h2 = h2_ref[...]
    j = lax.broadcasted_iota(jnp.int32, (SLOT_ROWS, h2.shape[0]), 0)

    chunks = _slot_chunks(ne)
    first_issue = max(len(chunks) - 2, 0)
    for ci, experts in enumerate(chunks):
        onehot = jnp.concatenate(
            [jnp.where(rank_ref[e:e + 1, :] == j, 1.0, 0.0) for e in experts], axis=0)
        r0 = experts[0] * SLOT_ROWS
        buf_ref[slot, r0:r0 + len(experts) * SLOT_ROWS, :] = (
            _dot(onehot.astype(BF16), h2).astype(BF16))
        if ci < first_issue:
            continue
        if ci == first_issue:
            @pl.when(b > 0)
            def _prev():
                _slots_wait(ne, buf_ref, 1 - slot, xs_hbm, sem.at[1 - slot], True)
                _overflow_copies(ne, b - 1, 1 - slot, odst_ref, otot_ref, buf_ref, xs_hbm,
                                 sem.at[1 - slot], True, True)
        ready = [e for c in chunks[:ci + 1] for e in c] if ci == first_issue else experts
        for e in ready:
            _slot_copy(buf_ref, slot, e, xs_hbm, dst_ref[b * ne + e], sem.at[slot], True).start()

    pos = pos_ref[...]

    def chunk(ci, carry):
        r0 = pl.multiple_of(ne * SLOT_ROWS + ci * ONEHOT_ROWS, ONEHOT_ROWS)
        onehot = _overflow_select(pos, None, r0).astype(BF16)
        buf_ref[slot, pl.ds(r0, ONEHOT_ROWS), :] = _dot(onehot, h2).astype(BF16)
        return carry

    lax.fori_loop(0, _overflow_chunks(otot_ref[b]), chunk, 0)
    _overflow_copies(ne, b, slot, odst_ref, otot_ref, buf_ref, xs_hbm, sem.at[slot], True, False)

    @pl.when(b == nb - 1)
    def _last():
        _slots_wait(ne, buf_ref, slot, xs_hbm, sem.at[slot], True)
        _overflow_copies(ne, b, slot, odst_ref, otot_ref, buf_ref, xs_hbm, sem.at[slot], True,
                         True)
        zero_ref[...] = jnp.zeros_like(zero_ref)

        def tail(wait):
            def per_expert(e, carry):
                def per_unit(i, carry2):
                    row = pl.multiple_of(tails_ref[e] + i * DMA_ROWS, DMA_ROWS)
                    c = pltpu.make_async_copy(zero_ref, xs_hbm.at[pl.ds(row, DMA_ROWS)], zsem)
                    c.wait() if wait else c.start()
                    return carry2
                lax.fori_loop(0, tailu_ref[e], per_unit, 0)
                return carry
            lax.fori_loop(0, ne, per_expert, 0)

        tail(False)
        tail(True)


def _dispatch(h2, rank, pos, lay, ne, rows_cap):
    t, d = h2.shape
    tb = TOKEN_BLOCK
    buf_rows = _block_buffer_rows(tb, ne)
    return pl.pallas_call(
        functools.partial(_dispatch_kernel, ne),
        out_shape=jax.ShapeDtypeStruct((rows_cap, d), BF16),
        grid_spec=pltpu.PrefetchScalarGridSpec(
            num_scalar_prefetch=5,
            grid=(t // tb,),
            in_specs=[
                pl.BlockSpec((tb, d), lambda i, *_: (i, 0)),
                pl.BlockSpec((ne, tb), lambda i, *_: (0, i)),
                pl.BlockSpec((TOP_K, tb), lambda i, *_: (0, i)),
            ],
            out_specs=pl.BlockSpec(memory_space=pl.ANY),
            scratch_shapes=[
                pltpu.VMEM((2, buf_rows, d), BF16),
                pltpu.VMEM((DMA_ROWS, d), BF16),
                pltpu.SemaphoreType.DMA((2,)),
                pltpu.SemaphoreType.DMA(()),
            ],
        ),
        compiler_params=pltpu.CompilerParams(
            dimension_semantics=("arbitrary",),
            vmem_limit_bytes=_vmem_limit(4 * buf_rows * d + 16 * 1024 * 1024),
        ),
        name="dispatch",
    )(lay.dst, lay.ovf_dst, lay.ovf_tot, lay.tail_start, lay.tail_units, h2, rank, pos)


def _experts_kernel(tiles_ref, xe_hbm, wg_ref, wu_ref, wd_ref, ye_hbm,
                    xbuf_ref, ybuf_ref, wgb_ref, wub_ref, wdb_ref, done_ref, xsem, ysem):
    e = pl.program_id(0)
    ne = pl.num_programs(0)
    tm = xbuf_ref.shape[1]

    def x_copy(row, slot):
        return pltpu.make_async_copy(xe_hbm.at[pl.ds(pl.multiple_of(row, tm), tm)],
                                     xbuf_ref.at[slot], xsem.at[slot])

    def y_copy(row, slot):
        return pltpu.make_async_copy(ybuf_ref.at[slot],
                                     ye_hbm.at[pl.ds(pl.multiple_of(row, tm), tm)], ysem.at[slot])

    total = tiles_ref[ne]

    @pl.when(e == 0)
    def _first():
        done_ref[0] = 0
        for g in range(RING_SLOTS - 1):
            x_copy(g * tm, g).start()

    wgb_ref[...] = wg_ref[0].astype(BF16)
    wub_ref[...] = wu_ref[0].astype(BF16)
    wdb_ref[...] = wd_ref[0].astype(BF16)
    done = done_ref[0]
    n = tiles_ref[e]

    def tile(j, carry):
        g = done + j
        slot = lax.rem(g, RING_SLOTS)
        x_copy(0, slot).wait()
        ahead = g + (RING_SLOTS - 1)

        @pl.when(ahead < total)
        def _prefetch():
            x_copy(ahead * tm, lax.rem(ahead, RING_SLOTS)).start()

        @pl.when(g >= RING_SLOTS)
        def _reuse():
            y_copy(0, slot).wait()

        part = tm // EXPERT_TILE_PARTS
        starts = range(0, tm, part)
        hids = []
        for r0 in starts:
            x = xbuf_ref[slot, r0:r0 + part, :]
            hids.append((jax.nn.silu(_dot(x, wgb_ref[...])) * _dot(x, wub_ref[...])).astype(BF16))
        for r0, hid in zip(starts, hids):
            ybuf_ref[slot, r0:r0 + part, :] = _dot(hid, wdb_ref[...]).astype(BF16)
        y_copy(g * tm, slot).start()
        return carry

    lax.fori_loop(0, n, tile, 0)
    done_ref[0] = done + n

    @pl.when(e == ne - 1)
    def _drain():
        for s in range(RING_SLOTS):
            y_copy(0, s).wait()


def _experts(xs, lay, wg, wu, wd):
    rows_cap, d = xs.shape
    ne, _, de = wg.shape
    assert ne >= RING_SLOTS
    tm = EXPERT_TILE
    ring = RING_SLOTS
    return pl.pallas_call(
        _experts_kernel,
        out_shape=jax.ShapeDtypeStruct((rows_cap, d), BF16),
        grid_spec=pltpu.PrefetchScalarGridSpec(
            num_scalar_prefetch=1,
            grid=(ne,),
            in_specs=[
                pl.BlockSpec(memory_space=pl.ANY),
                pl.BlockSpec((1, d, de), lambda e, *_: (e, 0, 0)),
                pl.BlockSpec((1, d, de), lambda e, *_: (e, 0, 0)),
                pl.BlockSpec((1, de, d), lambda e, *_: (e, 0, 0)),
            ],
            out_specs=pl.BlockSpec(memory_space=pl.ANY),
            scratch_shapes=[
                pltpu.VMEM((ring, tm, d), BF16), pltpu.VMEM((ring, tm, d), BF16),
                pltpu.VMEM((d, de), BF16), pltpu.VMEM((d, de), BF16), pltpu.VMEM((de, d), BF16),
                pltpu.SMEM((1,), jnp.int32),
                pltpu.SemaphoreType.DMA((ring,)), pltpu.SemaphoreType.DMA((ring,)),
            ],
        ),
        compiler_params=pltpu.CompilerParams(
            dimension_semantics=("arbitrary",),
            vmem_limit_bytes=_vmem_limit(32 * 1024 * 1024),
        ),
        name="experts",
    )(lay.region_tiles, xs, wg, wu, wd)


def _final_kernel(n_prompt_blocks, ne, dst_ref, odst_ref, otot_ref,
                  xp_ref, xs_ref, modp_ref, modt_ref, h2_ref, rank_ref, gates_ref, pos_ref, wts_ref,
                  wsg_ref, wsu_ref, wsd_ref, gf_ref, ye_hbm, yp_ref, ys_ref,
                  buf_ref, acc_ref, sem):
    i = pl.program_id(0)
    nb = pl.num_programs(0)
    slot = i % 2

    def fetch_slots(blk, sl, experts):
        for e in experts:
            _slot_copy(buf_ref, sl, e, ye_hbm, dst_ref[blk * ne + e], sem.at[sl], False).start()

    def overflow(blk, sl, wait):
        _overflow_copies(ne, blk, sl, odst_ref, otot_ref, buf_ref, ye_hbm, sem.at[sl], False, wait)

    @pl.when(i == 0)
    def _first():
        buf_ref[...] = jnp.zeros_like(buf_ref)
        fetch_slots(0, 0, range(ne))
        overflow(0, 0, False)

    @pl.when(i + 1 < nb)
    def _prefetch():
        fetch_slots(i + 1, 1 - slot, range(ne))
        overflow(i + 1, 1 - slot, False)

    _slots_wait(ne, buf_ref, slot, ye_hbm, sem.at[slot], False)
    overflow(i, slot, True)

    tb = h2_ref.shape[0]
    j = lax.broadcasted_iota(jnp.int32, (SLOT_ROWS, tb), 0)
    acc = jnp.zeros((tb, h2_ref.shape[1]), F32)
    for experts in _slot_chunks(ne):
        sel = jnp.concatenate(
            [jnp.where(rank_ref[e:e + 1, :] == j, gates_ref[e:e + 1, :], 0.0) for e in experts],
            axis=0)
        r0 = experts[0] * SLOT_ROWS
        acc = acc + _dot_tn(sel.astype(BF16), buf_ref[slot, r0:r0 + len(experts) * SLOT_ROWS, :])
    acc_ref[...] = acc

    pos = pos_ref[...]
    wts = wts_ref[...]

    def chunk(ci, carry):
        r0 = pl.multiple_of(ne * SLOT_ROWS + ci * ONEHOT_ROWS, ONEHOT_ROWS)
        sel = _overflow_select(pos, wts, r0).astype(BF16)
        acc_ref[...] += _dot_tn(sel, buf_ref[slot, pl.ds(r0, ONEHOT_ROWS), :])
        return carry

    lax.fori_loop(0, _overflow_chunks(otot_ref[i]), chunk, 0)

    is_p = i < n_prompt_blocks
    x = jnp.where(is_p, xp_ref[...], xs_ref[...])
    gate = jnp.where(is_p, modp_ref[0, 5:6, :], modt_ref[5])
    hb = h2_ref[...]
    hid = jax.nn.silu(_dot(hb, wsg_ref[...])) * _dot(hb, wsu_ref[...])
    f = acc_ref[...] + _dot(hid.astype(BF16), wsd_ref[...])
    x2 = x + gate * f
    y = x2 * lax.rsqrt(jnp.mean(x2 * x2, axis=-1, keepdims=True) + EPS) * gf_ref[...]

    @pl.when(is_p)
    def _p():
        yp_ref[...] = y

    @pl.when(jnp.logical_not(is_p))
    def _s():
        ys_ref[...] = y


def _final(xp2d, xs2d, modp, modt, h2, rank, gates, pos, wts, ye, lay, ne, wsg, wsu, wsd, gf):
    tp, d = xp2d.shape
    ts = xs2d.shape[0]
    tb = TOKEN_BLOCK
    npb, nsb = tp // tb, ts // tb
    seq = tp // modp.shape[0]
    bpb = seq // tb
    buf_rows = _block_buffer_rows(tb, ne)

    def p_idx(i):
        return jnp.minimum(i, npb - 1)

    def s_idx(i):
        return jnp.maximum(i - npb, 0)

    def const(shape):
        zeros = (0,) * len(shape)
        return pl.BlockSpec(shape, lambda i, *_: zeros)

    return pl.pallas_call(
        functools.partial(_final_kernel, npb, ne),
        out_shape=(jax.ShapeDtypeStruct((tp, d), F32), jax.ShapeDtypeStruct((ts, d), F32)),
        grid_spec=pltpu.PrefetchScalarGridSpec(
            num_scalar_prefetch=3,
            grid=(npb + nsb,),
            in_specs=[
                pl.BlockSpec((tb, d), lambda i, *_: (p_idx(i), 0)),
                pl.BlockSpec((tb, d), lambda i, *_: (s_idx(i), 0)),
                pl.BlockSpec((1, N_MOD, d), lambda i, *_: (p_idx(i) // bpb, 0, 0)),
                pl.BlockSpec((N_MOD, tb, d), lambda i, *_: (0, s_idx(i), 0)),
                pl.BlockSpec((tb, d), lambda i, *_: (i, 0)),
                pl.BlockSpec((ne, tb), lambda i, *_: (0, i)),
                pl.BlockSpec((ne, tb), lambda i, *_: (0, i)),
                pl.BlockSpec((TOP_K, tb), lambda i, *_: (0, i)),
                pl.BlockSpec((TOP_K, tb), lambda i, *_: (0, i)),
                const(wsg.shape), const(wsu.shape), const(wsd.shape), const(gf.shape),
                pl.BlockSpec(memory_space=pl.ANY),
            ],
            out_specs=(pl.BlockSpec((tb, d), lambda i, *_: (p_idx(i), 0)),
                       pl.BlockSpec((tb, d), lambda i, *_: (s_idx(i), 0))),
            scratch_shapes=[
                pltpu.VMEM((2, buf_rows, d), BF16),
                pltpu.VMEM((tb, d), F32),
                pltpu.SemaphoreType.DMA((2,)),
            ],
        ),
        compiler_params=pltpu.CompilerParams(
            dimension_semantics=("arbitrary",),
            vmem_limit_bytes=_vmem_limit(4 * buf_rows * d + 32 * 1024 * 1024),
        ),
        name="final",
    )(lay.dst, lay.ovf_dst, lay.ovf_tot,
      xp2d, xs2d, modp, modt, h2, rank, gates, pos, wts, wsg, wsu, wsd, gf, ye)


def kernel(x_prompt, x_sample, c_prompt, c_sample, state_pool, state_ret, w_ada, b_ada, g_norm1,
           w_in, w_pool_group, pool_scale, w_branch_pool, w_branch_ret, w_out, g_norm2, w_router,
           b_router, w_exp_gate, w_exp_up, w_exp_down, w_sh_gate, w_sh_up, w_sh_down, g_final):
    b, s, d = x_prompt.shape
    db, ds, _ = x_sample.shape
    depth = w_ada.shape[0]
    heads, dk, dv = state_ret.shape[2:]
    pool_buf, d_pool = state_pool.shape[2:]
    dims = _Dims(d, d_pool, heads, dk, dv)
    assert dims.d_in == w_in.shape[2]
    assert pool_buf == max(POOL_WINDOWS) - 1 and pool_buf < POOL_PREV
    assert s % PROMPT_STEP_ROWS == 0 and PROMPT_STEP_ROWS % RET_CHUNK == 0
    assert s % TOKEN_BLOCK == 0 and (db * ds) % TOKEN_BLOCK == 0
    assert db % (SAMPLE_SEQS_PER_STEP * SAMPLE_STEPS_PER_GROUP) == 0
    assert SLOT_ROWS % DMA_ROWS == 0 and (w_router.shape[2] * SLOT_ROWS) % ONEHOT_ROWS == 0

    log_g = np.log(1.0 - np.power(2.0, -5.0 - np.arange(heads, dtype=np.float64)))
    ns = SAMPLE_SEQS_PER_STEP
    cos_p, sin_p = _rotary_tables(np.arange(s), dk)
    cos_s, sin_s = _rotary_tables(PAST_LEN + np.arange(ds), dk)
    cos_s, sin_s = np.tile(cos_s, (ns, 1)), np.tile(sin_s, (ns, 1))
    dec_p = _decay_tables(log_g, RET_CHUNK, 1, dk, dv)
    dec_s = _decay_tables(log_g, ds, ns, dk, dv)
    pool_tabs = _sample_pool_tables(ns, ds, dims.gd, PAST_LEN)

    assert depth == 1, "the final kernel fuses the output norm into the (single) layer"
    l = 0
    xs = x_sample.reshape(db * ds, d)
    modp, modt = _ada(c_prompt, jnp.repeat(c_sample, ds, axis=0), w_ada[l], b_ada[l])
    modp = modp.reshape(b, N_MOD, d)
    g1 = g_norm1[l].reshape(1, d)
    weights = (w_in[l].astype(BF16), w_pool_group[l].astype(BF16),
               pool_scale[l].reshape(1, d_pool), w_branch_pool[l].astype(BF16),
               w_branch_ret[l].astype(BF16), w_out[l].astype(BF16))

    x1p, pool16, ret_p = _mixer_prompt(dims, x_prompt, modp, g1, weights,
                                       (cos_p, sin_p) + dec_p)
    pool_in = jnp.pad(state_pool[l], ((0, 0), (POOL_PREV - pool_buf, 0), (0, 0)))
    x1s, u_s, ret_s = _mixer_sample(dims, ds, xs, modt, g1, weights,
                                    (cos_s, sin_s) + dec_s + pool_tabs, pool_in,
                                    state_ret[l].astype(F32))
    pool_p = pool16[:, POOL_PREV - pool_buf:]
    pool_s = jnp.concatenate([state_pool[l], u_s.reshape(db, ds, d_pool)], axis=1)[:, -pool_buf:]

    x1p2d = x1p.reshape(b * s, d)
    ne = w_router.shape[2]
    h2, pos, wts, counts, rank, gates = _route(
        x1p2d, x1s, modp, modt, g_norm2[l].reshape(1, d), w_router[l].T,
        b_router[l].reshape(N_GROUPS, ne // N_GROUPS, 1))
    rows_cap = _expert_rows_cap(b * s + db * ds, TOKEN_BLOCK, ne)
    lay = _Layout(counts[:, :, 0])
    xe = _dispatch(h2, rank, pos, lay, ne, rows_cap)
    ye = _experts(xe, lay, w_exp_gate[l], w_exp_up[l], w_exp_down[l])
    yp, ys = _final(x1p2d, x1s, modp, modt, h2, rank, gates, pos, wts, ye, lay, ne,
                    w_sh_gate[l].astype(BF16), w_sh_up[l].astype(BF16),
                    w_sh_down[l].astype(BF16), g_final.reshape(1, d))

    return (yp.reshape(b, s, d), ys.reshape(db, ds, d), pool_p[None],
            ret_p.astype(x_prompt.dtype), pool_s[None], ret_s.astype(state_ret.dtype)[None])
```

```python
import functools

import jax
import jax.numpy as jnp
import numpy as np
from jax import lax
from jax.experimental import pallas as pl
from jax.experimental.pallas import tpu as pltpu

F32 = jnp.float32
BF16 = jnp.bfloat16

PAST_LEN = 16384
POOL_WINDOWS = (2, 4, 8, 16)
RET_CHUNK = 128
ROPE_BASE = 10000.0
TOP_K = 8
N_GROUPS = 8
TOPK_GROUPS = 4
ROUTED_SCALE = 2.5
N_MOD = 6
EPS = 1e-6

V7X_VMEM_BYTES = 64 * 1024 * 1024
SUBLANES = 8
LANES = 128

PROMPT_STEP_ROWS = 512
SAMPLE_SEQS_PER_STEP = 8
SAMPLE_STEPS_PER_GROUP = 4
TOKEN_BLOCK = 256
DMA_ROWS = 16
SLOT_ROWS = 3 * DMA_ROWS
ONEHOT_ROWS = 256
SLOT_CHUNK_EXPERTS = 16
EXPERT_TILE = 512
EXPERT_TILE_PARTS = 2
RING_SLOTS = 3
POOL_PREV = 16


def _vmem_limit(nbytes):
    return int(min(nbytes, V7X_VMEM_BYTES - 8 * 1024 * 1024))


def _dot(a, b):
    return jnp.dot(a, b, preferred_element_type=F32)


def _dot_nt(a, b):
    return lax.dot_general(a, b, (((1,), (1,)), ((), ())), preferred_element_type=F32)


def _dot_tn(a, b):
    return lax.dot_general(a, b, (((0,), (0,)), ((), ())), preferred_element_type=F32)


def _split_bf16(x, parts):
    out = []
    for _ in range(parts):
        p = x.astype(BF16)
        out.append(p)
        x = x - p.astype(F32)
    return out


def _rms_mod(x, g, shift, scale):
    y = x * lax.rsqrt(jnp.mean(x * x, axis=-1, keepdims=True) + EPS)
    return (y * g) * (1.0 + scale) + shift


def _rotary(x, cosb, sinb):
    return x * cosb + pltpu.roll(x, x.shape[-1] // 2, axis=1) * sinb


def _group_norm(o):
    return o * lax.rsqrt(jnp.mean(o * o, axis=-1, keepdims=True) + EPS)


def _ada_kernel(cp_ref, ct_ref, w_ref, b_ref, modp_ref, modt_ref):
    w = w_ref[...].astype(BF16)
    modp_ref[...] = _dot(cp_ref[...].astype(BF16), w) + b_ref[...]
    modt_ref[0] = _dot(ct_ref[...].astype(BF16), w) + b_ref[...]


def _ada(c_prompt, c_tokens, w_ada, b_ada):
    bp, d = c_prompt.shape
    ts = c_tokens.shape[0]
    n = w_ada.shape[1]
    return pl.pallas_call(
        _ada_kernel,
        out_shape=(jax.ShapeDtypeStruct((bp, n), F32), jax.ShapeDtypeStruct((n // d, ts, d), F32)),
        grid=(n // d,),
        in_specs=[
            pl.BlockSpec((bp, d), lambda j: (0, 0)),
            pl.BlockSpec((ts, d), lambda j: (0, 0)),
            pl.BlockSpec((d, d), lambda j: (0, j)),
            pl.BlockSpec((1, d), lambda j: (0, j)),
        ],
        out_specs=(pl.BlockSpec((bp, d), lambda j: (0, j)),
                   pl.BlockSpec((1, ts, d), lambda j: (j, 0, 0))),
        compiler_params=pltpu.CompilerParams(
            dimension_semantics=("arbitrary",),
            vmem_limit_bytes=_vmem_limit(32 * 1024 * 1024),
        ),
        name="ada",
    )(c_prompt, c_tokens, w_ada, b_ada.reshape(1, n))


class _Dims:
    def __init__(self, d_model, d_pool, heads, dk, dv):
        self.d = d_model
        self.d_pool = d_pool
        self.gd = d_pool // len(POOL_WINDOWS)
        self.heads = heads
        self.dk = dk
        self.dv = dv
        self.d_k = heads * dk
        self.d_v = heads * dv
        widths = (d_pool, self.d_k, self.d_k, self.d_v, self.d_v, d_model, d_model)
        offs = [0]
        for w in widths:
            offs.append(offs[-1] + w)
        self.cols = tuple((offs[i], offs[i + 1]) for i in range(len(widths)))
        self.d_in = offs[-1]


def _proj(hb, win_ref, cols):
    return _dot(hb, win_ref[:, cols[0]:cols[1]])


def _pool_branch(pm_parts, wpg_ref, ps_ref, wbp_ref):
    pg = [_dot(pm.astype(BF16), wpg_ref[gi]) for gi, pm in enumerate(pm_parts)]
    pm = jnp.concatenate(pg, axis=1) * ps_ref[...]
    return _dot(pm.astype(BF16), wbp_ref[...])


def _merge_out(x, gate, a, o, g, ga, gb, wbr_ref, wout_ref):
    r = _dot((jax.nn.silu(g) * o).astype(BF16), wbr_ref[...])
    merged = jax.nn.sigmoid(ga) * a + jax.nn.sigmoid(gb) * r
    m = _dot(merged.astype(BF16), wout_ref[...])
    return x + gate * m


def _mixer_prompt_kernel(dims, x_ref, mod_ref, g1_ref, win_ref, wpg_ref, ps_ref, wbp_ref,
                         wbr_ref, wout_ref, cos_ref, sin_ref, intra_ref, qdec_ref,
                         kdec_ref, sdec_ref, g2_ref, wr_ref, br_ref,
                         x1_ref, pool_ref, ret_ref, h2_ref, pos_ref, wts_ref, cnt_ref, rank_ref,
                         gates_ref, uext_ref, hnext_ref):
    c = pl.program_id(1)
    L = x_ref.shape[1]

    @pl.when(jnp.logical_and(pl.program_id(0) == 0, c == 0))
    def _init_route():
        hnext_ref[...] = jnp.zeros_like(hnext_ref)

    @pl.when(c == 0)
    def _init():
        ret_ref[...] = jnp.zeros_like(ret_ref)
        uext_ref[0:POOL_PREV, :] = jnp.zeros((POOL_PREV, dims.d_pool), F32)

    def route_prev(blk):
        _route_block(hnext_ref[blk * TOKEN_BLOCK:(blk + 1) * TOKEN_BLOCK, :], blk,
                     wr_ref, br_ref, h2_ref, pos_ref, wts_ref, cnt_ref, rank_ref, gates_ref)

    for blk in range(L // TOKEN_BLOCK):
        route_prev(blk)
    x = x_ref[0]
    shift, scale, gate = mod_ref[0, 0:1, :], mod_ref[0, 1:2, :], mod_ref[0, 2:3, :]
    hb = _rms_mod(x, g1_ref[...], shift, scale).astype(BF16)

    u = _proj(hb, win_ref, dims.cols[0])
    uext_ref[POOL_PREV:POOL_PREV + L, :] = u
    pos = c * L + lax.broadcasted_iota(jnp.int32, (L, 1), 0)
    pm_parts = []
    for gi, w in enumerate(POOL_WINDOWS):
        s = uext_ref[:, gi * dims.gd:(gi + 1) * dims.gd]
        sh = 1
        while sh < w:
            s = s + pltpu.roll(s, sh, axis=0)
            sh *= 2
        cnt = jnp.minimum(w, pos + 1).astype(F32)
        pm_parts.append(s[POOL_PREV:, :] / cnt - u[:, gi * dims.gd:(gi + 1) * dims.gd])
    a = _pool_branch(pm_parts, wpg_ref, ps_ref, wbp_ref)
    uext_ref[0:POOL_PREV, :] = uext_ref[L:L + POOL_PREV, :]
    pool_ref[0] = uext_ref[0:POOL_PREV, :]

    q = _proj(hb, win_ref, dims.cols[1])
    k = _proj(hb, win_ref, dims.cols[2])
    v = _proj(hb, win_ref, dims.cols[3])
    chunk_outs = []
    for r0 in range(0, L, RET_CHUNK):
        rows = slice(r0, r0 + RET_CHUNK)
        cosb, sinb = cos_ref[rows, :], sin_ref[rows, :]
        outs = []
        for h in range(dims.heads):
            qr = _rotary(q[rows, h * dims.dk:(h + 1) * dims.dk], cosb, sinb)
            kr = _rotary(k[rows, h * dims.dk:(h + 1) * dims.dk], cosb, sinb) * (dims.dk ** -0.5)
            vb = v[rows, h * dims.dv:(h + 1) * dims.dv].astype(BF16)
            scores = _dot_nt(qr.astype(BF16), kr.astype(BF16)) * intra_ref[h]
            s_prev = ret_ref[0, 0, h]
            o = _dot(scores.astype(BF16), vb) + _dot((qr * qdec_ref[h]).astype(BF16),
                                                      s_prev.astype(BF16))
            ret_ref[0, 0, h] = (sdec_ref[h] * s_prev
                                + _dot_tn((kr * kdec_ref[h]).astype(BF16), vb))
            outs.append(_group_norm(o))
        chunk_outs.append(jnp.concatenate(outs, axis=1))
    o = jnp.concatenate(chunk_outs, axis=0)

    g = _proj(hb, win_ref, dims.cols[4])
    ga = _proj(hb, win_ref, dims.cols[5])
    gb = _proj(hb, win_ref, dims.cols[6])
    x1 = _merge_out(x, gate, a, o, g, ga, gb, wbr_ref, wout_ref)
    x1_ref[0] = x1
    hnext_ref[...] = _rms_mod(x1, g2_ref[...], mod_ref[0, 3:4, :], mod_ref[0, 4:5, :])


def _const_spec(shape):
    zeros = (0,) * len(shape)
    return pl.BlockSpec(shape, lambda *_: zeros)


def _resident_spec(shape):
    zeros = (0,) * len(shape)
    return pl.BlockSpec(shape, lambda *_: zeros, pipeline_mode=pl.Buffered(1))


def _mixer_prompt(dims, x, mod, g1, weights, tables, router, t_total):
    b, s, d = x.shape
    L = PROMPT_STEP_ROWS
    nc = s // L
    win, wpg, ps, wbp, wbr, wout = weights
    cosb, sinb, intra, qdec, kdec, sdec = tables
    g2, wr_t, br = router
    ne = wr_t.shape[0]
    in_specs = [
        pl.BlockSpec((1, L, d), lambda i, c: (i, c, 0)),
        pl.BlockSpec((1, N_MOD, d), lambda i, c: (i, 0, 0)),
        _const_spec(g1.shape), _resident_spec(win.shape), _const_spec(wpg.shape),
        _const_spec(ps.shape), _resident_spec(wbp.shape), _resident_spec(wbr.shape),
        _resident_spec(wout.shape),
        pl.BlockSpec((L, dims.dk), lambda i, c: (c, 0)),
        pl.BlockSpec((L, dims.dk), lambda i, c: (c, 0)),
        _const_spec(intra.shape), _const_spec(qdec.shape), _const_spec(kdec.shape),
        _const_spec(sdec.shape),
        _const_spec(g2.shape), _const_spec(wr_t.shape), _const_spec(br.shape),
    ]
    out_shape = (
        jax.ShapeDtypeStruct((b, s, d), F32),
        jax.ShapeDtypeStruct((b, POOL_PREV, dims.d_pool), F32),
        jax.ShapeDtypeStruct((1, b, dims.heads, dims.dk, dims.dv), F32),
    ) + _route_out_shapes(t_total, d, ne)
    out_specs = (
        pl.BlockSpec((1, L, d), lambda i, c: (i, c, 0)),
        pl.BlockSpec((1, POOL_PREV, dims.d_pool), lambda i, c: (i, 0, 0)),
        pl.BlockSpec((1, 1, dims.heads, dims.dk, dims.dv), lambda i, c: (0, i, 0, 0, 0)),
    ) + _route_out_specs(L, ne, d, lambda i, c: jnp.maximum(i * nc + c - 1, 0))
    weight_bytes = 2 * (win.size + wpg.size + wbp.size + wbr.size + wout.size)
    return pl.pallas_call(
        functools.partial(_mixer_prompt_kernel, dims),
        out_shape=out_shape,
        grid=(b, nc),
        in_specs=in_specs,
        out_specs=out_specs,
        scratch_shapes=[pltpu.VMEM((POOL_PREV + L, dims.d_pool), F32), pltpu.VMEM((L, d), F32)],
        compiler_params=pltpu.CompilerParams(
            dimension_semantics=("arbitrary", "arbitrary"),
            vmem_limit_bytes=_vmem_limit(2 * weight_bytes + 20 * 1024 * 1024),
        ),
        name="mixer_prompt",
    )(x, mod, g1, win, wpg, ps, wbp, wbr, wout, cosb, sinb, intra, qdec, kdec, sdec,
      g2, wr_t, br)


def _mixer_sample_kernel(dims, ds, x_ref, modt_ref, g1_ref, win_ref, wpg_ref, ps_ref, wbp_ref,
                         wbr_ref, wout_ref, cos_ref, sin_ref, intra_ref, qdec_ref, kdec_ref,
                         sdec_ref, bandp_ref, bandu_ref, cnt_ref, poolin_ref, sret_ref,
                         x1_ref, u_ref, ret_ref, hb_ref, q_ref, k_ref, v_ref, pm_ref, o_ref):
    j = pl.program_id(1)
    R = cos_ref.shape[0]
    ns = R // ds

    @pl.when(j == 0)
    def _project():
        hb = _rms_mod(x_ref[...], g1_ref[...], modt_ref[0], modt_ref[1]).astype(BF16)
        hb_ref[...] = hb
        u_ref[...] = _proj(hb, win_ref, dims.cols[0])
        q_ref[...] = _proj(hb, win_ref, dims.cols[1])
        k_ref[...] = _proj(hb, win_ref, dims.cols[2])
        v_ref[...] = _proj(hb, win_ref, dims.cols[3]).astype(BF16)

    rows = pl.ds(pl.multiple_of(j * R, R), R)

    u = u_ref[rows, :]
    hist = poolin_ref[...].reshape(ns * POOL_PREV, dims.d_pool)
    pm_parts = []
    for gi in range(len(POOL_WINDOWS)):
        lo, hi = gi * dims.gd, (gi + 1) * dims.gd
        win_sum = jnp.zeros((R, dims.gd), F32)
        for part in _split_bf16(hist[:, lo:hi], 3):
            win_sum = win_sum + _dot(bandp_ref[gi], part)
        for part in _split_bf16(u[:, lo:hi], 3):
            win_sum = win_sum + _dot(bandu_ref[gi], part)
        pm_parts.append(win_sum / cnt_ref[gi] - u[:, lo:hi])
    pm_ref[rows, :] = jnp.concatenate(pm_parts, axis=1)

    q = q_ref[rows, :]
    k = k_ref[rows, :]
    v = v_ref[rows, :]
    cosb, sinb = cos_ref[...], sin_ref[...]
    row_seq = lax.broadcasted_iota(jnp.int32, (R, ns * dims.dk), 0) // ds
    col_seq = lax.broadcasted_iota(jnp.int32, (R, ns * dims.dk), 1) // dims.dk
    own = row_seq == col_seq
    outs = []
    for h in range(dims.heads):
        qr = _rotary(q[:, h * dims.dk:(h + 1) * dims.dk], cosb, sinb)
        kr = _rotary(k[:, h * dims.dk:(h + 1) * dims.dk], cosb, sinb) * (dims.dk ** -0.5)
        vb = v[:, h * dims.dv:(h + 1) * dims.dv]
        scores = _dot_nt(qr.astype(BF16), kr.astype(BF16)) * intra_ref[h]
        s_prev = sret_ref[:, h].reshape(ns * dims.dk, dims.dv)
        qd = jnp.concatenate([qr * qdec_ref[h]] * ns, axis=1)
        kd = jnp.concatenate([kr * kdec_ref[h]] * ns, axis=1)
        qexp = jnp.where(own, qd, 0.0).astype(BF16)
        kexp = jnp.where(own, kd, 0.0).astype(BF16)
        o = _dot(scores.astype(BF16), vb) + _dot(qexp, s_prev.astype(BF16))
        s_new = sdec_ref[h] * s_prev + _dot_tn(kexp, vb)
        ret_ref[:, h] = s_new.reshape(ns, dims.dk, dims.dv)
        outs.append(_group_norm(o))
    o_ref[rows, :] = jnp.concatenate(outs, axis=1)

    @pl.when(j == pl.num_programs(1) - 1)
    def _merge():
        hb = hb_ref[...]
        pm = pm_ref[...]
        pm_parts = [pm[:, gi * dims.gd:(gi + 1) * dims.gd] for gi in range(len(POOL_WINDOWS))]
        a = _pool_branch(pm_parts, wpg_ref, ps_ref, wbp_ref)
        g = _proj(hb, win_ref, dims.cols[4])
        ga = _proj(hb, win_ref, dims.cols[5])
        gb = _proj(hb, win_ref, dims.cols[6])
        x1_ref[...] = _merge_out(x_ref[...], modt_ref[2], a, o_ref[...], g, ga, gb,
                                 wbr_ref, wout_ref)


def _mixer_sample(dims, ds, x2d, modt, g1, weights, tables, pool_in, state_ret):
    t, d = x2d.shape
    ns = SAMPLE_SEQS_PER_STEP
    steps = SAMPLE_STEPS_PER_GROUP
    R = ns * ds
    RG = R * steps
    win, wpg, ps, wbp, wbr, wout = weights
    cosb, sinb, intra, qdec, kdec, sdec, bandp, bandu, cnt = tables
    db = state_ret.shape[0]
    in_specs = [
        pl.BlockSpec((RG, d), lambda i, j: (i, 0)),
        pl.BlockSpec((N_MOD, RG, d), lambda i, j: (0, i, 0)),
        _const_spec(g1.shape), _resident_spec(win.shape), _const_spec(wpg.shape),
        _const_spec(ps.shape), _resident_spec(wbp.shape), _resident_spec(wbr.shape),
        _resident_spec(wout.shape), _const_spec(cosb.shape), _const_spec(sinb.shape),
        _const_spec(intra.shape), _const_spec(qdec.shape), _const_spec(kdec.shape),
        _const_spec(sdec.shape), _const_spec(bandp.shape), _const_spec(bandu.shape),
        _const_spec(cnt.shape),
        pl.BlockSpec((ns, POOL_PREV, dims.d_pool), lambda i, j: (i * steps + j, 0, 0)),
        pl.BlockSpec((ns, dims.heads, dims.dk, dims.dv), lambda i, j: (i * steps + j, 0, 0, 0)),
    ]
    out_shape = (
        jax.ShapeDtypeStruct((t, d), F32),
        jax.ShapeDtypeStruct((t, dims.d_pool), F32),
        jax.ShapeDtypeStruct(state_ret.shape, state_ret.dtype),
    )
    out_specs = (
        pl.BlockSpec((RG, d), lambda i, j: (i, 0)),
        pl.BlockSpec((RG, dims.d_pool), lambda i, j: (i, 0)),
        pl.BlockSpec((ns, dims.heads, dims.dk, dims.dv), lambda i, j: (i * steps + j, 0, 0, 0)),
    )
    weight_bytes = 2 * (win.size + wpg.size + wbp.size + wbr.size + wout.size)
    state_bytes = 4 * ns * dims.heads * dims.dk * dims.dv
    return pl.pallas_call(
        functools.partial(_mixer_sample_kernel, dims, ds),
        out_shape=out_shape,
        grid=(db // (ns * steps), steps),
        in_specs=in_specs,
        out_specs=out_specs,
        scratch_shapes=[
            pltpu.VMEM((RG, d), BF16), pltpu.VMEM((RG, dims.d_k), F32),
            pltpu.VMEM((RG, dims.d_k), F32), pltpu.VMEM((RG, dims.d_v), BF16),
            pltpu.VMEM((RG, dims.d_pool), F32), pltpu.VMEM((RG, dims.d_v), F32),
        ],
        compiler_params=pltpu.CompilerParams(
            dimension_semantics=("arbitrary", "arbitrary"),
            vmem_limit_bytes=_vmem_limit(2 * weight_bytes + 4 * state_bytes + 8 * 1024 * 1024),
        ),
        name="mixer_sample",
    )(x2d, modt, g1, win, wpg, ps, wbp, wbr, wout, cosb, sinb, intra, qdec, kdec, sdec,
      bandp, bandu, cnt, pool_in, state_ret)


def _rotary_tables(pos, dk):
    half = dk // 2
    inv = np.power(ROPE_BASE, -np.arange(half, dtype=np.float64) / half)
    ang = pos.astype(np.float64)[:, None] * inv[None, :]
    cos, sin = np.cos(ang), np.sin(ang)
    return (np.concatenate([cos, cos], axis=1).astype(np.float32),
            np.concatenate([-sin, sin], axis=1).astype(np.float32))


def _decay_tables(log_g, chunk, reps, dk, dv):
    idx = np.arange(chunk, dtype=np.float64)
    diff = idx[:, None] - idx[None, :]
    intra = np.where((diff >= 0)[None],
                     np.exp(log_g[:, None, None] * np.maximum(diff, 0.0)[None]), 0.0)
    heads = log_g.shape[0]
    eye = np.eye(reps)
    intra = np.einsum("ab,hij->haibj", eye, intra).reshape(heads, reps * chunk, reps * chunk)
    q_dec = np.exp(log_g[:, None] * (idx[None, :] + 1.0))
    k_dec = np.exp(log_g[:, None] * (chunk - 1.0 - idx[None, :]))
    s_dec = np.exp(log_g * chunk)
    q_dec = np.broadcast_to(np.tile(q_dec, (1, reps))[:, :, None], (heads, reps * chunk, dk))
    k_dec = np.broadcast_to(np.tile(k_dec, (1, reps))[:, :, None], (heads, reps * chunk, dk))
    s_dec = np.broadcast_to(s_dec[:, None, None], (heads, 1, dv))
    return tuple(np.ascontiguousarray(t, dtype=np.float32) for t in (intra, q_dec, k_dec, s_dec))


def _sample_pool_tables(ns, ds, gd, start_pos):
    n = np.arange(ds)
    i = np.arange(POOL_PREV)
    eye = np.eye(ns, dtype=np.float32)
    bandp, bandu, cnt = [], [], []
    for w in POOL_WINDOWS:
        bp = ((i[None, :] - POOL_PREV) >= (n[:, None] - w + 1)).astype(np.float32)
        bu = ((n[None, :] <= n[:, None]) & (n[None, :] >= n[:, None] - w + 1)).astype(np.float32)
        bandp.append(np.einsum("ab,ni->anbi", eye, bp).reshape(ns * ds, ns * POOL_PREV))
        bandu.append(np.einsum("ab,nm->anbm", eye, bu).reshape(ns * ds, ns * ds))
        c = np.minimum(w, start_pos + n + 1).astype(np.float32)
        cnt.append(np.broadcast_to(np.tile(c, ns)[:, None], (ns * ds, gd)))
    return (np.stack(bandp).astype(BF16), np.stack(bandu).astype(BF16),
            np.ascontiguousarray(np.stack(cnt)))


def _exclusive_prefix(x, axis):
    pos = lax.broadcasted_iota(jnp.int32, x.shape, axis)
    acc = x
    sh = 1
    while sh < x.shape[axis]:
        acc = acc + jnp.where(pos >= sh, pltpu.roll(acc, sh, axis=axis), 0.0)
        sh *= 2
    return acc - x


def _first_max_onehot(v, idx, big, axes):
    m = jnp.max(v, axis=axes, keepdims=True)
    first = jnp.min(jnp.where(v == m, idx, big), axis=axes, keepdims=True)
    return idx == first


def _route_block(h2, blk, wr_ref, br_ref, h2_ref, pos_ref, wts_ref, cnt_ref, rank_ref, gates_ref):
    tok = slice(blk * h2.shape[0], (blk + 1) * h2.shape[0])
    hh = h2.astype(BF16)
    h2_ref[tok, :] = hh

    hl = (h2 - hh.astype(F32)).astype(BF16)
    wh, wl = _split_bf16(wr_ref[...], 2)
    logits = _dot_nt(wh, hh) + (_dot_nt(wh, hl) + _dot_nt(wl, hh))
    s = jax.nn.sigmoid(logits)
    e, t = s.shape
    ge = e // N_GROUPS
    neg = -jnp.inf
    s3 = s.reshape(N_GROUPS, ge, t)
    sb3 = s3 + br_ref[...]

    j3 = lax.broadcasted_iota(jnp.int32, (N_GROUPS, ge, t), 1)
    top1 = _first_max_onehot(sb3, j3, ge, 1)
    m1 = jnp.max(sb3, axis=1, keepdims=True)
    m2 = jnp.max(jnp.where(top1, neg, sb3), axis=1, keepdims=True)
    gv = m1 + m2

    g3 = lax.broadcasted_iota(jnp.int32, (N_GROUPS, 1, t), 0)
    gsel = jnp.zeros((N_GROUPS, 1, t), F32)
    for _ in range(TOPK_GROUPS):
        hit = _first_max_onehot(gv, g3, N_GROUPS, 0)
        gsel = jnp.where(hit, 1.0, gsel)
        gv = jnp.where(hit, neg, gv)

    e3 = lax.broadcasted_iota(jnp.int32, (N_GROUPS, ge, t), 0) * ge + j3
    ev = jnp.where(gsel > 0.5, sb3, neg)
    sel = jnp.zeros((N_GROUPS, ge, t), F32)
    chosen = jnp.zeros((N_GROUPS, ge, t), F32)
    hits = []
    for _ in range(TOP_K):
        hit = _first_max_onehot(ev, e3, e, (0, 1))
        hits.append(hit)
        sel = jnp.where(hit, s3, sel)
        chosen = jnp.where(hit, 1.0, chosen)
        ev = jnp.where(hit, neg, ev)
    tot = jnp.sum(sel, axis=(0, 1), keepdims=True)
    gates = sel / tot * ROUTED_SCALE

    chosen2 = chosen.reshape(e, t)
    rank = _exclusive_prefix(chosen2, 1)
    cnt = jnp.sum(chosen2, axis=1, keepdims=True)
    ovf_units = jnp.floor((jnp.maximum(cnt - SLOT_ROWS, 0.0) + (DMA_ROWS - 1)) / DMA_ROWS)
    ovf_off = _exclusive_prefix(jnp.broadcast_to(ovf_units, (e, LANES)), 0)[:, 0:1]
    e_col = lax.broadcasted_iota(jnp.int32, (e, 1), 0).astype(F32)
    slot = jnp.where(rank < SLOT_ROWS, e_col * SLOT_ROWS + rank,
                     e * SLOT_ROWS + ovf_off * DMA_ROWS + (rank - SLOT_ROWS))
    slot3 = slot.reshape(N_GROUPS, ge, t)
    for k, hit in enumerate(hits):
        pos_k = jnp.sum(jnp.where(hit, slot3, 0.0), axis=(0, 1), keepdims=True)
        w_k = jnp.sum(jnp.where(hit, gates, 0.0), axis=(0, 1), keepdims=True)
        pos_ref[k:k + 1, tok] = pos_k.reshape(1, t).astype(jnp.int32)
        wts_ref[k:k + 1, tok] = w_k.reshape(1, t)
    cnt_ref[blk] = jnp.broadcast_to(cnt, (e, LANES)).astype(jnp.int32)
    rank_ref[:, tok] = jnp.where(chosen2 > 0.5, rank, -1.0).astype(jnp.int32)
    gates_ref[:, tok] = gates.reshape(e, t)


def _route_out_shapes(t, d, ne):
    return (jax.ShapeDtypeStruct((t, d), BF16),
            jax.ShapeDtypeStruct((TOP_K, t), jnp.int32),
            jax.ShapeDtypeStruct((TOP_K, t), F32),
            jax.ShapeDtypeStruct((t // TOKEN_BLOCK, ne, LANES), jnp.int32),
            jax.ShapeDtypeStruct((ne, t), jnp.int32),
            jax.ShapeDtypeStruct((ne, t), F32))


def _route_out_specs(rows, ne, d, idx):
    return (pl.BlockSpec((rows, d), lambda *g: (idx(*g), 0)),
            pl.BlockSpec((TOP_K, rows), lambda *g: (0, idx(*g))),
            pl.BlockSpec((TOP_K, rows), lambda *g: (0, idx(*g))),
            pl.BlockSpec((rows // TOKEN_BLOCK, ne, LANES), lambda *g: (idx(*g), 0, 0)),
            pl.BlockSpec((ne, rows), lambda *g: (0, idx(*g))),
            pl.BlockSpec((ne, rows), lambda *g: (0, idx(*g))))


def _route_rows_kernel(per_token, x_ref, mod_ref, g2_ref, wr_ref, br_ref, *refs):
    outs = refs[len(refs) // 2:]
    if per_token:
        shift, scale = mod_ref[3], mod_ref[4]
    else:
        shift, scale = mod_ref[0, 3:4, :], mod_ref[0, 4:5, :]
    h2 = _rms_mod(x_ref[...], g2_ref[...], shift, scale)
    _route_block(h2, 0, wr_ref, br_ref, *outs)


def _route_rows(x2d, first_row, n_rows, mod, per_token, g2, wr_t, br, routed, first_block):
    d = x2d.shape[1]
    tb = TOKEN_BLOCK
    ne = wr_t.shape[0]
    t = routed[0].shape[0]
    x_block0 = first_row // tb
    if per_token:
        mod_spec = pl.BlockSpec((N_MOD, tb, d), lambda i: (0, x_block0 + i, 0))
    else:
        mod_spec = pl.BlockSpec((1, N_MOD, d), lambda i: (0, 0, 0))
    n_in = 5
    return pl.pallas_call(
        functools.partial(_route_rows_kernel, per_token),
        out_shape=_route_out_shapes(t, d, ne),
        grid=(n_rows // tb,),
        in_specs=[
            pl.BlockSpec((tb, d), lambda i: (x_block0 + i, 0)),
            mod_spec,
            _const_spec(g2.shape), _const_spec(wr_t.shape), _const_spec(br.shape),
        ] + [pl.BlockSpec(memory_space=pl.ANY)] * len(routed),
        out_specs=_route_out_specs(tb, ne, d, lambda i: first_block + i),
        input_output_aliases={n_in + k: k for k in range(len(routed))},
        compiler_params=pltpu.CompilerParams(
            dimension_semantics=("arbitrary",),
            vmem_limit_bytes=_vmem_limit(32 * 1024 * 1024),
        ),
        name="route_rows",
    )(x2d, mod, g2, wr_t, br, *routed)


class _Layout:
    def __init__(self, counts):
        nb, ne = counts.shape
        u, g, tm = SLOT_ROWS, DMA_ROWS, EXPERT_TILE
        pc = (counts + g - 1) // g * g
        ovf_units = (jnp.maximum(counts - u, 0) + g - 1) // g
        rows_e = jnp.sum(pc, axis=0)
        region = (rows_e + u + tm - 1) // tm * tm
        goff = jnp.cumsum(region) - region
        dst = goff[None, :] + jnp.cumsum(pc, axis=0) - pc
        self.dst = dst.reshape(-1)
        ovf_end = jnp.cumsum(ovf_units, axis=1)
        ovf_first = ovf_end - ovf_units
        unit = jnp.arange(_overflow_units_cap(ne))[None, :, None]
        owns = (ovf_first[:, None, :] <= unit) & (unit < ovf_end[:, None, :])
        row = dst[:, None, :] + u + (unit - ovf_first[:, None, :]) * g
        self.ovf_dst = jnp.sum(jnp.where(owns, row, 0), axis=2).reshape(-1)
        self.ovf_tot = ovf_end[:, -1]
        self.tail_start = goff + rows_e
        self.tail_units = (region - rows_e) // g
        tiles = region // tm
        self.region_tiles = jnp.concatenate([tiles, jnp.sum(tiles, keepdims=True)])


def _overflow_units_cap(ne):
    return TOKEN_BLOCK * TOP_K // DMA_ROWS + ne


def _block_buffer_rows(tb, ne):
    rows = ne * SLOT_ROWS + tb * TOP_K + ne * (DMA_ROWS - 1)
    return (rows + ONEHOT_ROWS - 1) // ONEHOT_ROWS * ONEHOT_ROWS


def _expert_rows_cap(t, tb, ne):
    nb = t // tb
    rows = t * TOP_K + nb * ne * (DMA_ROWS - 1) + ne * (SLOT_ROWS + EXPERT_TILE - 1)
    return (rows + EXPERT_TILE - 1) // EXPERT_TILE * EXPERT_TILE


def _slot_copy(buf_ref, slot, e, hbm_ref, row, sem, to_hbm):
    src = buf_ref.at[slot, pl.ds(pl.multiple_of(e * SLOT_ROWS, DMA_ROWS), SLOT_ROWS)]
    dst = hbm_ref.at[pl.ds(pl.multiple_of(row, DMA_ROWS), SLOT_ROWS)]
    return pltpu.make_async_copy(src, dst, sem) if to_hbm else pltpu.make_async_copy(dst, src, sem)


def _unit_copy(buf_ref, slot, buf_row, hbm_ref, row, sem, to_hbm):
    src = buf_ref.at[slot, pl.ds(pl.multiple_of(buf_row, DMA_ROWS), DMA_ROWS)]
    dst = hbm_ref.at[pl.ds(pl.multiple_of(row, DMA_ROWS), DMA_ROWS)]
    return pltpu.make_async_copy(src, dst, sem) if to_hbm else pltpu.make_async_copy(dst, src, sem)


def _slots_wait(ne, buf_ref, slot, hbm_ref, sem, to_hbm):
    src = buf_ref.at[slot, pl.ds(0, ne * SLOT_ROWS)]
    dst = hbm_ref.at[pl.ds(0, ne * SLOT_ROWS)]
    c = pltpu.make_async_copy(src, dst, sem) if to_hbm else pltpu.make_async_copy(dst, src, sem)
    c.wait()


def _overflow_copies(ne, blk, slot, odst_ref, otot_ref, buf_ref, hbm_ref, sem, to_hbm, wait):
    def per_unit(i, carry):
        row = odst_ref[blk * _overflow_units_cap(ne) + i]
        c = _unit_copy(buf_ref, slot, ne * SLOT_ROWS + i * DMA_ROWS, hbm_ref, row, sem, to_hbm)
        c.wait() if wait else c.start()
        return carry

    lax.fori_loop(0, otot_ref[blk], per_unit, 0)


def _slot_chunks(ne):
    return [range(c, c + SLOT_CHUNK_EXPERTS) for c in range(0, ne, SLOT_CHUNK_EXPERTS)]


def _overflow_chunks(n_units):
    return lax.shift_right_logical(n_units * DMA_ROWS + (ONEHOT_ROWS - 1),
                                   ONEHOT_ROWS.bit_length() - 1)


def _overflow_select(pos, vals, r0):
    rows = lax.broadcasted_iota(jnp.int32, (ONEHOT_ROWS, pos.shape[1]), 0) + r0
    sel = jnp.zeros(rows.shape, F32)
    for k in range(TOP_K):
        v = 1.0 if vals is None else vals[k:k + 1, :]
        sel = jnp.where(pos[k:k + 1, :] == rows, v, sel)
    return sel


def _dispatch_kernel(ne, dst_ref, odst_ref, otot_ref, tails_ref, tailu_ref,
                     h2_ref, rank_ref, pos_ref, xs_hbm, buf_ref, zero_ref, sem, zsem):
    b = pl.program_id(0)
    nb = pl.num_programs(0)
    slot = b % 2
    h2 = h2_ref[...]
    j = lax.broadcasted_iota(jnp.int32, (SLOT_ROWS, h2.shape[0]), 0)

    chunks = _slot_chunks(ne)
    first_issue = max(len(chunks) - 2, 0)
    for ci, experts in enumerate(chunks):
        onehot = jnp.concatenate(
            [jnp.where(rank_ref[e:e + 1, :] == j, 1.0, 0.0) for e in experts], axis=0)
        r0 = experts[0] * SLOT_ROWS
        buf_ref[slot, r0:r0 + len(experts) * SLOT_ROWS, :] = (
            _dot(onehot.astype(BF16), h2).astype(BF16))
        if ci < first_issue:
            continue
        if ci == first_issue:
            @pl.when(b > 0)
            def _prev():
                _slots_wait(ne, buf_ref, 1 - slot, xs_hbm, sem.at[1 - slot], True)
                _overflow_copies(ne, b - 1, 1 - slot, odst_ref, otot_ref, buf_ref, xs_hbm,
                                 sem.at[1 - slot], True, True)
        ready = [e for c in chunks[:ci + 1] for e in c] if ci == first_issue else experts
        for e in ready:
            _slot_copy(buf_ref, slot, e, xs_hbm, dst_ref[b * ne + e], sem.at[slot], True).start()

    pos = pos_ref[...]

    def chunk(ci, carry):
        r0 = pl.multiple_of(ne * SLOT_ROWS + ci * ONEHOT_ROWS, ONEHOT_ROWS)
        onehot = _overflow_select(pos, None, r0).astype(BF16)
        buf_ref[slot, pl.ds(r0, ONEHOT_ROWS), :] = _dot(onehot, h2).astype(BF16)
        return carry

    lax.fori_loop(0, _overflow_chunks(otot_ref[b]), chunk, 0)
    _overflow_copies(ne, b, slot, odst_ref, otot_ref, buf_ref, xs_hbm, sem.at[slot], True, False)

    @pl.when(b == nb - 1)
    def _last():
        _slots_wait(ne, buf_ref, slot, xs_hbm, sem.at[slot], True)
        _overflow_copies(ne, b, slot, odst_ref, otot_ref, buf_ref, xs_hbm, sem.at[slot], True,
                         True)
        zero_ref[...] = jnp.zeros_like(zero_ref)

        def tail(wait):
            def per_expert(e, carry):
                def per_unit(i, carry2):
                    row = pl.multiple_of(tails_ref[e] + i * DMA_ROWS, DMA_ROWS)
                    c = pltpu.make_async_copy(zero_ref, xs_hbm.at[pl.ds(row, DMA_ROWS)], zsem)
                    c.wait() if wait else c.start()
                    return carry2
                lax.fori_loop(0, tailu_ref[e], per_unit, 0)
                return carry
            lax.fori_loop(0, ne, per_expert, 0)

        tail(False)
        tail(True)


def _dispatch(h2, rank, pos, lay, ne, rows_cap):
    t, d = h2.shape
    tb = TOKEN_BLOCK
    buf_rows = _block_buffer_rows(tb, ne)
    return pl.pallas_call(
        functools.partial(_dispatch_kernel, ne),
        out_shape=jax.ShapeDtypeStruct((rows_cap, d), BF16),
        grid_spec=pltpu.PrefetchScalarGridSpec(
            num_scalar_prefetch=5,
            grid=(t // tb,),
            in_specs=[
                pl.BlockSpec((tb, d), lambda i, *_: (i, 0)),
                pl.BlockSpec((ne, tb), lambda i, *_: (0, i)),
                pl.BlockSpec((TOP_K, tb), lambda i, *_: (0, i)),
            ],
            out_specs=pl.BlockSpec(memory_space=pl.ANY),
            scratch_shapes=[
                pltpu.VMEM((2, buf_rows, d), BF16),
                pltpu.VMEM((DMA_ROWS, d), BF16),
                pltpu.SemaphoreType.DMA((2,)),
                pltpu.SemaphoreType.DMA(()),
            ],
        ),
        compiler_params=pltpu.CompilerParams(
            dimension_semantics=("arbitrary",),
            vmem_limit_bytes=_vmem_limit(4 * buf_rows * d + 16 * 1024 * 1024),
        ),
        name="dispatch",
    )(lay.dst, lay.ovf_dst, lay.ovf_tot, lay.tail_start, lay.tail_units, h2, rank, pos)


def _experts_kernel(tiles_ref, xe_hbm, wg_ref, wu_ref, wd_ref, ye_hbm,
                    xbuf_ref, ybuf_ref, wgb_ref, wub_ref, wdb_ref, done_ref, xsem, ysem):
    e = pl.program_id(0)
    ne = pl.num_programs(0)
    tm = xbuf_ref.shape[1]

    def x_copy(row, slot):
        return pltpu.make_async_copy(xe_hbm.at[pl.ds(pl.multiple_of(row, tm), tm)],
                                     xbuf_ref.at[slot], xsem.at[slot])

    def y_copy(row, slot):
        return pltpu.make_async_copy(ybuf_ref.at[slot],
                                     ye_hbm.at[pl.ds(pl.multiple_of(row, tm), tm)], ysem.at[slot])

    total = tiles_ref[ne]

    @pl.when(e == 0)
    def _first():
        done_ref[0] = 0
        for g in range(RING_SLOTS - 1):
            x_copy(g * tm, g).start()

    wgb_ref[...] = wg_ref[0].astype(BF16)
    wub_ref[...] = wu_ref[0].astype(BF16)
    wdb_ref[...] = wd_ref[0].astype(BF16)
    done = done_ref[0]
    n = tiles_ref[e]

    def tile(j, carry):
        g = done + j
        slot = lax.rem(g, RING_SLOTS)
        x_copy(0, slot).wait()
        ahead = g + (RING_SLOTS - 1)

        @pl.when(ahead < total)
        def _prefetch():
            x_copy(ahead * tm, lax.rem(ahead, RING_SLOTS)).start()

        @pl.when(g >= RING_SLOTS)
        def _reuse():
            y_copy(0, slot).wait()

        part = tm // EXPERT_TILE_PARTS
        starts = range(0, tm, part)
        hids = []
        for r0 in starts:
            x = xbuf_ref[slot, r0:r0 + part, :]
            hids.append((jax.nn.silu(_dot(x, wgb_ref[...])) * _dot(x, wub_ref[...])).astype(BF16))
        for r0, hid in zip(starts, hids):
            ybuf_ref[slot, r0:r0 + part, :] = _dot(hid, wdb_ref[...]).astype(BF16)
        y_copy(g * tm, slot).start()
        return carry

    lax.fori_loop(0, n, tile, 0)
    done_ref[0] = done + n

    @pl.when(e == ne - 1)
    def _drain():
        for s in range(RING_SLOTS):
            y_copy(0, s).wait()


def _experts(xs, lay, wg, wu, wd):
    rows_cap, d = xs.shape
    ne, _, de = wg.shape
    assert ne >= RING_SLOTS
    tm = EXPERT_TILE
    ring = RING_SLOTS
    return pl.pallas_call(
        _experts_kernel,
        out_shape=jax.ShapeDtypeStruct((rows_cap, d), BF16),
        grid_spec=pltpu.PrefetchScalarGridSpec(
            num_scalar_prefetch=1,
            grid=(ne,),
            in_specs=[
                pl.BlockSpec(memory_space=pl.ANY),
                pl.BlockSpec((1, d, de), lambda e, *_: (e, 0, 0)),
                pl.BlockSpec((1, d, de), lambda e, *_: (e, 0, 0)),
                pl.BlockSpec((1, de, d), lambda e, *_: (e, 0, 0)),
            ],
            out_specs=pl.BlockSpec(memory_space=pl.ANY),
            scratch_shapes=[
                pltpu.VMEM((ring, tm, d), BF16), pltpu.VMEM((ring, tm, d), BF16),
                pltpu.VMEM((d, de), BF16), pltpu.VMEM((d, de), BF16), pltpu.VMEM((de, d), BF16),
                pltpu.SMEM((1,), jnp.int32),
                pltpu.SemaphoreType.DMA((ring,)), pltpu.SemaphoreType.DMA((ring,)),
            ],
        ),
        compiler_params=pltpu.CompilerParams(
            dimension_semantics=("arbitrary",),
            vmem_limit_bytes=_vmem_limit(32 * 1024 * 1024),
        ),
        name="experts",
    )(lay.region_tiles, xs, wg, wu, wd)


def _final_kernel(n_prompt_blocks, ne, dst_ref, odst_ref, otot_ref,
                  xp_ref, xs_ref, modp_ref, modt_ref, h2_ref, rank_ref, gates_ref, pos_ref, wts_ref,
                  wsg_ref, wsu_ref, wsd_ref, gf_ref, ye_hbm, yp_ref, ys_ref,
                  buf_ref, acc_ref, sem):
    i = pl.program_id(0)
    nb = pl.num_programs(0)
    slot = i % 2

    def fetch_slots(blk, sl, experts):
        for e in experts:
            _slot_copy(buf_ref, sl, e, ye_hbm, dst_ref[blk * ne + e], sem.at[sl], False).start()

    def overflow(blk, sl, wait):
        _overflow_copies(ne, blk, sl, odst_ref, otot_ref, buf_ref, ye_hbm, sem.at[sl], False, wait)

    @pl.when(i == 0)
    def _first():
        buf_ref[...] = jnp.zeros_like(buf_ref)
        fetch_slots(0, 0, range(ne))
        overflow(0, 0, False)

    @pl.when(i + 1 < nb)
    def _prefetch():
        fetch_slots(i + 1, 1 - slot, range(ne))
        overflow(i + 1, 1 - slot, False)

    _slots_wait(ne, buf_ref, slot, ye_hbm, sem.at[slot], False)
    overflow(i, slot, True)

    tb = h2_ref.shape[0]
    j = lax.broadcasted_iota(jnp.int32, (SLOT_ROWS, tb), 0)
    acc = jnp.zeros((tb, h2_ref.shape[1]), F32)
    for experts in _slot_chunks(ne):
        sel = jnp.concatenate(
            [jnp.where(rank_ref[e:e + 1, :] == j, gates_ref[e:e + 1, :], 0.0) for e in experts],
            axis=0)
        r0 = experts[0] * SLOT_ROWS
        acc = acc + _dot_tn(sel.astype(BF16), buf_ref[slot, r0:r0 + len(experts) * SLOT_ROWS, :])
    acc_ref[...] = acc

    pos = pos_ref[...]
    wts = wts_ref[...]

    def chunk(ci, carry):
        r0 = pl.multiple_of(ne * SLOT_ROWS + ci * ONEHOT_ROWS, ONEHOT_ROWS)
        sel = _overflow_select(pos, wts, r0).astype(BF16)
        acc_ref[...] += _dot_tn(sel, buf_ref[slot, pl.ds(r0, ONEHOT_ROWS), :])
        return carry

    lax.fori_loop(0, _overflow_chunks(otot_ref[i]), chunk, 0)

    is_p = i < n_prompt_blocks
    x = jnp.where(is_p, xp_ref[...], xs_ref[...])
    gate = jnp.where(is_p, modp_ref[0, 5:6, :], modt_ref[5])
    hb = h2_ref[...]
    hid = jax.nn.silu(_dot(hb, wsg_ref[...])) * _dot(hb, wsu_ref[...])
    f = acc_ref[...] + _dot(hid.astype(BF16), wsd_ref[...])
    x2 = x + gate * f
    y = x2 * lax.rsqrt(jnp.mean(x2 * x2, axis=-1, keepdims=True) + EPS) * gf_ref[...]

    @pl.when(is_p)
    def _p():
        yp_ref[...] = y

    @pl.when(jnp.logical_not(is_p))
    def _s():
        ys_ref[...] = y


def _final(xp2d, xs2d, modp, modt, h2, rank, gates, pos, wts, ye, lay, ne, wsg, wsu, wsd, gf):
    tp, d = xp2d.shape
    ts = xs2d.shape[0]
    tb = TOKEN_BLOCK
    npb, nsb = tp // tb, ts // tb
    seq = tp // modp.shape[0]
    bpb = seq // tb
    buf_rows = _block_buffer_rows(tb, ne)

    def p_idx(i):
        return jnp.minimum(i, npb - 1)

    def s_idx(i):
        return jnp.maximum(i - npb, 0)

    def const(shape):
        zeros = (0,) * len(shape)
        return pl.BlockSpec(shape, lambda i, *_: zeros)

    return pl.pallas_call(
        functools.partial(_final_kernel, npb, ne),
        out_shape=(jax.ShapeDtypeStruct((tp, d), F32), jax.ShapeDtypeStruct((ts, d), F32)),
        grid_spec=pltpu.PrefetchScalarGridSpec(
            num_scalar_prefetch=3,
            grid=(npb + nsb,),
            in_specs=[
                pl.BlockSpec((tb, d), lambda i, *_: (p_idx(i), 0)),
                pl.BlockSpec((tb, d), lambda i, *_: (s_idx(i), 0)),
                pl.BlockSpec((1, N_MOD, d), lambda i, *_: (p_idx(i) // bpb, 0, 0)),
                pl.BlockSpec((N_MOD, tb, d), lambda i, *_: (0, s_idx(i), 0)),
                pl.BlockSpec((tb, d), lambda i, *_: (i, 0)),
                pl.BlockSpec((ne, tb), lambda i, *_: (0, i)),
                pl.BlockSpec((ne, tb), lambda i, *_: (0, i)),
                pl.BlockSpec((TOP_K, tb), lambda i, *_: (0, i)),
                pl.BlockSpec((TOP_K, tb), lambda i, *_: (0, i)),
                const(wsg.shape), const(wsu.shape), const(wsd.shape), const(gf.shape),
                pl.BlockSpec(memory_space=pl.ANY),
            ],
            out_specs=(pl.BlockSpec((tb, d), lambda i, *_: (p_idx(i), 0)),
                       pl.BlockSpec((tb, d), lambda i, *_: (s_idx(i), 0))),
            scratch_shapes=[
                pltpu.VMEM((2, buf_rows, d), BF16),
                pltpu.VMEM((tb, d), F32),
                pltpu.SemaphoreType.DMA((2,)),
            ],
        ),
        compiler_params=pltpu.CompilerParams(
            dimension_semantics=("arbitrary",),
            vmem_limit_bytes=_vmem_limit(4 * buf_rows * d + 32 * 1024 * 1024),
        ),
        name="final",
    )(lay.dst, lay.ovf_dst, lay.ovf_tot,
      xp2d, xs2d, modp, modt, h2, rank, gates, pos, wts, wsg, wsu, wsd, gf, ye)


def kernel(x_prompt, x_sample, c_prompt, c_sample, state_pool, state_ret, w_ada, b_ada, g_norm1,
           w_in, w_pool_group, pool_scale, w_branch_pool, w_branch_ret, w_out, g_norm2, w_router,
           b_router, w_exp_gate, w_exp_up, w_exp_down, w_sh_gate, w_sh_up, w_sh_down, g_final):
    b, s, d = x_prompt.shape
    db, ds, _ = x_sample.shape
    depth = w_ada.shape[0]
    heads, dk, dv = state_ret.shape[2:]
    pool_buf, d_pool = state_pool.shape[2:]
    dims = _Dims(d, d_pool, heads, dk, dv)
    assert dims.d_in == w_in.shape[2]
    assert pool_buf == max(POOL_WINDOWS) - 1 and pool_buf < POOL_PREV
    assert s % PROMPT_STEP_ROWS == 0 and PROMPT_STEP_ROWS % RET_CHUNK == 0
    assert s % TOKEN_BLOCK == 0 and (db * ds) % TOKEN_BLOCK == 0
    assert db % (SAMPLE_SEQS_PER_STEP * SAMPLE_STEPS_PER_GROUP) == 0
    assert SLOT_ROWS % DMA_ROWS == 0 and (w_router.shape[2] * SLOT_ROWS) % ONEHOT_ROWS == 0

    log_g = np.log(1.0 - np.power(2.0, -5.0 - np.arange(heads, dtype=np.float64)))
    ns = SAMPLE_SEQS_PER_STEP
    cos_p, sin_p = _rotary_tables(np.arange(s), dk)
    cos_s, sin_s = _rotary_tables(PAST_LEN + np.arange(ds), dk)
    cos_s, sin_s = np.tile(cos_s, (ns, 1)), np.tile(sin_s, (ns, 1))
    dec_p = _decay_tables(log_g, RET_CHUNK, 1, dk, dv)
    dec_s = _decay_tables(log_g, ds, ns, dk, dv)
    pool_tabs = _sample_pool_tables(ns, ds, dims.gd, PAST_LEN)

    assert depth == 1, "the final kernel fuses the output norm into the (single) layer"
    l = 0
    xs = x_sample.reshape(db * ds, d)
    modp, modt = _ada(c_prompt, jnp.repeat(c_sample, ds, axis=0), w_ada[l], b_ada[l])
    modp = modp.reshape(b, N_MOD, d)
    g1 = g_norm1[l].reshape(1, d)
    weights = (w_in[l].astype(BF16), w_pool_group[l].astype(BF16),
               pool_scale[l].reshape(1, d_pool), w_branch_pool[l].astype(BF16),
               w_branch_ret[l].astype(BF16), w_out[l].astype(BF16))

    ne = w_router.shape[2]
    router = (g_norm2[l].reshape(1, d), w_router[l].T,
              b_router[l].reshape(N_GROUPS, ne // N_GROUPS, 1))
    x1p, pool16, ret_p, *routed = _mixer_prompt(dims, x_prompt, modp, g1, weights,
                                                (cos_p, sin_p) + dec_p, router,
                                                b * s + db * ds)
    pool_in = jnp.pad(state_pool[l], ((0, 0), (POOL_PREV - pool_buf, 0), (0, 0)))
    x1s, u_s, ret_s = _mixer_sample(dims, ds, xs, modt, g1, weights,
                                    (cos_s, sin_s) + dec_s + pool_tabs, pool_in,
                                    state_ret[l].astype(F32))
    pool_p = pool16[:, POOL_PREV - pool_buf:]
    pool_s = jnp.concatenate([state_pool[l], u_s.reshape(db, ds, d_pool)], axis=1)[:, -pool_buf:]

    x1p2d = x1p.reshape(b * s, d)
    last = b * s - PROMPT_STEP_ROWS
    routed = _route_rows(x1p2d, last, PROMPT_STEP_ROWS, modp[b - 1:b], False, *router, routed,
                         last // TOKEN_BLOCK)
    h2, pos, wts, counts, rank, gates = _route_rows(x1s, 0, db * ds, modt, True, *router, routed,
                                                    (b * s) // TOKEN_BLOCK)
    rows_cap = _expert_rows_cap(b * s + db * ds, TOKEN_BLOCK, ne)
    lay = _Layout(counts[:, :, 0])
    xe = _dispatch(h2, rank, pos, lay, ne, rows_cap)
    ye = _experts(xe, lay, w_exp_gate[l], w_exp_up[l], w_exp_down[l])
    yp, ys = _final(x1p2d, x1s, modp, modt, h2, rank, gates, pos, wts, ye, lay, ne,
                    w_sh_gate[l].astype(BF16), w_sh_up[l].astype(BF16),
                    w_sh_down[l].astype(BF16), g_final.reshape(1, d))

    return (yp.reshape(b, s, d), ys.reshape(db, ds, d), pool_p[None],
            ret_p.astype(x_prompt.dtype), pool_s[None], ret_s.astype(state_ret.dtype)[None])
```

```python
import functools

import jax
import jax.numpy as jnp
import numpy as np
from jax import lax
from jax.experimental import pallas as pl
from jax.experimental.pallas import tpu as pltpu

F32 = jnp.float32
BF16 = jnp.bfloat16

PAST_LEN = 16384
POOL_WINDOWS = (2, 4, 8, 16)
RET_CHUNK = 128
ROPE_BASE = 10000.0
TOP_K = 8
N_GROUPS = 8
TOPK_GROUPS = 4
ROUTED_SCALE = 2.5
N_MOD = 6
EPS = 1e-6

V7X_VMEM_BYTES = 64 * 1024 * 1024
SUBLANES = 8
LANES = 128

PROMPT_STEP_ROWS = 512
SAMPLE_SEQS_PER_STEP = 8
SAMPLE_STEPS_PER_GROUP = 4
TOKEN_BLOCK = 256
DMA_ROWS = 16
SLOT_ROWS = 3 * DMA_ROWS
ONEHOT_ROWS = 256
SLOT_CHUNK_EXPERTS = 16
EXPERT_TILE = 512
EXPERT_TILE_PARTS = 2
RING_SLOTS = 4
POOL_PREV = 16


def _vmem_limit(nbytes):
    return int(min(nbytes, V7X_VMEM_BYTES - 8 * 1024 * 1024))


def _dot(a, b):
    return jnp.dot(a, b, preferred_element_type=F32)


def _dot_nt(a, b):
    return lax.dot_general(a, b, (((1,), (1,)), ((), ())), preferred_element_type=F32)


def _dot_tn(a, b):
    return lax.dot_general(a, b, (((0,), (0,)), ((), ())), preferred_element_type=F32)


def _split_bf16(x, parts):
    out = []
    for _ in range(parts):
        p = x.astype(BF16)
        out.append(p)
        x = x - p.astype(F32)
    return out


def _rms_mod(x, g, shift, scale):
    y = x * lax.rsqrt(jnp.mean(x * x, axis=-1, keepdims=True) + EPS)
    return (y * g) * (1.0 + scale) + shift


def _rotary(x, cosb, sinb):
    return x * cosb + pltpu.roll(x, x.shape[-1] // 2, axis=1) * sinb


def _group_norm(o):
    return o * lax.rsqrt(jnp.mean(o * o, axis=-1, keepdims=True) + EPS)


def _ada_kernel(cp_ref, ct_ref, w_ref, b_ref, modp_ref, modt_ref):
    w = w_ref[...].astype(BF16)
    modp_ref[...] = _dot(cp_ref[...].astype(BF16), w) + b_ref[...]
    modt_ref[0] = _dot(ct_ref[...].astype(BF16), w) + b_ref[...]


def _ada(c_prompt, c_tokens, w_ada, b_ada):
    bp, d = c_prompt.shape
    ts = c_tokens.shape[0]
    n = w_ada.shape[1]
    return pl.pallas_call(
        _ada_kernel,
        out_shape=(jax.ShapeDtypeStruct((bp, n), F32), jax.ShapeDtypeStruct((n // d, ts, d), F32)),
        grid=(n // d,),
        in_specs=[
            pl.BlockSpec((bp, d), lambda j: (0, 0)),
            pl.BlockSpec((ts, d), lambda j: (0, 0)),
            pl.BlockSpec((d, d), lambda j: (0, j)),
            pl.BlockSpec((1, d), lambda j: (0, j)),
        ],
        out_specs=(pl.BlockSpec((bp, d), lambda j: (0, j)),
                   pl.BlockSpec((1, ts, d), lambda j: (j, 0, 0))),
        compiler_params=pltpu.CompilerParams(
            dimension_semantics=("arbitrary",),
            vmem_limit_bytes=_vmem_limit(32 * 1024 * 1024),
        ),
        name="ada",
    )(c_prompt, c_tokens, w_ada, b_ada.reshape(1, n))


class _Dims:
    def __init__(self, d_model, d_pool, heads, dk, dv):
        self.d = d_model
        self.d_pool = d_pool
        self.gd = d_pool // len(POOL_WINDOWS)
        self.heads = heads
        self.dk = dk
        self.dv = dv
        self.d_k = heads * dk
        self.d_v = heads * dv
        widths = (d_pool, self.d_k, self.d_k, self.d_v, self.d_v, d_model, d_model)
        offs = [0]
        for w in widths:
            offs.append(offs[-1] + w)
        self.cols = tuple((offs[i], offs[i + 1]) for i in range(len(widths)))
        self.d_in = offs[-1]


def _proj(hb, win_ref, cols):
    return _dot(hb, win_ref[:, cols[0]:cols[1]])


def _pool_branch(pm_parts, wpg_ref, ps_ref, wbp_ref):
    pg = [_dot(pm.astype(BF16), wpg_ref[gi]) for gi, pm in enumerate(pm_parts)]
    pm = jnp.concatenate(pg, axis=1) * ps_ref[...]
    return _dot(pm.astype(BF16), wbp_ref[...])


def _merge_out(x, gate, a, o, g, ga, gb, wbr_ref, wout_ref):
    r = _dot((jax.nn.silu(g) * o).astype(BF16), wbr_ref[...])
    merged = jax.nn.sigmoid(ga) * a + jax.nn.sigmoid(gb) * r
    m = _dot(merged.astype(BF16), wout_ref[...])
    return x + gate * m


def _mixer_prompt_kernel(dims, x_ref, mod_ref, g1_ref, win_ref, wpg_ref, ps_ref, wbp_ref,
                         wbr_ref, wout_ref, cos_ref, sin_ref, intra_ref, qdec_ref,
                         kdec_ref, sdec_ref, g2_ref, wr_ref, br_ref,
                         x1_ref, pool_ref, ret_ref, h2_ref, pos_ref, wts_ref, cnt_ref, rank_ref,
                         gates_ref, uext_ref, hnext_ref):
    c = pl.program_id(1)
    L = x_ref.shape[1]

    @pl.when(jnp.logical_and(pl.program_id(0) == 0, c == 0))
    def _init_route():
        hnext_ref[...] = jnp.zeros_like(hnext_ref)

    @pl.when(c == 0)
    def _init():
        ret_ref[...] = jnp.zeros_like(ret_ref)
        uext_ref[0:POOL_PREV, :] = jnp.zeros((POOL_PREV, dims.d_pool), F32)

    def route_prev(blk):
        _route_block(hnext_ref[blk * TOKEN_BLOCK:(blk + 1) * TOKEN_BLOCK, :], blk,
                     wr_ref, br_ref, h2_ref, pos_ref, wts_ref, cnt_ref, rank_ref, gates_ref)

    for blk in range(L // TOKEN_BLOCK):
        route_prev(blk)
    x = x_ref[0]
    shift, scale, gate = mod_ref[0, 0:1, :], mod_ref[0, 1:2, :], mod_ref[0, 2:3, :]
    hb = _rms_mod(x, g1_ref[...], shift, scale).astype(BF16)

    u = _proj(hb, win_ref, dims.cols[0])
    uext_ref[POOL_PREV:POOL_PREV + L, :] = u
    pos = c * L + lax.broadcasted_iota(jnp.int32, (L, 1), 0)
    pm_parts = []
    for gi, w in enumerate(POOL_WINDOWS):
        s = uext_ref[:, gi * dims.gd:(gi + 1) * dims.gd]
        sh = 1
        while sh < w:
            s = s + pltpu.roll(s, sh, axis=0)
            sh *= 2
        cnt = jnp.minimum(w, pos + 1).astype(F32)
        pm_parts.append(s[POOL_PREV:, :] / cnt - u[:, gi * dims.gd:(gi + 1) * dims.gd])
    a = _pool_branch(pm_parts, wpg_ref, ps_ref, wbp_ref)
    uext_ref[0:POOL_PREV, :] = uext_ref[L:L + POOL_PREV, :]
    pool_ref[0] = uext_ref[0:POOL_PREV, :]

    q = _proj(hb, win_ref, dims.cols[1])
    k = _proj(hb, win_ref, dims.cols[2])
    v = _proj(hb, win_ref, dims.cols[3])
    chunk_outs = []
    for r0 in range(0, L, RET_CHUNK):
        rows = slice(r0, r0 + RET_CHUNK)
        cosb, sinb = cos_ref[rows, :], sin_ref[rows, :]
        outs = []
        for h in range(dims.heads):
            qr = _rotary(q[rows, h * dims.dk:(h + 1) * dims.dk], cosb, sinb)
            kr = _rotary(k[rows, h * dims.dk:(h + 1) * dims.dk], cosb, sinb) * (dims.dk ** -0.5)
            vb = v[rows, h * dims.dv:(h + 1) * dims.dv].astype(BF16)
            scores = _dot_nt(qr.astype(BF16), kr.astype(BF16)) * intra_ref[h]
            s_prev = ret_ref[0, 0, h]
            o = _dot(scores.astype(BF16), vb) + _dot((qr * qdec_ref[h]).astype(BF16),
                                                      s_prev.astype(BF16))
            ret_ref[0, 0, h] = (sdec_ref[h] * s_prev
                                + _dot_tn((kr * kdec_ref[h]).astype(BF16), vb))
            outs.append(_group_norm(o))
        chunk_outs.append(jnp.concatenate(outs, axis=1))
    o = jnp.concatenate(chunk_outs, axis=0)

    g = _proj(hb, win_ref, dims.cols[4])
    ga = _proj(hb, win_ref, dims.cols[5])
    gb = _proj(hb, win_ref, dims.cols[6])
    x1 = _merge_out(x, gate, a, o, g, ga, gb, wbr_ref, wout_ref)
    x1_ref[0] = x1
    hnext_ref[...] = _rms_mod(x1, g2_ref[...], mod_ref[0, 3:4, :], mod_ref[0, 4:5, :])


def _const_spec(shape):
    zeros = (0,) * len(shape)
    return pl.BlockSpec(shape, lambda *_: zeros)


def _resident_spec(shape):
    zeros = (0,) * len(shape)
    return pl.BlockSpec(shape, lambda *_: zeros, pipeline_mode=pl.Buffered(1))


def _mixer_prompt(dims, x, mod, g1, weights, tables, router, t_total):
    b, s, d = x.shape
    L = PROMPT_STEP_ROWS
    nc = s // L
    win, wpg, ps, wbp, wbr, wout = weights
    cosb, sinb, intra, qdec, kdec, sdec = tables
    g2, wr_t, br = router
    ne = wr_t.shape[0]
    in_specs = [
        pl.BlockSpec((1, L, d), lambda i, c: (i, c, 0)),
        pl.BlockSpec((1, N_MOD, d), lambda i, c: (i, 0, 0)),
        _const_spec(g1.shape), _resident_spec(win.shape), _const_spec(wpg.shape),
        _const_spec(ps.shape), _resident_spec(wbp.shape), _resident_spec(wbr.shape),
        _resident_spec(wout.shape),
        pl.BlockSpec((L, dims.dk), lambda i, c: (c, 0)),
        pl.BlockSpec((L, dims.dk), lambda i, c: (c, 0)),
        _const_spec(intra.shape), _const_spec(qdec.shape), _const_spec(kdec.shape),
        _const_spec(sdec.shape),
        _const_spec(g2.shape), _const_spec(wr_t.shape), _const_spec(br.shape),
    ]
    out_shape = (
        jax.ShapeDtypeStruct((b, s, d), F32),
        jax.ShapeDtypeStruct((b, POOL_PREV, dims.d_pool), F32),
        jax.ShapeDtypeStruct((1, b, dims.heads, dims.dk, dims.dv), F32),
    ) + _route_out_shapes(t_total, d, ne)
    out_specs = (
        pl.BlockSpec((1, L, d), lambda i, c: (i, c, 0)),
        pl.BlockSpec((1, POOL_PREV, dims.d_pool), lambda i, c: (i, 0, 0)),
        pl.BlockSpec((1, 1, dims.heads, dims.dk, dims.dv), lambda i, c: (0, i, 0, 0, 0)),
    ) + _route_out_specs(L, ne, d, lambda i, c: jnp.maximum(i * nc + c - 1, 0))
    weight_bytes = 2 * (win.size + wpg.size + wbp.size + wbr.size + wout.size)
    return pl.pallas_call(
        functools.partial(_mixer_prompt_kernel, dims),
        out_shape=out_shape,
        grid=(b, nc),
        in_specs=in_specs,
        out_specs=out_specs,
        scratch_shapes=[pltpu.VMEM((POOL_PREV + L, dims.d_pool), F32), pltpu.VMEM((L, d), F32)],
        compiler_params=pltpu.CompilerParams(
            dimension_semantics=("arbitrary", "arbitrary"),
            vmem_limit_bytes=_vmem_limit(2 * weight_bytes + 20 * 1024 * 1024),
        ),
        name="mixer_prompt",
    )(x, mod, g1, win, wpg, ps, wbp, wbr, wout, cosb, sinb, intra, qdec, kdec, sdec,
      g2, wr_t, br)


def _mixer_sample_kernel(dims, ds, x_ref, modt_ref, g1_ref, win_ref, wpg_ref, ps_ref, wbp_ref,
                         wbr_ref, wout_ref, cos_ref, sin_ref, intra_ref, qdec_ref, kdec_ref,
                         sdec_ref, bandp_ref, bandu_ref, cnt_ref, poolin_ref, sret_ref,
                         x1_ref, u_ref, ret_ref, hb_ref, q_ref, k_ref, v_ref, pm_ref, o_ref):
    j = pl.program_id(1)
    R = cos_ref.shape[0]
    ns = R // ds

    @pl.when(j == 0)
    def _project():
        hb = _rms_mod(x_ref[...], g1_ref[...], modt_ref[0], modt_ref[1]).astype(BF16)
        hb_ref[...] = hb
        u_ref[...] = _proj(hb, win_ref, dims.cols[0])
        q_ref[...] = _proj(hb, win_ref, dims.cols[1])
        k_ref[...] = _proj(hb, win_ref, dims.cols[2])
        v_ref[...] = _proj(hb, win_ref, dims.cols[3]).astype(BF16)

    rows = pl.ds(pl.multiple_of(j * R, R), R)

    u = u_ref[rows, :]
    hist = poolin_ref[...].reshape(ns * POOL_PREV, dims.d_pool)
    pm_parts = []
    for gi in range(len(POOL_WINDOWS)):
        lo, hi = gi * dims.gd, (gi + 1) * dims.gd
        win_sum = jnp.zeros((R, dims.gd), F32)
        for part in _split_bf16(hist[:, lo:hi], 3):
            win_sum = win_sum + _dot(bandp_ref[gi], part)
        for part in _split_bf16(u[:, lo:hi], 3):
            win_sum = win_sum + _dot(bandu_ref[gi], part)
        pm_parts.append(win_sum / cnt_ref[gi] - u[:, lo:hi])
    pm_ref[rows, :] = jnp.concatenate(pm_parts, axis=1)

    q = q_ref[rows, :]
    k = k_ref[rows, :]
    v = v_ref[rows, :]
    cosb, sinb = cos_ref[...], sin_ref[...]
    row_seq = lax.broadcasted_iota(jnp.int32, (R, ns * dims.dk), 0) // ds
    col_seq = lax.broadcasted_iota(jnp.int32, (R, ns * dims.dk), 1) // dims.dk
    own = row_seq == col_seq
    outs = []
    for h in range(dims.heads):
        qr = _rotary(q[:, h * dims.dk:(h + 1) * dims.dk], cosb, sinb)
        kr = _rotary(k[:, h * dims.dk:(h + 1) * dims.dk], cosb, sinb) * (dims.dk ** -0.5)
        vb = v[:, h * dims.dv:(h + 1) * dims.dv]
        scores = _dot_nt(qr.astype(BF16), kr.astype(BF16)) * intra_ref[h]
        s_prev = sret_ref[:, h].reshape(ns * dims.dk, dims.dv)
        qd = jnp.concatenate([qr * qdec_ref[h]] * ns, axis=1)
        kd = jnp.concatenate([kr * kdec_ref[h]] * ns, axis=1)
        qexp = jnp.where(own, qd, 0.0).astype(BF16)
        kexp = jnp.where(own, kd, 0.0).astype(BF16)
        o = _dot(scores.astype(BF16), vb) + _dot(qexp, s_prev.astype(BF16))
        s_new = sdec_ref[h] * s_prev + _dot_tn(kexp, vb)
        ret_ref[:, h] = s_new.reshape(ns, dims.dk, dims.dv)
        outs.append(_group_norm(o))
    o_ref[rows, :] = jnp.concatenate(outs, axis=1)

    @pl.when(j == pl.num_programs(1) - 1)
    def _merge():
        hb = hb_ref[...]
        pm = pm_ref[...]
        pm_parts = [pm[:, gi * dims.gd:(gi + 1) * dims.gd] for gi in range(len(POOL_WINDOWS))]
        a = _pool_branch(pm_parts, wpg_ref, ps_ref, wbp_ref)
        g = _proj(hb, win_ref, dims.cols[4])
        ga = _proj(hb, win_ref, dims.cols[5])
        gb = _proj(hb, win_ref, dims.cols[6])
        x1_ref[...] = _merge_out(x_ref[...], modt_ref[2], a, o_ref[...], g, ga, gb,
                                 wbr_ref, wout_ref)


def _mixer_sample(dims, ds, x2d, modt, g1, weights, tables, pool_in, state_ret):
    t, d = x2d.shape
    ns = SAMPLE_SEQS_PER_STEP
    steps = SAMPLE_STEPS_PER_GROUP
    R = ns * ds
    RG = R * steps
    win, wpg, ps, wbp, wbr, wout = weights
    cosb, sinb, intra, qdec, kdec, sdec, bandp, bandu, cnt = tables
    db = state_ret.shape[0]
    in_specs = [
        pl.BlockSpec((RG, d), lambda i, j: (i, 0)),
        pl.BlockSpec((N_MOD, RG, d), lambda i, j: (0, i, 0)),
        _const_spec(g1.shape), _resident_spec(win.shape), _const_spec(wpg.shape),
        _const_spec(ps.shape), _resident_spec(wbp.shape), _resident_spec(wbr.shape),
        _resident_spec(wout.shape), _const_spec(cosb.shape), _const_spec(sinb.shape),
        _const_spec(intra.shape), _const_spec(qdec.shape), _const_spec(kdec.shape),
        _const_spec(sdec.shape), _const_spec(bandp.shape), _const_spec(bandu.shape),
        _const_spec(cnt.shape),
        pl.BlockSpec((ns, POOL_PREV, dims.d_pool), lambda i, j: (i * steps + j, 0, 0)),
        pl.BlockSpec((ns, dims.heads, dims.dk, dims.dv), lambda i, j: (i * steps + j, 0, 0, 0)),
    ]
    out_shape = (
        jax.ShapeDtypeStruct((t, d), F32),
        jax.ShapeDtypeStruct((t, dims.d_pool), F32),
        jax.ShapeDtypeStruct(state_ret.shape, state_ret.dtype),
    )
    out_specs = (
        pl.BlockSpec((RG, d), lambda i, j: (i, 0)),
        pl.BlockSpec((RG, dims.d_pool), lambda i, j: (i, 0)),
        pl.BlockSpec((ns, dims.heads, dims.dk, dims.dv), lambda i, j: (i * steps + j, 0, 0, 0)),
    )
    weight_bytes = 2 * (win.size + wpg.size + wbp.size + wbr.size + wout.size)
    state_bytes = 4 * ns * dims.heads * dims.dk * dims.dv
    return pl.pallas_call(
        functools.partial(_mixer_sample_kernel, dims, ds),
        out_shape=out_shape,
        grid=(db // (ns * steps), steps),
        in_specs=in_specs,
        out_specs=out_specs,
        scratch_shapes=[
            pltpu.VMEM((RG, d), BF16), pltpu.VMEM((RG, dims.d_k), F32),
            pltpu.VMEM((RG, dims.d_k), F32), pltpu.VMEM((RG, dims.d_v), BF16),
            pltpu.VMEM((RG, dims.d_pool), F32), pltpu.VMEM((RG, dims.d_v), F32),
        ],
        compiler_params=pltpu.CompilerParams(
            dimension_semantics=("arbitrary", "arbitrary"),
            vmem_limit_bytes=_vmem_limit(2 * weight_bytes + 4 * state_bytes + 8 * 1024 * 1024),
        ),
        name="mixer_sample",
    )(x2d, modt, g1, win, wpg, ps, wbp, wbr, wout, cosb, sinb, intra, qdec, kdec, sdec,
      bandp, bandu, cnt, pool_in, state_ret)


def _rotary_tables(pos, dk):
    half = dk // 2
    inv = np.power(ROPE_BASE, -np.arange(half, dtype=np.float64) / half)
    ang = pos.astype(np.float64)[:, None] * inv[None, :]
    cos, sin = np.cos(ang), np.sin(ang)
    return (np.concatenate([cos, cos], axis=1).astype(np.float32),
            np.concatenate([-sin, sin], axis=1).astype(np.float32))


def _decay_tables(log_g, chunk, reps, dk, dv):
    idx = np.arange(chunk, dtype=np.float64)
    diff = idx[:, None] - idx[None, :]
    intra = np.where((diff >= 0)[None],
                     np.exp(log_g[:, None, None] * np.maximum(diff, 0.0)[None]), 0.0)
    heads = log_g.shape[0]
    eye = np.eye(reps)
    intra = np.einsum("ab,hij->haibj", eye, intra).reshape(heads, reps * chunk, reps * chunk)
    q_dec = np.exp(log_g[:, None] * (idx[None, :] + 1.0))
    k_dec = np.exp(log_g[:, None] * (chunk - 1.0 - idx[None, :]))
    s_dec = np.exp(log_g * chunk)
    q_dec = np.broadcast_to(np.tile(q_dec, (1, reps))[:, :, None], (heads, reps * chunk, dk))
    k_dec = np.broadcast_to(np.tile(k_dec, (1, reps))[:, :, None], (heads, reps * chunk, dk))
    s_dec = np.broadcast_to(s_dec[:, None, None], (heads, 1, dv))
    return tuple(np.ascontiguousarray(t, dtype=np.float32) for t in (intra, q_dec, k_dec, s_dec))


def _sample_pool_tables(ns, ds, gd, start_pos):
    n = np.arange(ds)
    i = np.arange(POOL_PREV)
    eye = np.eye(ns, dtype=np.float32)
    bandp, bandu, cnt = [], [], []
    for w in POOL_WINDOWS:
        bp = ((i[None, :] - POOL_PREV) >= (n[:, None] - w + 1)).astype(np.float32)
        bu = ((n[None, :] <= n[:, None]) & (n[None, :] >= n[:, None] - w + 1)).astype(np.float32)
        bandp.append(np.einsum("ab,ni->anbi", eye, bp).reshape(ns * ds, ns * POOL_PREV))
        bandu.append(np.einsum("ab,nm->anbm", eye, bu).reshape(ns * ds, ns * ds))
        c = np.minimum(w, start_pos + n + 1).astype(np.float32)
        cnt.append(np.broadcast_to(np.tile(c, ns)[:, None], (ns * ds, gd)))
    return (np.stack(bandp).astype(BF16), np.stack(bandu).astype(BF16),
            np.ascontiguousarray(np.stack(cnt)))


def _exclusive_prefix(x, axis):
    pos = lax.broadcasted_iota(jnp.int32, x.shape, axis)
    acc = x
    sh = 1
    while sh < x.shape[axis]:
        acc = acc + jnp.where(pos >= sh, pltpu.roll(acc, sh, axis=axis), 0.0)
        sh *= 2
    return acc - x


def _first_max_onehot(v, idx, big, axes):
    m = jnp.max(v, axis=axes, keepdims=True)
    first = jnp.min(jnp.where(v == m, idx, big), axis=axes, keepdims=True)
    return idx == first


def _route_block(h2, blk, wr_ref, br_ref, h2_ref, pos_ref, wts_ref, cnt_ref, rank_ref, gates_ref):
    tok = slice(blk * h2.shape[0], (blk + 1) * h2.shape[0])
    hh = h2.astype(BF16)
    h2_ref[tok, :] = hh

    hl = (h2 - hh.astype(F32)).astype(BF16)
    wh, wl = _split_bf16(wr_ref[...], 2)
    logits = _dot_nt(wh, hh) + (_dot_nt(wh, hl) + _dot_nt(wl, hh))
    s = jax.nn.sigmoid(logits)
    e, t = s.shape
    ge = e // N_GROUPS
    neg = -jnp.inf
    s3 = s.reshape(N_GROUPS, ge, t)
    sb3 = s3 + br_ref[...]

    j3 = lax.broadcasted_iota(jnp.int32, (N_GROUPS, ge, t), 1)
    top1 = _first_max_onehot(sb3, j3, ge, 1)
    m1 = jnp.max(sb3, axis=1, keepdims=True)
    m2 = jnp.max(jnp.where(top1, neg, sb3), axis=1, keepdims=True)
    gv = m1 + m2

    g3 = lax.broadcasted_iota(jnp.int32, (N_GROUPS, 1, t), 0)
    gsel = jnp.zeros((N_GROUPS, 1, t), F32)
    for _ in range(TOPK_GROUPS):
        hit = _first_max_onehot(gv, g3, N_GROUPS, 0)
        gsel = jnp.where(hit, 1.0, gsel)
        gv = jnp.where(hit, neg, gv)

    e3 = lax.broadcasted_iota(jnp.int32, (N_GROUPS, ge, t), 0) * ge + j3
    ev = jnp.where(gsel > 0.5, sb3, neg)
    sel = jnp.zeros((N_GROUPS, ge, t), F32)
    chosen = jnp.zeros((N_GROUPS, ge, t), F32)
    hits = []
    for _ in range(TOP_K):
        hit = _first_max_onehot(ev, e3, e, (0, 1))
        hits.append(hit)
        sel = jnp.where(hit, s3, sel)
        chosen = jnp.where(hit, 1.0, chosen)
        ev = jnp.where(hit, neg, ev)
    tot = jnp.sum(sel, axis=(0, 1), keepdims=True)
    gates = sel / tot * ROUTED_SCALE

    chosen2 = chosen.reshape(e, t)
    rank = _exclusive_prefix(chosen2, 1)
    cnt = jnp.sum(chosen2, axis=1, keepdims=True)
    ovf_units = jnp.floor((jnp.maximum(cnt - SLOT_ROWS, 0.0) + (DMA_ROWS - 1)) / DMA_ROWS)
    ovf_off = _exclusive_prefix(jnp.broadcast_to(ovf_units, (e, LANES)), 0)[:, 0:1]
    e_col = lax.broadcasted_iota(jnp.int32, (e, 1), 0).astype(F32)
    slot = jnp.where(rank < SLOT_ROWS, e_col * SLOT_ROWS + rank,
                     e * SLOT_ROWS + ovf_off * DMA_ROWS + (rank - SLOT_ROWS))
    slot3 = slot.reshape(N_GROUPS, ge, t)
    for k, hit in enumerate(hits):
        pos_k = jnp.sum(jnp.where(hit, slot3, 0.0), axis=(0, 1), keepdims=True)
        w_k = jnp.sum(jnp.where(hit, gates, 0.0), axis=(0, 1), keepdims=True)
        pos_ref[k:k + 1, tok] = pos_k.reshape(1, t).astype(jnp.int32)
        wts_ref[k:k + 1, tok] = w_k.reshape(1, t)
    cnt_ref[blk] = jnp.broadcast_to(cnt, (e, LANES)).astype(jnp.int32)
    rank_ref[:, tok] = jnp.where(chosen2 > 0.5, rank, -1.0).astype(jnp.int32)
    gates_ref[:, tok] = gates.reshape(e, t)


def _route_out_shapes(t, d, ne):
    return (jax.ShapeDtypeStruct((t, d), BF16),
            jax.ShapeDtypeStruct((TOP_K, t), jnp.int32),
            jax.ShapeDtypeStruct((TOP_K, t), F32),
            jax.ShapeDtypeStruct((t // TOKEN_BLOCK, ne, LANES), jnp.int32),
            jax.ShapeDtypeStruct((ne, t), jnp.int32),
            jax.ShapeDtypeStruct((ne, t), F32))


def _route_out_specs(rows, ne, d, idx):
    return (pl.BlockSpec((rows, d), lambda *g: (idx(*g), 0)),
            pl.BlockSpec((TOP_K, rows), lambda *g: (0, idx(*g))),
            pl.BlockSpec((TOP_K, rows), lambda *g: (0, idx(*g))),
            pl.BlockSpec((rows // TOKEN_BLOCK, ne, LANES), lambda *g: (idx(*g), 0, 0)),
            pl.BlockSpec((ne, rows), lambda *g: (0, idx(*g))),
            pl.BlockSpec((ne, rows), lambda *g: (0, idx(*g))))


def _route_rows_kernel(per_token, x_ref, mod_ref, g2_ref, wr_ref, br_ref, *refs):
    outs = refs[len(refs) // 2:]
    if per_token:
        shift, scale = mod_ref[3], mod_ref[4]
    else:
        shift, scale = mod_ref[0, 3:4, :], mod_ref[0, 4:5, :]
    h2 = _rms_mod(x_ref[...], g2_ref[...], shift, scale)
    _route_block(h2, 0, wr_ref, br_ref, *outs)


def _route_rows(x2d, first_row, n_rows, mod, per_token, g2, wr_t, br, routed, first_block):
    d = x2d.shape[1]
    tb = TOKEN_BLOCK
    ne = wr_t.shape[0]
    t = routed[0].shape[0]
    x_block0 = first_row // tb
    if per_token:
        mod_spec = pl.BlockSpec((N_MOD, tb, d), lambda i: (0, x_block0 + i, 0))
    else:
        mod_spec = pl.BlockSpec((1, N_MOD, d), lambda i: (0, 0, 0))
    n_in = 5
    return pl.pallas_call(
        functools.partial(_route_rows_kernel, per_token),
        out_shape=_route_out_shapes(t, d, ne),
        grid=(n_rows // tb,),
        in_specs=[
            pl.BlockSpec((tb, d), lambda i: (x_block0 + i, 0)),
            mod_spec,
            _const_spec(g2.shape), _const_spec(wr_t.shape), _const_spec(br.shape),
        ] + [pl.BlockSpec(memory_space=pl.ANY)] * len(routed),
        out_specs=_route_out_specs(tb, ne, d, lambda i: first_block + i),
        input_output_aliases={n_in + k: k for k in range(len(routed))},
        compiler_params=pltpu.CompilerParams(
            dimension_semantics=("arbitrary",),
            vmem_limit_bytes=_vmem_limit(32 * 1024 * 1024),
        ),
        name="route_rows",
    )(x2d, mod, g2, wr_t, br, *routed)


class _Layout:
    def __init__(self, counts):
        nb, ne = counts.shape
        u, g, tm = SLOT_ROWS, DMA_ROWS, EXPERT_TILE
        pc = (counts + g - 1) // g * g
        ovf_units = (jnp.maximum(counts - u, 0) + g - 1) // g
        rows_e = jnp.sum(pc, axis=0)
        region = (rows_e + u + tm - 1) // tm * tm
        goff = jnp.cumsum(region) - region
        dst = goff[None, :] + jnp.cumsum(pc, axis=0) - pc
        self.dst = dst.reshape(-1)
        ovf_end = jnp.cumsum(ovf_units, axis=1)
        ovf_first = ovf_end - ovf_units
        unit = jnp.arange(_overflow_units_cap(ne))[None, :, None]
        owns = (ovf_first[:, None, :] <= unit) & (unit < ovf_end[:, None, :])
        row = dst[:, None, :] + u + (unit - ovf_first[:, None, :]) * g
        self.ovf_dst = jnp.sum(jnp.where(owns, row, 0), axis=2).reshape(-1)
        self.ovf_tot = ovf_end[:, -1]
        self.tail_start = goff + rows_e
        self.tail_units = (region - rows_e) // g
        tiles = region // tm
        self.region_tiles = jnp.concatenate([tiles, jnp.sum(tiles, keepdims=True)])


def _overflow_units_cap(ne):
    return TOKEN_BLOCK * TOP_K // DMA_ROWS + ne


def _block_buffer_rows(tb, ne):
    rows = ne * SLOT_ROWS + tb * TOP_K + ne * (DMA_ROWS - 1)
    return (rows + ONEHOT_ROWS - 1) // ONEHOT_ROWS * ONEHOT_ROWS


def _expert_rows_cap(t, tb, ne):
    nb = t // tb
    rows = t * TOP_K + nb * ne * (DMA_ROWS - 1) + ne * (SLOT_ROWS + EXPERT_TILE - 1)
    return (rows + EXPERT_TILE - 1) // EXPERT_TILE * EXPERT_TILE


def _slot_copy(buf_ref, slot, e, hbm_ref, row, sem, to_hbm):
    src = buf_ref.at[slot, pl.ds(pl.multiple_of(e * SLOT_ROWS, DMA_ROWS), SLOT_ROWS)]
    dst = hbm_ref.at[pl.ds(pl.multiple_of(row, DMA_ROWS), SLOT_ROWS)]
    return pltpu.make_async_copy(src, dst, sem) if to_hbm else pltpu.make_async_copy(dst, src, sem)


def _unit_copy(buf_ref, slot, buf_row, hbm_ref, row, sem, to_hbm):
    src = buf_ref.at[slot, pl.ds(pl.multiple_of(buf_row, DMA_ROWS), DMA_ROWS)]
    dst = hbm_ref.at[pl.ds(pl.multiple_of(row, DMA_ROWS), DMA_ROWS)]
    return pltpu.make_async_copy(src, dst, sem) if to_hbm else pltpu.make_async_copy(dst, src, sem)


def _slots_wait(ne, buf_ref, slot, hbm_ref, sem, to_hbm):
    src = buf_ref.at[slot, pl.ds(0, ne * SLOT_ROWS)]
    dst = hbm_ref.at[pl.ds(0, ne * SLOT_ROWS)]
    c = pltpu.make_async_copy(src, dst, sem) if to_hbm else pltpu.make_async_copy(dst, src, sem)
    c.wait()


def _overflow_copies(ne, blk, slot, odst_ref, otot_ref, buf_ref, hbm_ref, sem, to_hbm, wait):
    def per_unit(i, carry):
        row = odst_ref[blk * _overflow_units_cap(ne) + i]
        c = _unit_copy(buf_ref, slot, ne * SLOT_ROWS + i * DMA_ROWS, hbm_ref, row, sem, to_hbm)
        c.wait() if wait else c.start()
        return carry

    lax.fori_loop(0, otot_ref[blk], per_unit, 0)


def _slot_chunks(ne):
    return [range(c, c + SLOT_CHUNK_EXPERTS) for c in range(0, ne, SLOT_CHUNK_EXPERTS)]


def _overflow_chunks(n_units):
    return lax.shift_right_logical(n_units * DMA_ROWS + (ONEHOT_ROWS - 1),
                                   ONEHOT_ROWS.bit_length() - 1)


def _overflow_select(pos, vals, r0):
    rows = lax.broadcasted_iota(jnp.int32, (ONEHOT_ROWS, pos.shape[1]), 0) + r0
    sel = jnp.zeros(rows.shape, F32)
    for k in range(TOP_K):
        v = 1.0 if vals is None else vals[k:k + 1, :]
        sel = jnp.where(pos[k:k + 1, :] == rows, v, sel)
    return sel


def _dispatch_kernel(ne, dst_ref, odst_ref, otot_ref, tails_ref, tailu_ref,
                     h2_ref, rank_ref, pos_ref, xs_hbm, buf_ref, zero_ref, sem, zsem):
    b = pl.program_id(0)
    nb = pl.num_programs(0)
    slot = b % 2
    h2 = h2_ref[...]
    j = lax.broadcasted_iota(jnp.int32, (SLOT_ROWS, h2.shape[0]), 0)

    chunks = _slot_chunks(ne)
    first_issue = max(len(chunks) - 2, 0)
    for ci, experts in enumerate(chunks):
        onehot = jnp.concatenate(
            [jnp.where(rank_ref[e:e + 1, :] == j, 1.0, 0.0) for e in experts], axis=0)
        r0 = experts[0] * SLOT_ROWS
        buf_ref[slot, r0:r0 + len(experts) * SLOT_ROWS, :] = (
            _dot(onehot.astype(BF16), h2).astype(BF16))
        if ci < first_issue:
            continue
        if ci == first_issue:
            @pl.when(b > 0)
            def _prev():
                _slots_wait(ne, buf_ref, 1 - slot, xs_hbm, sem.at[1 - slot], True)
                _overflow_copies(ne, b - 1, 1 - slot, odst_ref, otot_ref, buf_ref, xs_hbm,
                                 sem.at[1 - slot], True, True)
        ready = [e for c in chunks[:ci + 1] for e in c] if ci == first_issue else experts
        for e in ready:
            _slot_copy(buf_ref, slot, e, xs_hbm, dst_ref[b * ne + e], sem.at[slot], True).start()

    pos = pos_ref[...]

    def chunk(ci, carry):
        r0 = pl.multiple_of(ne * SLOT_ROWS + ci * ONEHOT_ROWS, ONEHOT_ROWS)
        onehot = _overflow_select(pos, None, r0).astype(BF16)
        buf_ref[slot, pl.ds(r0, ONEHOT_ROWS), :] = _dot(onehot, h2).astype(BF16)
        return carry

    lax.fori_loop(0, _overflow_chunks(otot_ref[b]), chunk, 0)
    _overflow_copies(ne, b, slot, odst_ref, otot_ref, buf_ref, xs_hbm, sem.at[slot], True, False)

    @pl.when(b == nb - 1)
    def _last():
        _slots_wait(ne, buf_ref, slot, xs_hbm, sem.at[slot], True)
        _overflow_copies(ne, b, slot, odst_ref, otot_ref, buf_ref, xs_hbm, sem.at[slot], True,
                         True)
        zero_ref[...] = jnp.zeros_like(zero_ref)

        def tail(wait):
            def per_expert(e, carry):
                def per_unit(i, carry2):
                    row = pl.multiple_of(tails_ref[e] + i * DMA_ROWS, DMA_ROWS)
                    c = pltpu.make_async_copy(zero_ref, xs_hbm.at[pl.ds(row, DMA_ROWS)], zsem)
                    c.wait() if wait else c.start()
                    return carry2
                lax.fori_loop(0, tailu_ref[e], per_unit, 0)
                return carry
            lax.fori_loop(0, ne, per_expert, 0)

        tail(False)
        tail(True)


def _dispatch(h2, rank, pos, lay, ne, rows_cap):
    t, d = h2.shape
    tb = TOKEN_BLOCK
    buf_rows = _block_buffer_rows(tb, ne)
    return pl.pallas_call(
        functools.partial(_dispatch_kernel, ne),
        out_shape=jax.ShapeDtypeStruct((rows_cap, d), BF16),
        grid_spec=pltpu.PrefetchScalarGridSpec(
            num_scalar_prefetch=5,
            grid=(t // tb,),
            in_specs=[
                pl.BlockSpec((tb, d), lambda i, *_: (i, 0)),
                pl.BlockSpec((ne, tb), lambda i, *_: (0, i)),
                pl.BlockSpec((TOP_K, tb), lambda i, *_: (0, i)),
            ],
            out_specs=pl.BlockSpec(memory_space=pl.ANY),
            scratch_shapes=[
                pltpu.VMEM((2, buf_rows, d), BF16),
                pltpu.VMEM((DMA_ROWS, d), BF16),
                pltpu.SemaphoreType.DMA((2,)),
                pltpu.SemaphoreType.DMA(()),
            ],
        ),
        compiler_params=pltpu.CompilerParams(
            dimension_semantics=("arbitrary",),
            vmem_limit_bytes=_vmem_limit(4 * buf_rows * d + 16 * 1024 * 1024),
        ),
        name="dispatch",
    )(lay.dst, lay.ovf_dst, lay.ovf_tot, lay.tail_start, lay.tail_units, h2, rank, pos)


def _experts_kernel(tiles_ref, xe_hbm, wg_ref, wu_ref, wd_ref, ye_hbm,
                    xbuf_ref, ybuf_ref, wgb_ref, wub_ref, wdb_ref, done_ref, xsem, ysem):
    e = pl.program_id(0)
    ne = pl.num_programs(0)
    tm = xbuf_ref.shape[1]

    def x_copy(row, slot):
        return pltpu.make_async_copy(xe_hbm.at[pl.ds(pl.multiple_of(row, tm), tm)],
                                     xbuf_ref.at[slot], xsem.at[slot])

    def y_copy(row, slot):
        return pltpu.make_async_copy(ybuf_ref.at[slot],
                                     ye_hbm.at[pl.ds(pl.multiple_of(row, tm), tm)], ysem.at[slot])

    total = tiles_ref[ne]

    @pl.when(e == 0)
    def _first():
        done_ref[0] = 0
        for g in range(RING_SLOTS - 1):
            x_copy(g * tm, g).start()

    wgb_ref[...] = wg_ref[0].astype(BF16)
    wub_ref[...] = wu_ref[0].astype(BF16)
    wdb_ref[...] = wd_ref[0].astype(BF16)
    done = done_ref[0]
    n = tiles_ref[e]

    def tile(j, carry):
        g = done + j
        slot = lax.rem(g, RING_SLOTS)
        x_copy(0, slot).wait()
        ahead = g + (RING_SLOTS - 1)

        @pl.when(ahead < total)
        def _prefetch():
            x_copy(ahead * tm, lax.rem(ahead, RING_SLOTS)).start()

        @pl.when(g >= RING_SLOTS)
        def _reuse():
            y_copy(0, slot).wait()

        part = tm // EXPERT_TILE_PARTS
        starts = range(0, tm, part)
        hids = []
        for r0 in starts:
            x = xbuf_ref[slot, r0:r0 + part, :]
            hids.append((jax.nn.silu(_dot(x, wgb_ref[...])) * _dot(x, wub_ref[...])).astype(BF16))
        for r0, hid in zip(starts, hids):
            ybuf_ref[slot, r0:r0 + part, :] = _dot(hid, wdb_ref[...]).astype(BF16)
        y_copy(g * tm, slot).start()
        return carry

    lax.fori_loop(0, n, tile, 0)
    done_ref[0] = done + n

    @pl.when(e == ne - 1)
    def _drain():
        for s in range(RING_SLOTS):
            y_copy(0, s).wait()


def _experts(xs, lay, wg, wu, wd):
    rows_cap, d = xs.shape
    ne, _, de = wg.shape
    assert ne >= RING_SLOTS
    tm = EXPERT_TILE
    ring = RING_SLOTS
    return pl.pallas_call(
        _experts_kernel,
        out_shape=jax.ShapeDtypeStruct((rows_cap, d), BF16),
        grid_spec=pltpu.PrefetchScalarGridSpec(
            num_scalar_prefetch=1,
            grid=(ne,),
            in_specs=[
                pl.BlockSpec(memory_space=pl.ANY),
                pl.BlockSpec((1, d, de), lambda e, *_: (e, 0, 0)),
                pl.BlockSpec((1, d, de), lambda e, *_: (e, 0, 0)),
                pl.BlockSpec((1, de, d), lambda e, *_: (e, 0, 0)),
            ],
            out_specs=pl.BlockSpec(memory_space=pl.ANY),
            scratch_shapes=[
                pltpu.VMEM((ring, tm, d), BF16), pltpu.VMEM((ring, tm, d), BF16),
                pltpu.VMEM((d, de), BF16), pltpu.VMEM((d, de), BF16), pltpu.VMEM((de, d), BF16),
                pltpu.SMEM((1,), jnp.int32),
                pltpu.SemaphoreType.DMA((ring,)), pltpu.SemaphoreType.DMA((ring,)),
            ],
        ),
        compiler_params=pltpu.CompilerParams(
            dimension_semantics=("arbitrary",),
            vmem_limit_bytes=_vmem_limit(32 * 1024 * 1024),
        ),
        name="experts",
    )(lay.region_tiles, xs, wg, wu, wd)


def _final_kernel(n_prompt_blocks, ne, dst_ref, odst_ref, otot_ref,
                  xp_ref, xs_ref, modp_ref, modt_ref, h2_ref, rank_ref, gates_ref, pos_ref, wts_ref,
                  wsg_ref, wsu_ref, wsd_ref, gf_ref, ye_hbm, yp_ref, ys_ref,
                  buf_ref, acc_ref, sem):
    i = pl.program_id(0)
    nb = pl.num_programs(0)
    slot = i % 2

    def fetch_slots(blk, sl, experts):
        for e in experts:
            _slot_copy(buf_ref, sl, e, ye_hbm, dst_ref[blk * ne + e], sem.at[sl], False).start()

    def overflow(blk, sl, wait):
        _overflow_copies(ne, blk, sl, odst_ref, otot_ref, buf_ref, ye_hbm, sem.at[sl], False, wait)

    @pl.when(i == 0)
    def _first():
        buf_ref[...] = jnp.zeros_like(buf_ref)
        fetch_slots(0, 0, range(ne))
        overflow(0, 0, False)

    @pl.when(i + 1 < nb)
    def _prefetch():
        fetch_slots(i + 1, 1 - slot, range(ne))
        overflow(i + 1, 1 - slot, False)

    _slots_wait(ne, buf_ref, slot, ye_hbm, sem.at[slot], False)
    overflow(i, slot, True)

    tb = h2_ref.shape[0]
    j = lax.broadcasted_iota(jnp.int32, (SLOT_ROWS, tb), 0)
    acc = jnp.zeros((tb, h2_ref.shape[1]), F32)
    for experts in _slot_chunks(ne):
        sel = jnp.concatenate(
            [jnp.where(rank_ref[e:e + 1, :] == j, gates_ref[e:e + 1, :], 0.0) for e in experts],
            axis=0)
        r0 = experts[0] * SLOT_ROWS
        acc = acc + _dot_tn(sel.astype(BF16), buf_ref[slot, r0:r0 + len(experts) * SLOT_ROWS, :])
    acc_ref[...] = acc

    pos = pos_ref[...]
    wts = wts_ref[...]

    def chunk(ci, carry):
        r0 = pl.multiple_of(ne * SLOT_ROWS + ci * ONEHOT_ROWS, ONEHOT_ROWS)
        sel = _overflow_select(pos, wts, r0).astype(BF16)
        acc_ref[...] += _dot_tn(sel, buf_ref[slot, pl.ds(r0, ONEHOT_ROWS), :])
        return carry

    lax.fori_loop(0, _overflow_chunks(otot_ref[i]), chunk, 0)

    is_p = i < n_prompt_blocks
    x = jnp.where(is_p, xp_ref[...], xs_ref[...])
    gate = jnp.where(is_p, modp_ref[0, 5:6, :], modt_ref[5])
    hb = h2_ref[...]
    hid = jax.nn.silu(_dot(hb, wsg_ref[...])) * _dot(hb, wsu_ref[...])
    f = acc_ref[...] + _dot(hid.astype(BF16), wsd_ref[...])
    x2 = x + gate * f
    y = x2 * lax.rsqrt(jnp.mean(x2 * x2, axis=-1, keepdims=True) + EPS) * gf_ref[...]

    @pl.when(is_p)
    def _p():
        yp_ref[...] = y

    @pl.when(jnp.logical_not(is_p))
    def _s():
        ys_ref[...] = y


def _final(xp2d, xs2d, modp, modt, h2, rank, gates, pos, wts, ye, lay, ne, wsg, wsu, wsd, gf):
    tp, d = xp2d.shape
    ts = xs2d.shape[0]
    tb = TOKEN_BLOCK
    npb, nsb = tp // tb, ts // tb
    seq = tp // modp.shape[0]
    bpb = seq // tb
    buf_rows = _block_buffer_rows(tb, ne)

    def p_idx(i):
        return jnp.minimum(i, npb - 1)

    def s_idx(i):
        return jnp.maximum(i - npb, 0)

    def const(shape):
        zeros = (0,) * len(shape)
        return pl.BlockSpec(shape, lambda i, *_: zeros)

    return pl.pallas_call(
        functools.partial(_final_kernel, npb, ne),
        out_shape=(jax.ShapeDtypeStruct((tp, d), F32), jax.ShapeDtypeStruct((ts, d), F32)),
        grid_spec=pltpu.PrefetchScalarGridSpec(
            num_scalar_prefetch=3,
            grid=(npb + nsb,),
            in_specs=[
                pl.BlockSpec((tb, d), lambda i, *_: (p_idx(i), 0)),
                pl.BlockSpec((tb, d), lambda i, *_: (s_idx(i), 0)),
                pl.BlockSpec((1, N_MOD, d), lambda i, *_: (p_idx(i) // bpb, 0, 0)),
                pl.BlockSpec((N_MOD, tb, d), lambda i, *_: (0, s_idx(i), 0)),
                pl.BlockSpec((tb, d), lambda i, *_: (i, 0)),
                pl.BlockSpec((ne, tb), lambda i, *_: (0, i)),
                pl.BlockSpec((ne, tb), lambda i, *_: (0, i)),
                pl.BlockSpec((TOP_K, tb), lambda i, *_: (0, i)),
                pl.BlockSpec((TOP_K, tb), lambda i, *_: (0, i)),
                const(wsg.shape), const(wsu.shape), const(wsd.shape), const(gf.shape),
                pl.BlockSpec(memory_space=pl.ANY),
            ],
            out_specs=(pl.BlockSpec((tb, d), lambda i, *_: (p_idx(i), 0)),
                       pl.BlockSpec((tb, d), lambda i, *_: (s_idx(i), 0))),
            scratch_shapes=[
                pltpu.VMEM((2, buf_rows, d), BF16),
                pltpu.VMEM((tb, d), F32),
                pltpu.SemaphoreType.DMA((2,)),
            ],
        ),
        compiler_params=pltpu.CompilerParams(
            dimension_semantics=("arbitrary",),
            vmem_limit_bytes=_vmem_limit(4 * buf_rows * d + 32 * 1024 * 1024),
        ),
        name="final",
    )(lay.dst, lay.ovf_dst, lay.ovf_tot,
      xp2d, xs2d, modp, modt, h2, rank, gates, pos, wts, wsg, wsu, wsd, gf, ye)


def kernel(x_prompt, x_sample, c_prompt, c_sample, state_pool, state_ret, w_ada, b_ada, g_norm1,
           w_in, w_pool_group, pool_scale, w_branch_pool, w_branch_ret, w_out, g_norm2, w_router,
           b_router, w_exp_gate, w_exp_up, w_exp_down, w_sh_gate, w_sh_up, w_sh_down, g_final):
    b, s, d = x_prompt.shape
    db, ds, _ = x_sample.shape
    depth = w_ada.shape[0]
    heads, dk, dv = state_ret.shape[2:]
    pool_buf, d_pool = state_pool.shape[2:]
    dims = _Dims(d, d_pool, heads, dk, dv)
    assert dims.d_in == w_in.shape[2]
    assert pool_buf == max(POOL_WINDOWS) - 1 and pool_buf < POOL_PREV
    assert s % PROMPT_STEP_ROWS == 0 and PROMPT_STEP_ROWS % RET_CHUNK == 0
    assert s % TOKEN_BLOCK == 0 and (db * ds) % TOKEN_BLOCK == 0
    assert db % (SAMPLE_SEQS_PER_STEP * SAMPLE_STEPS_PER_GROUP) == 0
    assert SLOT_ROWS % DMA_ROWS == 0 and (w_router.shape[2] * SLOT_ROWS) % ONEHOT_ROWS == 0

    log_g = np.log(1.0 - np.power(2.0, -5.0 - np.arange(heads, dtype=np.float64)))
    ns = SAMPLE_SEQS_PER_STEP
    cos_p, sin_p = _rotary_tables(np.arange(s), dk)
    cos_s, sin_s = _rotary_tables(PAST_LEN + np.arange(ds), dk)
    cos_s, sin_s = np.tile(cos_s, (ns, 1)), np.tile(sin_s, (ns, 1))
    dec_p = _decay_tables(log_g, RET_CHUNK, 1, dk, dv)
    dec_s = _decay_tables(log_g, ds, ns, dk, dv)
    pool_tabs = _sample_pool_tables(ns, ds, dims.gd, PAST_LEN)

    assert depth == 1, "the final kernel fuses the output norm into the (single) layer"
    l = 0
    xs = x_sample.reshape(db * ds, d)
    modp, modt = _ada(c_prompt, jnp.repeat(c_sample, ds, axis=0), w_ada[l], b_ada[l])
    modp = modp.reshape(b, N_MOD, d)
    g1 = g_norm1[l].reshape(1, d)
    weights = (w_in[l].astype(BF16), w_pool_group[l].astype(BF16),
               pool_scale[l].reshape(1, d_pool), w_branch_pool[l].astype(BF16),
               w_branch_ret[l].astype(BF16), w_out[l].astype(BF16))

    ne = w_router.shape[2]
    router = (g_norm2[l].reshape(1, d), w_router[l].T,
              b_router[l].reshape(N_GROUPS, ne // N_GROUPS, 1))
    x1p, pool16, ret_p, *routed = _mixer_prompt(dims, x_prompt, modp, g1, weights,
                                                (cos_p, sin_p) + dec_p, router,
                                                b * s + db * ds)
    pool_in = jnp.pad(state_pool[l], ((0, 0), (POOL_PREV - pool_buf, 0), (0, 0)))
    x1s, u_s, ret_s = _mixer_sample(dims, ds, xs, modt, g1, weights,
                                    (cos_s, sin_s) + dec_s + pool_tabs, pool_in,
                                    state_ret[l].astype(F32))
    pool_p = pool16[:, POOL_PREV - pool_buf:]
    pool_s = jnp.concatenate([state_pool[l], u_s.reshape(db, ds, d_pool)], axis=1)[:, -pool_buf:]

    x1p2d = x1p.reshape(b * s, d)
    last = b * s - PROMPT_STEP_ROWS
    routed = _route_rows(x1p2d, last, PROMPT_STEP_ROWS, modp[b - 1:b], False, *router, routed,
                         last // TOKEN_BLOCK)
    h2, pos, wts, counts, rank, gates = _route_rows(x1s, 0, db * ds, modt, True, *router, routed,
                                                    (b * s) // TOKEN_BLOCK)
    rows_cap = _expert_rows_cap(b * s + db * ds, TOKEN_BLOCK, ne)
    lay = _Layout(counts[:, :, 0])
    xe = _dispatch(h2, rank, pos, lay, ne, rows_cap)
    ye = _experts(xe, lay, w_exp_gate[l], w_exp_up[l], w_exp_down[l])
    yp, ys = _final(x1p2d, x1s, modp, modt, h2, rank, gates, pos, wts, ye, lay, ne,
                    w_sh_gate[l].astype(BF16), w_sh_up[l].astype(BF16),
                    w_sh_down[l].astype(BF16), g_final.reshape(1, d))

    return (yp.reshape(b, s, d), ys.reshape(db, ds, d), pool_p[None],
            ret_p.astype(x_prompt.dtype), pool_s[None], ret_s.astype(state_ret.dtype)[None])
```

```python
import functools

import jax
import jax.numpy as jnp
import numpy as np
from jax import lax
from jax.experimental import pallas as pl
from jax.experimental.pallas import tpu as pltpu

F32 = jnp.float32
BF16 = jnp.bfloat16

PAST_LEN = 16384
POOL_WINDOWS = (2, 4, 8, 16)
RET_CHUNK = 128
ROPE_BASE = 10000.0
TOP_K = 8
N_GROUPS = 8
TOPK_GROUPS = 4
ROUTED_SCALE = 2.5
N_MOD = 6
EPS = 1e-6

V7X_VMEM_BYTES = 64 * 1024 * 1024
LANES = 128

PROMPT_STEP_ROWS = 512
SAMPLE_SEQS_PER_STEP = 8
SAMPLE_STEPS_PER_GROUP = 4
TOKEN_BLOCK = 256
DMA_ROWS = 16
SLOT_ROWS = 3 * DMA_ROWS
ONEHOT_ROWS = 256
SLOT_CHUNK_EXPERTS = 16
EXPERT_TILE = 1024
EXPERT_TILE_PARTS = 4
RING_SLOTS = 4
POOL_PREV = 16


def _vmem_limit(nbytes):
    return int(min(nbytes, V7X_VMEM_BYTES - 8 * 1024 * 1024))


def _dot(a, b):
    return jnp.dot(a, b, preferred_element_type=F32)


def _dot_nt(a, b):
    return lax.dot_general(a, b, (((1,), (1,)), ((), ())), preferred_element_type=F32)


def _dot_tn(a, b):
    return lax.dot_general(a, b, (((0,), (0,)), ((), ())), preferred_element_type=F32)


def _split_bf16(x, parts):
    out = []
    for _ in range(parts):
        p = x.astype(BF16)
        out.append(p)
        x = x - p.astype(F32)
    return out


def _rms_mod(x, g, shift, scale):
    y = x * lax.rsqrt(jnp.mean(x * x, axis=-1, keepdims=True) + EPS)
    return (y * g) * (1.0 + scale) + shift


def _rotary(x, cosb, sinb):
    return x * cosb + pltpu.roll(x, x.shape[-1] // 2, axis=1) * sinb


def _group_norm(o):
    return o * lax.rsqrt(jnp.mean(o * o, axis=-1, keepdims=True) + EPS)


def _ada_kernel(cp_ref, ct_ref, w_ref, b_ref, modp_ref, modt_ref):
    w = w_ref[...].astype(BF16)
    modp_ref[...] = _dot(cp_ref[...].astype(BF16), w) + b_ref[...]
    modt_ref[0] = _dot(ct_ref[...].astype(BF16), w) + b_ref[...]


def _ada(c_prompt, c_tokens, w_ada, b_ada):
    bp, d = c_prompt.shape
    ts = c_tokens.shape[0]
    n = w_ada.shape[1]
    return pl.pallas_call(
        _ada_kernel,
        out_shape=(jax.ShapeDtypeStruct((bp, n), F32), jax.ShapeDtypeStruct((n // d, ts, d), F32)),
        grid=(n // d,),
        in_specs=[
            pl.BlockSpec((bp, d), lambda j: (0, 0)),
            pl.BlockSpec((ts, d), lambda j: (0, 0)),
            pl.BlockSpec((d, d), lambda j: (0, j)),
            pl.BlockSpec((1, d), lambda j: (0, j)),
        ],
        out_specs=(pl.BlockSpec((bp, d), lambda j: (0, j)),
                   pl.BlockSpec((1, ts, d), lambda j: (j, 0, 0))),
        compiler_params=pltpu.CompilerParams(
            dimension_semantics=("arbitrary",),
            vmem_limit_bytes=_vmem_limit(32 * 1024 * 1024),
        ),
        name="ada",
    )(c_prompt, c_tokens, w_ada, b_ada.reshape(1, n))


class _Dims:
    def __init__(self, d_model, d_pool, heads, dk, dv):
        self.d = d_model
        self.d_pool = d_pool
        self.gd = d_pool // len(POOL_WINDOWS)
        self.heads = heads
        self.dk = dk
        self.dv = dv
        self.d_k = heads * dk
        self.d_v = heads * dv
        widths = (d_pool, self.d_k, self.d_k, self.d_v, self.d_v, d_model, d_model)
        offs = [0]
        for w in widths:
            offs.append(offs[-1] + w)
        self.cols = tuple((offs[i], offs[i + 1]) for i in range(len(widths)))
        self.d_in = offs[-1]


def _proj(hb, win_ref, cols):
    return _dot(hb, win_ref[:, cols[0]:cols[1]])


def _pool_branch(pm_parts, wpg_ref, ps_ref, wbp_ref):
    pg = [_dot(pm.astype(BF16), wpg_ref[gi]) for gi, pm in enumerate(pm_parts)]
    pm = jnp.concatenate(pg, axis=1) * ps_ref[...]
    return _dot(pm.astype(BF16), wbp_ref[...])


def _merge_out(x, gate, a, o, g, ga, gb, wbr_ref, wout_ref):
    r = _dot((jax.nn.silu(g) * o).astype(BF16), wbr_ref[...])
    merged = jax.nn.sigmoid(ga) * a + jax.nn.sigmoid(gb) * r
    m = _dot(merged.astype(BF16), wout_ref[...])
    return x + gate * m


def _mixer_prompt_kernel(dims, x_ref, mod_ref, g1_ref, win_ref, wpg_ref, ps_ref, wbp_ref,
                         wbr_ref, wout_ref, cos_ref, sin_ref, intra_ref, qdec_ref,
                         kdec_ref, sdec_ref, g2_ref, wr_ref, br_ref,
                         x1_ref, pool_ref, ret_ref, h2_ref, pos_ref, wts_ref, cnt_ref, rank_ref,
                         gates_ref, uext_ref, hnext_ref):
    c = pl.program_id(1)
    L = x_ref.shape[1]

    @pl.when(jnp.logical_and(pl.program_id(0) == 0, c == 0))
    def _init_route():
        hnext_ref[...] = jnp.zeros_like(hnext_ref)

    @pl.when(c == 0)
    def _init():
        ret_ref[...] = jnp.zeros_like(ret_ref)
        uext_ref[0:POOL_PREV, :] = jnp.zeros((POOL_PREV, dims.d_pool), F32)

    def route_prev(blk):
        _route_block(hnext_ref[blk * TOKEN_BLOCK:(blk + 1) * TOKEN_BLOCK, :], blk,
                     wr_ref, br_ref, h2_ref, pos_ref, wts_ref, cnt_ref, rank_ref, gates_ref)

    for blk in range(L // TOKEN_BLOCK):
        route_prev(blk)
    x = x_ref[0]
    shift, scale, gate = mod_ref[0, 0:1, :], mod_ref[0, 1:2, :], mod_ref[0, 2:3, :]
    hb = _rms_mod(x, g1_ref[...], shift, scale).astype(BF16)

    u = _proj(hb, win_ref, dims.cols[0])
    uext_ref[POOL_PREV:POOL_PREV + L, :] = u
    pos = c * L + lax.broadcasted_iota(jnp.int32, (L, 1), 0)
    pm_parts = []
    for gi, w in enumerate(POOL_WINDOWS):
        s = uext_ref[:, gi * dims.gd:(gi + 1) * dims.gd]
        sh = 1
        while sh < w:
            s = s + pltpu.roll(s, sh, axis=0)
            sh *= 2
        cnt = jnp.minimum(w, pos + 1).astype(F32)
        pm_parts.append(s[POOL_PREV:, :] / cnt - u[:, gi * dims.gd:(gi + 1) * dims.gd])
    a = _pool_branch(pm_parts, wpg_ref, ps_ref, wbp_ref)
    uext_ref[0:POOL_PREV, :] = uext_ref[L:L + POOL_PREV, :]
    pool_ref[0] = uext_ref[0:POOL_PREV, :]

    q = _proj(hb, win_ref, dims.cols[1])
    k = _proj(hb, win_ref, dims.cols[2])
    v = _proj(hb, win_ref, dims.cols[3])
    chunk_outs = []
    for r0 in range(0, L, RET_CHUNK):
        rows = slice(r0, r0 + RET_CHUNK)
        cosb, sinb = cos_ref[rows, :], sin_ref[rows, :]
        outs = []
        for h in range(dims.heads):
            qr = _rotary(q[rows, h * dims.dk:(h + 1) * dims.dk], cosb, sinb)
            kr = _rotary(k[rows, h * dims.dk:(h + 1) * dims.dk], cosb, sinb) * (dims.dk ** -0.5)
            vb = v[rows, h * dims.dv:(h + 1) * dims.dv].astype(BF16)
            scores = _dot_nt(qr.astype(BF16), kr.astype(BF16)) * intra_ref[h]
            s_prev = ret_ref[0, 0, h]
            o = _dot(scores.astype(BF16), vb) + _dot((qr * qdec_ref[h]).astype(BF16),
                                                      s_prev.astype(BF16))
            ret_ref[0, 0, h] = (sdec_ref[h] * s_prev
                                + _dot_tn((kr * kdec_ref[h]).astype(BF16), vb))
            outs.append(_group_norm(o))
        chunk_outs.append(jnp.concatenate(outs, axis=1))
    o = jnp.concatenate(chunk_outs, axis=0)

    g = _proj(hb, win_ref, dims.cols[4])
    ga = _proj(hb, win_ref, dims.cols[5])
    gb = _proj(hb, win_ref, dims.cols[6])
    x1 = _merge_out(x, gate, a, o, g, ga, gb, wbr_ref, wout_ref)
    x1_ref[0] = x1
    hnext_ref[...] = _rms_mod(x1, g2_ref[...], mod_ref[0, 3:4, :], mod_ref[0, 4:5, :])


def _const_spec(shape):
    zeros = (0,) * len(shape)
    return pl.BlockSpec(shape, lambda *_: zeros)


def _resident_spec(shape):
    zeros = (0,) * len(shape)
    return pl.BlockSpec(shape, lambda *_: zeros, pipeline_mode=pl.Buffered(1))


def _mixer_prompt(dims, x, mod, g1, weights, tables, router, t_total):
    b, s, d = x.shape
    L = PROMPT_STEP_ROWS
    nc = s // L
    win, wpg, ps, wbp, wbr, wout = weights
    cosb, sinb, intra, qdec, kdec, sdec = tables
    g2, wr_t, br = router
    ne = wr_t.shape[0]
    in_specs = [
        pl.BlockSpec((1, L, d), lambda i, c: (i, c, 0)),
        pl.BlockSpec((1, N_MOD, d), lambda i, c: (i, 0, 0)),
        _const_spec(g1.shape), _resident_spec(win.shape), _const_spec(wpg.shape),
        _const_spec(ps.shape), _resident_spec(wbp.shape), _resident_spec(wbr.shape),
        _resident_spec(wout.shape),
        pl.BlockSpec((L, dims.dk), lambda i, c: (c, 0)),
        pl.BlockSpec((L, dims.dk), lambda i, c: (c, 0)),
        _const_spec(intra.shape), _const_spec(qdec.shape), _const_spec(kdec.shape),
        _const_spec(sdec.shape),
        _const_spec(g2.shape), _const_spec(wr_t.shape), _const_spec(br.shape),
    ]
    out_shape = (
        jax.ShapeDtypeStruct((b, s, d), F32),
        jax.ShapeDtypeStruct((b, POOL_PREV, dims.d_pool), F32),
        jax.ShapeDtypeStruct((1, b, dims.heads, dims.dk, dims.dv), F32),
    ) + _route_out_shapes(t_total, d, ne)
    out_specs = (
        pl.BlockSpec((1, L, d), lambda i, c: (i, c, 0)),
        pl.BlockSpec((1, POOL_PREV, dims.d_pool), lambda i, c: (i, 0, 0)),
        pl.BlockSpec((1, 1, dims.heads, dims.dk, dims.dv), lambda i, c: (0, i, 0, 0, 0)),
    ) + _route_out_specs(L, ne, d, lambda i, c: jnp.maximum(i * nc + c - 1, 0))
    weight_bytes = 2 * (win.size + wpg.size + wbp.size + wbr.size + wout.size)
    return pl.pallas_call(
        functools.partial(_mixer_prompt_kernel, dims),
        out_shape=out_shape,
        grid=(b, nc),
        in_specs=in_specs,
        out_specs=out_specs,
        scratch_shapes=[pltpu.VMEM((POOL_PREV + L, dims.d_pool), F32), pltpu.VMEM((L, d), F32)],
        compiler_params=pltpu.CompilerParams(
            dimension_semantics=("arbitrary", "arbitrary"),
            vmem_limit_bytes=_vmem_limit(2 * weight_bytes + 20 * 1024 * 1024),
        ),
        name="mixer_prompt",
    )(x, mod, g1, win, wpg, ps, wbp, wbr, wout, cosb, sinb, intra, qdec, kdec, sdec,
      g2, wr_t, br)


def _mixer_sample_kernel(dims, ds, x_ref, modt_ref, g1_ref, win_ref, wpg_ref, ps_ref, wbp_ref,
                         wbr_ref, wout_ref, cos_ref, sin_ref, intra_ref, qdec_ref, kdec_ref,
                         sdec_ref, bandp_ref, bandu_ref, cnt_ref, poolin_ref, sret_ref,
                         x1_ref, u_ref, ret_ref, hb_ref, q_ref, k_ref, v_ref, pm_ref, o_ref):
    j = pl.program_id(1)
    R = cos_ref.shape[0]
    ns = R // ds

    @pl.when(j == 0)
    def _project():
        hb = _rms_mod(x_ref[...], g1_ref[...], modt_ref[0], modt_ref[1]).astype(BF16)
        hb_ref[...] = hb
        u_ref[...] = _proj(hb, win_ref, dims.cols[0])
        q_ref[...] = _proj(hb, win_ref, dims.cols[1])
        k_ref[...] = _proj(hb, win_ref, dims.cols[2])
        v_ref[...] = _proj(hb, win_ref, dims.cols[3]).astype(BF16)

    rows = pl.ds(pl.multiple_of(j * R, R), R)

    u = u_ref[rows, :]
    hist = poolin_ref[...].reshape(ns * POOL_PREV, dims.d_pool)
    pm_parts = []
    for gi in range(len(POOL_WINDOWS)):
        lo, hi = gi * dims.gd, (gi + 1) * dims.gd
        win_sum = jnp.zeros((R, dims.gd), F32)
        for part in _split_bf16(hist[:, lo:hi], 3):
            win_sum = win_sum + _dot(bandp_ref[gi], part)
        for part in _split_bf16(u[:, lo:hi], 3):
            win_sum = win_sum + _dot(bandu_ref[gi], part)
        pm_parts.append(win_sum / cnt_ref[gi] - u[:, lo:hi])
    pm_ref[rows, :] = jnp.concatenate(pm_parts, axis=1)

    q = q_ref[rows, :]
    k = k_ref[rows, :]
    v = v_ref[rows, :]
    cosb, sinb = cos_ref[...], sin_ref[...]
    row_seq = lax.broadcasted_iota(jnp.int32, (R, ns * dims.dk), 0) // ds
    col_seq = lax.broadcasted_iota(jnp.int32, (R, ns * dims.dk), 1) // dims.dk
    own = row_seq == col_seq
    outs = []
    for h in range(dims.heads):
        qr = _rotary(q[:, h * dims.dk:(h + 1) * dims.dk], cosb, sinb)
        kr = _rotary(k[:, h * dims.dk:(h + 1) * dims.dk], cosb, sinb) * (dims.dk ** -0.5)
        vb = v[:, h * dims.dv:(h + 1) * dims.dv]
        scores = _dot_nt(qr.astype(BF16), kr.astype(BF16)) * intra_ref[h]
        s_prev = sret_ref[:, h].reshape(ns * dims.dk, dims.dv)
        qd = jnp.concatenate([qr * qdec_ref[h]] * ns, axis=1)
        kd = jnp.concatenate([kr * kdec_ref[h]] * ns, axis=1)
        qexp = jnp.where(own, qd, 0.0).astype(BF16)
        kexp = jnp.where(own, kd, 0.0).astype(BF16)
        o = _dot(scores.astype(BF16), vb) + _dot(qexp, s_prev.astype(BF16))
        s_new = sdec_ref[h] * s_prev + _dot_tn(kexp, vb)
        ret_ref[:, h] = s_new.reshape(ns, dims.dk, dims.dv)
        outs.append(_group_norm(o))
    o_ref[rows, :] = jnp.concatenate(outs, axis=1)

    @pl.when(j == pl.num_programs(1) - 1)
    def _merge():
        hb = hb_ref[...]
        pm = pm_ref[...]
        pm_parts = [pm[:, gi * dims.gd:(gi + 1) * dims.gd] for gi in range(len(POOL_WINDOWS))]
        a = _pool_branch(pm_parts, wpg_ref, ps_ref, wbp_ref)
        g = _proj(hb, win_ref, dims.cols[4])
        ga = _proj(hb, win_ref, dims.cols[5])
        gb = _proj(hb, win_ref, dims.cols[6])
        x1_ref[...] = _merge_out(x_ref[...], modt_ref[2], a, o_ref[...], g, ga, gb,
                                 wbr_ref, wout_ref)


def _mixer_sample(dims, ds, x2d, modt, g1, weights, tables, pool_in, state_ret):
    t, d = x2d.shape
    ns = SAMPLE_SEQS_PER_STEP
    steps = SAMPLE_STEPS_PER_GROUP
    R = ns * ds
    RG = R * steps
    win, wpg, ps, wbp, wbr, wout = weights
    cosb, sinb, intra, qdec, kdec, sdec, bandp, bandu, cnt = tables
    db = state_ret.shape[0]
    in_specs = [
        pl.BlockSpec((RG, d), lambda i, j: (i, 0)),
        pl.BlockSpec((N_MOD, RG, d), lambda i, j: (0, i, 0)),
        _const_spec(g1.shape), _resident_spec(win.shape), _const_spec(wpg.shape),
        _const_spec(ps.shape), _resident_spec(wbp.shape), _resident_spec(wbr.shape),
        _resident_spec(wout.shape), _const_spec(cosb.shape), _const_spec(sinb.shape),
        _const_spec(intra.shape), _const_spec(qdec.shape), _const_spec(kdec.shape),
        _const_spec(sdec.shape), _const_spec(bandp.shape), _const_spec(bandu.shape),
        _const_spec(cnt.shape),
        pl.BlockSpec((ns, POOL_PREV, dims.d_pool), lambda i, j: (i * steps + j, 0, 0)),
        pl.BlockSpec((ns, dims.heads, dims.dk, dims.dv), lambda i, j: (i * steps + j, 0, 0, 0)),
    ]
    out_shape = (
        jax.ShapeDtypeStruct((t, d), F32),
        jax.ShapeDtypeStruct((t, dims.d_pool), F32),
        jax.ShapeDtypeStruct(state_ret.shape, state_ret.dtype),
    )
    out_specs = (
        pl.BlockSpec((RG, d), lambda i, j: (i, 0)),
        pl.BlockSpec((RG, dims.d_pool), lambda i, j: (i, 0)),
        pl.BlockSpec((ns, dims.heads, dims.dk, dims.dv), lambda i, j: (i * steps + j, 0, 0, 0)),
    )
    weight_bytes = 2 * (win.size + wpg.size + wbp.size + wbr.size + wout.size)
    state_bytes = 4 * ns * dims.heads * dims.dk * dims.dv
    return pl.pallas_call(
        functools.partial(_mixer_sample_kernel, dims, ds),
        out_shape=out_shape,
        grid=(db // (ns * steps), steps),
        in_specs=in_specs,
        out_specs=out_specs,
        scratch_shapes=[
            pltpu.VMEM((RG, d), BF16), pltpu.VMEM((RG, dims.d_k), F32),
            pltpu.VMEM((RG, dims.d_k), F32), pltpu.VMEM((RG, dims.d_v), BF16),
            pltpu.VMEM((RG, dims.d_pool), F32), pltpu.VMEM((RG, dims.d_v), F32),
        ],
        compiler_params=pltpu.CompilerParams(
            dimension_semantics=("arbitrary", "arbitrary"),
            vmem_limit_bytes=_vmem_limit(2 * weight_bytes + 4 * state_bytes + 8 * 1024 * 1024),
        ),
        name="mixer_sample",
    )(x2d, modt, g1, win, wpg, ps, wbp, wbr, wout, cosb, sinb, intra, qdec, kdec, sdec,
      bandp, bandu, cnt, pool_in, state_ret)


def _rotary_tables(pos, dk):
    half = dk // 2
    inv = np.power(ROPE_BASE, -np.arange(half, dtype=np.float64) / half)
    ang = pos.astype(np.float64)[:, None] * inv[None, :]
    cos, sin = np.cos(ang), np.sin(ang)
    return (np.concatenate([cos, cos], axis=1).astype(np.float32),
            np.concatenate([-sin, sin], axis=1).astype(np.float32))


def _decay_tables(log_g, chunk, reps, dk, dv):
    idx = np.arange(chunk, dtype=np.float64)
    diff = idx[:, None] - idx[None, :]
    intra = np.where((diff >= 0)[None],
                     np.exp(log_g[:, None, None] * np.maximum(diff, 0.0)[None]), 0.0)
    heads = log_g.shape[0]
    eye = np.eye(reps)
    intra = np.einsum("ab,hij->haibj", eye, intra).reshape(heads, reps * chunk, reps * chunk)
    q_dec = np.exp(log_g[:, None] * (idx[None, :] + 1.0))
    k_dec = np.exp(log_g[:, None] * (chunk - 1.0 - idx[None, :]))
    s_dec = np.exp(log_g * chunk)
    q_dec = np.broadcast_to(np.tile(q_dec, (1, reps))[:, :, None], (heads, reps * chunk, dk))
    k_dec = np.broadcast_to(np.tile(k_dec, (1, reps))[:, :, None], (heads, reps * chunk, dk))
    s_dec = np.broadcast_to(s_dec[:, None, None], (heads, 1, dv))
    return tuple(np.ascontiguousarray(t, dtype=np.float32) for t in (intra, q_dec, k_dec, s_dec))


def _sample_pool_tables(ns, ds, gd, start_pos):
    n = np.arange(ds)
    i = np.arange(POOL_PREV)
    eye = np.eye(ns, dtype=np.float32)
    bandp, bandu, cnt = [], [], []
    for w in POOL_WINDOWS:
        bp = ((i[None, :] - POOL_PREV) >= (n[:, None] - w + 1)).astype(np.float32)
        bu = ((n[None, :] <= n[:, None]) & (n[None, :] >= n[:, None] - w + 1)).astype(np.float32)
        bandp.append(np.einsum("ab,ni->anbi", eye, bp).reshape(ns * ds, ns * POOL_PREV))
        bandu.append(np.einsum("ab,nm->anbm", eye, bu).reshape(ns * ds, ns * ds))
        c = np.minimum(w, start_pos + n + 1).astype(np.float32)
        cnt.append(np.broadcast_to(np.tile(c, ns)[:, None], (ns * ds, gd)))
    return (np.stack(bandp).astype(BF16), np.stack(bandu).astype(BF16),
            np.ascontiguousarray(np.stack(cnt)))


def _exclusive_prefix(x, axis):
    pos = lax.broadcasted_iota(jnp.int32, x.shape, axis)
    acc = x
    sh = 1
    while sh < x.shape[axis]:
        acc = acc + jnp.where(pos >= sh, pltpu.roll(acc, sh, axis=axis), 0.0)
        sh *= 2
    return acc - x


def _first_max_onehot(v, idx, big, axes):
    m = jnp.max(v, axis=axes, keepdims=True)
    first = jnp.min(jnp.where(v == m, idx, big), axis=axes, keepdims=True)
    return idx == first


def _route_block(h2, blk, wr_ref, br_ref, h2_ref, pos_ref, wts_ref, cnt_ref, rank_ref, gates_ref):
    tok = slice(blk * h2.shape[0], (blk + 1) * h2.shape[0])
    hh = h2.astype(BF16)
    h2_ref[tok, :] = hh

    hl = (h2 - hh.astype(F32)).astype(BF16)
    wh, wl = _split_bf16(wr_ref[...], 2)
    logits = _dot_nt(wh, hh) + (_dot_nt(wh, hl) + _dot_nt(wl, hh))
    s = jax.nn.sigmoid(logits)
    e, t = s.shape
    ge = e // N_GROUPS
    neg = -jnp.inf
    s3 = s.reshape(N_GROUPS, ge, t)
    sb3 = s3 + br_ref[...]

    j3 = lax.broadcasted_iota(jnp.int32, (N_GROUPS, ge, t), 1)
    top1 = _first_max_onehot(sb3, j3, ge, 1)
    m1 = jnp.max(sb3, axis=1, keepdims=True)
    m2 = jnp.max(jnp.where(top1, neg, sb3), axis=1, keepdims=True)
    gv = m1 + m2

    g3 = lax.broadcasted_iota(jnp.int32, (N_GROUPS, 1, t), 0)
    gsel = jnp.zeros((N_GROUPS, 1, t), F32)
    for _ in range(TOPK_GROUPS):
        hit = _first_max_onehot(gv, g3, N_GROUPS, 0)
        gsel = jnp.where(hit, 1.0, gsel)
        gv = jnp.where(hit, neg, gv)

    e3 = lax.broadcasted_iota(jnp.int32, (N_GROUPS, ge, t), 0) * ge + j3
    ev = jnp.where(gsel > 0.5, sb3, neg)
    sel = jnp.zeros((N_GROUPS, ge, t), F32)
    chosen = jnp.zeros((N_GROUPS, ge, t), F32)
    hits = []
    for _ in range(TOP_K):
        hit = _first_max_onehot(ev, e3, e, (0, 1))
        hits.append(hit)
        sel = jnp.where(hit, s3, sel)
        chosen = jnp.where(hit, 1.0, chosen)
        ev = jnp.where(hit, neg, ev)
    tot = jnp.sum(sel, axis=(0, 1), keepdims=True)
    gates = sel / tot * ROUTED_SCALE

    chosen2 = chosen.reshape(e, t)
    rank = _exclusive_prefix(chosen2, 1)
    cnt = jnp.sum(chosen2, axis=1, keepdims=True)
    ovf_units = jnp.floor((jnp.maximum(cnt - SLOT_ROWS, 0.0) + (DMA_ROWS - 1)) / DMA_ROWS)
    ovf_off = _exclusive_prefix(jnp.broadcast_to(ovf_units, (e, LANES)), 0)[:, 0:1]
    e_col = lax.broadcasted_iota(jnp.int32, (e, 1), 0).astype(F32)
    slot = jnp.where(rank < SLOT_ROWS, e_col * SLOT_ROWS + rank,
                     e * SLOT_ROWS + ovf_off * DMA_ROWS + (rank - SLOT_ROWS))
    slot3 = slot.reshape(N_GROUPS, ge, t)
    for k, hit in enumerate(hits):
        pos_k = jnp.sum(jnp.where(hit, slot3, 0.0), axis=(0, 1), keepdims=True)
        w_k = jnp.sum(jnp.where(hit, gates, 0.0), axis=(0, 1), keepdims=True)
        pos_ref[k:k + 1, tok] = pos_k.reshape(1, t).astype(jnp.int32)
        wts_ref[k:k + 1, tok] = w_k.reshape(1, t)
    cnt_ref[blk] = jnp.broadcast_to(cnt, (e, LANES)).astype(jnp.int32)
    rank_ref[:, tok] = jnp.where(chosen2 > 0.5, rank, -1.0).astype(jnp.int32)
    gates_ref[:, tok] = gates.reshape(e, t)


def _route_out_shapes(t, d, ne):
    return (jax.ShapeDtypeStruct((t, d), BF16),
            jax.ShapeDtypeStruct((TOP_K, t), jnp.int32),
            jax.ShapeDtypeStruct((TOP_K, t), F32),
            jax.ShapeDtypeStruct((t // TOKEN_BLOCK, ne, LANES), jnp.int32),
            jax.ShapeDtypeStruct((ne, t), jnp.int32),
            jax.ShapeDtypeStruct((ne, t), F32))


def _route_out_specs(rows, ne, d, idx):
    return (pl.BlockSpec((rows, d), lambda *g: (idx(*g), 0)),
            pl.BlockSpec((TOP_K, rows), lambda *g: (0, idx(*g))),
            pl.BlockSpec((TOP_K, rows), lambda *g: (0, idx(*g))),
            pl.BlockSpec((rows // TOKEN_BLOCK, ne, LANES), lambda *g: (idx(*g), 0, 0)),
            pl.BlockSpec((ne, rows), lambda *g: (0, idx(*g))),
            pl.BlockSpec((ne, rows), lambda *g: (0, idx(*g))))


def _route_rows_kernel(per_token, x_ref, mod_ref, g2_ref, wr_ref, br_ref, *refs):
    outs = refs[len(refs) // 2:]
    if per_token:
        shift, scale = mod_ref[3], mod_ref[4]
    else:
        shift, scale = mod_ref[0, 3:4, :], mod_ref[0, 4:5, :]
    h2 = _rms_mod(x_ref[...], g2_ref[...], shift, scale)
    _route_block(h2, 0, wr_ref, br_ref, *outs)


def _route_rows(x2d, first_row, n_rows, mod, per_token, g2, wr_t, br, routed, first_block):
    d = x2d.shape[1]
    tb = TOKEN_BLOCK
    ne = wr_t.shape[0]
    t = routed[0].shape[0]
    x_block0 = first_row // tb
    if per_token:
        mod_spec = pl.BlockSpec((N_MOD, tb, d), lambda i: (0, x_block0 + i, 0))
    else:
        mod_spec = pl.BlockSpec((1, N_MOD, d), lambda i: (0, 0, 0))
    n_in = 5
    return pl.pallas_call(
        functools.partial(_route_rows_kernel, per_token),
        out_shape=_route_out_shapes(t, d, ne),
        grid=(n_rows // tb,),
        in_specs=[
            pl.BlockSpec((tb, d), lambda i: (x_block0 + i, 0)),
            mod_spec,
            _const_spec(g2.shape), _const_spec(wr_t.shape), _const_spec(br.shape),
        ] + [pl.BlockSpec(memory_space=pl.ANY)] * len(routed),
        out_specs=_route_out_specs(tb, ne, d, lambda i: first_block + i),
        input_output_aliases={n_in + k: k for k in range(len(routed))},
        compiler_params=pltpu.CompilerParams(
            dimension_semantics=("arbitrary",),
            vmem_limit_bytes=_vmem_limit(32 * 1024 * 1024),
        ),
        name="route_rows",
    )(x2d, mod, g2, wr_t, br, *routed)


class _Layout:
    def __init__(self, counts):
        nb, ne = counts.shape
        u, g, tm = SLOT_ROWS, DMA_ROWS, EXPERT_TILE
        pc = (counts + g - 1) // g * g
        ovf_units = (jnp.maximum(counts - u, 0) + g - 1) // g
        rows_e = jnp.sum(pc, axis=0)
        region = (rows_e + u + tm - 1) // tm * tm
        goff = jnp.cumsum(region) - region
        dst = goff[None, :] + jnp.cumsum(pc, axis=0) - pc
        self.dst = dst.reshape(-1)
        ovf_end = jnp.cumsum(ovf_units, axis=1)
        ovf_first = ovf_end - ovf_units
        unit = jnp.arange(_overflow_units_cap(ne))[None, :, None]
        owns = (ovf_first[:, None, :] <= unit) & (unit < ovf_end[:, None, :])
        row = dst[:, None, :] + u + (unit - ovf_first[:, None, :]) * g
        self.ovf_dst = jnp.sum(jnp.where(owns, row, 0), axis=2).reshape(-1)
        self.ovf_tot = ovf_end[:, -1]
        self.tail_start = goff + rows_e
        self.tail_units = (region - rows_e) // g
        tiles = region // tm
        self.region_tiles = jnp.concatenate([tiles, jnp.sum(tiles, keepdims=True)])


def _overflow_units_cap(ne):
    return TOKEN_BLOCK * TOP_K // DMA_ROWS + ne


def _block_buffer_rows(tb, ne):
    rows = ne * SLOT_ROWS + tb * TOP_K + ne * (DMA_ROWS - 1)
    return (rows + ONEHOT_ROWS - 1) // ONEHOT_ROWS * ONEHOT_ROWS


def _expert_rows_cap(t, tb, ne):
    nb = t // tb
    rows = t * TOP_K + nb * ne * (DMA_ROWS - 1) + ne * (SLOT_ROWS + EXPERT_TILE - 1)
    return (rows + EXPERT_TILE - 1) // EXPERT_TILE * EXPERT_TILE


def _slot_copy(buf_ref, slot, e, hbm_ref, row, sem, to_hbm):
    src = buf_ref.at[slot, pl.ds(pl.multiple_of(e * SLOT_ROWS, DMA_ROWS), SLOT_ROWS)]
    dst = hbm_ref.at[pl.ds(pl.multiple_of(row, DMA_ROWS), SLOT_ROWS)]
    return pltpu.make_async_copy(src, dst, sem) if to_hbm else pltpu.make_async_copy(dst, src, sem)


def _unit_copy(buf_ref, slot, buf_row, hbm_ref, row, sem, to_hbm):
    src = buf_ref.at[slot, pl.ds(pl.multiple_of(buf_row, DMA_ROWS), DMA_ROWS)]
    dst = hbm_ref.at[pl.ds(pl.multiple_of(row, DMA_ROWS), DMA_ROWS)]
    return pltpu.make_async_copy(src, dst, sem) if to_hbm else pltpu.make_async_copy(dst, src, sem)


def _slots_wait(ne, buf_ref, slot, hbm_ref, sem, to_hbm):
    src = buf_ref.at[slot, pl.ds(0, ne * SLOT_ROWS)]
    dst = hbm_ref.at[pl.ds(0, ne * SLOT_ROWS)]
    c = pltpu.make_async_copy(src, dst, sem) if to_hbm else pltpu.make_async_copy(dst, src, sem)
    c.wait()


def _overflow_copies(ne, blk, slot, odst_ref, otot_ref, buf_ref, hbm_ref, sem, to_hbm, wait):
    def per_unit(i, carry):
        row = odst_ref[blk * _overflow_units_cap(ne) + i]
        c = _unit_copy(buf_ref, slot, ne * SLOT_ROWS + i * DMA_ROWS, hbm_ref, row, sem, to_hbm)
        c.wait() if wait else c.start()
        return carry

    lax.fori_loop(0, otot_ref[blk], per_unit, 0)


def _slot_chunks(ne):
    return [range(c, c + SLOT_CHUNK_EXPERTS) for c in range(0, ne, SLOT_CHUNK_EXPERTS)]


def _overflow_chunks(n_units):
    return lax.shift_right_logical(n_units * DMA_ROWS + (ONEHOT_ROWS - 1),
                                   ONEHOT_ROWS.bit_length() - 1)


def _overflow_select(pos, vals, r0):
    rows = lax.broadcasted_iota(jnp.int32, (ONEHOT_ROWS, pos.shape[1]), 0) + r0
    sel = jnp.zeros(rows.shape, F32)
    for k in range(TOP_K):
        v = 1.0 if vals is None else vals[k:k + 1, :]
        sel = jnp.where(pos[k:k + 1, :] == rows, v, sel)
    return sel


def _dispatch_kernel(ne, dst_ref, odst_ref, otot_ref, tails_ref, tailu_ref,
                     h2_ref, rank_ref, pos_ref, xs_hbm, buf_ref, zero_ref, sem, zsem):
    b = pl.program_id(0)
    nb = pl.num_programs(0)
    slot = b % 2
    h2 = h2_ref[...]
    j = lax.broadcasted_iota(jnp.int32, (SLOT_ROWS, h2.shape[0]), 0)

    def zero_copy(row, rows, wait):
        c = pltpu.make_async_copy(zero_ref.at[pl.ds(0, rows)],
                                  xs_hbm.at[pl.ds(pl.multiple_of(row, DMA_ROWS), rows)], zsem)
        c.wait() if wait else c.start()

    def far_tail(e, wait):
        per_big = ONEHOT_ROWS // DMA_ROWS
        units = tailu_ref[e] - SLOT_ROWS // DMA_ROWS
        n_big = lax.shift_right_logical(units, per_big.bit_length() - 1)
        start = tails_ref[e] + SLOT_ROWS

        def big(i, carry):
            zero_copy(start + i * ONEHOT_ROWS, ONEHOT_ROWS, wait)
            return carry

        def small(i, carry):
            zero_copy(start + n_big * ONEHOT_ROWS + i * DMA_ROWS, DMA_ROWS, wait)
            return carry

        lax.fori_loop(0, n_big, big, 0)
        lax.fori_loop(0, units - n_big * per_big, small, 0)

    @pl.when(b == 0)
    def _zeros():
        zero_ref[...] = jnp.zeros_like(zero_ref)

    @pl.when(b < ne)
    def _far_start():
        far_tail(b, False)

    chunks = _slot_chunks(ne)
    first_issue = max(len(chunks) - 2, 0)
    for ci, experts in enumerate(chunks):
        onehot = jnp.concatenate(
            [jnp.where(rank_ref[e:e + 1, :] == j, 1.0, 0.0) for e in experts], axis=0)
        r0 = experts[0] * SLOT_ROWS
        buf_ref[slot, r0:r0 + len(experts) * SLOT_ROWS, :] = (
            _dot(onehot.astype(BF16), h2).astype(BF16))
        if ci < first_issue:
            continue
        if ci == first_issue:
            @pl.when(b > 0)
            def _prev():
                _slots_wait(ne, buf_ref, 1 - slot, xs_hbm, sem.at[1 - slot], True)
                _overflow_copies(ne, b - 1, 1 - slot, odst_ref, otot_ref, buf_ref, xs_hbm,
                                 sem.at[1 - slot], True, True)
        ready = [e for c in chunks[:ci + 1] for e in c] if ci == first_issue else experts
        for e in ready:
            _slot_copy(buf_ref, slot, e, xs_hbm, dst_ref[b * ne + e], sem.at[slot], True).start()

    pos = pos_ref[...]

    def chunk(ci, carry):
        r0 = pl.multiple_of(ne * SLOT_ROWS + ci * ONEHOT_ROWS, ONEHOT_ROWS)
        onehot = _overflow_select(pos, None, r0).astype(BF16)
        buf_ref[slot, pl.ds(r0, ONEHOT_ROWS), :] = _dot(onehot, h2).astype(BF16)
        return carry

    lax.fori_loop(0, _overflow_chunks(otot_ref[b]), chunk, 0)
    _overflow_copies(ne, b, slot, odst_ref, otot_ref, buf_ref, xs_hbm, sem.at[slot], True, False)

    @pl.when(b < ne)
    def _far_done():
        far_tail(b, True)

    @pl.when(b == nb - 1)
    def _last():
        _slots_wait(ne, buf_ref, slot, xs_hbm, sem.at[slot], True)
        _overflow_copies(ne, b, slot, odst_ref, otot_ref, buf_ref, xs_hbm, sem.at[slot], True,
                         True)
        def near_tail(wait):
            def per_expert(e, carry):
                zero_copy(tails_ref[e], SLOT_ROWS, wait)
                return carry
            lax.fori_loop(0, ne, per_expert, 0)

        near_tail(False)
        near_tail(True)


def _dispatch(h2, rank, pos, lay, ne, rows_cap):
    t, d = h2.shape
    tb = TOKEN_BLOCK
    buf_rows = _block_buffer_rows(tb, ne)
    assert t // tb >= ne
    return pl.pallas_call(
        functools.partial(_dispatch_kernel, ne),
        out_shape=jax.ShapeDtypeStruct((rows_cap, d), BF16),
        grid_spec=pltpu.PrefetchScalarGridSpec(
            num_scalar_prefetch=5,
            grid=(t // tb,),
            in_specs=[
                pl.BlockSpec((tb, d), lambda i, *_: (i, 0)),
                pl.BlockSpec((ne, tb), lambda i, *_: (0, i)),
                pl.BlockSpec((TOP_K, tb), lambda i, *_: (0, i)),
            ],
            out_specs=pl.BlockSpec(memory_space=pl.ANY),
            scratch_shapes=[
                pltpu.VMEM((2, buf_rows, d), BF16),
                pltpu.VMEM((ONEHOT_ROWS, d), BF16),
                pltpu.SemaphoreType.DMA((2,)),
                pltpu.SemaphoreType.DMA(()),
            ],
        ),
        compiler_params=pltpu.CompilerParams(
            dimension_semantics=("arbitrary",),
            vmem_limit_bytes=_vmem_limit(4 * buf_rows * d + 16 * 1024 * 1024),
        ),
        name="dispatch",
    )(lay.dst, lay.ovf_dst, lay.ovf_tot, lay.tail_start, lay.tail_units, h2, rank, pos)


def _experts_kernel(tiles_ref, xe_hbm, wg_ref, wu_ref, wd_ref, ye_hbm,
                    xbuf_ref, ybuf_ref, wgb_ref, wub_ref, wdb_ref, done_ref, xsem, ysem):
    e = pl.program_id(0)
    ne = pl.num_programs(0)
    tm = xbuf_ref.shape[1]

    def x_copy(row, slot):
        return pltpu.make_async_copy(xe_hbm.at[pl.ds(pl.multiple_of(row, tm), tm)],
                                     xbuf_ref.at[slot], xsem.at[slot])

    def y_copy(row, slot):
        return pltpu.make_async_copy(ybuf_ref.at[slot],
                                     ye_hbm.at[pl.ds(pl.multiple_of(row, tm), tm)], ysem.at[slot])

    total = tiles_ref[ne]

    @pl.when(e == 0)
    def _first():
        done_ref[0] = 0
        for g in range(RING_SLOTS - 1):
            x_copy(g * tm, g).start()

    wgb_ref[...] = wg_ref[0].astype(BF16)
    wub_ref[...] = wu_ref[0].astype(BF16)
    wdb_ref[...] = wd_ref[0].astype(BF16)
    done = done_ref[0]
    n = tiles_ref[e]

    def tile(j, carry):
        g = done + j
        slot = lax.rem(g, RING_SLOTS)
        x_copy(0, slot).wait()
        ahead = g + (RING_SLOTS - 1)

        @pl.when(ahead < total)
        def _prefetch():
            x_copy(ahead * tm, lax.rem(ahead, RING_SLOTS)).start()

        @pl.when(g >= RING_SLOTS)
        def _reuse():
            y_copy(0, slot).wait()

        part = tm // EXPERT_TILE_PARTS
        starts = range(0, tm, part)
        hids = []
        for r0 in starts:
            x = xbuf_ref[slot, r0:r0 + part, :]
            hids.append((jax.nn.silu(_dot(x, wgb_ref[...])) * _dot(x, wub_ref[...])).astype(BF16))
        for r0, hid in zip(starts, hids):
            ybuf_ref[slot, r0:r0 + part, :] = _dot(hid, wdb_ref[...]).astype(BF16)
        y_copy(g * tm, slot).start()
        return carry

    lax.fori_loop(0, n, tile, 0)
    done_ref[0] = done + n

    @pl.when(e == ne - 1)
    def _drain():
        for s in range(RING_SLOTS):
            y_copy(0, s).wait()


def _experts(xs, lay, wg, wu, wd):
    rows_cap, d = xs.shape
    ne, _, de = wg.shape
    assert ne >= RING_SLOTS
    tm = EXPERT_TILE
    ring = RING_SLOTS
    return pl.pallas_call(
        _experts_kernel,
        out_shape=jax.ShapeDtypeStruct((rows_cap, d), BF16),
        grid_spec=pltpu.PrefetchScalarGridSpec(
            num_scalar_prefetch=1,
            grid=(ne,),
            in_specs=[
                pl.BlockSpec(memory_space=pl.ANY),
                pl.BlockSpec((1, d, de), lambda e, *_: (e, 0, 0)),
                pl.BlockSpec((1, d, de), lambda e, *_: (e, 0, 0)),
                pl.BlockSpec((1, de, d), lambda e, *_: (e, 0, 0)),
            ],
            out_specs=pl.BlockSpec(memory_space=pl.ANY),
            scratch_shapes=[
                pltpu.VMEM((ring, tm, d), BF16), pltpu.VMEM((ring, tm, d), BF16),
                pltpu.VMEM((d, de), BF16), pltpu.VMEM((d, de), BF16), pltpu.VMEM((de, d), BF16),
                pltpu.SMEM((1,), jnp.int32),
                pltpu.SemaphoreType.DMA((ring,)), pltpu.SemaphoreType.DMA((ring,)),
            ],
        ),
        compiler_params=pltpu.CompilerParams(
            dimension_semantics=("arbitrary",),
            vmem_limit_bytes=_vmem_limit(32 * 1024 * 1024),
        ),
        name="experts",
    )(lay.region_tiles, xs, wg, wu, wd)


def _final_kernel(n_prompt_blocks, ne, dst_ref, odst_ref, otot_ref,
                  xp_ref, xs_ref, modp_ref, modt_ref, h2_ref, rank_ref, gates_ref, pos_ref, wts_ref,
                  wsg_ref, wsu_ref, wsd_ref, gf_ref, ye_hbm, yp_ref, ys_ref,
                  buf_ref, acc_ref, sem):
    i = pl.program_id(0)
    nb = pl.num_programs(0)
    slot = i % 2

    def fetch_slots(blk, sl, experts):
        for e in experts:
            _slot_copy(buf_ref, sl, e, ye_hbm, dst_ref[blk * ne + e], sem.at[sl], False).start()

    def overflow(blk, sl, wait):
        _overflow_copies(ne, blk, sl, odst_ref, otot_ref, buf_ref, ye_hbm, sem.at[sl], False, wait)

    @pl.when(i == 0)
    def _first():
        buf_ref[...] = jnp.zeros_like(buf_ref)
        fetch_slots(0, 0, range(ne))
        overflow(0, 0, False)

    @pl.when(i + 1 < nb)
    def _prefetch():
        fetch_slots(i + 1, 1 - slot, range(ne))
        overflow(i + 1, 1 - slot, False)

    _slots_wait(ne, buf_ref, slot, ye_hbm, sem.at[slot], False)
    overflow(i, slot, True)

    tb = h2_ref.shape[0]
    j = lax.broadcasted_iota(jnp.int32, (SLOT_ROWS, tb), 0)
    acc = jnp.zeros((tb, h2_ref.shape[1]), F32)
    for experts in _slot_chunks(ne):
        sel = jnp.concatenate(
            [jnp.where(rank_ref[e:e + 1, :] == j, gates_ref[e:e + 1, :], 0.0) for e in experts],
            axis=0)
        r0 = experts[0] * SLOT_ROWS
        acc = acc + _dot_tn(sel.astype(BF16), buf_ref[slot, r0:r0 + len(experts) * SLOT_ROWS, :])
    acc_ref[...] = acc

    pos = pos_ref[...]
    wts = wts_ref[...]

    def chunk(ci, carry):
        r0 = pl.multiple_of(ne * SLOT_ROWS + ci * ONEHOT_ROWS, ONEHOT_ROWS)
        sel = _overflow_select(pos, wts, r0).astype(BF16)
        acc_ref[...] += _dot_tn(sel, buf_ref[slot, pl.ds(r0, ONEHOT_ROWS), :])
        return carry

    lax.fori_loop(0, _overflow_chunks(otot_ref[i]), chunk, 0)

    is_p = i < n_prompt_blocks
    x = jnp.where(is_p, xp_ref[...], xs_ref[...])
    gate = jnp.where(is_p, modp_ref[0, 5:6, :], modt_ref[5])
    hb = h2_ref[...]
    hid = jax.nn.silu(_dot(hb, wsg_ref[...])) * _dot(hb, wsu_ref[...])
    f = acc_ref[...] + _dot(hid.astype(BF16), wsd_ref[...])
    x2 = x + gate * f
    y = x2 * lax.rsqrt(jnp.mean(x2 * x2, axis=-1, keepdims=True) + EPS) * gf_ref[...]

    @pl.when(is_p)
    def _p():
        yp_ref[...] = y

    @pl.when(jnp.logical_not(is_p))
    def _s():
        ys_ref[...] = y


def _final(xp2d, xs2d, modp, modt, h2, rank, gates, pos, wts, ye, lay, ne, wsg, wsu, wsd, gf):
    tp, d = xp2d.shape
    ts = xs2d.shape[0]
    tb = TOKEN_BLOCK
    npb, nsb = tp // tb, ts // tb
    seq = tp // modp.shape[0]
    bpb = seq // tb
    buf_rows = _block_buffer_rows(tb, ne)

    def p_idx(i):
        return jnp.minimum(i, npb - 1)

    def s_idx(i):
        return jnp.maximum(i - npb, 0)

    def const(shape):
        zeros = (0,) * len(shape)
        return pl.BlockSpec(shape, lambda i, *_: zeros)

    return pl.pallas_call(
        functools.partial(_final_kernel, npb, ne),
        out_shape=(jax.ShapeDtypeStruct((tp, d), F32), jax.ShapeDtypeStruct((ts, d), F32)),
        grid_spec=pltpu.PrefetchScalarGridSpec(
            num_scalar_prefetch=3,
            grid=(npb + nsb,),
            in_specs=[
                pl.BlockSpec((tb, d), lambda i, *_: (p_idx(i), 0)),
                pl.BlockSpec((tb, d), lambda i, *_: (s_idx(i), 0)),
                pl.BlockSpec((1, N_MOD, d), lambda i, *_: (p_idx(i) // bpb, 0, 0)),
                pl.BlockSpec((N_MOD, tb, d), lambda i, *_: (0, s_idx(i), 0)),
                pl.BlockSpec((tb, d), lambda i, *_: (i, 0)),
                pl.BlockSpec((ne, tb), lambda i, *_: (0, i)),
                pl.BlockSpec((ne, tb), lambda i, *_: (0, i)),
                pl.BlockSpec((TOP_K, tb), lambda i, *_: (0, i)),
                pl.BlockSpec((TOP_K, tb), lambda i, *_: (0, i)),
                const(wsg.shape), const(wsu.shape), const(wsd.shape), const(gf.shape),
                pl.BlockSpec(memory_space=pl.ANY),
            ],
            out_specs=(pl.BlockSpec((tb, d), lambda i, *_: (p_idx(i), 0)),
                       pl.BlockSpec((tb, d), lambda i, *_: (s_idx(i), 0))),
            scratch_shapes=[
                pltpu.VMEM((2, buf_rows, d), BF16),
                pltpu.VMEM((tb, d), F32),
                pltpu.SemaphoreType.DMA((2,)),
            ],
        ),
        compiler_params=pltpu.CompilerParams(
            dimension_semantics=("arbitrary",),
            vmem_limit_bytes=_vmem_limit(4 * buf_rows * d + 32 * 1024 * 1024),
        ),
        name="final",
    )(lay.dst, lay.ovf_dst, lay.ovf_tot,
      xp2d, xs2d, modp, modt, h2, rank, gates, pos, wts, wsg, wsu, wsd, gf, ye)


def kernel(x_prompt, x_sample, c_prompt, c_sample, state_pool, state_ret, w_ada, b_ada, g_norm1,
           w_in, w_pool_group, pool_scale, w_branch_pool, w_branch_ret, w_out, g_norm2, w_router,
           b_router, w_exp_gate, w_exp_up, w_exp_down, w_sh_gate, w_sh_up, w_sh_down, g_final):
    b, s, d = x_prompt.shape
    db, ds, _ = x_sample.shape
    depth = w_ada.shape[0]
    heads, dk, dv = state_ret.shape[2:]
    pool_buf, d_pool = state_pool.shape[2:]
    dims = _Dims(d, d_pool, heads, dk, dv)
    assert dims.d_in == w_in.shape[2]
    assert pool_buf == max(POOL_WINDOWS) - 1 and pool_buf < POOL_PREV
    assert s % PROMPT_STEP_ROWS == 0 and PROMPT_STEP_ROWS % RET_CHUNK == 0
    assert s % TOKEN_BLOCK == 0 and (db * ds) % TOKEN_BLOCK == 0
    assert db % (SAMPLE_SEQS_PER_STEP * SAMPLE_STEPS_PER_GROUP) == 0
    assert SLOT_ROWS % DMA_ROWS == 0 and (w_router.shape[2] * SLOT_ROWS) % ONEHOT_ROWS == 0

    log_g = np.log(1.0 - np.power(2.0, -5.0 - np.arange(heads, dtype=np.float64)))
    ns = SAMPLE_SEQS_PER_STEP
    cos_p, sin_p = _rotary_tables(np.arange(s), dk)
    cos_s, sin_s = _rotary_tables(PAST_LEN + np.arange(ds), dk)
    cos_s, sin_s = np.tile(cos_s, (ns, 1)), np.tile(sin_s, (ns, 1))
    dec_p = _decay_tables(log_g, RET_CHUNK, 1, dk, dv)
    dec_s = _decay_tables(log_g, ds, ns, dk, dv)
    pool_tabs = _sample_pool_tables(ns, ds, dims.gd, PAST_LEN)

    assert depth == 1, "the final kernel fuses the output norm into the (single) layer"
    l = 0
    xs = x_sample.reshape(db * ds, d)
    modp, modt = _ada(c_prompt, jnp.repeat(c_sample, ds, axis=0), w_ada[l], b_ada[l])
    modp = modp.reshape(b, N_MOD, d)
    g1 = g_norm1[l].reshape(1, d)
    weights = (w_in[l].astype(BF16), w_pool_group[l].astype(BF16),
               pool_scale[l].reshape(1, d_pool), w_branch_pool[l].astype(BF16),
               w_branch_ret[l].astype(BF16), w_out[l].astype(BF16))

    ne = w_router.shape[2]
    router = (g_norm2[l].reshape(1, d), w_router[l].T,
              b_router[l].reshape(N_GROUPS, ne // N_GROUPS, 1))
    x1p, pool16, ret_p, *routed = _mixer_prompt(dims, x_prompt, modp, g1, weights,
                                                (cos_p, sin_p) + dec_p, router,
                                                b * s + db * ds)
    pool_in = jnp.pad(state_pool[l], ((0, 0), (POOL_PREV - pool_buf, 0), (0, 0)))
    x1s, u_s, ret_s = _mixer_sample(dims, ds, xs, modt, g1, weights,
                                    (cos_s, sin_s) + dec_s + pool_tabs, pool_in,
                                    state_ret[l].astype(F32))
    pool_p = pool16[:, POOL_PREV - pool_buf:]
    pool_s = jnp.concatenate([state_pool[l], u_s.reshape(db, ds, d_pool)], axis=1)[:, -pool_buf:]

    x1p2d = x1p.reshape(b * s, d)
    last = b * s - PROMPT_STEP_ROWS
    routed = _route_rows(x1p2d, last, PROMPT_STEP_ROWS, modp[b - 1:b], False, *router, routed,
                         last // TOKEN_BLOCK)
    h2, pos, wts, counts, rank, gates = _route_rows(x1s, 0, db * ds, modt, True, *router, routed,
                                                    (b * s) // TOKEN_BLOCK)
    rows_cap = _expert_rows_cap(b * s + db * ds, TOKEN_BLOCK, ne)
    lay = _Layout(counts[:, :, 0])
    xe = _dispatch(h2, rank, pos, lay, ne, rows_cap)
    ye = _experts(xe, lay, w_exp_gate[l], w_exp_up[l], w_exp_down[l])
    yp, ys = _final(x1p2d, x1s, modp, modt, h2, rank, gates, pos, wts, ye, lay, ne,
                    w_sh_gate[l].astype(BF16), w_sh_up[l].astype(BF16),
                    w_sh_down[l].astype(BF16), g_final.reshape(1, d))

    return (yp.reshape(b, s, d), ys.reshape(db, ds, d), pool_p[None],
            ret_p.astype(x_prompt.dtype), pool_s[None], ret_s.astype(state_ret.dtype)[None])
```

```python
import functools

import jax
import jax.numpy as jnp
import numpy as np
from jax import lax
from jax.experimental import pallas as pl
from jax.experimental.pallas import tpu as pltpu

F32 = jnp.float32
BF16 = jnp.bfloat16

PAST_LEN = 16384
POOL_WINDOWS = (2, 4, 8, 16)
RET_CHUNK = 128
ROPE_BASE = 10000.0
TOP_K = 8
N_GROUPS = 8
TOPK_GROUPS = 4
ROUTED_SCALE = 2.5
N_MOD = 6
EPS = 1e-6

V7X_VMEM_BYTES = 64 * 1024 * 1024
LANES = 128

PROMPT_STEP_ROWS = 512
SAMPLE_SEQS_PER_STEP = 8
SAMPLE_STEPS_PER_GROUP = 4
TOKEN_BLOCK = 256
DMA_ROWS = 16
SLOT_ROWS = 3 * DMA_ROWS
ONEHOT_ROWS = 256
SLOT_CHUNK_EXPERTS = 16
FINAL_BUFFERS = 3
FINAL_WINDOW_UNITS = 64
EXPERT_TILE = 1024
EXPERT_TILE_PARTS = 4
RING_SLOTS = 4
POOL_PREV = 16


def _vmem_limit(nbytes):
    return int(min(nbytes, V7X_VMEM_BYTES - 8 * 1024 * 1024))


def _dot(a, b):
    return jnp.dot(a, b, preferred_element_type=F32)


def _dot_nt(a, b):
    return lax.dot_general(a, b, (((1,), (1,)), ((), ())), preferred_element_type=F32)


def _dot_tn(a, b):
    return lax.dot_general(a, b, (((0,), (0,)), ((), ())), preferred_element_type=F32)


def _split_bf16(x, parts):
    out = []
    for _ in range(parts):
        p = x.astype(BF16)
        out.append(p)
        x = x - p.astype(F32)
    return out


def _rms_mod(x, g, shift, scale):
    y = x * lax.rsqrt(jnp.mean(x * x, axis=-1, keepdims=True) + EPS)
    return (y * g) * (1.0 + scale) + shift


def _rotary(x, cosb, sinb):
    return x * cosb + pltpu.roll(x, x.shape[-1] // 2, axis=1) * sinb


def _group_norm(o):
    return o * lax.rsqrt(jnp.mean(o * o, axis=-1, keepdims=True) + EPS)


def _ada_kernel(cp_ref, ct_ref, w_ref, b_ref, modp_ref, modt_ref):
    w = w_ref[...].astype(BF16)
    modp_ref[...] = _dot(cp_ref[...].astype(BF16), w) + b_ref[...]
    modt_ref[0] = _dot(ct_ref[...].astype(BF16), w) + b_ref[...]


def _ada(c_prompt, c_tokens, w_ada, b_ada):
    bp, d = c_prompt.shape
    ts = c_tokens.shape[0]
    n = w_ada.shape[1]
    return pl.pallas_call(
        _ada_kernel,
        out_shape=(jax.ShapeDtypeStruct((bp, n), F32), jax.ShapeDtypeStruct((n // d, ts, d), F32)),
        grid=(n // d,),
        in_specs=[
            pl.BlockSpec((bp, d), lambda j: (0, 0)),
            pl.BlockSpec((ts, d), lambda j: (0, 0)),
            pl.BlockSpec((d, d), lambda j: (0, j)),
            pl.BlockSpec((1, d), lambda j: (0, j)),
        ],
        out_specs=(pl.BlockSpec((bp, d), lambda j: (0, j)),
                   pl.BlockSpec((1, ts, d), lambda j: (j, 0, 0))),
        compiler_params=pltpu.CompilerParams(
            dimension_semantics=("arbitrary",),
            vmem_limit_bytes=_vmem_limit(32 * 1024 * 1024),
        ),
        name="ada",
    )(c_prompt, c_tokens, w_ada, b_ada.reshape(1, n))


class _Dims:
    def __init__(self, d_model, d_pool, heads, dk, dv):
        self.d = d_model
        self.d_pool = d_pool
        self.gd = d_pool // len(POOL_WINDOWS)
        self.heads = heads
        self.dk = dk
        self.dv = dv
        self.d_k = heads * dk
        self.d_v = heads * dv
        widths = (d_pool, self.d_k, self.d_k, self.d_v, self.d_v, d_model, d_model)
        offs = [0]
        for w in widths:
            offs.append(offs[-1] + w)
        self.cols = tuple((offs[i], offs[i + 1]) for i in range(len(widths)))
        self.d_in = offs[-1]


def _proj(hb, win_ref, cols):
    return _dot(hb, win_ref[:, cols[0]:cols[1]])


def _pool_branch(pm_parts, wpg_ref, ps_ref, wbp_ref):
    pg = [_dot(pm.astype(BF16), wpg_ref[gi]) for gi, pm in enumerate(pm_parts)]
    pm = jnp.concatenate(pg, axis=1) * ps_ref[...]
    return _dot(pm.astype(BF16), wbp_ref[...])


def _merge_out(x, gate, a, o, g, ga, gb, wbr_ref, wout_ref):
    r = _dot((jax.nn.silu(g) * o).astype(BF16), wbr_ref[...])
    merged = jax.nn.sigmoid(ga) * a + jax.nn.sigmoid(gb) * r
    m = _dot(merged.astype(BF16), wout_ref[...])
    return x + gate * m


def _mixer_prompt_kernel(dims, x_ref, mod_ref, g1_ref, win_ref, wpg_ref, ps_ref, wbp_ref,
                         wbr_ref, wout_ref, cos_ref, sin_ref, intra_ref, qdec_ref,
                         kdec_ref, sdec_ref, g2_ref, wr_ref, br_ref,
                         x1_ref, pool_ref, ret_ref, h2_ref, pos_ref, wts_ref, cnt_ref, rank_ref,
                         gates_ref, uext_ref, hnext_ref):
    c = pl.program_id(1)
    L = x_ref.shape[1]

    @pl.when(jnp.logical_and(pl.program_id(0) == 0, c == 0))
    def _init_route():
        hnext_ref[...] = jnp.zeros_like(hnext_ref)

    @pl.when(c == 0)
    def _init():
        ret_ref[...] = jnp.zeros_like(ret_ref)
        uext_ref[0:POOL_PREV, :] = jnp.zeros((POOL_PREV, dims.d_pool), F32)

    def route_prev(blk):
        _route_block(hnext_ref[blk * TOKEN_BLOCK:(blk + 1) * TOKEN_BLOCK, :], blk,
                     wr_ref, br_ref, h2_ref, pos_ref, wts_ref, cnt_ref, rank_ref, gates_ref)

    for blk in range(L // TOKEN_BLOCK):
        route_prev(blk)
    x = x_ref[0]
    shift, scale, gate = mod_ref[0, 0:1, :], mod_ref[0, 1:2, :], mod_ref[0, 2:3, :]
    hb = _rms_mod(x, g1_ref[...], shift, scale).astype(BF16)

    u = _proj(hb, win_ref, dims.cols[0])
    uext_ref[POOL_PREV:POOL_PREV + L, :] = u
    pos = c * L + lax.broadcasted_iota(jnp.int32, (L, 1), 0)
    pm_parts = []
    for gi, w in enumerate(POOL_WINDOWS):
        s = uext_ref[:, gi * dims.gd:(gi + 1) * dims.gd]
        sh = 1
        while sh < w:
            s = s + pltpu.roll(s, sh, axis=0)
            sh *= 2
        cnt = jnp.minimum(w, pos + 1).astype(F32)
        pm_parts.append(s[POOL_PREV:, :] / cnt - u[:, gi * dims.gd:(gi + 1) * dims.gd])
    a = _pool_branch(pm_parts, wpg_ref, ps_ref, wbp_ref)
    uext_ref[0:POOL_PREV, :] = uext_ref[L:L + POOL_PREV, :]
    pool_ref[0] = uext_ref[0:POOL_PREV, :]

    q = _proj(hb, win_ref, dims.cols[1])
    k = _proj(hb, win_ref, dims.cols[2])
    v = _proj(hb, win_ref, dims.cols[3])
    chunk_outs = []
    for r0 in range(0, L, RET_CHUNK):
        rows = slice(r0, r0 + RET_CHUNK)
        cosb, sinb = cos_ref[rows, :], sin_ref[rows, :]
        outs = []
        for h in range(dims.heads):
            qr = _rotary(q[rows, h * dims.dk:(h + 1) * dims.dk], cosb, sinb)
            kr = _rotary(k[rows, h * dims.dk:(h + 1) * dims.dk], cosb, sinb) * (dims.dk ** -0.5)
            vb = v[rows, h * dims.dv:(h + 1) * dims.dv].astype(BF16)
            scores = _dot_nt(qr.astype(BF16), kr.astype(BF16)) * intra_ref[h]
            s_prev = ret_ref[0, 0, h]
            o = _dot(scores.astype(BF16), vb) + _dot((qr * qdec_ref[h]).astype(BF16),
                                                      s_prev.astype(BF16))
            ret_ref[0, 0, h] = (sdec_ref[h] * s_prev
                                + _dot_tn((kr * kdec_ref[h]).astype(BF16), vb))
            outs.append(_group_norm(o))
        chunk_outs.append(jnp.concatenate(outs, axis=1))
    o = jnp.concatenate(chunk_outs, axis=0)

    g = _proj(hb, win_ref, dims.cols[4])
    ga = _proj(hb, win_ref, dims.cols[5])
    gb = _proj(hb, win_ref, dims.cols[6])
    x1 = _merge_out(x, gate, a, o, g, ga, gb, wbr_ref, wout_ref)
    x1_ref[0] = x1
    hnext_ref[...] = _rms_mod(x1, g2_ref[...], mod_ref[0, 3:4, :], mod_ref[0, 4:5, :])


def _const_spec(shape):
    zeros = (0,) * len(shape)
    return pl.BlockSpec(shape, lambda *_: zeros)


def _resident_spec(shape):
    zeros = (0,) * len(shape)
    return pl.BlockSpec(shape, lambda *_: zeros, pipeline_mode=pl.Buffered(1))


def _mixer_prompt(dims, x, mod, g1, weights, tables, router, t_total):
    b, s, d = x.shape
    L = PROMPT_STEP_ROWS
    nc = s // L
    win, wpg, ps, wbp, wbr, wout = weights
    cosb, sinb, intra, qdec, kdec, sdec = tables
    g2, wr_t, br = router
    ne = wr_t.shape[0]
    in_specs = [
        pl.BlockSpec((1, L, d), lambda i, c: (i, c, 0)),
        pl.BlockSpec((1, N_MOD, d), lambda i, c: (i, 0, 0)),
        _const_spec(g1.shape), _resident_spec(win.shape), _const_spec(wpg.shape),
        _const_spec(ps.shape), _resident_spec(wbp.shape), _resident_spec(wbr.shape),
        _resident_spec(wout.shape),
        pl.BlockSpec((L, dims.dk), lambda i, c: (c, 0)),
        pl.BlockSpec((L, dims.dk), lambda i, c: (c, 0)),
        _const_spec(intra.shape), _const_spec(qdec.shape), _const_spec(kdec.shape),
        _const_spec(sdec.shape),
        _const_spec(g2.shape), _const_spec(wr_t.shape), _const_spec(br.shape),
    ]
    out_shape = (
        jax.ShapeDtypeStruct((b, s, d), F32),
        jax.ShapeDtypeStruct((b, POOL_PREV, dims.d_pool), F32),
        jax.ShapeDtypeStruct((1, b, dims.heads, dims.dk, dims.dv), F32),
    ) + _route_out_shapes(t_total, d, ne)
    out_specs = (
        pl.BlockSpec((1, L, d), lambda i, c: (i, c, 0)),
        pl.BlockSpec((1, POOL_PREV, dims.d_pool), lambda i, c: (i, 0, 0)),
        pl.BlockSpec((1, 1, dims.heads, dims.dk, dims.dv), lambda i, c: (0, i, 0, 0, 0)),
    ) + _route_out_specs(L, ne, d, lambda i, c: jnp.maximum(i * nc + c - 1, 0))
    weight_bytes = 2 * (win.size + wpg.size + wbp.size + wbr.size + wout.size)
    return pl.pallas_call(
        functools.partial(_mixer_prompt_kernel, dims),
        out_shape=out_shape,
        grid=(b, nc),
        in_specs=in_specs,
        out_specs=out_specs,
        scratch_shapes=[pltpu.VMEM((POOL_PREV + L, dims.d_pool), F32), pltpu.VMEM((L, d), F32)],
        compiler_params=pltpu.CompilerParams(
            dimension_semantics=("arbitrary", "arbitrary"),
            vmem_limit_bytes=_vmem_limit(2 * weight_bytes + 20 * 1024 * 1024),
        ),
        name="mixer_prompt",
    )(x, mod, g1, win, wpg, ps, wbp, wbr, wout, cosb, sinb, intra, qdec, kdec, sdec,
      g2, wr_t, br)


def _mixer_sample_kernel(dims, ds, x_ref, modt_ref, g1_ref, win_ref, wpg_ref, ps_ref, wbp_ref,
                         wbr_ref, wout_ref, cos_ref, sin_ref, intra_ref, qdec_ref, kdec_ref,
                         sdec_ref, bandp_ref, bandu_ref, cnt_ref, poolin_ref, sret_ref,
                         x1_ref, u_ref, ret_ref, hb_ref, q_ref, k_ref, v_ref, pm_ref, o_ref):
    j = pl.program_id(1)
    R = cos_ref.shape[0]
    ns = R // ds

    @pl.when(j == 0)
    def _project():
        hb = _rms_mod(x_ref[...], g1_ref[...], modt_ref[0], modt_ref[1]).astype(BF16)
        hb_ref[...] = hb
        u_ref[...] = _proj(hb, win_ref, dims.cols[0])
        q_ref[...] = _proj(hb, win_ref, dims.cols[1])
        k_ref[...] = _proj(hb, win_ref, dims.cols[2])
        v_ref[...] = _proj(hb, win_ref, dims.cols[3]).astype(BF16)

    rows = pl.ds(pl.multiple_of(j * R, R), R)

    u = u_ref[rows, :]
    hist = poolin_ref[...].reshape(ns * POOL_PREV, dims.d_pool)
    pm_parts = []
    for gi in range(len(POOL_WINDOWS)):
        lo, hi = gi * dims.gd, (gi + 1) * dims.gd
        win_sum = jnp.zeros((R, dims.gd), F32)
        for part in _split_bf16(hist[:, lo:hi], 3):
            win_sum = win_sum + _dot(bandp_ref[gi], part)
        for part in _split_bf16(u[:, lo:hi], 3):
            win_sum = win_sum + _dot(bandu_ref[gi], part)
        pm_parts.append(win_sum / cnt_ref[gi] - u[:, lo:hi])
    pm_ref[rows, :] = jnp.concatenate(pm_parts, axis=1)

    q = q_ref[rows, :]
    k = k_ref[rows, :]
    v = v_ref[rows, :]
    cosb, sinb = cos_ref[...], sin_ref[...]
    row_seq = lax.broadcasted_iota(jnp.int32, (R, ns * dims.dk), 0) // ds
    col_seq = lax.broadcasted_iota(jnp.int32, (R, ns * dims.dk), 1) // dims.dk
    own = row_seq == col_seq
    outs = []
    for h in range(dims.heads):
        qr = _rotary(q[:, h * dims.dk:(h + 1) * dims.dk], cosb, sinb)
        kr = _rotary(k[:, h * dims.dk:(h + 1) * dims.dk], cosb, sinb) * (dims.dk ** -0.5)
        vb = v[:, h * dims.dv:(h + 1) * dims.dv]
        scores = _dot_nt(qr.astype(BF16), kr.astype(BF16)) * intra_ref[h]
        s_prev = sret_ref[:, h].reshape(ns * dims.dk, dims.dv)
        qd = jnp.concatenate([qr * qdec_ref[h]] * ns, axis=1)
        kd = jnp.concatenate([kr * kdec_ref[h]] * ns, axis=1)
        qexp = jnp.where(own, qd, 0.0).astype(BF16)
        kexp = jnp.where(own, kd, 0.0).astype(BF16)
        o = _dot(scores.astype(BF16), vb) + _dot(qexp, s_prev.astype(BF16))
        s_new = sdec_ref[h] * s_prev + _dot_tn(kexp, vb)
        ret_ref[:, h] = s_new.reshape(ns, dims.dk, dims.dv)
        outs.append(_group_norm(o))
    o_ref[rows, :] = jnp.concatenate(outs, axis=1)

    @pl.when(j == pl.num_programs(1) - 1)
    def _merge():
        hb = hb_ref[...]
        pm = pm_ref[...]
        pm_parts = [pm[:, gi * dims.gd:(gi + 1) * dims.gd] for gi in range(len(POOL_WINDOWS))]
        a = _pool_branch(pm_parts, wpg_ref, ps_ref, wbp_ref)
        g = _proj(hb, win_ref, dims.cols[4])
        ga = _proj(hb, win_ref, dims.cols[5])
        gb = _proj(hb, win_ref, dims.cols[6])
        x1_ref[...] = _merge_out(x_ref[...], modt_ref[2], a, o_ref[...], g, ga, gb,
                                 wbr_ref, wout_ref)


def _mixer_sample(dims, ds, x2d, modt, g1, weights, tables, pool_in, state_ret):
    t, d = x2d.shape
    ns = SAMPLE_SEQS_PER_STEP
    steps = SAMPLE_STEPS_PER_GROUP
    R = ns * ds
    RG = R * steps
    win, wpg, ps, wbp, wbr, wout = weights
    cosb, sinb, intra, qdec, kdec, sdec, bandp, bandu, cnt = tables
    db = state_ret.shape[0]
    in_specs = [
        pl.BlockSpec((RG, d), lambda i, j: (i, 0)),
        pl.BlockSpec((N_MOD, RG, d), lambda i, j: (0, i, 0)),
        _const_spec(g1.shape), _resident_spec(win.shape), _const_spec(wpg.shape),
        _const_spec(ps.shape), _resident_spec(wbp.shape), _resident_spec(wbr.shape),
        _resident_spec(wout.shape), _const_spec(cosb.shape), _const_spec(sinb.shape),
        _const_spec(intra.shape), _const_spec(qdec.shape), _const_spec(kdec.shape),
        _const_spec(sdec.shape), _const_spec(bandp.shape), _const_spec(bandu.shape),
        _const_spec(cnt.shape),
        pl.BlockSpec((ns, POOL_PREV, dims.d_pool), lambda i, j: (i * steps + j, 0, 0)),
        pl.BlockSpec((ns, dims.heads, dims.dk, dims.dv), lambda i, j: (i * steps + j, 0, 0, 0)),
    ]
    out_shape = (
        jax.ShapeDtypeStruct((t, d), F32),
        jax.ShapeDtypeStruct((t, dims.d_pool), F32),
        jax.ShapeDtypeStruct(state_ret.shape, state_ret.dtype),
    )
    out_specs = (
        pl.BlockSpec((RG, d), lambda i, j: (i, 0)),
        pl.BlockSpec((RG, dims.d_pool), lambda i, j: (i, 0)),
        pl.BlockSpec((ns, dims.heads, dims.dk, dims.dv), lambda i, j: (i * steps + j, 0, 0, 0)),
    )
    weight_bytes = 2 * (win.size + wpg.size + wbp.size + wbr.size + wout.size)
    state_bytes = 4 * ns * dims.heads * dims.dk * dims.dv
    return pl.pallas_call(
        functools.partial(_mixer_sample_kernel, dims, ds),
        out_shape=out_shape,
        grid=(db // (ns * steps), steps),
        in_specs=in_specs,
        out_specs=out_specs,
        scratch_shapes=[
            pltpu.VMEM((RG, d), BF16), pltpu.VMEM((RG, dims.d_k), F32),
            pltpu.VMEM((RG, dims.d_k), F32), pltpu.VMEM((RG, dims.d_v), BF16),
            pltpu.VMEM((RG, dims.d_pool), F32), pltpu.VMEM((RG, dims.d_v), F32),
        ],
        compiler_params=pltpu.CompilerParams(
            dimension_semantics=("arbitrary", "arbitrary"),
            vmem_limit_bytes=_vmem_limit(2 * weight_bytes + 4 * state_bytes + 8 * 1024 * 1024),
        ),
        name="mixer_sample",
    )(x2d, modt, g1, win, wpg, ps, wbp, wbr, wout, cosb, sinb, intra, qdec, kdec, sdec,
      bandp, bandu, cnt, pool_in, state_ret)


def _rotary_tables(pos, dk):
    half = dk // 2
    inv = np.power(ROPE_BASE, -np.arange(half, dtype=np.float64) / half)
    ang = pos.astype(np.float64)[:, None] * inv[None, :]
    cos, sin = np.cos(ang), np.sin(ang)
    return (np.concatenate([cos, cos], axis=1).astype(np.float32),
            np.concatenate([-sin, sin], axis=1).astype(np.float32))


def _decay_tables(log_g, chunk, reps, dk, dv):
    idx = np.arange(chunk, dtype=np.float64)
    diff = idx[:, None] - idx[None, :]
    intra = np.where((diff >= 0)[None],
                     np.exp(log_g[:, None, None] * np.maximum(diff, 0.0)[None]), 0.0)
    heads = log_g.shape[0]
    eye = np.eye(reps)
    intra = np.einsum("ab,hij->haibj", eye, intra).reshape(heads, reps * chunk, reps * chunk)
    q_dec = np.exp(log_g[:, None] * (idx[None, :] + 1.0))
    k_dec = np.exp(log_g[:, None] * (chunk - 1.0 - idx[None, :]))
    s_dec = np.exp(log_g * chunk)
    q_dec = np.broadcast_to(np.tile(q_dec, (1, reps))[:, :, None], (heads, reps * chunk, dk))
    k_dec = np.broadcast_to(np.tile(k_dec, (1, reps))[:, :, None], (heads, reps * chunk, dk))
    s_dec = np.broadcast_to(s_dec[:, None, None], (heads, 1, dv))
    return tuple(np.ascontiguousarray(t, dtype=np.float32) for t in (intra, q_dec, k_dec, s_dec))


def _sample_pool_tables(ns, ds, gd, start_pos):
    n = np.arange(ds)
    i = np.arange(POOL_PREV)
    eye = np.eye(ns, dtype=np.float32)
    bandp, bandu, cnt = [], [], []
    for w in POOL_WINDOWS:
        bp = ((i[None, :] - POOL_PREV) >= (n[:, None] - w + 1)).astype(np.float32)
        bu = ((n[None, :] <= n[:, None]) & (n[None, :] >= n[:, None] - w + 1)).astype(np.float32)
        bandp.append(np.einsum("ab,ni->anbi", eye, bp).reshape(ns * ds, ns * POOL_PREV))
        bandu.append(np.einsum("ab,nm->anbm", eye, bu).reshape(ns * ds, ns * ds))
        c = np.minimum(w, start_pos + n + 1).astype(np.float32)
        cnt.append(np.broadcast_to(np.tile(c, ns)[:, None], (ns * ds, gd)))
    return (np.stack(bandp).astype(BF16), np.stack(bandu).astype(BF16),
            np.ascontiguousarray(np.stack(cnt)))


def _exclusive_prefix(x, axis):
    pos = lax.broadcasted_iota(jnp.int32, x.shape, axis)
    acc = x
    sh = 1
    while sh < x.shape[axis]:
        acc = acc + jnp.where(pos >= sh, pltpu.roll(acc, sh, axis=axis), 0.0)
        sh *= 2
    return acc - x


def _first_max_onehot(v, idx, big, axes):
    m = jnp.max(v, axis=axes, keepdims=True)
    first = jnp.min(jnp.where(v == m, idx, big), axis=axes, keepdims=True)
    return idx == first


def _route_block(h2, blk, wr_ref, br_ref, h2_ref, pos_ref, wts_ref, cnt_ref, rank_ref, gates_ref):
    tok = slice(blk * h2.shape[0], (blk + 1) * h2.shape[0])
    hh = h2.astype(BF16)
    h2_ref[tok, :] = hh

    hl = (h2 - hh.astype(F32)).astype(BF16)
    wh, wl = _split_bf16(wr_ref[...], 2)
    logits = _dot_nt(wh, hh) + (_dot_nt(wh, hl) + _dot_nt(wl, hh))
    s = jax.nn.sigmoid(logits)
    e, t = s.shape
    ge = e // N_GROUPS
    neg = -jnp.inf
    s3 = s.reshape(N_GROUPS, ge, t)
    sb3 = s3 + br_ref[...]

    j3 = lax.broadcasted_iota(jnp.int32, (N_GROUPS, ge, t), 1)
    top1 = _first_max_onehot(sb3, j3, ge, 1)
    m1 = jnp.max(sb3, axis=1, keepdims=True)
    m2 = jnp.max(jnp.where(top1, neg, sb3), axis=1, keepdims=True)
    gv = m1 + m2

    g3 = lax.broadcasted_iota(jnp.int32, (N_GROUPS, 1, t), 0)
    gsel = jnp.zeros((N_GROUPS, 1, t), F32)
    for _ in range(TOPK_GROUPS):
        hit = _first_max_onehot(gv, g3, N_GROUPS, 0)
        gsel = jnp.where(hit, 1.0, gsel)
        gv = jnp.where(hit, neg, gv)

    e3 = lax.broadcasted_iota(jnp.int32, (N_GROUPS, ge, t), 0) * ge + j3
    ev = jnp.where(gsel > 0.5, sb3, neg)
    sel = jnp.zeros((N_GROUPS, ge, t), F32)
    chosen = jnp.zeros((N_GROUPS, ge, t), F32)
    hits = []
    for _ in range(TOP_K):
        hit = _first_max_onehot(ev, e3, e, (0, 1))
        hits.append(hit)
        sel = jnp.where(hit, s3, sel)
        chosen = jnp.where(hit, 1.0, chosen)
        ev = jnp.where(hit, neg, ev)
    tot = jnp.sum(sel, axis=(0, 1), keepdims=True)
    gates = sel / tot * ROUTED_SCALE

    chosen2 = chosen.reshape(e, t)
    rank = _exclusive_prefix(chosen2, 1)
    cnt = jnp.sum(chosen2, axis=1, keepdims=True)
    ovf_units = jnp.floor((jnp.maximum(cnt - SLOT_ROWS, 0.0) + (DMA_ROWS - 1)) / DMA_ROWS)
    ovf_off = _exclusive_prefix(jnp.broadcast_to(ovf_units, (e, LANES)), 0)[:, 0:1]
    e_col = lax.broadcasted_iota(jnp.int32, (e, 1), 0).astype(F32)
    slot = jnp.where(rank < SLOT_ROWS, e_col * SLOT_ROWS + rank,
                     e * SLOT_ROWS + ovf_off * DMA_ROWS + (rank - SLOT_ROWS))
    slot3 = slot.reshape(N_GROUPS, ge, t)
    for k, hit in enumerate(hits):
        pos_k = jnp.sum(jnp.where(hit, slot3, 0.0), axis=(0, 1), keepdims=True)
        w_k = jnp.sum(jnp.where(hit, gates, 0.0), axis=(0, 1), keepdims=True)
        pos_ref[k:k + 1, tok] = pos_k.reshape(1, t).astype(jnp.int32)
        wts_ref[k:k + 1, tok] = w_k.reshape(1, t)
    cnt_ref[blk] = jnp.broadcast_to(cnt, (e, LANES)).astype(jnp.int32)
    rank_ref[:, tok] = jnp.where(chosen2 > 0.5, rank, -1.0).astype(jnp.int32)
    gates_ref[:, tok] = gates.reshape(e, t)


def _route_out_shapes(t, d, ne):
    return (jax.ShapeDtypeStruct((t, d), BF16),
            jax.ShapeDtypeStruct((TOP_K, t), jnp.int32),
            jax.ShapeDtypeStruct((TOP_K, t), F32),
            jax.ShapeDtypeStruct((t // TOKEN_BLOCK, ne, LANES), jnp.int32),
            jax.ShapeDtypeStruct((ne, t), jnp.int32),
            jax.ShapeDtypeStruct((ne, t), F32))


def _route_out_specs(rows, ne, d, idx):
    return (pl.BlockSpec((rows, d), lambda *g: (idx(*g), 0)),
            pl.BlockSpec((TOP_K, rows), lambda *g: (0, idx(*g))),
            pl.BlockSpec((TOP_K, rows), lambda *g: (0, idx(*g))),
            pl.BlockSpec((rows // TOKEN_BLOCK, ne, LANES), lambda *g: (idx(*g), 0, 0)),
            pl.BlockSpec((ne, rows), lambda *g: (0, idx(*g))),
            pl.BlockSpec((ne, rows), lambda *g: (0, idx(*g))))


def _route_rows_kernel(per_token, x_ref, mod_ref, g2_ref, wr_ref, br_ref, *refs):
    outs = refs[len(refs) // 2:]
    if per_token:
        shift, scale = mod_ref[3], mod_ref[4]
    else:
        shift, scale = mod_ref[0, 3:4, :], mod_ref[0, 4:5, :]
    h2 = _rms_mod(x_ref[...], g2_ref[...], shift, scale)
    _route_block(h2, 0, wr_ref, br_ref, *outs)


def _route_rows(x2d, first_row, n_rows, mod, per_token, g2, wr_t, br, routed, first_block):
    d = x2d.shape[1]
    tb = TOKEN_BLOCK
    ne = wr_t.shape[0]
    t = routed[0].shape[0]
    x_block0 = first_row // tb
    if per_token:
        mod_spec = pl.BlockSpec((N_MOD, tb, d), lambda i: (0, x_block0 + i, 0))
    else:
        mod_spec = pl.BlockSpec((1, N_MOD, d), lambda i: (0, 0, 0))
    n_in = 5
    return pl.pallas_call(
        functools.partial(_route_rows_kernel, per_token),
        out_shape=_route_out_shapes(t, d, ne),
        grid=(n_rows // tb,),
        in_specs=[
            pl.BlockSpec((tb, d), lambda i: (x_block0 + i, 0)),
            mod_spec,
            _const_spec(g2.shape), _const_spec(wr_t.shape), _const_spec(br.shape),
        ] + [pl.BlockSpec(memory_space=pl.ANY)] * len(routed),
        out_specs=_route_out_specs(tb, ne, d, lambda i: first_block + i),
        input_output_aliases={n_in + k: k for k in range(len(routed))},
        compiler_params=pltpu.CompilerParams(
            dimension_semantics=("arbitrary",),
            vmem_limit_bytes=_vmem_limit(32 * 1024 * 1024),
        ),
        name="route_rows",
    )(x2d, mod, g2, wr_t, br, *routed)


class _Layout:
    def __init__(self, counts):
        nb, ne = counts.shape
        u, g, tm = SLOT_ROWS, DMA_ROWS, EXPERT_TILE
        pc = (counts + g - 1) // g * g
        ovf_units = (jnp.maximum(counts - u, 0) + g - 1) // g
        rows_e = jnp.sum(pc, axis=0)
        region = (rows_e + u + tm - 1) // tm * tm
        goff = jnp.cumsum(region) - region
        dst = goff[None, :] + jnp.cumsum(pc, axis=0) - pc
        self.dst = dst.reshape(-1)
        ovf_end = jnp.cumsum(ovf_units, axis=1)
        ovf_first = ovf_end - ovf_units
        unit = jnp.arange(_overflow_units_cap(ne))[None, :, None]
        owns = (ovf_first[:, None, :] <= unit) & (unit < ovf_end[:, None, :])
        row = dst[:, None, :] + u + (unit - ovf_first[:, None, :]) * g
        self.ovf_dst = jnp.sum(jnp.where(owns, row, 0), axis=2).reshape(-1)
        self.ovf_tot = ovf_end[:, -1]
        self.tail_start = goff + rows_e
        self.tail_units = (region - rows_e) // g
        tiles = region // tm
        self.region_tiles = jnp.concatenate([tiles, jnp.sum(tiles, keepdims=True)])


def _overflow_units_cap(ne):
    return TOKEN_BLOCK * TOP_K // DMA_ROWS + ne


def _block_buffer_rows(tb, ne):
    rows = ne * SLOT_ROWS + tb * TOP_K + ne * (DMA_ROWS - 1)
    return (rows + ONEHOT_ROWS - 1) // ONEHOT_ROWS * ONEHOT_ROWS


def _expert_rows_cap(t, tb, ne):
    nb = t // tb
    rows = t * TOP_K + nb * ne * (DMA_ROWS - 1) + ne * (SLOT_ROWS + EXPERT_TILE - 1)
    return (rows + EXPERT_TILE - 1) // EXPERT_TILE * EXPERT_TILE


def _slot_copy(buf_ref, slot, e, hbm_ref, row, sem, to_hbm):
    src = buf_ref.at[slot, pl.ds(pl.multiple_of(e * SLOT_ROWS, DMA_ROWS), SLOT_ROWS)]
    dst = hbm_ref.at[pl.ds(pl.multiple_of(row, DMA_ROWS), SLOT_ROWS)]
    return pltpu.make_async_copy(src, dst, sem) if to_hbm else pltpu.make_async_copy(dst, src, sem)


def _unit_copy(buf_ref, slot, buf_row, hbm_ref, row, sem, to_hbm):
    src = buf_ref.at[slot, pl.ds(pl.multiple_of(buf_row, DMA_ROWS), DMA_ROWS)]
    dst = hbm_ref.at[pl.ds(pl.multiple_of(row, DMA_ROWS), DMA_ROWS)]
    return pltpu.make_async_copy(src, dst, sem) if to_hbm else pltpu.make_async_copy(dst, src, sem)


def _slots_wait(ne, buf_ref, slot, hbm_ref, sem, to_hbm):
    src = buf_ref.at[slot, pl.ds(0, ne * SLOT_ROWS)]
    dst = hbm_ref.at[pl.ds(0, ne * SLOT_ROWS)]
    c = pltpu.make_async_copy(src, dst, sem) if to_hbm else pltpu.make_async_copy(dst, src, sem)
    c.wait()


def _overflow_copies(ne, blk, slot, odst_ref, otot_ref, buf_ref, hbm_ref, sem, to_hbm, wait):
    def per_unit(i, carry):
        row = odst_ref[blk * _overflow_units_cap(ne) + i]
        c = _unit_copy(buf_ref, slot, ne * SLOT_ROWS + i * DMA_ROWS, hbm_ref, row, sem, to_hbm)
        c.wait() if wait else c.start()
        return carry

    lax.fori_loop(0, otot_ref[blk], per_unit, 0)


def _slot_chunks(ne):
    return [range(c, c + SLOT_CHUNK_EXPERTS) for c in range(0, ne, SLOT_CHUNK_EXPERTS)]


def _overflow_chunks(n_units):
    return lax.shift_right_logical(n_units * DMA_ROWS + (ONEHOT_ROWS - 1),
                                   ONEHOT_ROWS.bit_length() - 1)


def _overflow_select(pos, vals, r0):
    rows = lax.broadcasted_iota(jnp.int32, (ONEHOT_ROWS, pos.shape[1]), 0) + r0
    sel = jnp.zeros(rows.shape, F32)
    for k in range(TOP_K):
        v = 1.0 if vals is None else vals[k:k + 1, :]
        sel = jnp.where(pos[k:k + 1, :] == rows, v, sel)
    return sel


def _dispatch_kernel(ne, dst_ref, odst_ref, otot_ref, tails_ref, tailu_ref,
                     h2_ref, rank_ref, pos_ref, xs_hbm, buf_ref, zero_ref, sem, zsem):
    b = pl.program_id(0)
    nb = pl.num_programs(0)
    slot = b % 2
    h2 = h2_ref[...]
    j = lax.broadcasted_iota(jnp.int32, (SLOT_ROWS, h2.shape[0]), 0)

    def zero_copy(row, rows, wait):
        c = pltpu.make_async_copy(zero_ref.at[pl.ds(0, rows)],
                                  xs_hbm.at[pl.ds(pl.multiple_of(row, DMA_ROWS), rows)], zsem)
        c.wait() if wait else c.start()

    def far_tail(e, wait):
        per_big = ONEHOT_ROWS // DMA_ROWS
        units = tailu_ref[e] - SLOT_ROWS // DMA_ROWS
        n_big = lax.shift_right_logical(units, per_big.bit_length() - 1)
        start = tails_ref[e] + SLOT_ROWS

        def big(i, carry):
            zero_copy(start + i * ONEHOT_ROWS, ONEHOT_ROWS, wait)
            return carry

        def small(i, carry):
            zero_copy(start + n_big * ONEHOT_ROWS + i * DMA_ROWS, DMA_ROWS, wait)
            return carry

        lax.fori_loop(0, n_big, big, 0)
        lax.fori_loop(0, units - n_big * per_big, small, 0)

    @pl.when(b == 0)
    def _zeros():
        zero_ref[...] = jnp.zeros_like(zero_ref)

    @pl.when(b < ne)
    def _far_start():
        far_tail(b, False)

    chunks = _slot_chunks(ne)
    first_issue = max(len(chunks) - 2, 0)
    for ci, experts in enumerate(chunks):
        onehot = jnp.concatenate(
            [jnp.where(rank_ref[e:e + 1, :] == j, 1.0, 0.0) for e in experts], axis=0)
        r0 = experts[0] * SLOT_ROWS
        buf_ref[slot, r0:r0 + len(experts) * SLOT_ROWS, :] = (
            _dot(onehot.astype(BF16), h2).astype(BF16))
        if ci < first_issue:
            continue
        if ci == first_issue:
            @pl.when(b > 0)
            def _prev():
                _slots_wait(ne, buf_ref, 1 - slot, xs_hbm, sem.at[1 - slot], True)
                _overflow_copies(ne, b - 1, 1 - slot, odst_ref, otot_ref, buf_ref, xs_hbm,
                                 sem.at[1 - slot], True, True)
        ready = [e for c in chunks[:ci + 1] for e in c] if ci == first_issue else experts
        for e in ready:
            _slot_copy(buf_ref, slot, e, xs_hbm, dst_ref[b * ne + e], sem.at[slot], True).start()

    pos = pos_ref[...]

    def chunk(ci, carry):
        r0 = pl.multiple_of(ne * SLOT_ROWS + ci * ONEHOT_ROWS, ONEHOT_ROWS)
        onehot = _overflow_select(pos, None, r0).astype(BF16)
        buf_ref[slot, pl.ds(r0, ONEHOT_ROWS), :] = _dot(onehot, h2).astype(BF16)
        return carry

    lax.fori_loop(0, _overflow_chunks(otot_ref[b]), chunk, 0)
    _overflow_copies(ne, b, slot, odst_ref, otot_ref, buf_ref, xs_hbm, sem.at[slot], True, False)

    @pl.when(b < ne)
    def _far_done():
        far_tail(b, True)

    @pl.when(b == nb - 1)
    def _last():
        _slots_wait(ne, buf_ref, slot, xs_hbm, sem.at[slot], True)
        _overflow_copies(ne, b, slot, odst_ref, otot_ref, buf_ref, xs_hbm, sem.at[slot], True,
                         True)
        def near_tail(wait):
            def per_expert(e, carry):
                zero_copy(tails_ref[e], SLOT_ROWS, wait)
                return carry
            lax.fori_loop(0, ne, per_expert, 0)

        near_tail(False)
        near_tail(True)


def _dispatch(h2, rank, pos, lay, ne, rows_cap):
    t, d = h2.shape
    tb = TOKEN_BLOCK
    buf_rows = _block_buffer_rows(tb, ne)
    assert t // tb >= ne
    return pl.pallas_call(
        functools.partial(_dispatch_kernel, ne),
        out_shape=jax.ShapeDtypeStruct((rows_cap, d), BF16),
        grid_spec=pltpu.PrefetchScalarGridSpec(
            num_scalar_prefetch=5,
            grid=(t // tb,),
            in_specs=[
                pl.BlockSpec((tb, d), lambda i, *_: (i, 0)),
                pl.BlockSpec((ne, tb), lambda i, *_: (0, i)),
                pl.BlockSpec((TOP_K, tb), lambda i, *_: (0, i)),
            ],
            out_specs=pl.BlockSpec(memory_space=pl.ANY),
            scratch_shapes=[
                pltpu.VMEM((2, buf_rows, d), BF16),
                pltpu.VMEM((ONEHOT_ROWS, d), BF16),
                pltpu.SemaphoreType.DMA((2,)),
                pltpu.SemaphoreType.DMA(()),
            ],
        ),
        compiler_params=pltpu.CompilerParams(
            dimension_semantics=("arbitrary",),
            vmem_limit_bytes=_vmem_limit(4 * buf_rows * d + 16 * 1024 * 1024),
        ),
        name="dispatch",
    )(lay.dst, lay.ovf_dst, lay.ovf_tot, lay.tail_start, lay.tail_units, h2, rank, pos)


def _experts_kernel(tiles_ref, xe_hbm, wg_ref, wu_ref, wd_ref, ye_hbm,
                    xbuf_ref, ybuf_ref, wgb_ref, wub_ref, wdb_ref, done_ref, xsem, ysem):
    e = pl.program_id(0)
    ne = pl.num_programs(0)
    tm = xbuf_ref.shape[1]

    def x_copy(row, slot):
        return pltpu.make_async_copy(xe_hbm.at[pl.ds(pl.multiple_of(row, tm), tm)],
                                     xbuf_ref.at[slot], xsem.at[slot])

    def y_copy(row, slot):
        return pltpu.make_async_copy(ybuf_ref.at[slot],
                                     ye_hbm.at[pl.ds(pl.multiple_of(row, tm), tm)], ysem.at[slot])

    total = tiles_ref[ne]

    @pl.when(e == 0)
    def _first():
        done_ref[0] = 0
        for g in range(RING_SLOTS - 1):
            x_copy(g * tm, g).start()

    wgb_ref[...] = wg_ref[0].astype(BF16)
    wub_ref[...] = wu_ref[0].astype(BF16)
    wdb_ref[...] = wd_ref[0].astype(BF16)
    done = done_ref[0]
    n = tiles_ref[e]

    def tile(j, carry):
        g = done + j
        slot = lax.rem(g, RING_SLOTS)
        x_copy(0, slot).wait()
        ahead = g + (RING_SLOTS - 1)

        @pl.when(ahead < total)
        def _prefetch():
            x_copy(ahead * tm, lax.rem(ahead, RING_SLOTS)).start()

        @pl.when(g >= RING_SLOTS)
        def _reuse():
            y_copy(0, slot).wait()

        part = tm // EXPERT_TILE_PARTS
        starts = range(0, tm, part)
        hids = []
        for r0 in starts:
            x = xbuf_ref[slot, r0:r0 + part, :]
            hids.append((jax.nn.silu(_dot(x, wgb_ref[...])) * _dot(x, wub_ref[...])).astype(BF16))
        for r0, hid in zip(starts, hids):
            ybuf_ref[slot, r0:r0 + part, :] = _dot(hid, wdb_ref[...]).astype(BF16)
        y_copy(g * tm, slot).start()
        return carry

    lax.fori_loop(0, n, tile, 0)
    done_ref[0] = done + n

    @pl.when(e == ne - 1)
    def _drain():
        for s in range(RING_SLOTS):
            y_copy(0, s).wait()


def _experts(xs, lay, wg, wu, wd):
    rows_cap, d = xs.shape
    ne, _, de = wg.shape
    assert ne >= RING_SLOTS
    tm = EXPERT_TILE
    ring = RING_SLOTS
    return pl.pallas_call(
        _experts_kernel,
        out_shape=jax.ShapeDtypeStruct((rows_cap, d), BF16),
        grid_spec=pltpu.PrefetchScalarGridSpec(
            num_scalar_prefetch=1,
            grid=(ne,),
            in_specs=[
                pl.BlockSpec(memory_space=pl.ANY),
                pl.BlockSpec((1, d, de), lambda e, *_: (e, 0, 0)),
                pl.BlockSpec((1, d, de), lambda e, *_: (e, 0, 0)),
                pl.BlockSpec((1, de, d), lambda e, *_: (e, 0, 0)),
            ],
            out_specs=pl.BlockSpec(memory_space=pl.ANY),
            scratch_shapes=[
                pltpu.VMEM((ring, tm, d), BF16), pltpu.VMEM((ring, tm, d), BF16),
                pltpu.VMEM((d, de), BF16), pltpu.VMEM((d, de), BF16), pltpu.VMEM((de, d), BF16),
                pltpu.SMEM((1,), jnp.int32),
                pltpu.SemaphoreType.DMA((ring,)), pltpu.SemaphoreType.DMA((ring,)),
            ],
        ),
        compiler_params=pltpu.CompilerParams(
            dimension_semantics=("arbitrary",),
            vmem_limit_bytes=_vmem_limit(32 * 1024 * 1024),
        ),
        name="experts",
    )(lay.region_tiles, xs, wg, wu, wd)


def _final_kernel(n_prompt_blocks, ne, dst_ref, odst_ref, otot_ref,
                  xp_ref, xs_ref, modp_ref, modt_ref, h2_ref, rank_ref, gates_ref, pos_ref, wts_ref,
                  wsg_ref, wsu_ref, wsd_ref, gf_ref, ye_hbm, yp_ref, ys_ref,
                  buf_ref, acc_ref, sem):
    i = pl.program_id(0)
    nb = pl.num_programs(0)
    slot = lax.rem(i, FINAL_BUFFERS)
    win = FINAL_WINDOW_UNITS

    def window(blk, sl, lo, hi, wait):
        def per_unit(u, carry):
            row = odst_ref[blk * _overflow_units_cap(ne) + u]
            c = _unit_copy(buf_ref, sl, ne * SLOT_ROWS + (u - lo) * DMA_ROWS, ye_hbm, row,
                           sem.at[sl], False)
            c.wait() if wait else c.start()
            return carry
        lax.fori_loop(lo, hi, per_unit, 0)

    def fetch(blk, sl):
        for e in range(ne):
            _slot_copy(buf_ref, sl, e, ye_hbm, dst_ref[blk * ne + e], sem.at[sl], False).start()
        window(blk, sl, 0, jnp.minimum(otot_ref[blk], win), False)

    @pl.when(i == 0)
    def _first():
        buf_ref[...] = jnp.zeros_like(buf_ref)
        for blk in range(FINAL_BUFFERS - 1):
            fetch(blk, blk)

    ahead = i + (FINAL_BUFFERS - 1)

    @pl.when(ahead < nb)
    def _prefetch():
        fetch(ahead, lax.rem(ahead, FINAL_BUFFERS))

    _slots_wait(ne, buf_ref, slot, ye_hbm, sem.at[slot], False)
    window(i, slot, 0, jnp.minimum(otot_ref[i], win), True)

    tb = h2_ref.shape[0]
    j = lax.broadcasted_iota(jnp.int32, (SLOT_ROWS, tb), 0)
    acc = jnp.zeros((tb, h2_ref.shape[1]), F32)
    for experts in _slot_chunks(ne):
        sel = jnp.concatenate(
            [jnp.where(rank_ref[e:e + 1, :] == j, gates_ref[e:e + 1, :], 0.0) for e in experts],
            axis=0)
        r0 = experts[0] * SLOT_ROWS
        acc = acc + _dot_tn(sel.astype(BF16), buf_ref[slot, r0:r0 + len(experts) * SLOT_ROWS, :])
    acc_ref[...] = acc

    pos = pos_ref[...]
    wts = wts_ref[...]

    n_units = otot_ref[i]

    def ovf_pass(p, carry):
        lo = p * win
        hi = jnp.minimum(n_units, lo + win)

        @pl.when(p > 0)
        def _refill():
            window(i, slot, lo, hi, False)
            window(i, slot, lo, hi, True)

        pos_p = pos - lo * DMA_ROWS

        def chunk(ci, carry2):
            r0 = pl.multiple_of(ne * SLOT_ROWS + ci * ONEHOT_ROWS, ONEHOT_ROWS)
            sel = _overflow_select(pos_p, wts, r0).astype(BF16)
            acc_ref[...] += _dot_tn(sel, buf_ref[slot, pl.ds(r0, ONEHOT_ROWS), :])
            return carry2

        lax.fori_loop(0, _overflow_chunks(hi - lo), chunk, 0)
        return carry

    n_pass = lax.shift_right_logical(n_units + (win - 1), win.bit_length() - 1)
    lax.fori_loop(0, n_pass, ovf_pass, 0)

    is_p = i < n_prompt_blocks
    x = jnp.where(is_p, xp_ref[...], xs_ref[...])
    gate = jnp.where(is_p, modp_ref[0, 5:6, :], modt_ref[5])
    hb = h2_ref[...]
    hid = jax.nn.silu(_dot(hb, wsg_ref[...])) * _dot(hb, wsu_ref[...])
    f = acc_ref[...] + _dot(hid.astype(BF16), wsd_ref[...])
    x2 = x + gate * f
    y = x2 * lax.rsqrt(jnp.mean(x2 * x2, axis=-1, keepdims=True) + EPS) * gf_ref[...]

    @pl.when(is_p)
    def _p():
        yp_ref[...] = y

    @pl.when(jnp.logical_not(is_p))
    def _s():
        ys_ref[...] = y


def _final(xp2d, xs2d, modp, modt, h2, rank, gates, pos, wts, ye, lay, ne, wsg, wsu, wsd, gf):
    tp, d = xp2d.shape
    ts = xs2d.shape[0]
    tb = TOKEN_BLOCK
    npb, nsb = tp // tb, ts // tb
    seq = tp // modp.shape[0]
    bpb = seq // tb
    buf_rows = ne * SLOT_ROWS + FINAL_WINDOW_UNITS * DMA_ROWS
    assert npb + nsb >= FINAL_BUFFERS and (FINAL_WINDOW_UNITS * DMA_ROWS) % ONEHOT_ROWS == 0

    def p_idx(i):
        return jnp.minimum(i, npb - 1)

    def s_idx(i):
        return jnp.maximum(i - npb, 0)

    def const(shape):
        zeros = (0,) * len(shape)
        return pl.BlockSpec(shape, lambda i, *_: zeros)

    return pl.pallas_call(
        functools.partial(_final_kernel, npb, ne),
        out_shape=(jax.ShapeDtypeStruct((tp, d), F32), jax.ShapeDtypeStruct((ts, d), F32)),
        grid_spec=pltpu.PrefetchScalarGridSpec(
            num_scalar_prefetch=3,
            grid=(npb + nsb,),
            in_specs=[
                pl.BlockSpec((tb, d), lambda i, *_: (p_idx(i), 0)),
                pl.BlockSpec((tb, d), lambda i, *_: (s_idx(i), 0)),
                pl.BlockSpec((1, N_MOD, d), lambda i, *_: (p_idx(i) // bpb, 0, 0)),
                pl.BlockSpec((N_MOD, tb, d), lambda i, *_: (0, s_idx(i), 0)),
                pl.BlockSpec((tb, d), lambda i, *_: (i, 0)),
                pl.BlockSpec((ne, tb), lambda i, *_: (0, i)),
                pl.BlockSpec((ne, tb), lambda i, *_: (0, i)),
                pl.BlockSpec((TOP_K, tb), lambda i, *_: (0, i)),
                pl.BlockSpec((TOP_K, tb), lambda i, *_: (0, i)),
                const(wsg.shape), const(wsu.shape), const(wsd.shape), const(gf.shape),
                pl.BlockSpec(memory_space=pl.ANY),
            ],
            out_specs=(pl.BlockSpec((tb, d), lambda i, *_: (p_idx(i), 0)),
                       pl.BlockSpec((tb, d), lambda i, *_: (s_idx(i), 0))),
            scratch_shapes=[
                pltpu.VMEM((FINAL_BUFFERS, buf_rows, d), BF16),
                pltpu.VMEM((tb, d), F32),
                pltpu.SemaphoreType.DMA((FINAL_BUFFERS,)),
            ],
        ),
        compiler_params=pltpu.CompilerParams(
            dimension_semantics=("arbitrary",),
            vmem_limit_bytes=_vmem_limit(2 * FINAL_BUFFERS * buf_rows * d + 32 * 1024 * 1024),
        ),
        name="final",
    )(lay.dst, lay.ovf_dst, lay.ovf_tot,
      xp2d, xs2d, modp, modt, h2, rank, gates, pos, wts, wsg, wsu, wsd, gf, ye)


def kernel(x_prompt, x_sample, c_prompt, c_sample, state_pool, state_ret, w_ada, b_ada, g_norm1,
           w_in, w_pool_group, pool_scale, w_branch_pool, w_branch_ret, w_out, g_norm2, w_router,
           b_router, w_exp_gate, w_exp_up, w_exp_down, w_sh_gate, w_sh_up, w_sh_down, g_final):
    b, s, d = x_prompt.shape
    db, ds, _ = x_sample.shape
    depth = w_ada.shape[0]
    heads, dk, dv = state_ret.shape[2:]
    pool_buf, d_pool = state_pool.shape[2:]
    dims = _Dims(d, d_pool, heads, dk, dv)
    assert dims.d_in == w_in.shape[2]
    assert pool_buf == max(POOL_WINDOWS) - 1 and pool_buf < POOL_PREV
    assert s % PROMPT_STEP_ROWS == 0 and PROMPT_STEP_ROWS % RET_CHUNK == 0
    assert s % TOKEN_BLOCK == 0 and (db * ds) % TOKEN_BLOCK == 0
    assert db % (SAMPLE_SEQS_PER_STEP * SAMPLE_STEPS_PER_GROUP) == 0
    assert SLOT_ROWS % DMA_ROWS == 0 and (w_router.shape[2] * SLOT_ROWS) % ONEHOT_ROWS == 0

    log_g = np.log(1.0 - np.power(2.0, -5.0 - np.arange(heads, dtype=np.float64)))
    ns = SAMPLE_SEQS_PER_STEP
    cos_p, sin_p = _rotary_tables(np.arange(s), dk)
    cos_s, sin_s = _rotary_tables(PAST_LEN + np.arange(ds), dk)
    cos_s, sin_s = np.tile(cos_s, (ns, 1)), np.tile(sin_s, (ns, 1))
    dec_p = _decay_tables(log_g, RET_CHUNK, 1, dk, dv)
    dec_s = _decay_tables(log_g, ds, ns, dk, dv)
    pool_tabs = _sample_pool_tables(ns, ds, dims.gd, PAST_LEN)

    assert depth == 1, "the final kernel fuses the output norm into the (single) layer"
    l = 0
    xs = x_sample.reshape(db * ds, d)
    modp, modt = _ada(c_prompt, jnp.repeat(c_sample, ds, axis=0), w_ada[l], b_ada[l])
    modp = modp.reshape(b, N_MOD, d)
    g1 = g_norm1[l].reshape(1, d)
    weights = (w_in[l].astype(BF16), w_pool_group[l].astype(BF16),
               pool_scale[l].reshape(1, d_pool), w_branch_pool[l].astype(BF16),
               w_branch_ret[l].astype(BF16), w_out[l].astype(BF16))

    ne = w_router.shape[2]
    router = (g_norm2[l].reshape(1, d), w_router[l].T,
              b_router[l].reshape(N_GROUPS, ne // N_GROUPS, 1))
    x1p, pool16, ret_p, *routed = _mixer_prompt(dims, x_prompt, modp, g1, weights,
                                                (cos_p, sin_p) + dec_p, router,
                                                b * s + db * ds)
    pool_in = jnp.pad(state_pool[l], ((0, 0), (POOL_PREV - pool_buf, 0), (0, 0)))
    x1s, u_s, ret_s = _mixer_sample(dims, ds, xs, modt, g1, weights,
                                    (cos_s, sin_s) + dec_s + pool_tabs, pool_in,
                                    state_ret[l].astype(F32))
    pool_p = pool16[:, POOL_PREV - pool_buf:]
    pool_s = jnp.concatenate([state_pool[l], u_s.reshape(db, ds, d_pool)], axis=1)[:, -pool_buf:]

    x1p2d = x1p.reshape(b * s, d)
    last = b * s - PROMPT_STEP_ROWS
    routed = _route_rows(x1p2d, last, PROMPT_STEP_ROWS, modp[b - 1:b], False, *router, routed,
                         last // TOKEN_BLOCK)
    h2, pos, wts, counts, rank, gates = _route_rows(x1s, 0, db * ds, modt, True, *router, routed,
                                                    (b * s) // TOKEN_BLOCK)
    rows_cap = _expert_rows_cap(b * s + db * ds, TOKEN_BLOCK, ne)
    lay = _Layout(counts[:, :, 0])
    xe = _dispatch(h2, rank, pos, lay, ne, rows_cap)
    ye = _experts(xe, lay, w_exp_gate[l], w_exp_up[l], w_exp_down[l])
    yp, ys = _final(x1p2d, x1s, modp, modt, h2, rank, gates, pos, wts, ye, lay, ne,
                    w_sh_gate[l].astype(BF16), w_sh_up[l].astype(BF16),
                    w_sh_down[l].astype(BF16), g_final.reshape(1, d))

    return (yp.reshape(b, s, d), ys.reshape(db, ds, d), pool_p[None],
            ret_p.astype(x_prompt.dtype), pool_s[None], ret_s.astype(state_ret.dtype)[None])
```

```python
import functools

import jax
import jax.numpy as jnp
import numpy as np
from jax import lax
from jax.experimental import pallas as pl
from jax.experimental.pallas import tpu as pltpu

F32 = jnp.float32
BF16 = jnp.bfloat16

PAST_LEN = 16384
POOL_WINDOWS = (2, 4, 8, 16)
RET_CHUNK = 128
ROPE_BASE = 10000.0
TOP_K = 8
N_GROUPS = 8
TOPK_GROUPS = 4
ROUTED_SCALE = 2.5
N_MOD = 6
EPS = 1e-6

V7X_VMEM_BYTES = 64 * 1024 * 1024
LANES = 128

PROMPT_STEP_ROWS = 512
SAMPLE_SEQS_PER_STEP = 8
SAMPLE_STEPS_PER_GROUP = 4
TOKEN_BLOCK = 256
DMA_ROWS = 16
SLOT_ROWS = 3 * DMA_ROWS
ONEHOT_ROWS = 256
SLOT_CHUNK_EXPERTS = 16
EXPERT_TILE = 1024
EXPERT_TILE_PARTS = 4
RING_SLOTS = 4
POOL_PREV = 16


def _vmem_limit(nbytes):
    return int(min(nbytes, V7X_VMEM_BYTES - 8 * 1024 * 1024))


def _dot(a, b):
    return jnp.dot(a, b, preferred_element_type=F32)


def _dot_nt(a, b):
    return lax.dot_general(a, b, (((1,), (1,)), ((), ())), preferred_element_type=F32)


def _dot_tn(a, b):
    return lax.dot_general(a, b, (((0,), (0,)), ((), ())), preferred_element_type=F32)


def _split_bf16(x, parts):
    out = []
    for _ in range(parts):
        p = x.astype(BF16)
        out.append(p)
        x = x - p.astype(F32)
    return out


def _rms_mod(x, g, shift, scale):
    y = x * lax.rsqrt(jnp.mean(x * x, axis=-1, keepdims=True) + EPS)
    return (y * g) * (1.0 + scale) + shift


def _rotary(x, cosb, sinb):
    return x * cosb + pltpu.roll(x, x.shape[-1] // 2, axis=1) * sinb


def _group_norm(o):
    return o * lax.rsqrt(jnp.mean(o * o, axis=-1, keepdims=True) + EPS)


def _ada_kernel(cp_ref, ct_ref, w_ref, b_ref, modp_ref, modt_ref):
    w = w_ref[...].astype(BF16)
    modp_ref[...] = _dot(cp_ref[...].astype(BF16), w) + b_ref[...]
    modt_ref[0] = _dot(ct_ref[...].astype(BF16), w) + b_ref[...]


def _ada(c_prompt, c_tokens, w_ada, b_ada):
    bp, d = c_prompt.shape
    ts = c_tokens.shape[0]
    n = w_ada.shape[1]
    return pl.pallas_call(
        _ada_kernel,
        out_shape=(jax.ShapeDtypeStruct((bp, n), F32), jax.ShapeDtypeStruct((n // d, ts, d), F32)),
        grid=(n // d,),
        in_specs=[
            pl.BlockSpec((bp, d), lambda j: (0, 0)),
            pl.BlockSpec((ts, d), lambda j: (0, 0)),
            pl.BlockSpec((d, d), lambda j: (0, j)),
            pl.BlockSpec((1, d), lambda j: (0, j)),
        ],
        out_specs=(pl.BlockSpec((bp, d), lambda j: (0, j)),
                   pl.BlockSpec((1, ts, d), lambda j: (j, 0, 0))),
        compiler_params=pltpu.CompilerParams(
            dimension_semantics=("arbitrary",),
            vmem_limit_bytes=_vmem_limit(32 * 1024 * 1024),
        ),
        name="ada",
    )(c_prompt, c_tokens, w_ada, b_ada.reshape(1, n))


class _Dims:
    def __init__(self, d_model, d_pool, heads, dk, dv):
        self.d = d_model
        self.d_pool = d_pool
        self.gd = d_pool // len(POOL_WINDOWS)
        self.heads = heads
        self.dk = dk
        self.dv = dv
        self.d_k = heads * dk
        self.d_v = heads * dv
        widths = (d_pool, self.d_k, self.d_k, self.d_v, self.d_v, d_model, d_model)
        offs = [0]
        for w in widths:
            offs.append(offs[-1] + w)
        self.cols = tuple((offs[i], offs[i + 1]) for i in range(len(widths)))
        self.d_in = offs[-1]


def _proj(hb, win_ref, cols):
    return _dot(hb, win_ref[:, cols[0]:cols[1]])


def _pool_branch(pm_parts, wpg_ref, ps_ref, wbp_ref):
    pg = [_dot(pm.astype(BF16), wpg_ref[gi]) for gi, pm in enumerate(pm_parts)]
    pm = jnp.concatenate(pg, axis=1) * ps_ref[...]
    return _dot(pm.astype(BF16), wbp_ref[...])


def _merge_out(x, gate, a, o, g, ga, gb, wbr_ref, wout_ref):
    r = _dot((jax.nn.silu(g) * o).astype(BF16), wbr_ref[...])
    merged = jax.nn.sigmoid(ga) * a + jax.nn.sigmoid(gb) * r
    m = _dot(merged.astype(BF16), wout_ref[...])
    return x + gate * m


def _mixer_prompt_kernel(dims, x_ref, mod_ref, g1_ref, win_ref, wpg_ref, ps_ref, wbp_ref,
                         wbr_ref, wout_ref, cos_ref, sin_ref, intra_ref, qdec_ref,
                         kdec_ref, sdec_ref, g2_ref, wr_ref, br_ref,
                         x1_ref, pool_ref, ret_ref, h2_ref, pos_ref, wts_ref, cnt_ref, rank_ref,
                         gates_ref, uext_ref, hnext_ref):
    c = pl.program_id(1)
    L = x_ref.shape[1]

    @pl.when(jnp.logical_and(pl.program_id(0) == 0, c == 0))
    def _init_route():
        hnext_ref[...] = jnp.zeros_like(hnext_ref)

    @pl.when(c == 0)
    def _init():
        ret_ref[...] = jnp.zeros_like(ret_ref)
        uext_ref[0:POOL_PREV, :] = jnp.zeros((POOL_PREV, dims.d_pool), F32)

    def route_prev(blk):
        _route_block(hnext_ref[blk * TOKEN_BLOCK:(blk + 1) * TOKEN_BLOCK, :], blk,
                     wr_ref, br_ref, h2_ref, pos_ref, wts_ref, cnt_ref, rank_ref, gates_ref)

    for blk in range(L // TOKEN_BLOCK):
        route_prev(blk)
    x = x_ref[0]
    shift, scale, gate = mod_ref[0, 0:1, :], mod_ref[0, 1:2, :], mod_ref[0, 2:3, :]
    hb = _rms_mod(x, g1_ref[...], shift, scale).astype(BF16)

    u = _proj(hb, win_ref, dims.cols[0])
    uext_ref[POOL_PREV:POOL_PREV + L, :] = u
    pos = c * L + lax.broadcasted_iota(jnp.int32, (L, 1), 0)
    pm_parts = []
    for gi, w in enumerate(POOL_WINDOWS):
        s = uext_ref[:, gi * dims.gd:(gi + 1) * dims.gd]
        sh = 1
        while sh < w:
            s = s + pltpu.roll(s, sh, axis=0)
            sh *= 2
        cnt = jnp.minimum(w, pos + 1).astype(F32)
        pm_parts.append(s[POOL_PREV:, :] / cnt - u[:, gi * dims.gd:(gi + 1) * dims.gd])
    a = _pool_branch(pm_parts, wpg_ref, ps_ref, wbp_ref)
    uext_ref[0:POOL_PREV, :] = uext_ref[L:L + POOL_PREV, :]
    pool_ref[0] = uext_ref[0:POOL_PREV, :]

    q = _proj(hb, win_ref, dims.cols[1])
    k = _proj(hb, win_ref, dims.cols[2])
    v = _proj(hb, win_ref, dims.cols[3])
    chunk_outs = []
    for r0 in range(0, L, RET_CHUNK):
        rows = slice(r0, r0 + RET_CHUNK)
        cosb, sinb = cos_ref[rows, :], sin_ref[rows, :]
        outs = []
        for h in range(dims.heads):
            qr = _rotary(q[rows, h * dims.dk:(h + 1) * dims.dk], cosb, sinb)
            kr = _rotary(k[rows, h * dims.dk:(h + 1) * dims.dk], cosb, sinb) * (dims.dk ** -0.5)
            vb = v[rows, h * dims.dv:(h + 1) * dims.dv].astype(BF16)
            scores = _dot_nt(qr.astype(BF16), kr.astype(BF16)) * intra_ref[h]
            s_prev = ret_ref[0, 0, h]
            o = _dot(scores.astype(BF16), vb) + _dot((qr * qdec_ref[h]).astype(BF16),
                                                      s_prev.astype(BF16))
            ret_ref[0, 0, h] = (sdec_ref[h] * s_prev
                                + _dot_tn((kr * kdec_ref[h]).astype(BF16), vb))
            outs.append(_group_norm(o))
        chunk_outs.append(jnp.concatenate(outs, axis=1))
    o = jnp.concatenate(chunk_outs, axis=0)

    g = _proj(hb, win_ref, dims.cols[4])
    ga = _proj(hb, win_ref, dims.cols[5])
    gb = _proj(hb, win_ref, dims.cols[6])
    x1 = _merge_out(x, gate, a, o, g, ga, gb, wbr_ref, wout_ref)
    x1_ref[0] = x1
    hnext_ref[...] = _rms_mod(x1, g2_ref[...], mod_ref[0, 3:4, :], mod_ref[0, 4:5, :])


def _const_spec(shape):
    zeros = (0,) * len(shape)
    return pl.BlockSpec(shape, lambda *_: zeros)


def _resident_spec(shape):
    zeros = (0,) * len(shape)
    return pl.BlockSpec(shape, lambda *_: zeros, pipeline_mode=pl.Buffered(1))


def _mixer_prompt(dims, x, mod, g1, weights, tables, router, t_total):
    b, s, d = x.shape
    L = PROMPT_STEP_ROWS
    nc = s // L
    win, wpg, ps, wbp, wbr, wout = weights
    cosb, sinb, intra, qdec, kdec, sdec = tables
    g2, wr_t, br = router
    ne = wr_t.shape[0]
    in_specs = [
        pl.BlockSpec((1, L, d), lambda i, c: (i, c, 0)),
        pl.BlockSpec((1, N_MOD, d), lambda i, c: (i, 0, 0)),
        _const_spec(g1.shape), _resident_spec(win.shape), _const_spec(wpg.shape),
        _const_spec(ps.shape), _resident_spec(wbp.shape), _resident_spec(wbr.shape),
        _resident_spec(wout.shape),
        pl.BlockSpec((L, dims.dk), lambda i, c: (c, 0)),
        pl.BlockSpec((L, dims.dk), lambda i, c: (c, 0)),
        _const_spec(intra.shape), _const_spec(qdec.shape), _const_spec(kdec.shape),
        _const_spec(sdec.shape),
        _const_spec(g2.shape), _const_spec(wr_t.shape), _const_spec(br.shape),
    ]
    out_shape = (
        jax.ShapeDtypeStruct((b, s, d), F32),
        jax.ShapeDtypeStruct((b, POOL_PREV, dims.d_pool), F32),
        jax.ShapeDtypeStruct((1, b, dims.heads, dims.dk, dims.dv), F32),
    ) + _route_out_shapes(t_total, d, ne)
    out_specs = (
        pl.BlockSpec((1, L, d), lambda i, c: (i, c, 0)),
        pl.BlockSpec((1, POOL_PREV, dims.d_pool), lambda i, c: (i, 0, 0)),
        pl.BlockSpec((1, 1, dims.heads, dims.dk, dims.dv), lambda i, c: (0, i, 0, 0, 0)),
    ) + _route_out_specs(L, ne, d, lambda i, c: jnp.maximum(i * nc + c - 1, 0))
    weight_bytes = 2 * (win.size + wpg.size + wbp.size + wbr.size + wout.size)
    return pl.pallas_call(
        functools.partial(_mixer_prompt_kernel, dims),
        out_shape=out_shape,
        grid=(b, nc),
        in_specs=in_specs,
        out_specs=out_specs,
        scratch_shapes=[pltpu.VMEM((POOL_PREV + L, dims.d_pool), F32), pltpu.VMEM((L, d), F32)],
        compiler_params=pltpu.CompilerParams(
            dimension_semantics=("arbitrary", "arbitrary"),
            vmem_limit_bytes=_vmem_limit(2 * weight_bytes + 20 * 1024 * 1024),
        ),
        name="mixer_prompt",
    )(x, mod, g1, win, wpg, ps, wbp, wbr, wout, cosb, sinb, intra, qdec, kdec, sdec,
      g2, wr_t, br)


def _mixer_sample_kernel(dims, ds, x_ref, modt_ref, g1_ref, win_ref, wpg_ref, ps_ref, wbp_ref,
                         wbr_ref, wout_ref, cos_ref, sin_ref, intra_ref, qdec_ref, kdec_ref,
                         sdec_ref, bandp_ref, bandu_ref, cnt_ref, poolin_ref, sret_ref,
                         x1_ref, u_ref, ret_ref, hb_ref, q_ref, k_ref, v_ref, pm_ref, o_ref):
    j = pl.program_id(1)
    R = cos_ref.shape[0]
    ns = R // ds

    @pl.when(j == 0)
    def _project():
        hb = _rms_mod(x_ref[...], g1_ref[...], modt_ref[0], modt_ref[1]).astype(BF16)
        hb_ref[...] = hb
        u_ref[...] = _proj(hb, win_ref, dims.cols[0])
        q_ref[...] = _proj(hb, win_ref, dims.cols[1])
        k_ref[...] = _proj(hb, win_ref, dims.cols[2])
        v_ref[...] = _proj(hb, win_ref, dims.cols[3]).astype(BF16)

    rows = pl.ds(pl.multiple_of(j * R, R), R)

    u = u_ref[rows, :]
    hist = poolin_ref[...].reshape(ns * POOL_PREV, dims.d_pool)
    pm_parts = []
    for gi in range(len(POOL_WINDOWS)):
        lo, hi = gi * dims.gd, (gi + 1) * dims.gd
        win_sum = jnp.zeros((R, dims.gd), F32)
        for part in _split_bf16(hist[:, lo:hi], 3):
            win_sum = win_sum + _dot(bandp_ref[gi], part)
        for part in _split_bf16(u[:, lo:hi], 3):
            win_sum = win_sum + _dot(bandu_ref[gi], part)
        pm_parts.append(win_sum / cnt_ref[gi] - u[:, lo:hi])
    pm_ref[rows, :] = jnp.concatenate(pm_parts, axis=1)

    q = q_ref[rows, :]
    k = k_ref[rows, :]
    v = v_ref[rows, :]
    cosb, sinb = cos_ref[...], sin_ref[...]
    row_seq = lax.broadcasted_iota(jnp.int32, (R, ns * dims.dk), 0) // ds
    col_seq = lax.broadcasted_iota(jnp.int32, (R, ns * dims.dk), 1) // dims.dk
    own = row_seq == col_seq
    outs = []
    for h in range(dims.heads):
        qr = _rotary(q[:, h * dims.dk:(h + 1) * dims.dk], cosb, sinb)
        kr = _rotary(k[:, h * dims.dk:(h + 1) * dims.dk], cosb, sinb) * (dims.dk ** -0.5)
        vb = v[:, h * dims.dv:(h + 1) * dims.dv]
        scores = _dot_nt(qr.astype(BF16), kr.astype(BF16)) * intra_ref[h]
        s_prev = sret_ref[:, h].reshape(ns * dims.dk, dims.dv)
        qd = jnp.concatenate([qr * qdec_ref[h]] * ns, axis=1)
        kd = jnp.concatenate([kr * kdec_ref[h]] * ns, axis=1)
        qexp = jnp.where(own, qd, 0.0).astype(BF16)
        kexp = jnp.where(own, kd, 0.0).astype(BF16)
        o = _dot(scores.astype(BF16), vb) + _dot(qexp, s_prev.astype(BF16))
        s_new = sdec_ref[h] * s_prev + _dot_tn(kexp, vb)
        ret_ref[:, h] = s_new.reshape(ns, dims.dk, dims.dv)
        outs.append(_group_norm(o))
    o_ref[rows, :] = jnp.concatenate(outs, axis=1)

    @pl.when(j == pl.num_programs(1) - 1)
    def _merge():
        hb = hb_ref[...]
        pm = pm_ref[...]
        pm_parts = [pm[:, gi * dims.gd:(gi + 1) * dims.gd] for gi in range(len(POOL_WINDOWS))]
        a = _pool_branch(pm_parts, wpg_ref, ps_ref, wbp_ref)
        g = _proj(hb, win_ref, dims.cols[4])
        ga = _proj(hb, win_ref, dims.cols[5])
        gb = _proj(hb, win_ref, dims.cols[6])
        x1_ref[...] = _merge_out(x_ref[...], modt_ref[2], a, o_ref[...], g, ga, gb,
                                 wbr_ref, wout_ref)


def _mixer_sample(dims, ds, x2d, modt, g1, weights, tables, pool_in, state_ret):
    t, d = x2d.shape
    ns = SAMPLE_SEQS_PER_STEP
    steps = SAMPLE_STEPS_PER_GROUP
    R = ns * ds
    RG = R * steps
    win, wpg, ps, wbp, wbr, wout = weights
    cosb, sinb, intra, qdec, kdec, sdec, bandp, bandu, cnt = tables
    db = state_ret.shape[0]
    in_specs = [
        pl.BlockSpec((RG, d), lambda i, j: (i, 0)),
        pl.BlockSpec((N_MOD, RG, d), lambda i, j: (0, i, 0)),
        _const_spec(g1.shape), _resident_spec(win.shape), _const_spec(wpg.shape),
        _const_spec(ps.shape), _resident_spec(wbp.shape), _resident_spec(wbr.shape),
        _resident_spec(wout.shape), _const_spec(cosb.shape), _const_spec(sinb.shape),
        _const_spec(intra.shape), _const_spec(qdec.shape), _const_spec(kdec.shape),
        _const_spec(sdec.shape), _const_spec(bandp.shape), _const_spec(bandu.shape),
        _const_spec(cnt.shape),
        pl.BlockSpec((ns, POOL_PREV, dims.d_pool), lambda i, j: (i * steps + j, 0, 0)),
        pl.BlockSpec((ns, dims.heads, dims.dk, dims.dv), lambda i, j: (i * steps + j, 0, 0, 0)),
    ]
    out_shape = (
        jax.ShapeDtypeStruct((t, d), F32),
        jax.ShapeDtypeStruct((t, dims.d_pool), F32),
        jax.ShapeDtypeStruct(state_ret.shape, state_ret.dtype),
    )
    out_specs = (
        pl.BlockSpec((RG, d), lambda i, j: (i, 0)),
        pl.BlockSpec((RG, dims.d_pool), lambda i, j: (i, 0)),
        pl.BlockSpec((ns, dims.heads, dims.dk, dims.dv), lambda i, j: (i * steps + j, 0, 0, 0)),
    )
    weight_bytes = 2 * (win.size + wpg.size + wbp.size + wbr.size + wout.size)
    state_bytes = 4 * ns * dims.heads * dims.dk * dims.dv
    return pl.pallas_call(
        functools.partial(_mixer_sample_kernel, dims, ds),
        out_shape=out_shape,
        grid=(db // (ns * steps), steps),
        in_specs=in_specs,
        out_specs=out_specs,
        scratch_shapes=[
            pltpu.VMEM((RG, d), BF16), pltpu.VMEM((RG, dims.d_k), F32),
            pltpu.VMEM((RG, dims.d_k), F32), pltpu.VMEM((RG, dims.d_v), BF16),
            pltpu.VMEM((RG, dims.d_pool), F32), pltpu.VMEM((RG, dims.d_v), F32),
        ],
        compiler_params=pltpu.CompilerParams(
            dimension_semantics=("arbitrary", "arbitrary"),
            vmem_limit_bytes=_vmem_limit(2 * weight_bytes + 4 * state_bytes + 8 * 1024 * 1024),
        ),
        name="mixer_sample",
    )(x2d, modt, g1, win, wpg, ps, wbp, wbr, wout, cosb, sinb, intra, qdec, kdec, sdec,
      bandp, bandu, cnt, pool_in, state_ret)


def _rotary_tables(pos, dk):
    half = dk // 2
    inv = np.power(ROPE_BASE, -np.arange(half, dtype=np.float64) / half)
    ang = pos.astype(np.float64)[:, None] * inv[None, :]
    cos, sin = np.cos(ang), np.sin(ang)
    return (np.concatenate([cos, cos], axis=1).astype(np.float32),
            np.concatenate([-sin, sin], axis=1).astype(np.float32))


def _decay_tables(log_g, chunk, reps, dk, dv):
    idx = np.arange(chunk, dtype=np.float64)
    diff = idx[:, None] - idx[None, :]
    intra = np.where((diff >= 0)[None],
                     np.exp(log_g[:, None, None] * np.maximum(diff, 0.0)[None]), 0.0)
    heads = log_g.shape[0]
    eye = np.eye(reps)
    intra = np.einsum("ab,hij->haibj", eye, intra).reshape(heads, reps * chunk, reps * chunk)
    q_dec = np.exp(log_g[:, None] * (idx[None, :] + 1.0))
    k_dec = np.exp(log_g[:, None] * (chunk - 1.0 - idx[None, :]))
    s_dec = np.exp(log_g * chunk)
    q_dec = np.broadcast_to(np.tile(q_dec, (1, reps))[:, :, None], (heads, reps * chunk, dk))
    k_dec = np.broadcast_to(np.tile(k_dec, (1, reps))[:, :, None], (heads, reps * chunk, dk))
    s_dec = np.broadcast_to(s_dec[:, None, None], (heads, 1, dv))
    return tuple(np.ascontiguousarray(t, dtype=np.float32) for t in (intra, q_dec, k_dec, s_dec))


def _sample_pool_tables(ns, ds, gd, start_pos):
    n = np.arange(ds)
    i = np.arange(POOL_PREV)
    eye = np.eye(ns, dtype=np.float32)
    bandp, bandu, cnt = [], [], []
    for w in POOL_WINDOWS:
        bp = ((i[None, :] - POOL_PREV) >= (n[:, None] - w + 1)).astype(np.float32)
        bu = ((n[None, :] <= n[:, None]) & (n[None, :] >= n[:, None] - w + 1)).astype(np.float32)
        bandp.append(np.einsum("ab,ni->anbi", eye, bp).reshape(ns * ds, ns * POOL_PREV))
        bandu.append(np.einsum("ab,nm->anbm", eye, bu).reshape(ns * ds, ns * ds))
        c = np.minimum(w, start_pos + n + 1).astype(np.float32)
        cnt.append(np.broadcast_to(np.tile(c, ns)[:, None], (ns * ds, gd)))
    return (np.stack(bandp).astype(BF16), np.stack(bandu).astype(BF16),
            np.ascontiguousarray(np.stack(cnt)))


def _exclusive_prefix(x, axis):
    pos = lax.broadcasted_iota(jnp.int32, x.shape, axis)
    acc = x
    sh = 1
    while sh < x.shape[axis]:
        acc = acc + jnp.where(pos >= sh, pltpu.roll(acc, sh, axis=axis), 0.0)
        sh *= 2
    return acc - x


def _first_max_onehot(v, idx, big, axes):
    m = jnp.max(v, axis=axes, keepdims=True)
    first = jnp.min(jnp.where(v == m, idx, big), axis=axes, keepdims=True)
    return idx == first


def _route_block(h2, blk, wr_ref, br_ref, h2_ref, pos_ref, wts_ref, cnt_ref, rank_ref, gates_ref):
    tok = slice(blk * h2.shape[0], (blk + 1) * h2.shape[0])
    hh = h2.astype(BF16)
    h2_ref[tok, :] = hh

    hl = (h2 - hh.astype(F32)).astype(BF16)
    wh, wl = _split_bf16(wr_ref[...], 2)
    logits = _dot_nt(wh, hh) + (_dot_nt(wh, hl) + _dot_nt(wl, hh))
    s = jax.nn.sigmoid(logits)
    e, t = s.shape
    ge = e // N_GROUPS
    neg = -jnp.inf
    s3 = s.reshape(N_GROUPS, ge, t)
    sb3 = s3 + br_ref[...]

    j3 = lax.broadcasted_iota(jnp.int32, (N_GROUPS, ge, t), 1)
    top1 = _first_max_onehot(sb3, j3, ge, 1)
    m1 = jnp.max(sb3, axis=1, keepdims=True)
    m2 = jnp.max(jnp.where(top1, neg, sb3), axis=1, keepdims=True)
    gv = m1 + m2

    g3 = lax.broadcasted_iota(jnp.int32, (N_GROUPS, 1, t), 0)
    gsel = jnp.zeros((N_GROUPS, 1, t), F32)
    for _ in range(TOPK_GROUPS):
        hit = _first_max_onehot(gv, g3, N_GROUPS, 0)
        gsel = jnp.where(hit, 1.0, gsel)
        gv = jnp.where(hit, neg, gv)

    e3 = lax.broadcasted_iota(jnp.int32, (N_GROUPS, ge, t), 0) * ge + j3
    ev = jnp.where(gsel > 0.5, sb3, neg)
    sel = jnp.zeros((N_GROUPS, ge, t), F32)
    chosen = jnp.zeros((N_GROUPS, ge, t), F32)
    hits = []
    for _ in range(TOP_K):
        hit = _first_max_onehot(ev, e3, e, (0, 1))
        hits.append(hit)
        sel = jnp.where(hit, s3, sel)
        chosen = jnp.where(hit, 1.0, chosen)
        ev = jnp.where(hit, neg, ev)
    tot = jnp.sum(sel, axis=(0, 1), keepdims=True)
    gates = sel / tot * ROUTED_SCALE

    chosen2 = chosen.reshape(e, t)
    rank = _exclusive_prefix(chosen2, 1)
    cnt = jnp.sum(chosen2, axis=1, keepdims=True)
    ovf_units = jnp.floor((jnp.maximum(cnt - SLOT_ROWS, 0.0) + (DMA_ROWS - 1)) / DMA_ROWS)
    ovf_off = _exclusive_prefix(jnp.broadcast_to(ovf_units, (e, LANES)), 0)[:, 0:1]
    e_col = lax.broadcasted_iota(jnp.int32, (e, 1), 0).astype(F32)
    slot = jnp.where(rank < SLOT_ROWS, e_col * SLOT_ROWS + rank,
                     e * SLOT_ROWS + ovf_off * DMA_ROWS + (rank - SLOT_ROWS))
    slot3 = slot.reshape(N_GROUPS, ge, t)
    for k, hit in enumerate(hits):
        pos_k = jnp.sum(jnp.where(hit, slot3, 0.0), axis=(0, 1), keepdims=True)
        w_k = jnp.sum(jnp.where(hit, gates, 0.0), axis=(0, 1), keepdims=True)
        pos_ref[k:k + 1, tok] = pos_k.reshape(1, t).astype(jnp.int32)
        wts_ref[k:k + 1, tok] = w_k.reshape(1, t)
    cnt_ref[blk] = jnp.broadcast_to(cnt, (e, LANES)).astype(jnp.int32)
    rank_ref[:, tok] = jnp.where(chosen2 > 0.5, rank, -1.0).astype(jnp.int32)
    gates_ref[:, tok] = gates.reshape(e, t)


def _route_out_shapes(t, d, ne):
    return (jax.ShapeDtypeStruct((t, d), BF16),
            jax.ShapeDtypeStruct((TOP_K, t), jnp.int32),
            jax.ShapeDtypeStruct((TOP_K, t), F32),
            jax.ShapeDtypeStruct((t // TOKEN_BLOCK, ne, LANES), jnp.int32),
            jax.ShapeDtypeStruct((ne, t), jnp.int32),
            jax.ShapeDtypeStruct((ne, t), F32))


def _route_out_specs(rows, ne, d, idx):
    return (pl.BlockSpec((rows, d), lambda *g: (idx(*g), 0)),
            pl.BlockSpec((TOP_K, rows), lambda *g: (0, idx(*g))),
            pl.BlockSpec((TOP_K, rows), lambda *g: (0, idx(*g))),
            pl.BlockSpec((rows // TOKEN_BLOCK, ne, LANES), lambda *g: (idx(*g), 0, 0)),
            pl.BlockSpec((ne, rows), lambda *g: (0, idx(*g))),
            pl.BlockSpec((ne, rows), lambda *g: (0, idx(*g))))


def _route_rows_kernel(per_token, x_ref, mod_ref, g2_ref, wr_ref, br_ref, *refs):
    outs = refs[len(refs) // 2:]
    if per_token:
        shift, scale = mod_ref[3], mod_ref[4]
    else:
        shift, scale = mod_ref[0, 3:4, :], mod_ref[0, 4:5, :]
    h2 = _rms_mod(x_ref[...], g2_ref[...], shift, scale)
    _route_block(h2, 0, wr_ref, br_ref, *outs)


def _route_rows(x2d, first_row, n_rows, mod, per_token, g2, wr_t, br, routed, first_block):
    d = x2d.shape[1]
    tb = TOKEN_BLOCK
    ne = wr_t.shape[0]
    t = routed[0].shape[0]
    x_block0 = first_row // tb
    if per_token:
        mod_spec = pl.BlockSpec((N_MOD, tb, d), lambda i: (0, x_block0 + i, 0))
    else:
        mod_spec = pl.BlockSpec((1, N_MOD, d), lambda i: (0, 0, 0))
    n_in = 5
    return pl.pallas_call(
        functools.partial(_route_rows_kernel, per_token),
        out_shape=_route_out_shapes(t, d, ne),
        grid=(n_rows // tb,),
        in_specs=[
            pl.BlockSpec((tb, d), lambda i: (x_block0 + i, 0)),
            mod_spec,
            _const_spec(g2.shape), _const_spec(wr_t.shape), _const_spec(br.shape),
        ] + [pl.BlockSpec(memory_space=pl.ANY)] * len(routed),
        out_specs=_route_out_specs(tb, ne, d, lambda i: first_block + i),
        input_output_aliases={n_in + k: k for k in range(len(routed))},
        compiler_params=pltpu.CompilerParams(
            dimension_semantics=("arbitrary",),
            vmem_limit_bytes=_vmem_limit(32 * 1024 * 1024),
        ),
        name="route_rows",
    )(x2d, mod, g2, wr_t, br, *routed)


class _Layout:
    def __init__(self, counts):
        nb, ne = counts.shape
        u, g, tm = SLOT_ROWS, DMA_ROWS, EXPERT_TILE
        pc = (counts + g - 1) // g * g
        ovf_units = (jnp.maximum(counts - u, 0) + g - 1) // g
        rows_e = jnp.sum(pc, axis=0)
        region = (rows_e + u + tm - 1) // tm * tm
        goff = jnp.cumsum(region) - region
        dst = goff[None, :] + jnp.cumsum(pc, axis=0) - pc
        self.dst = dst.reshape(-1)
        ovf_end = jnp.cumsum(ovf_units, axis=1)
        ovf_first = ovf_end - ovf_units
        unit = jnp.arange(_overflow_units_cap(ne))[None, :, None]
        owns = (ovf_first[:, None, :] <= unit) & (unit < ovf_end[:, None, :])
        row = dst[:, None, :] + u + (unit - ovf_first[:, None, :]) * g
        self.ovf_dst = jnp.sum(jnp.where(owns, row, 0), axis=2).reshape(-1)
        self.ovf_tot = ovf_end[:, -1]
        self.tail_start = goff + rows_e
        self.tail_units = (region - rows_e) // g
        tiles = region // tm
        self.region_tiles = jnp.concatenate([tiles, jnp.sum(tiles, keepdims=True)])


def _overflow_units_cap(ne):
    return TOKEN_BLOCK * TOP_K // DMA_ROWS + ne


def _block_buffer_rows(tb, ne):
    rows = ne * SLOT_ROWS + tb * TOP_K + ne * (DMA_ROWS - 1)
    return (rows + ONEHOT_ROWS - 1) // ONEHOT_ROWS * ONEHOT_ROWS


def _expert_rows_cap(t, tb, ne):
    nb = t // tb
    rows = t * TOP_K + nb * ne * (DMA_ROWS - 1) + ne * (SLOT_ROWS + EXPERT_TILE - 1)
    return (rows + EXPERT_TILE - 1) // EXPERT_TILE * EXPERT_TILE


def _slot_copy(buf_ref, slot, e, hbm_ref, row, sem, to_hbm):
    src = buf_ref.at[slot, pl.ds(pl.multiple_of(e * SLOT_ROWS, DMA_ROWS), SLOT_ROWS)]
    dst = hbm_ref.at[pl.ds(pl.multiple_of(row, DMA_ROWS), SLOT_ROWS)]
    return pltpu.make_async_copy(src, dst, sem) if to_hbm else pltpu.make_async_copy(dst, src, sem)


def _unit_copy(buf_ref, slot, buf_row, hbm_ref, row, sem, to_hbm):
    src = buf_ref.at[slot, pl.ds(pl.multiple_of(buf_row, DMA_ROWS), DMA_ROWS)]
    dst = hbm_ref.at[pl.ds(pl.multiple_of(row, DMA_ROWS), DMA_ROWS)]
    return pltpu.make_async_copy(src, dst, sem) if to_hbm else pltpu.make_async_copy(dst, src, sem)


def _slots_wait(ne, buf_ref, slot, hbm_ref, sem, to_hbm):
    src = buf_ref.at[slot, pl.ds(0, ne * SLOT_ROWS)]
    dst = hbm_ref.at[pl.ds(0, ne * SLOT_ROWS)]
    c = pltpu.make_async_copy(src, dst, sem) if to_hbm else pltpu.make_async_copy(dst, src, sem)
    c.wait()


def _overflow_copies(ne, blk, slot, odst_ref, otot_ref, buf_ref, hbm_ref, sem, to_hbm, wait):
    def per_unit(i, carry):
        row = odst_ref[blk * _overflow_units_cap(ne) + i]
        c = _unit_copy(buf_ref, slot, ne * SLOT_ROWS + i * DMA_ROWS, hbm_ref, row, sem, to_hbm)
        c.wait() if wait else c.start()
        return carry

    lax.fori_loop(0, otot_ref[blk], per_unit, 0)


def _slot_chunks(ne):
    return [range(c, c + SLOT_CHUNK_EXPERTS) for c in range(0, ne, SLOT_CHUNK_EXPERTS)]


def _overflow_chunks(n_units):
    return lax.shift_right_logical(n_units * DMA_ROWS + (ONEHOT_ROWS - 1),
                                   ONEHOT_ROWS.bit_length() - 1)


def _overflow_select(pos, vals, r0):
    rows = lax.broadcasted_iota(jnp.int32, (ONEHOT_ROWS, pos.shape[1]), 0) + r0
    sel = jnp.zeros(rows.shape, F32)
    for k in range(TOP_K):
        v = 1.0 if vals is None else vals[k:k + 1, :]
        sel = jnp.where(pos[k:k + 1, :] == rows, v, sel)
    return sel


def _dispatch_kernel(ne, dst_ref, odst_ref, otot_ref, tails_ref, tailu_ref,
                     h2_ref, rank_ref, pos_ref, xs_hbm, buf_ref, zero_ref, sem, zsem):
    b = pl.program_id(0)
    nb = pl.num_programs(0)
    slot = b % 2
    h2 = h2_ref[...]
    j = lax.broadcasted_iota(jnp.int32, (SLOT_ROWS, h2.shape[0]), 0)

    def zero_copy(row, rows, wait):
        c = pltpu.make_async_copy(zero_ref.at[pl.ds(0, rows)],
                                  xs_hbm.at[pl.ds(pl.multiple_of(row, DMA_ROWS), rows)], zsem)
        c.wait() if wait else c.start()

    def far_tail(e, wait):
        per_big = ONEHOT_ROWS // DMA_ROWS
        units = tailu_ref[e] - SLOT_ROWS // DMA_ROWS
        n_big = lax.shift_right_logical(units, per_big.bit_length() - 1)
        start = tails_ref[e] + SLOT_ROWS

        def big(i, carry):
            zero_copy(start + i * ONEHOT_ROWS, ONEHOT_ROWS, wait)
            return carry

        def small(i, carry):
            zero_copy(start + n_big * ONEHOT_ROWS + i * DMA_ROWS, DMA_ROWS, wait)
            return carry

        lax.fori_loop(0, n_big, big, 0)
        lax.fori_loop(0, units - n_big * per_big, small, 0)

    @pl.when(b == 0)
    def _zeros():
        zero_ref[...] = jnp.zeros_like(zero_ref)

    @pl.when(b < ne)
    def _far_start():
        far_tail(b, False)

    chunks = _slot_chunks(ne)
    first_issue = max(len(chunks) - 2, 0)
    for ci, experts in enumerate(chunks):
        onehot = jnp.concatenate(
            [jnp.where(rank_ref[e:e + 1, :] == j, 1.0, 0.0) for e in experts], axis=0)
        r0 = experts[0] * SLOT_ROWS
        buf_ref[slot, r0:r0 + len(experts) * SLOT_ROWS, :] = (
            _dot(onehot.astype(BF16), h2).astype(BF16))
        if ci < first_issue:
            continue
        if ci == first_issue:
            @pl.when(b > 0)
            def _prev():
                _slots_wait(ne, buf_ref, 1 - slot, xs_hbm, sem.at[1 - slot], True)
                _overflow_copies(ne, b - 1, 1 - slot, odst_ref, otot_ref, buf_ref, xs_hbm,
                                 sem.at[1 - slot], True, True)
        ready = [e for c in chunks[:ci + 1] for e in c] if ci == first_issue else experts
        for e in ready:
            _slot_copy(buf_ref, slot, e, xs_hbm, dst_ref[b * ne + e], sem.at[slot], True).start()

    pos = pos_ref[...]

    def chunk(ci, carry):
        r0 = pl.multiple_of(ne * SLOT_ROWS + ci * ONEHOT_ROWS, ONEHOT_ROWS)
        onehot = _overflow_select(pos, None, r0).astype(BF16)
        buf_ref[slot, pl.ds(r0, ONEHOT_ROWS), :] = _dot(onehot, h2).astype(BF16)
        return carry

    lax.fori_loop(0, _overflow_chunks(otot_ref[b]), chunk, 0)
    _overflow_copies(ne, b, slot, odst_ref, otot_ref, buf_ref, xs_hbm, sem.at[slot], True, False)

    @pl.when(b < ne)
    def _far_done():
        far_tail(b, True)

    @pl.when(b == nb - 1)
    def _last():
        _slots_wait(ne, buf_ref, slot, xs_hbm, sem.at[slot], True)
        _overflow_copies(ne, b, slot, odst_ref, otot_ref, buf_ref, xs_hbm, sem.at[slot], True,
                         True)
        def near_tail(wait):
            def per_expert(e, carry):
                zero_copy(tails_ref[e], SLOT_ROWS, wait)
                return carry
            lax.fori_loop(0, ne, per_expert, 0)

        near_tail(False)
        near_tail(True)


def _dispatch(h2, rank, pos, lay, ne, rows_cap):
    t, d = h2.shape
    tb = TOKEN_BLOCK
    buf_rows = _block_buffer_rows(tb, ne)
    assert t // tb >= ne
    return pl.pallas_call(
        functools.partial(_dispatch_kernel, ne),
        out_shape=jax.ShapeDtypeStruct((rows_cap, d), BF16),
        grid_spec=pltpu.PrefetchScalarGridSpec(
            num_scalar_prefetch=5,
            grid=(t // tb,),
            in_specs=[
                pl.BlockSpec((tb, d), lambda i, *_: (i, 0)),
                pl.BlockSpec((ne, tb), lambda i, *_: (0, i)),
                pl.BlockSpec((TOP_K, tb), lambda i, *_: (0, i)),
            ],
            out_specs=pl.BlockSpec(memory_space=pl.ANY),
            scratch_shapes=[
                pltpu.VMEM((2, buf_rows, d), BF16),
                pltpu.VMEM((ONEHOT_ROWS, d), BF16),
                pltpu.SemaphoreType.DMA((2,)),
                pltpu.SemaphoreType.DMA(()),
            ],
        ),
        compiler_params=pltpu.CompilerParams(
            dimension_semantics=("arbitrary",),
            vmem_limit_bytes=_vmem_limit(4 * buf_rows * d + 16 * 1024 * 1024),
        ),
        name="dispatch",
    )(lay.dst, lay.ovf_dst, lay.ovf_tot, lay.tail_start, lay.tail_units, h2, rank, pos)


def _experts_kernel(tiles_ref, xe_hbm, wg_ref, wu_ref, wd_ref, ye_hbm,
                    xbuf_ref, ybuf_ref, wgb_ref, wub_ref, wdb_ref, done_ref, xsem, ysem):
    e = pl.program_id(0)
    ne = pl.num_programs(0)
    tm = xbuf_ref.shape[1]

    def x_copy(row, slot):
        return pltpu.make_async_copy(xe_hbm.at[pl.ds(pl.multiple_of(row, tm), tm)],
                                     xbuf_ref.at[slot], xsem.at[slot])

    def y_copy(row, slot):
        return pltpu.make_async_copy(ybuf_ref.at[slot],
                                     ye_hbm.at[pl.ds(pl.multiple_of(row, tm), tm)], ysem.at[slot])

    total = tiles_ref[ne]

    @pl.when(e == 0)
    def _first():
        done_ref[0] = 0
        for g in range(RING_SLOTS - 1):
            x_copy(g * tm, g).start()

    wgb_ref[...] = wg_ref[0].astype(BF16)
    wub_ref[...] = wu_ref[0].astype(BF16)
    wdb_ref[...] = wd_ref[0].astype(BF16)
    done = done_ref[0]
    n = tiles_ref[e]

    def tile(j, carry):
        g = done + j
        slot = lax.rem(g, RING_SLOTS)
        x_copy(0, slot).wait()
        ahead = g + (RING_SLOTS - 1)

        @pl.when(ahead < total)
        def _prefetch():
            x_copy(ahead * tm, lax.rem(ahead, RING_SLOTS)).start()

        @pl.when(g >= RING_SLOTS)
        def _reuse():
            y_copy(0, slot).wait()

        part = tm // EXPERT_TILE_PARTS
        starts = range(0, tm, part)
        hids = []
        for r0 in starts:
            x = xbuf_ref[slot, r0:r0 + part, :]
            hids.append((jax.nn.silu(_dot(x, wgb_ref[...])) * _dot(x, wub_ref[...])).astype(BF16))
        for r0, hid in zip(starts, hids):
            ybuf_ref[slot, r0:r0 + part, :] = _dot(hid, wdb_ref[...]).astype(BF16)
        y_copy(g * tm, slot).start()
        return carry

    lax.fori_loop(0, n, tile, 0)
    done_ref[0] = done + n

    @pl.when(e == ne - 1)
    def _drain():
        for s in range(RING_SLOTS):
            y_copy(0, s).wait()


def _experts(xs, lay, wg, wu, wd):
    rows_cap, d = xs.shape
    ne, _, de = wg.shape
    assert ne >= RING_SLOTS
    tm = EXPERT_TILE
    ring = RING_SLOTS
    return pl.pallas_call(
        _experts_kernel,
        out_shape=jax.ShapeDtypeStruct((rows_cap, d), BF16),
        grid_spec=pltpu.PrefetchScalarGridSpec(
            num_scalar_prefetch=1,
            grid=(ne,),
            in_specs=[
                pl.BlockSpec(memory_space=pl.ANY),
                pl.BlockSpec((1, d, de), lambda e, *_: (e, 0, 0)),
                pl.BlockSpec((1, d, de), lambda e, *_: (e, 0, 0)),
                pl.BlockSpec((1, de, d), lambda e, *_: (e, 0, 0)),
            ],
            out_specs=pl.BlockSpec(memory_space=pl.ANY),
            scratch_shapes=[
                pltpu.VMEM((ring, tm, d), BF16), pltpu.VMEM((ring, tm, d), BF16),
                pltpu.VMEM((d, de), BF16), pltpu.VMEM((d, de), BF16), pltpu.VMEM((de, d), BF16),
                pltpu.SMEM((1,), jnp.int32),
                pltpu.SemaphoreType.DMA((ring,)), pltpu.SemaphoreType.DMA((ring,)),
            ],
        ),
        compiler_params=pltpu.CompilerParams(
            dimension_semantics=("arbitrary",),
            vmem_limit_bytes=_vmem_limit(32 * 1024 * 1024),
        ),
        name="experts",
    )(lay.region_tiles, xs, wg, wu, wd)


def _final_kernel(n_prompt_blocks, ne, dst_ref, odst_ref, otot_ref,
                  xp_ref, xs_ref, modp_ref, modt_ref, h2_ref, rank_ref, gates_ref, pos_ref, wts_ref,
                  wsg_ref, wsu_ref, wsd_ref, gf_ref, ye_hbm, yp_ref, ys_ref,
                  buf_ref, acc_ref, sem):
    i = pl.program_id(0)
    nb = pl.num_programs(0)
    slot = i % 2

    def fetch_slots(blk, sl, experts):
        for e in experts:
            _slot_copy(buf_ref, sl, e, ye_hbm, dst_ref[blk * ne + e], sem.at[sl], False).start()

    def overflow(blk, sl, wait):
        _overflow_copies(ne, blk, sl, odst_ref, otot_ref, buf_ref, ye_hbm, sem.at[sl], False, wait)

    @pl.when(i == 0)
    def _first():
        buf_ref[...] = jnp.zeros_like(buf_ref)
        fetch_slots(0, 0, range(ne))
        overflow(0, 0, False)

    @pl.when(i + 1 < nb)
    def _prefetch():
        fetch_slots(i + 1, 1 - slot, range(ne))
        overflow(i + 1, 1 - slot, False)

    _slots_wait(ne, buf_ref, slot, ye_hbm, sem.at[slot], False)
    overflow(i, slot, True)

    tb = h2_ref.shape[0]
    j = lax.broadcasted_iota(jnp.int32, (SLOT_ROWS, tb), 0)
    acc = jnp.zeros((tb, h2_ref.shape[1]), F32)
    for experts in _slot_chunks(ne):
        sel = jnp.concatenate(
            [jnp.where(rank_ref[e:e + 1, :] == j, gates_ref[e:e + 1, :], 0.0) for e in experts],
            axis=0)
        r0 = experts[0] * SLOT_ROWS
        acc = acc + _dot_tn(sel.astype(BF16), buf_ref[slot, r0:r0 + len(experts) * SLOT_ROWS, :])
    acc_ref[...] = acc

    pos = pos_ref[...]
    wts = wts_ref[...]

    def chunk(ci, carry):
        r0 = pl.multiple_of(ne * SLOT_ROWS + ci * ONEHOT_ROWS, ONEHOT_ROWS)
        sel = _overflow_select(pos, wts, r0).astype(BF16)
        acc_ref[...] += _dot_tn(sel, buf_ref[slot, pl.ds(r0, ONEHOT_ROWS), :])
        return carry

    lax.fori_loop(0, _overflow_chunks(otot_ref[i]), chunk, 0)

    is_p = i < n_prompt_blocks
    x = jnp.where(is_p, xp_ref[...], xs_ref[...])
    gate = jnp.where(is_p, modp_ref[0, 5:6, :], modt_ref[5])
    hb = h2_ref[...]
    hid = jax.nn.silu(_dot(hb, wsg_ref[...])) * _dot(hb, wsu_ref[...])
    f = acc_ref[...] + _dot(hid.astype(BF16), wsd_ref[...])
    x2 = x + gate * f
    y = x2 * lax.rsqrt(jnp.mean(x2 * x2, axis=-1, keepdims=True) + EPS) * gf_ref[...]

    @pl.when(is_p)
    def _p():
        yp_ref[...] = y

    @pl.when(jnp.logical_not(is_p))
    def _s():
        ys_ref[...] = y


def _final(xp2d, xs2d, modp, modt, h2, rank, gates, pos, wts, ye, lay, ne, wsg, wsu, wsd, gf):
    tp, d = xp2d.shape
    ts = xs2d.shape[0]
    tb = TOKEN_BLOCK
    npb, nsb = tp // tb, ts // tb
    seq = tp // modp.shape[0]
    bpb = seq // tb
    buf_rows = _block_buffer_rows(tb, ne)

    def p_idx(i):
        return jnp.minimum(i, npb - 1)

    def s_idx(i):
        return jnp.maximum(i - npb, 0)

    def const(shape):
        zeros = (0,) * len(shape)
        return pl.BlockSpec(shape, lambda i, *_: zeros)

    return pl.pallas_call(
        functools.partial(_final_kernel, npb, ne),
        out_shape=(jax.ShapeDtypeStruct((tp, d), F32), jax.ShapeDtypeStruct((ts, d), F32)),
        grid_spec=pltpu.PrefetchScalarGridSpec(
            num_scalar_prefetch=3,
            grid=(npb + nsb,),
            in_specs=[
                pl.BlockSpec((tb, d), lambda i, *_: (p_idx(i), 0)),
                pl.BlockSpec((tb, d), lambda i, *_: (s_idx(i), 0)),
                pl.BlockSpec((1, N_MOD, d), lambda i, *_: (p_idx(i) // bpb, 0, 0)),
                pl.BlockSpec((N_MOD, tb, d), lambda i, *_: (0, s_idx(i), 0)),
                pl.BlockSpec((tb, d), lambda i, *_: (i, 0)),
                pl.BlockSpec((ne, tb), lambda i, *_: (0, i)),
                pl.BlockSpec((ne, tb), lambda i, *_: (0, i)),
                pl.BlockSpec((TOP_K, tb), lambda i, *_: (0, i)),
                pl.BlockSpec((TOP_K, tb), lambda i, *_: (0, i)),
                const(wsg.shape), const(wsu.shape), const(wsd.shape), const(gf.shape),
                pl.BlockSpec(memory_space=pl.ANY),
            ],
            out_specs=(pl.BlockSpec((tb, d), lambda i, *_: (p_idx(i), 0)),
                       pl.BlockSpec((tb, d), lambda i, *_: (s_idx(i), 0))),
            scratch_shapes=[
                pltpu.VMEM((2, buf_rows, d), BF16),
                pltpu.VMEM((tb, d), F32),
                pltpu.SemaphoreType.DMA((2,)),
            ],
        ),
        compiler_params=pltpu.CompilerParams(
            dimension_semantics=("arbitrary",),
            vmem_limit_bytes=_vmem_limit(4 * buf_rows * d + 32 * 1024 * 1024),
        ),
        name="final",
    )(lay.dst, lay.ovf_dst, lay.ovf_tot,
      xp2d, xs2d, modp, modt, h2, rank, gates, pos, wts, wsg, wsu, wsd, gf, ye)


def kernel(x_prompt, x_sample, c_prompt, c_sample, state_pool, state_ret, w_ada, b_ada, g_norm1,
           w_in, w_pool_group, pool_scale, w_branch_pool, w_branch_ret, w_out, g_norm2, w_router,
           b_router, w_exp_gate, w_exp_up, w_exp_down, w_sh_gate, w_sh_up, w_sh_down, g_final):
    b, s, d = x_prompt.shape
    db, ds, _ = x_sample.shape
    depth = w_ada.shape[0]
    heads, dk, dv = state_ret.shape[2:]
    pool_buf, d_pool = state_pool.shape[2:]
    dims = _Dims(d, d_pool, heads, dk, dv)
    assert dims.d_in == w_in.shape[2]
    assert pool_buf == max(POOL_WINDOWS) - 1 and pool_buf < POOL_PREV
    assert s % PROMPT_STEP_ROWS == 0 and PROMPT_STEP_ROWS % RET_CHUNK == 0
    assert s % TOKEN_BLOCK == 0 and (db * ds) % TOKEN_BLOCK == 0
    assert db % (SAMPLE_SEQS_PER_STEP * SAMPLE_STEPS_PER_GROUP) == 0
    assert SLOT_ROWS % DMA_ROWS == 0 and (w_router.shape[2] * SLOT_ROWS) % ONEHOT_ROWS == 0

    log_g = np.log(1.0 - np.power(2.0, -5.0 - np.arange(heads, dtype=np.float64)))
    ns = SAMPLE_SEQS_PER_STEP
    cos_p, sin_p = _rotary_tables(np.arange(s), dk)
    cos_s, sin_s = _rotary_tables(PAST_LEN + np.arange(ds), dk)
    cos_s, sin_s = np.tile(cos_s, (ns, 1)), np.tile(sin_s, (ns, 1))
    dec_p = _decay_tables(log_g, RET_CHUNK, 1, dk, dv)
    dec_s = _decay_tables(log_g, ds, ns, dk, dv)
    pool_tabs = _sample_pool_tables(ns, ds, dims.gd, PAST_LEN)

    assert depth == 1, "the final kernel fuses the output norm into the (single) layer"
    l = 0
    xs = x_sample.reshape(db * ds, d)
    modp, modt = _ada(c_prompt, jnp.repeat(c_sample, ds, axis=0), w_ada[l], b_ada[l])
    modp = modp.reshape(b, N_MOD, d)
    g1 = g_norm1[l].reshape(1, d)
    weights = (w_in[l].astype(BF16), w_pool_group[l].astype(BF16),
               pool_scale[l].reshape(1, d_pool), w_branch_pool[l].astype(BF16),
               w_branch_ret[l].astype(BF16), w_out[l].astype(BF16))

    ne = w_router.shape[2]
    router = (g_norm2[l].reshape(1, d), w_router[l].T,
              b_router[l].reshape(N_GROUPS, ne // N_GROUPS, 1))
    x1p, pool16, ret_p, *routed = _mixer_prompt(dims, x_prompt, modp, g1, weights,
                                                (cos_p, sin_p) + dec_p, router,
                                                b * s + db * ds)
    pool_in = jnp.pad(state_pool[l], ((0, 0), (POOL_PREV - pool_buf, 0), (0, 0)))
    x1s, u_s, ret_s = _mixer_sample(dims, ds, xs, modt, g1, weights,
                                    (cos_s, sin_s) + dec_s + pool_tabs, pool_in,
                                    state_ret[l].astype(F32))
    pool_p = pool16[:, POOL_PREV - pool_buf:]
    pool_s = jnp.concatenate([state_pool[l], u_s.reshape(db, ds, d_pool)], axis=1)[:, -pool_buf:]

    x1p2d = x1p.reshape(b * s, d)
    last = b * s - PROMPT_STEP_ROWS
    routed = _route_rows(x1p2d, last, PROMPT_STEP_ROWS, modp[b - 1:b], False, *router, routed,
                         last // TOKEN_BLOCK)
    h2, pos, wts, counts, rank, gates = _route_rows(x1s, 0, db * ds, modt, True, *router, routed,
                                                    (b * s) // TOKEN_BLOCK)
    rows_cap = _expert_rows_cap(b * s + db * ds, TOKEN_BLOCK, ne)
    lay = _Layout(counts[:, :, 0])
    xe = _dispatch(h2, rank, pos, lay, ne, rows_cap)
    ye = _experts(xe, lay, w_exp_gate[l], w_exp_up[l], w_exp_down[l])
    yp, ys = _final(x1p2d, x1s, modp, modt, h2, rank, gates, pos, wts, ye, lay, ne,
                    w_sh_gate[l].astype(BF16), w_sh_up[l].astype(BF16),
                    w_sh_down[l].astype(BF16), g_final.reshape(1, d))

    return (yp.reshape(b, s, d), ys.reshape(db, ds, d), pool_p[None],
            ret_p.astype(x_prompt.dtype), pool_s[None], ret_s.astype(state_ret.dtype)[None])
```

```python
import functools

import jax
import jax.numpy as jnp
import numpy as np
from jax import lax
from jax.experimental import pallas as pl
from jax.experimental.pallas import tpu as pltpu

F32 = jnp.float32
BF16 = jnp.bfloat16

PAST_LEN = 16384
POOL_WINDOWS = (2, 4, 8, 16)
RET_CHUNK = 128
ROPE_BASE = 10000.0
TOP_K = 8
N_GROUPS = 8
TOPK_GROUPS = 4
ROUTED_SCALE = 2.5
N_MOD = 6
EPS = 1e-6

V7X_VMEM_BYTES = 64 * 1024 * 1024
LANES = 128

PROMPT_STEP_ROWS = 512
SAMPLE_SEQS_PER_STEP = 8
SAMPLE_STEPS_PER_GROUP = 4
TOKEN_BLOCK = 256
DMA_ROWS = 16
SLOT_ROWS = 3 * DMA_ROWS
ONEHOT_ROWS = 256
SLOT_CHUNK_EXPERTS = 64
EXPERT_TILE = 1024
EXPERT_TILE_PARTS = 2
RING_SLOTS = 4
POOL_PREV = 16


def _vmem_limit(nbytes):
    return int(min(nbytes, V7X_VMEM_BYTES - 8 * 1024 * 1024))


def _dot(a, b):
    return jnp.dot(a, b, preferred_element_type=F32)


def _dot_nt(a, b):
    return lax.dot_general(a, b, (((1,), (1,)), ((), ())), preferred_element_type=F32)


def _dot_tn(a, b):
    return lax.dot_general(a, b, (((0,), (0,)), ((), ())), preferred_element_type=F32)


def _split_bf16(x, parts):
    out = []
    for _ in range(parts):
        p = x.astype(BF16)
        out.append(p)
        x = x - p.astype(F32)
    return out


def _rms_mod(x, g, shift, scale):
    y = x * lax.rsqrt(jnp.mean(x * x, axis=-1, keepdims=True) + EPS)
    return (y * g) * (1.0 + scale) + shift


def _rotary(x, cosb, sinb):
    return x * cosb + pltpu.roll(x, x.shape[-1] // 2, axis=1) * sinb


def _group_norm(o):
    return o * lax.rsqrt(jnp.mean(o * o, axis=-1, keepdims=True) + EPS)


def _ada_kernel(cp_ref, ct_ref, w_ref, b_ref, modp_ref, modt_ref):
    w = w_ref[...].astype(BF16)
    modp_ref[...] = _dot(cp_ref[...].astype(BF16), w) + b_ref[...]
    modt_ref[0] = _dot(ct_ref[...].astype(BF16), w) + b_ref[...]


def _ada(c_prompt, c_tokens, w_ada, b_ada):
    bp, d = c_prompt.shape
    ts = c_tokens.shape[0]
    n = w_ada.shape[1]
    return pl.pallas_call(
        _ada_kernel,
        out_shape=(jax.ShapeDtypeStruct((bp, n), F32), jax.ShapeDtypeStruct((n // d, ts, d), F32)),
        grid=(n // d,),
        in_specs=[
            pl.BlockSpec((bp, d), lambda j: (0, 0)),
            pl.BlockSpec((ts, d), lambda j: (0, 0)),
            pl.BlockSpec((d, d), lambda j: (0, j)),
            pl.BlockSpec((1, d), lambda j: (0, j)),
        ],
        out_specs=(pl.BlockSpec((bp, d), lambda j: (0, j)),
                   pl.BlockSpec((1, ts, d), lambda j: (j, 0, 0))),
        compiler_params=pltpu.CompilerParams(
            dimension_semantics=("arbitrary",),
            vmem_limit_bytes=_vmem_limit(32 * 1024 * 1024),
        ),
        name="ada",
    )(c_prompt, c_tokens, w_ada, b_ada.reshape(1, n))


class _Dims:
    def __init__(self, d_model, d_pool, heads, dk, dv):
        self.d = d_model
        self.d_pool = d_pool
        self.gd = d_pool // len(POOL_WINDOWS)
        self.heads = heads
        self.dk = dk
        self.dv = dv
        self.d_k = heads * dk
        self.d_v = heads * dv
        widths = (d_pool, self.d_k, self.d_k, self.d_v, self.d_v, d_model, d_model)
        offs = [0]
        for w in widths:
            offs.append(offs[-1] + w)
        self.cols = tuple((offs[i], offs[i + 1]) for i in range(len(widths)))
        self.d_in = offs[-1]


def _proj(hb, win_ref, cols):
    return _dot(hb, win_ref[:, cols[0]:cols[1]])


def _pool_branch(pm_parts, wpg_ref, ps_ref, wbp_ref):
    pg = [_dot(pm.astype(BF16), wpg_ref[gi]) for gi, pm in enumerate(pm_parts)]
    pm = jnp.concatenate(pg, axis=1) * ps_ref[...]
    return _dot(pm.astype(BF16), wbp_ref[...])


def _merge_out(x, gate, a, o, g, ga, gb, wbr_ref, wout_ref):
    r = _dot((jax.nn.silu(g) * o).astype(BF16), wbr_ref[...])
    merged = jax.nn.sigmoid(ga) * a + jax.nn.sigmoid(gb) * r
    m = _dot(merged.astype(BF16), wout_ref[...])
    return x + gate * m


def _mixer_prompt_kernel(dims, x_ref, mod_ref, g1_ref, win_ref, wpg_ref, ps_ref, wbp_ref,
                         wbr_ref, wout_ref, cos_ref, sin_ref, intra_ref, qdec_ref,
                         kdec_ref, sdec_ref, g2_ref, wr_ref, br_ref,
                         x1_ref, pool_ref, ret_ref, h2_ref, pos_ref, wts_ref, cnt_ref, rank_ref,
                         gates_ref, uext_ref, hnext_ref):
    c = pl.program_id(1)
    L = x_ref.shape[1]

    @pl.when(jnp.logical_and(pl.program_id(0) == 0, c == 0))
    def _init_route():
        hnext_ref[...] = jnp.zeros_like(hnext_ref)

    @pl.when(c == 0)
    def _init():
        ret_ref[...] = jnp.zeros_like(ret_ref)
        uext_ref[0:POOL_PREV, :] = jnp.zeros((POOL_PREV, dims.d_pool), F32)

    def route_prev(blk):
        _route_block(hnext_ref[blk * TOKEN_BLOCK:(blk + 1) * TOKEN_BLOCK, :], blk,
                     wr_ref, br_ref, h2_ref, pos_ref, wts_ref, cnt_ref, rank_ref, gates_ref)

    for blk in range(L // TOKEN_BLOCK):
        route_prev(blk)
    x = x_ref[0]
    shift, scale, gate = mod_ref[0, 0:1, :], mod_ref[0, 1:2, :], mod_ref[0, 2:3, :]
    hb = _rms_mod(x, g1_ref[...], shift, scale).astype(BF16)

    u = _proj(hb, win_ref, dims.cols[0])
    uext_ref[POOL_PREV:POOL_PREV + L, :] = u
    pos = c * L + lax.broadcasted_iota(jnp.int32, (L, 1), 0)
    pm_parts = []
    for gi, w in enumerate(POOL_WINDOWS):
        s = uext_ref[:, gi * dims.gd:(gi + 1) * dims.gd]
        sh = 1
        while sh < w:
            s = s + pltpu.roll(s, sh, axis=0)
            sh *= 2
        cnt = jnp.minimum(w, pos + 1).astype(F32)
        pm_parts.append(s[POOL_PREV:, :] / cnt - u[:, gi * dims.gd:(gi + 1) * dims.gd])
    a = _pool_branch(pm_parts, wpg_ref, ps_ref, wbp_ref)
    uext_ref[0:POOL_PREV, :] = uext_ref[L:L + POOL_PREV, :]
    pool_ref[0] = uext_ref[0:POOL_PREV, :]

    q = _proj(hb, win_ref, dims.cols[1])
    k = _proj(hb, win_ref, dims.cols[2])
    v = _proj(hb, win_ref, dims.cols[3])
    chunk_outs = []
    for r0 in range(0, L, RET_CHUNK):
        rows = slice(r0, r0 + RET_CHUNK)
        cosb, sinb = cos_ref[rows, :], sin_ref[rows, :]
        outs = []
        for h in range(dims.heads):
            qr = _rotary(q[rows, h * dims.dk:(h + 1) * dims.dk], cosb, sinb)
            kr = _rotary(k[rows, h * dims.dk:(h + 1) * dims.dk], cosb, sinb) * (dims.dk ** -0.5)
            vb = v[rows, h * dims.dv:(h + 1) * dims.dv].astype(BF16)
            scores = _dot_nt(qr.astype(BF16), kr.astype(BF16)) * intra_ref[h]
            s_prev = ret_ref[0, 0, h]
            o = _dot(scores.astype(BF16), vb) + _dot((qr * qdec_ref[h]).astype(BF16),
                                                      s_prev.astype(BF16))
            ret_ref[0, 0, h] = (sdec_ref[h] * s_prev
                                + _dot_tn((kr * kdec_ref[h]).astype(BF16), vb))
            outs.append(_group_norm(o))
        chunk_outs.append(jnp.concatenate(outs, axis=1))
    o = jnp.concatenate(chunk_outs, axis=0)

    g = _proj(hb, win_ref, dims.cols[4])
    ga = _proj(hb, win_ref, dims.cols[5])
    gb = _proj(hb, win_ref, dims.cols[6])
    x1 = _merge_out(x, gate, a, o, g, ga, gb, wbr_ref, wout_ref)
    x1_ref[0] = x1
    hnext_ref[...] = _rms_mod(x1, g2_ref[...], mod_ref[0, 3:4, :], mod_ref[0, 4:5, :])


def _const_spec(shape):
    zeros = (0,) * len(shape)
    return pl.BlockSpec(shape, lambda *_: zeros)


def _resident_spec(shape):
    zeros = (0,) * len(shape)
    return pl.BlockSpec(shape, lambda *_: zeros, pipeline_mode=pl.Buffered(1))


def _mixer_prompt(dims, x, mod, g1, weights, tables, router, t_total):
    b, s, d = x.shape
    L = PROMPT_STEP_ROWS
    nc = s // L
    win, wpg, ps, wbp, wbr, wout = weights
    cosb, sinb, intra, qdec, kdec, sdec = tables
    g2, wr_t, br = router
    ne = wr_t.shape[0]
    in_specs = [
        pl.BlockSpec((1, L, d), lambda i, c: (i, c, 0)),
        pl.BlockSpec((1, N_MOD, d), lambda i, c: (i, 0, 0)),
        _const_spec(g1.shape), _resident_spec(win.shape), _const_spec(wpg.shape),
        _const_spec(ps.shape), _resident_spec(wbp.shape), _resident_spec(wbr.shape),
        _resident_spec(wout.shape),
        pl.BlockSpec((L, dims.dk), lambda i, c: (c, 0)),
        pl.BlockSpec((L, dims.dk), lambda i, c: (c, 0)),
        _const_spec(intra.shape), _const_spec(qdec.shape), _const_spec(kdec.shape),
        _const_spec(sdec.shape),
        _const_spec(g2.shape), _const_spec(wr_t.shape), _const_spec(br.shape),
    ]
    out_shape = (
        jax.ShapeDtypeStruct((b, s, d), F32),
        jax.ShapeDtypeStruct((b, POOL_PREV, dims.d_pool), F32),
        jax.ShapeDtypeStruct((1, b, dims.heads, dims.dk, dims.dv), F32),
    ) + _route_out_shapes(t_total, d, ne)
    out_specs = (
        pl.BlockSpec((1, L, d), lambda i, c: (i, c, 0)),
        pl.BlockSpec((1, POOL_PREV, dims.d_pool), lambda i, c: (i, 0, 0)),
        pl.BlockSpec((1, 1, dims.heads, dims.dk, dims.dv), lambda i, c: (0, i, 0, 0, 0)),
    ) + _route_out_specs(L, ne, d, lambda i, c: jnp.maximum(i * nc + c - 1, 0))
    weight_bytes = 2 * (win.size + wpg.size + wbp.size + wbr.size + wout.size)
    return pl.pallas_call(
        functools.partial(_mixer_prompt_kernel, dims),
        out_shape=out_shape,
        grid=(b, nc),
        in_specs=in_specs,
        out_specs=out_specs,
        scratch_shapes=[pltpu.VMEM((POOL_PREV + L, dims.d_pool), F32), pltpu.VMEM((L, d), F32)],
        compiler_params=pltpu.CompilerParams(
            dimension_semantics=("arbitrary", "arbitrary"),
            vmem_limit_bytes=_vmem_limit(2 * weight_bytes + 20 * 1024 * 1024),
        ),
        name="mixer_prompt",
    )(x, mod, g1, win, wpg, ps, wbp, wbr, wout, cosb, sinb, intra, qdec, kdec, sdec,
      g2, wr_t, br)


def _mixer_sample_kernel(dims, ds, x_ref, modt_ref, g1_ref, win_ref, wpg_ref, ps_ref, wbp_ref,
                         wbr_ref, wout_ref, cos_ref, sin_ref, intra_ref, qdec_ref, kdec_ref,
                         sdec_ref, bandp_ref, bandu_ref, cnt_ref, poolin_ref, sret_ref,
                         x1_ref, u_ref, ret_ref, hb_ref, q_ref, k_ref, v_ref, pm_ref, o_ref):
    j = pl.program_id(1)
    R = cos_ref.shape[0]
    ns = R // ds

    @pl.when(j == 0)
    def _project():
        hb = _rms_mod(x_ref[...], g1_ref[...], modt_ref[0], modt_ref[1]).astype(BF16)
        hb_ref[...] = hb
        u_ref[...] = _proj(hb, win_ref, dims.cols[0])
        q_ref[...] = _proj(hb, win_ref, dims.cols[1])
        k_ref[...] = _proj(hb, win_ref, dims.cols[2])
        v_ref[...] = _proj(hb, win_ref, dims.cols[3]).astype(BF16)

    rows = pl.ds(pl.multiple_of(j * R, R), R)

    u = u_ref[rows, :]
    hist = poolin_ref[...].reshape(ns * POOL_PREV, dims.d_pool)
    pm_parts = []
    for gi in range(len(POOL_WINDOWS)):
        lo, hi = gi * dims.gd, (gi + 1) * dims.gd
        win_sum = jnp.zeros((R, dims.gd), F32)
        for part in _split_bf16(hist[:, lo:hi], 3):
            win_sum = win_sum + _dot(bandp_ref[gi], part)
        for part in _split_bf16(u[:, lo:hi], 3):
            win_sum = win_sum + _dot(bandu_ref[gi], part)
        pm_parts.append(win_sum / cnt_ref[gi] - u[:, lo:hi])
    pm_ref[rows, :] = jnp.concatenate(pm_parts, axis=1)

    q = q_ref[rows, :]
    k = k_ref[rows, :]
    v = v_ref[rows, :]
    cosb, sinb = cos_ref[...], sin_ref[...]
    row_seq = lax.broadcasted_iota(jnp.int32, (R, ns * dims.dk), 0) // ds
    col_seq = lax.broadcasted_iota(jnp.int32, (R, ns * dims.dk), 1) // dims.dk
    own = row_seq == col_seq
    outs = []
    for h in range(dims.heads):
        qr = _rotary(q[:, h * dims.dk:(h + 1) * dims.dk], cosb, sinb)
        kr = _rotary(k[:, h * dims.dk:(h + 1) * dims.dk], cosb, sinb) * (dims.dk ** -0.5)
        vb = v[:, h * dims.dv:(h + 1) * dims.dv]
        scores = _dot_nt(qr.astype(BF16), kr.astype(BF16)) * intra_ref[h]
        s_prev = sret_ref[:, h].reshape(ns * dims.dk, dims.dv)
        qd = jnp.concatenate([qr * qdec_ref[h]] * ns, axis=1)
        kd = jnp.concatenate([kr * kdec_ref[h]] * ns, axis=1)
        qexp = jnp.where(own, qd, 0.0).astype(BF16)
        kexp = jnp.where(own, kd, 0.0).astype(BF16)
        o = _dot(scores.astype(BF16), vb) + _dot(qexp, s_prev.astype(BF16))
        s_new = sdec_ref[h] * s_prev + _dot_tn(kexp, vb)
        ret_ref[:, h] = s_new.reshape(ns, dims.dk, dims.dv)
        outs.append(_group_norm(o))
    o_ref[rows, :] = jnp.concatenate(outs, axis=1)

    @pl.when(j == pl.num_programs(1) - 1)
    def _merge():
        hb = hb_ref[...]
        pm = pm_ref[...]
        pm_parts = [pm[:, gi * dims.gd:(gi + 1) * dims.gd] for gi in range(len(POOL_WINDOWS))]
        a = _pool_branch(pm_parts, wpg_ref, ps_ref, wbp_ref)
        g = _proj(hb, win_ref, dims.cols[4])
        ga = _proj(hb, win_ref, dims.cols[5])
        gb = _proj(hb, win_ref, dims.cols[6])
        x1_ref[...] = _merge_out(x_ref[...], modt_ref[2], a, o_ref[...], g, ga, gb,
                                 wbr_ref, wout_ref)


def _mixer_sample(dims, ds, x2d, modt, g1, weights, tables, pool_in, state_ret):
    t, d = x2d.shape
    ns = SAMPLE_SEQS_PER_STEP
    steps = SAMPLE_STEPS_PER_GROUP
    R = ns * ds
    RG = R * steps
    win, wpg, ps, wbp, wbr, wout = weights
    cosb, sinb, intra, qdec, kdec, sdec, bandp, bandu, cnt = tables
    db = state_ret.shape[0]
    in_specs = [
        pl.BlockSpec((RG, d), lambda i, j: (i, 0)),
        pl.BlockSpec((N_MOD, RG, d), lambda i, j: (0, i, 0)),
        _const_spec(g1.shape), _resident_spec(win.shape), _const_spec(wpg.shape),
        _const_spec(ps.shape), _resident_spec(wbp.shape), _resident_spec(wbr.shape),
        _resident_spec(wout.shape), _const_spec(cosb.shape), _const_spec(sinb.shape),
        _const_spec(intra.shape), _const_spec(qdec.shape), _const_spec(kdec.shape),
        _const_spec(sdec.shape), _const_spec(bandp.shape), _const_spec(bandu.shape),
        _const_spec(cnt.shape),
        pl.BlockSpec((ns, POOL_PREV, dims.d_pool), lambda i, j: (i * steps + j, 0, 0)),
        pl.BlockSpec((ns, dims.heads, dims.dk, dims.dv), lambda i, j: (i * steps + j, 0, 0, 0)),
    ]
    out_shape = (
        jax.ShapeDtypeStruct((t, d), F32),
        jax.ShapeDtypeStruct((t, dims.d_pool), F32),
        jax.ShapeDtypeStruct(state_ret.shape, state_ret.dtype),
    )
    out_specs = (
        pl.BlockSpec((RG, d), lambda i, j: (i, 0)),
        pl.BlockSpec((RG, dims.d_pool), lambda i, j: (i, 0)),
        pl.BlockSpec((ns, dims.heads, dims.dk, dims.dv), lambda i, j: (i * steps + j, 0, 0, 0)),
    )
    weight_bytes = 2 * (win.size + wpg.size + wbp.size + wbr.size + wout.size)
    state_bytes = 4 * ns * dims.heads * dims.dk * dims.dv
    return pl.pallas_call(
        functools.partial(_mixer_sample_kernel, dims, ds),
        out_shape=out_shape,
        grid=(db // (ns * steps), steps),
        in_specs=in_specs,
        out_specs=out_specs,
        scratch_shapes=[
            pltpu.VMEM((RG, d), BF16), pltpu.VMEM((RG, dims.d_k), F32),
            pltpu.VMEM((RG, dims.d_k), F32), pltpu.VMEM((RG, dims.d_v), BF16),
            pltpu.VMEM((RG, dims.d_pool), F32), pltpu.VMEM((RG, dims.d_v), F32),
        ],
        compiler_params=pltpu.CompilerParams(
            dimension_semantics=("arbitrary", "arbitrary"),
            vmem_limit_bytes=_vmem_limit(2 * weight_bytes + 4 * state_bytes + 8 * 1024 * 1024),
        ),
        name="mixer_sample",
    )(x2d, modt, g1, win, wpg, ps, wbp, wbr, wout, cosb, sinb, intra, qdec, kdec, sdec,
      bandp, bandu, cnt, pool_in, state_ret)


def _rotary_tables(pos, dk):
    half = dk // 2
    inv = np.power(ROPE_BASE, -np.arange(half, dtype=np.float64) / half)
    ang = pos.astype(np.float64)[:, None] * inv[None, :]
    cos, sin = np.cos(ang), np.sin(ang)
    return (np.concatenate([cos, cos], axis=1).astype(np.float32),
            np.concatenate([-sin, sin], axis=1).astype(np.float32))


def _decay_tables(log_g, chunk, reps, dk, dv):
    idx = np.arange(chunk, dtype=np.float64)
    diff = idx[:, None] - idx[None, :]
    intra = np.where((diff >= 0)[None],
                     np.exp(log_g[:, None, None] * np.maximum(diff, 0.0)[None]), 0.0)
    heads = log_g.shape[0]
    eye = np.eye(reps)
    intra = np.einsum("ab,hij->haibj", eye, intra).reshape(heads, reps * chunk, reps * chunk)
    q_dec = np.exp(log_g[:, None] * (idx[None, :] + 1.0))
    k_dec = np.exp(log_g[:, None] * (chunk - 1.0 - idx[None, :]))
    s_dec = np.exp(log_g * chunk)
    q_dec = np.broadcast_to(np.tile(q_dec, (1, reps))[:, :, None], (heads, reps * chunk, dk))
    k_dec = np.broadcast_to(np.tile(k_dec, (1, reps))[:, :, None], (heads, reps * chunk, dk))
    s_dec = np.broadcast_to(s_dec[:, None, None], (heads, 1, dv))
    return tuple(np.ascontiguousarray(t, dtype=np.float32) for t in (intra, q_dec, k_dec, s_dec))


def _sample_pool_tables(ns, ds, gd, start_pos):
    n = np.arange(ds)
    i = np.arange(POOL_PREV)
    eye = np.eye(ns, dtype=np.float32)
    bandp, bandu, cnt = [], [], []
    for w in POOL_WINDOWS:
        bp = ((i[None, :] - POOL_PREV) >= (n[:, None] - w + 1)).astype(np.float32)
        bu = ((n[None, :] <= n[:, None]) & (n[None, :] >= n[:, None] - w + 1)).astype(np.float32)
        bandp.append(np.einsum("ab,ni->anbi", eye, bp).reshape(ns * ds, ns * POOL_PREV))
        bandu.append(np.einsum("ab,nm->anbm", eye, bu).reshape(ns * ds, ns * ds))
        c = np.minimum(w, start_pos + n + 1).astype(np.float32)
        cnt.append(np.broadcast_to(np.tile(c, ns)[:, None], (ns * ds, gd)))
    return (np.stack(bandp).astype(BF16), np.stack(bandu).astype(BF16),
            np.ascontiguousarray(np.stack(cnt)))


def _exclusive_prefix(x, axis):
    pos = lax.broadcasted_iota(jnp.int32, x.shape, axis)
    acc = x
    sh = 1
    while sh < x.shape[axis]:
        acc = acc + jnp.where(pos >= sh, pltpu.roll(acc, sh, axis=axis), 0.0)
        sh *= 2
    return acc - x


def _first_max_onehot(v, idx, big, axes):
    m = jnp.max(v, axis=axes, keepdims=True)
    first = jnp.min(jnp.where(v == m, idx, big), axis=axes, keepdims=True)
    return idx == first


def _route_block(h2, blk, wr_ref, br_ref, h2_ref, pos_ref, wts_ref, cnt_ref, rank_ref, gates_ref):
    tok = slice(blk * h2.shape[0], (blk + 1) * h2.shape[0])
    hh = h2.astype(BF16)
    h2_ref[tok, :] = hh

    hl = (h2 - hh.astype(F32)).astype(BF16)
    wh, wl = _split_bf16(wr_ref[...], 2)
    logits = _dot_nt(wh, hh) + (_dot_nt(wh, hl) + _dot_nt(wl, hh))
    s = jax.nn.sigmoid(logits)
    e, t = s.shape
    ge = e // N_GROUPS
    neg = -jnp.inf
    s3 = s.reshape(N_GROUPS, ge, t)
    sb3 = s3 + br_ref[...]

    j3 = lax.broadcasted_iota(jnp.int32, (N_GROUPS, ge, t), 1)
    top1 = _first_max_onehot(sb3, j3, ge, 1)
    m1 = jnp.max(sb3, axis=1, keepdims=True)
    m2 = jnp.max(jnp.where(top1, neg, sb3), axis=1, keepdims=True)
    gv = m1 + m2

    g3 = lax.broadcasted_iota(jnp.int32, (N_GROUPS, 1, t), 0)
    gsel = jnp.zeros((N_GROUPS, 1, t), F32)
    for _ in range(TOPK_GROUPS):
        hit = _first_max_onehot(gv, g3, N_GROUPS, 0)
        gsel = jnp.where(hit, 1.0, gsel)
        gv = jnp.where(hit, neg, gv)

    e3 = lax.broadcasted_iota(jnp.int32, (N_GROUPS, ge, t), 0) * ge + j3
    ev = jnp.where(gsel > 0.5, sb3, neg)
    sel = jnp.zeros((N_GROUPS, ge, t), F32)
    chosen = jnp.zeros((N_GROUPS, ge, t), F32)
    hits = []
    for _ in range(TOP_K):
        hit = _first_max_onehot(ev, e3, e, (0, 1))
        hits.append(hit)
        sel = jnp.where(hit, s3, sel)
        chosen = jnp.where(hit, 1.0, chosen)
        ev = jnp.where(hit, neg, ev)
    tot = jnp.sum(sel, axis=(0, 1), keepdims=True)
    gates = sel / tot * ROUTED_SCALE

    chosen2 = chosen.reshape(e, t)
    rank = _exclusive_prefix(chosen2, 1)
    cnt = jnp.sum(chosen2, axis=1, keepdims=True)
    ovf_units = jnp.floor((jnp.maximum(cnt - SLOT_ROWS, 0.0) + (DMA_ROWS - 1)) / DMA_ROWS)
    ovf_off = _exclusive_prefix(jnp.broadcast_to(ovf_units, (e, LANES)), 0)[:, 0:1]
    e_col = lax.broadcasted_iota(jnp.int32, (e, 1), 0).astype(F32)
    slot = jnp.where(rank < SLOT_ROWS, e_col * SLOT_ROWS + rank,
                     e * SLOT_ROWS + ovf_off * DMA_ROWS + (rank - SLOT_ROWS))
    slot3 = slot.reshape(N_GROUPS, ge, t)
    for k, hit in enumerate(hits):
        pos_k = jnp.sum(jnp.where(hit, slot3, 0.0), axis=(0, 1), keepdims=True)
        w_k = jnp.sum(jnp.where(hit, gates, 0.0), axis=(0, 1), keepdims=True)
        pos_ref[k:k + 1, tok] = pos_k.reshape(1, t).astype(jnp.int32)
        wts_ref[k:k + 1, tok] = w_k.reshape(1, t)
    cnt_ref[blk] = jnp.broadcast_to(cnt, (e, LANES)).astype(jnp.int32)
    rank_ref[:, tok] = jnp.where(chosen2 > 0.5, rank, -1.0).astype(jnp.int32)
    gates_ref[:, tok] = gates.reshape(e, t)


def _route_out_shapes(t, d, ne):
    return (jax.ShapeDtypeStruct((t, d), BF16),
            jax.ShapeDtypeStruct((TOP_K, t), jnp.int32),
            jax.ShapeDtypeStruct((TOP_K, t), F32),
            jax.ShapeDtypeStruct((t // TOKEN_BLOCK, ne, LANES), jnp.int32),
            jax.ShapeDtypeStruct((ne, t), jnp.int32),
            jax.ShapeDtypeStruct((ne, t), F32))


def _route_out_specs(rows, ne, d, idx):
    return (pl.BlockSpec((rows, d), lambda *g: (idx(*g), 0)),
            pl.BlockSpec((TOP_K, rows), lambda *g: (0, idx(*g))),
            pl.BlockSpec((TOP_K, rows), lambda *g: (0, idx(*g))),
            pl.BlockSpec((rows // TOKEN_BLOCK, ne, LANES), lambda *g: (idx(*g), 0, 0)),
            pl.BlockSpec((ne, rows), lambda *g: (0, idx(*g))),
            pl.BlockSpec((ne, rows), lambda *g: (0, idx(*g))))


def _route_rows_kernel(per_token, x_ref, mod_ref, g2_ref, wr_ref, br_ref, *refs):
    outs = refs[len(refs) // 2:]
    if per_token:
        shift, scale = mod_ref[3], mod_ref[4]
    else:
        shift, scale = mod_ref[0, 3:4, :], mod_ref[0, 4:5, :]
    h2 = _rms_mod(x_ref[...], g2_ref[...], shift, scale)
    _route_block(h2, 0, wr_ref, br_ref, *outs)


def _route_rows(x2d, first_row, n_rows, mod, per_token, g2, wr_t, br, routed, first_block):
    d = x2d.shape[1]
    tb = TOKEN_BLOCK
    ne = wr_t.shape[0]
    t = routed[0].shape[0]
    x_block0 = first_row // tb
    if per_token:
        mod_spec = pl.BlockSpec((N_MOD, tb, d), lambda i: (0, x_block0 + i, 0))
    else:
        mod_spec = pl.BlockSpec((1, N_MOD, d), lambda i: (0, 0, 0))
    n_in = 5
    return pl.pallas_call(
        functools.partial(_route_rows_kernel, per_token),
        out_shape=_route_out_shapes(t, d, ne),
        grid=(n_rows // tb,),
        in_specs=[
            pl.BlockSpec((tb, d), lambda i: (x_block0 + i, 0)),
            mod_spec,
            _const_spec(g2.shape), _const_spec(wr_t.shape), _const_spec(br.shape),
        ] + [pl.BlockSpec(memory_space=pl.ANY)] * len(routed),
        out_specs=_route_out_specs(tb, ne, d, lambda i: first_block + i),
        input_output_aliases={n_in + k: k for k in range(len(routed))},
        compiler_params=pltpu.CompilerParams(
            dimension_semantics=("arbitrary",),
            vmem_limit_bytes=_vmem_limit(32 * 1024 * 1024),
        ),
        name="route_rows",
    )(x2d, mod, g2, wr_t, br, *routed)


class _Layout:
    def __init__(self, counts):
        nb, ne = counts.shape
        u, g, tm = SLOT_ROWS, DMA_ROWS, EXPERT_TILE
        pc = (counts + g - 1) // g * g
        ovf_units = (jnp.maximum(counts - u, 0) + g - 1) // g
        rows_e = jnp.sum(pc, axis=0)
        region = (rows_e + u + tm - 1) // tm * tm
        goff = jnp.cumsum(region) - region
        dst = goff[None, :] + jnp.cumsum(pc, axis=0) - pc
        self.dst = dst.reshape(-1)
        ovf_end = jnp.cumsum(ovf_units, axis=1)
        ovf_first = ovf_end - ovf_units
        unit = jnp.arange(_overflow_units_cap(ne))[None, :, None]
        owns = (ovf_first[:, None, :] <= unit) & (unit < ovf_end[:, None, :])
        row = dst[:, None, :] + u + (unit - ovf_first[:, None, :]) * g
        self.ovf_dst = jnp.sum(jnp.where(owns, row, 0), axis=2).reshape(-1)
        self.ovf_tot = ovf_end[:, -1]
        self.tail_start = goff + rows_e
        self.tail_units = (region - rows_e) // g
        tiles = region // tm
        self.region_tiles = jnp.concatenate([tiles, jnp.sum(tiles, keepdims=True)])


def _overflow_units_cap(ne):
    return TOKEN_BLOCK * TOP_K // DMA_ROWS + ne


def _block_buffer_rows(tb, ne):
    rows = ne * SLOT_ROWS + tb * TOP_K + ne * (DMA_ROWS - 1)
    return (rows + ONEHOT_ROWS - 1) // ONEHOT_ROWS * ONEHOT_ROWS


def _expert_rows_cap(t, tb, ne):
    nb = t // tb
    rows = t * TOP_K + nb * ne * (DMA_ROWS - 1) + ne * (SLOT_ROWS + EXPERT_TILE - 1)
    return (rows + EXPERT_TILE - 1) // EXPERT_TILE * EXPERT_TILE


def _slot_copy(buf_ref, slot, e, hbm_ref, row, sem, to_hbm):
    src = buf_ref.at[slot, pl.ds(pl.multiple_of(e * SLOT_ROWS, DMA_ROWS), SLOT_ROWS)]
    dst = hbm_ref.at[pl.ds(pl.multiple_of(row, DMA_ROWS), SLOT_ROWS)]
    return pltpu.make_async_copy(src, dst, sem) if to_hbm else pltpu.make_async_copy(dst, src, sem)


def _unit_copy(buf_ref, slot, buf_row, hbm_ref, row, sem, to_hbm):
    src = buf_ref.at[slot, pl.ds(pl.multiple_of(buf_row, DMA_ROWS), DMA_ROWS)]
    dst = hbm_ref.at[pl.ds(pl.multiple_of(row, DMA_ROWS), DMA_ROWS)]
    return pltpu.make_async_copy(src, dst, sem) if to_hbm else pltpu.make_async_copy(dst, src, sem)


def _slots_wait(ne, buf_ref, slot, hbm_ref, sem, to_hbm):
    src = buf_ref.at[slot, pl.ds(0, ne * SLOT_ROWS)]
    dst = hbm_ref.at[pl.ds(0, ne * SLOT_ROWS)]
    c = pltpu.make_async_copy(src, dst, sem) if to_hbm else pltpu.make_async_copy(dst, src, sem)
    c.wait()


def _overflow_copies(ne, blk, slot, odst_ref, otot_ref, buf_ref, hbm_ref, sem, to_hbm, wait):
    def per_unit(i, carry):
        row = odst_ref[blk * _overflow_units_cap(ne) + i]
        c = _unit_copy(buf_ref, slot, ne * SLOT_ROWS + i * DMA_ROWS, hbm_ref, row, sem, to_hbm)
        c.wait() if wait else c.start()
        return carry

    lax.fori_loop(0, otot_ref[blk], per_unit, 0)


def _slot_chunks(ne):
    return [range(c, c + SLOT_CHUNK_EXPERTS) for c in range(0, ne, SLOT_CHUNK_EXPERTS)]


def _overflow_chunks(n_units):
    return lax.shift_right_logical(n_units * DMA_ROWS + (ONEHOT_ROWS - 1),
                                   ONEHOT_ROWS.bit_length() - 1)


def _overflow_select(pos, vals, r0):
    rows = lax.broadcasted_iota(jnp.int32, (ONEHOT_ROWS, pos.shape[1]), 0) + r0
    sel = jnp.zeros(rows.shape, F32)
    for k in range(TOP_K):
        v = 1.0 if vals is None else vals[k:k + 1, :]
        sel = jnp.where(pos[k:k + 1, :] == rows, v, sel)
    return sel


def _dispatch_kernel(ne, dst_ref, odst_ref, otot_ref, tails_ref, tailu_ref,
                     h2_ref, rank_ref, pos_ref, xs_hbm, buf_ref, zero_ref, sem, zsem):
    b = pl.program_id(0)
    nb = pl.num_programs(0)
    slot = b % 2
    h2 = h2_ref[...]
    j = lax.broadcasted_iota(jnp.int32, (SLOT_ROWS, h2.shape[0]), 0)

    def zero_copy(row, rows, wait):
        c = pltpu.make_async_copy(zero_ref.at[pl.ds(0, rows)],
                                  xs_hbm.at[pl.ds(pl.multiple_of(row, DMA_ROWS), rows)], zsem)
        c.wait() if wait else c.start()

    def far_tail(e, wait):
        per_big = ONEHOT_ROWS // DMA_ROWS
        units = tailu_ref[e] - SLOT_ROWS // DMA_ROWS
        n_big = lax.shift_right_logical(units, per_big.bit_length() - 1)
        start = tails_ref[e] + SLOT_ROWS

        def big(i, carry):
            zero_copy(start + i * ONEHOT_ROWS, ONEHOT_ROWS, wait)
            return carry

        def small(i, carry):
            zero_copy(start + n_big * ONEHOT_ROWS + i * DMA_ROWS, DMA_ROWS, wait)
            return carry

        lax.fori_loop(0, n_big, big, 0)
        lax.fori_loop(0, units - n_big * per_big, small, 0)

    @pl.when(b == 0)
    def _zeros():
        zero_ref[...] = jnp.zeros_like(zero_ref)

    @pl.when(b < ne)
    def _far_start():
        far_tail(b, False)

    chunks = _slot_chunks(ne)
    first_issue = max(len(chunks) - 2, 0)
    for ci, experts in enumerate(chunks):
        onehot = jnp.concatenate(
            [jnp.where(rank_ref[e:e + 1, :] == j, 1.0, 0.0) for e in experts], axis=0)
        r0 = experts[0] * SLOT_ROWS
        buf_ref[slot, r0:r0 + len(experts) * SLOT_ROWS, :] = (
            _dot(onehot.astype(BF16), h2).astype(BF16))
        if ci < first_issue:
            continue
        if ci == first_issue:
            @pl.when(b > 0)
            def _prev():
                _slots_wait(ne, buf_ref, 1 - slot, xs_hbm, sem.at[1 - slot], True)
                _overflow_copies(ne, b - 1, 1 - slot, odst_ref, otot_ref, buf_ref, xs_hbm,
                                 sem.at[1 - slot], True, True)
        ready = [e for c in chunks[:ci + 1] for e in c] if ci == first_issue else experts
        for e in ready:
            _slot_copy(buf_ref, slot, e, xs_hbm, dst_ref[b * ne + e], sem.at[slot], True).start()

    pos = pos_ref[...]

    def chunk(ci, carry):
        r0 = pl.multiple_of(ne * SLOT_ROWS + ci * ONEHOT_ROWS, ONEHOT_ROWS)
        onehot = _overflow_select(pos, None, r0).astype(BF16)
        buf_ref[slot, pl.ds(r0, ONEHOT_ROWS), :] = _dot(onehot, h2).astype(BF16)
        return carry

    lax.fori_loop(0, _overflow_chunks(otot_ref[b]), chunk, 0)
    _overflow_copies(ne, b, slot, odst_ref, otot_ref, buf_ref, xs_hbm, sem.at[slot], True, False)

    @pl.when(b < ne)
    def _far_done():
        far_tail(b, True)

    @pl.when(b == nb - 1)
    def _last():
        _slots_wait(ne, buf_ref, slot, xs_hbm, sem.at[slot], True)
        _overflow_copies(ne, b, slot, odst_ref, otot_ref, buf_ref, xs_hbm, sem.at[slot], True,
                         True)
        def near_tail(wait):
            def per_expert(e, carry):
                zero_copy(tails_ref[e], SLOT_ROWS, wait)
                return carry
            lax.fori_loop(0, ne, per_expert, 0)

        near_tail(False)
        near_tail(True)


def _dispatch(h2, rank, pos, lay, ne, rows_cap):
    t, d = h2.shape
    tb = TOKEN_BLOCK
    buf_rows = _block_buffer_rows(tb, ne)
    assert t // tb >= ne
    return pl.pallas_call(
        functools.partial(_dispatch_kernel, ne),
        out_shape=jax.ShapeDtypeStruct((rows_cap, d), BF16),
        grid_spec=pltpu.PrefetchScalarGridSpec(
            num_scalar_prefetch=5,
            grid=(t // tb,),
            in_specs=[
                pl.BlockSpec((tb, d), lambda i, *_: (i, 0)),
                pl.BlockSpec((ne, tb), lambda i, *_: (0, i)),
                pl.BlockSpec((TOP_K, tb), lambda i, *_: (0, i)),
            ],
            out_specs=pl.BlockSpec(memory_space=pl.ANY),
            scratch_shapes=[
                pltpu.VMEM((2, buf_rows, d), BF16),
                pltpu.VMEM((ONEHOT_ROWS, d), BF16),
                pltpu.SemaphoreType.DMA((2,)),
                pltpu.SemaphoreType.DMA(()),
            ],
        ),
        compiler_params=pltpu.CompilerParams(
            dimension_semantics=("arbitrary",),
            vmem_limit_bytes=_vmem_limit(4 * buf_rows * d + 16 * 1024 * 1024),
        ),
        name="dispatch",
    )(lay.dst, lay.ovf_dst, lay.ovf_tot, lay.tail_start, lay.tail_units, h2, rank, pos)


def _experts_kernel(tiles_ref, xe_hbm, wg_ref, wu_ref, wd_ref, ye_hbm,
                    xbuf_ref, ybuf_ref, wgb_ref, wub_ref, wdb_ref, done_ref, xsem, ysem):
    e = pl.program_id(0)
    ne = pl.num_programs(0)
    tm = xbuf_ref.shape[1]

    def x_copy(row, slot):
        return pltpu.make_async_copy(xe_hbm.at[pl.ds(pl.multiple_of(row, tm), tm)],
                                     xbuf_ref.at[slot], xsem.at[slot])

    def y_copy(row, slot):
        return pltpu.make_async_copy(ybuf_ref.at[slot],
                                     ye_hbm.at[pl.ds(pl.multiple_of(row, tm), tm)], ysem.at[slot])

    total = tiles_ref[ne]

    @pl.when(e == 0)
    def _first():
        done_ref[0] = 0
        for g in range(RING_SLOTS - 1):
            x_copy(g * tm, g).start()

    wgb_ref[...] = wg_ref[0].astype(BF16)
    wub_ref[...] = wu_ref[0].astype(BF16)
    wdb_ref[...] = wd_ref[0].astype(BF16)
    done = done_ref[0]
    n = tiles_ref[e]

    def tile(j, carry):
        g = done + j
        slot = lax.rem(g, RING_SLOTS)
        x_copy(0, slot).wait()
        ahead = g + (RING_SLOTS - 1)

        @pl.when(ahead < total)
        def _prefetch():
            x_copy(ahead * tm, lax.rem(ahead, RING_SLOTS)).start()

        @pl.when(g >= RING_SLOTS)
        def _reuse():
            y_copy(0, slot).wait()

        part = tm // EXPERT_TILE_PARTS
        starts = range(0, tm, part)
        hids = []
        for r0 in starts:
            x = xbuf_ref[slot, r0:r0 + part, :]
            hids.append((jax.nn.silu(_dot(x, wgb_ref[...])) * _dot(x, wub_ref[...])).astype(BF16))
        for r0, hid in zip(starts, hids):
            ybuf_ref[slot, r0:r0 + part, :] = _dot(hid, wdb_ref[...]).astype(BF16)
        y_copy(g * tm, slot).start()
        return carry

    lax.fori_loop(0, n, tile, 0)
    done_ref[0] = done + n

    @pl.when(e == ne - 1)
    def _drain():
        for s in range(RING_SLOTS):
            y_copy(0, s).wait()


def _experts(xs, lay, wg, wu, wd):
    rows_cap, d = xs.shape
    ne, _, de = wg.shape
    assert ne >= RING_SLOTS
    tm = EXPERT_TILE
    ring = RING_SLOTS
    return pl.pallas_call(
        _experts_kernel,
        out_shape=jax.ShapeDtypeStruct((rows_cap, d), BF16),
        grid_spec=pltpu.PrefetchScalarGridSpec(
            num_scalar_prefetch=1,
            grid=(ne,),
            in_specs=[
                pl.BlockSpec(memory_space=pl.ANY),
                pl.BlockSpec((1, d, de), lambda e, *_: (e, 0, 0)),
                pl.BlockSpec((1, d, de), lambda e, *_: (e, 0, 0)),
                pl.BlockSpec((1, de, d), lambda e, *_: (e, 0, 0)),
            ],
            out_specs=pl.BlockSpec(memory_space=pl.ANY),
            scratch_shapes=[
                pltpu.VMEM((ring, tm, d), BF16), pltpu.VMEM((ring, tm, d), BF16),
                pltpu.VMEM((d, de), BF16), pltpu.VMEM((d, de), BF16), pltpu.VMEM((de, d), BF16),
                pltpu.SMEM((1,), jnp.int32),
                pltpu.SemaphoreType.DMA((ring,)), pltpu.SemaphoreType.DMA((ring,)),
            ],
        ),
        compiler_params=pltpu.CompilerParams(
            dimension_semantics=("arbitrary",),
            vmem_limit_bytes=_vmem_limit(32 * 1024 * 1024),
        ),
        name="experts",
    )(lay.region_tiles, xs, wg, wu, wd)


def _final_kernel(n_prompt_blocks, ne, dst_ref, odst_ref, otot_ref,
                  xp_ref, xs_ref, modp_ref, modt_ref, h2_ref, rank_ref, gates_ref, pos_ref, wts_ref,
                  wsg_ref, wsu_ref, wsd_ref, gf_ref, ye_hbm, yp_ref, ys_ref,
                  buf_ref, acc_ref, sem):
    i = pl.program_id(0)
    nb = pl.num_programs(0)
    slot = i % 2

    def fetch_slots(blk, sl, experts):
        for e in experts:
            _slot_copy(buf_ref, sl, e, ye_hbm, dst_ref[blk * ne + e], sem.at[sl], False).start()

    def overflow(blk, sl, wait):
        _overflow_copies(ne, blk, sl, odst_ref, otot_ref, buf_ref, ye_hbm, sem.at[sl], False, wait)

    @pl.when(i == 0)
    def _first():
        buf_ref[...] = jnp.zeros_like(buf_ref)
        fetch_slots(0, 0, range(ne))
        overflow(0, 0, False)

    @pl.when(i + 1 < nb)
    def _prefetch():
        fetch_slots(i + 1, 1 - slot, range(ne))
        overflow(i + 1, 1 - slot, False)

    _slots_wait(ne, buf_ref, slot, ye_hbm, sem.at[slot], False)
    overflow(i, slot, True)

    tb = h2_ref.shape[0]
    j = lax.broadcasted_iota(jnp.int32, (SLOT_ROWS, tb), 0)
    acc = jnp.zeros((tb, h2_ref.shape[1]), F32)
    for experts in _slot_chunks(ne):
        sel = jnp.concatenate(
            [jnp.where(rank_ref[e:e + 1, :] == j, gates_ref[e:e + 1, :], 0.0) for e in experts],
            axis=0)
        r0 = experts[0] * SLOT_ROWS
        acc = acc + _dot_tn(sel.astype(BF16), buf_ref[slot, r0:r0 + len(experts) * SLOT_ROWS, :])
    acc_ref[...] = acc

    pos = pos_ref[...]
    wts = wts_ref[...]

    def chunk(ci, carry):
        r0 = pl.multiple_of(ne * SLOT_ROWS + ci * ONEHOT_ROWS, ONEHOT_ROWS)
        sel = _overflow_select(pos, wts, r0).astype(BF16)
        acc_ref[...] += _dot_tn(sel, buf_ref[slot, pl.ds(r0, ONEHOT_ROWS), :])
        return carry

    lax.fori_loop(0, _overflow_chunks(otot_ref[i]), chunk, 0)

    is_p = i < n_prompt_blocks
    x = jnp.where(is_p, xp_ref[...], xs_ref[...])
    gate = jnp.where(is_p, modp_ref[0, 5:6, :], modt_ref[5])
    hb = h2_ref[...]
    hid = jax.nn.silu(_dot(hb, wsg_ref[...])) * _dot(hb, wsu_ref[...])
    f = acc_ref[...] + _dot(hid.astype(BF16), wsd_ref[...])
    x2 = x + gate * f
    y = x2 * lax.rsqrt(jnp.mean(x2 * x2, axis=-1, keepdims=True) + EPS) * gf_ref[...]

    @pl.when(is_p)
    def _p():
        yp_ref[...] = y

    @pl.when(jnp.logical_not(is_p))
    def _s():
        ys_ref[...] = y


def _final(xp2d, xs2d, modp, modt, h2, rank, gates, pos, wts, ye, lay, ne, wsg, wsu, wsd, gf):
    tp, d = xp2d.shape
    ts = xs2d.shape[0]
    tb = TOKEN_BLOCK
    npb, nsb = tp // tb, ts // tb
    seq = tp // modp.shape[0]
    bpb = seq // tb
    buf_rows = _block_buffer_rows(tb, ne)

    def p_idx(i):
        return jnp.minimum(i, npb - 1)

    def s_idx(i):
        return jnp.maximum(i - npb, 0)

    def const(shape):
        zeros = (0,) * len(shape)
        return pl.BlockSpec(shape, lambda i, *_: zeros)

    return pl.pallas_call(
        functools.partial(_final_kernel, npb, ne),
        out_shape=(jax.ShapeDtypeStruct((tp, d), F32), jax.ShapeDtypeStruct((ts, d), F32)),
        grid_spec=pltpu.PrefetchScalarGridSpec(
            num_scalar_prefetch=3,
            grid=(npb + nsb,),
            in_specs=[
                pl.BlockSpec((tb, d), lambda i, *_: (p_idx(i), 0)),
                pl.BlockSpec((tb, d), lambda i, *_: (s_idx(i), 0)),
                pl.BlockSpec((1, N_MOD, d), lambda i, *_: (p_idx(i) // bpb, 0, 0)),
                pl.BlockSpec((N_MOD, tb, d), lambda i, *_: (0, s_idx(i), 0)),
                pl.BlockSpec((tb, d), lambda i, *_: (i, 0)),
                pl.BlockSpec((ne, tb), lambda i, *_: (0, i)),
                pl.BlockSpec((ne, tb), lambda i, *_: (0, i)),
                pl.BlockSpec((TOP_K, tb), lambda i, *_: (0, i)),
                pl.BlockSpec((TOP_K, tb), lambda i, *_: (0, i)),
                const(wsg.shape), const(wsu.shape), const(wsd.shape), const(gf.shape),
                pl.BlockSpec(memory_space=pl.ANY),
            ],
            out_specs=(pl.BlockSpec((tb, d), lambda i, *_: (p_idx(i), 0)),
                       pl.BlockSpec((tb, d), lambda i, *_: (s_idx(i), 0))),
            scratch_shapes=[
                pltpu.VMEM((2, buf_rows, d), BF16),
                pltpu.VMEM((tb, d), F32),
                pltpu.SemaphoreType.DMA((2,)),
            ],
        ),
        compiler_params=pltpu.CompilerParams(
            dimension_semantics=("arbitrary",),
            vmem_limit_bytes=_vmem_limit(4 * buf_rows * d + 32 * 1024 * 1024),
        ),
        name="final",
    )(lay.dst, lay.ovf_dst, lay.ovf_tot,
      xp2d, xs2d, modp, modt, h2, rank, gates, pos, wts, wsg, wsu, wsd, gf, ye)


def kernel(x_prompt, x_sample, c_prompt, c_sample, state_pool, state_ret, w_ada, b_ada, g_norm1,
           w_in, w_pool_group, pool_scale, w_branch_pool, w_branch_ret, w_out, g_norm2, w_router,
           b_router, w_exp_gate, w_exp_up, w_exp_down, w_sh_gate, w_sh_up, w_sh_down, g_final):
    b, s, d = x_prompt.shape
    db, ds, _ = x_sample.shape
    depth = w_ada.shape[0]
    heads, dk, dv = state_ret.shape[2:]
    pool_buf, d_pool = state_pool.shape[2:]
    dims = _Dims(d, d_pool, heads, dk, dv)
    assert dims.d_in == w_in.shape[2]
    assert pool_buf == max(POOL_WINDOWS) - 1 and pool_buf < POOL_PREV
    assert s % PROMPT_STEP_ROWS == 0 and PROMPT_STEP_ROWS % RET_CHUNK == 0
    assert s % TOKEN_BLOCK == 0 and (db * ds) % TOKEN_BLOCK == 0
    assert db % (SAMPLE_SEQS_PER_STEP * SAMPLE_STEPS_PER_GROUP) == 0
    assert SLOT_ROWS % DMA_ROWS == 0 and (w_router.shape[2] * SLOT_ROWS) % ONEHOT_ROWS == 0

    log_g = np.log(1.0 - np.power(2.0, -5.0 - np.arange(heads, dtype=np.float64)))
    ns = SAMPLE_SEQS_PER_STEP
    cos_p, sin_p = _rotary_tables(np.arange(s), dk)
    cos_s, sin_s = _rotary_tables(PAST_LEN + np.arange(ds), dk)
    cos_s, sin_s = np.tile(cos_s, (ns, 1)), np.tile(sin_s, (ns, 1))
    dec_p = _decay_tables(log_g, RET_CHUNK, 1, dk, dv)
    dec_s = _decay_tables(log_g, ds, ns, dk, dv)
    pool_tabs = _sample_pool_tables(ns, ds, dims.gd, PAST_LEN)

    assert depth == 1, "the final kernel fuses the output norm into the (single) layer"
    l = 0
    xs = x_sample.reshape(db * ds, d)
    modp, modt = _ada(c_prompt, jnp.repeat(c_sample, ds, axis=0), w_ada[l], b_ada[l])
    modp = modp.reshape(b, N_MOD, d)
    g1 = g_norm1[l].reshape(1, d)
    weights = (w_in[l].astype(BF16), w_pool_group[l].astype(BF16),
               pool_scale[l].reshape(1, d_pool), w_branch_pool[l].astype(BF16),
               w_branch_ret[l].astype(BF16), w_out[l].astype(BF16))

    ne = w_router.shape[2]
    router = (g_norm2[l].reshape(1, d), w_router[l].T,
              b_router[l].reshape(N_GROUPS, ne // N_GROUPS, 1))
    x1p, pool16, ret_p, *routed = _mixer_prompt(dims, x_prompt, modp, g1, weights,
                                                (cos_p, sin_p) + dec_p, router,
                                                b * s + db * ds)
    pool_in = jnp.pad(state_pool[l], ((0, 0), (POOL_PREV - pool_buf, 0), (0, 0)))
    x1s, u_s, ret_s = _mixer_sample(dims, ds, xs, modt, g1, weights,
                                    (cos_s, sin_s) + dec_s + pool_tabs, pool_in,
                                    state_ret[l].astype(F32))
    pool_p = pool16[:, POOL_PREV - pool_buf:]
    pool_s = jnp.concatenate([state_pool[l], u_s.reshape(db, ds, d_pool)], axis=1)[:, -pool_buf:]

    x1p2d = x1p.reshape(b * s, d)
    last = b * s - PROMPT_STEP_ROWS
    routed = _route_rows(x1p2d, last, PROMPT_STEP_ROWS, modp[b - 1:b], False, *router, routed,
                         last // TOKEN_BLOCK)
    h2, pos, wts, counts, rank, gates = _route_rows(x1s, 0, db * ds, modt, True, *router, routed,
                                                    (b * s) // TOKEN_BLOCK)
    rows_cap = _expert_rows_cap(b * s + db * ds, TOKEN_BLOCK, ne)
    lay = _Layout(counts[:, :, 0])
    xe = _dispatch(h2, rank, pos, lay, ne, rows_cap)
    ye = _experts(xe, lay, w_exp_gate[l], w_exp_up[l], w_exp_down[l])
    yp, ys = _final(x1p2d, x1s, modp, modt, h2, rank, gates, pos, wts, ye, lay, ne,
                    w_sh_gate[l].astype(BF16), w_sh_up[l].astype(BF16),
                    w_sh_down[l].astype(BF16), g_final.reshape(1, d))

    return (yp.reshape(b, s, d), ys.reshape(db, ds, d), pool_p[None],
            ret_p.astype(x_prompt.dtype), pool_s[None], ret_s.astype(state_ret.dtype)[None])
```
